```python
import math
import jax, jax.numpy as jnp
from jax import lax
import numpy as np

D_MODEL = 1024
BATCH = 16
SEQ = 4096
DEPTH = 1

MEM_LEN = 256
EXPAND = 2
D_MIX = EXPAND * D_MODEL
D_CONV = 3 * D_MIX // 8
D_GMLP = 3 * D_MIX // 8
D_XATT = D_MIX - D_CONV - D_GMLP
N_XHEADS = 4
XHEAD_DIM = D_XATT // N_XHEADS
CONV_WIDTH = 31
N_CONV_GROUPS = D_CONV // 128
CHUNK = 128
GMLP_HEAD = 128
N_GMLP_HEADS = D_GMLP // GMLP_HEAD
ALPHA = (2.0 * DEPTH) ** 0.25
BETA = (8.0 * DEPTH) ** -0.25
LN_EPS = 1e-5
D_IN_PROJ = 3 * D_CONV + 3 * D_GMLP + 2 * D_XATT

kernel_name = "hybrid_conv_gmlp_memxattn_deepnorm"


def _layernorm(h, g, b):
    hf = h.astype(jnp.float32)
    mu = jnp.mean(hf, axis=-1, keepdims=True)
    var = jnp.mean(jnp.square(hf - mu), axis=-1, keepdims=True)
    return ((hf - mu) * lax.rsqrt(var + LN_EPS)).astype(h.dtype) * g + b


def _conv_branch(a, glu_g, conv_w, conv_b, ln_g, ln_b):
    h = a * jax.nn.sigmoid(glu_g)
    h = lax.conv_general_dilated(
        h, conv_w[:, None, :].astype(h.dtype), window_strides=(1,),
        padding=[(CONV_WIDTH - 1, 0)],
        dimension_numbers=("NWC", "WIO", "NWC"),
        feature_group_count=D_CONV) + conv_b
    h = _layernorm(h, ln_g, ln_b)
    return jax.nn.silu(h)


def _gmlp_branch(u, v, ln_g, ln_b, ws, bs):
    B, S, _ = u.shape
    v = _layernorm(v, ln_g, ln_b)
    v = v.reshape(B, S // CHUNK, CHUNK, N_GMLP_HEADS, GMLP_HEAD)
    causal = jnp.tril(jnp.ones((CHUNK, CHUNK), dtype=bool))
    ws_c = jnp.where(causal[None], ws, jnp.zeros((), ws.dtype))
    s = jnp.einsum("hts,bnshc->bnthc", ws_c, v) + jnp.transpose(bs)[None, None, :, :, None]
    return u * s.reshape(B, S, D_GMLP)


def _mem_xattn(q, mem, w_kv, b_kv):
    B, S, _ = q.shape
    kv = jnp.einsum("bmd,de->bme", mem, w_kv) + b_kv
    k, v = jnp.split(kv, 2, axis=-1)
    q = q.reshape(B, S, N_XHEADS, XHEAD_DIM)
    k = k.reshape(B, MEM_LEN, N_XHEADS, XHEAD_DIM)
    v = v.reshape(B, MEM_LEN, N_XHEADS, XHEAD_DIM)
    scores = jnp.einsum("bshd,bmhd->bhsm", q, k).astype(jnp.float32) * (XHEAD_DIM ** -0.5)
    p = jax.nn.softmax(scores, axis=-1).astype(v.dtype)
    o = jnp.einsum("bhsm,bmhd->bshd", p, v)
    return o.reshape(B, S, D_XATT)


def _fwd_setup_inputs(seed: int = 0) -> dict:
    key = jax.random.key(seed)
    ks = jax.random.split(key, 20)
    f32 = jnp.float32
    L = DEPTH
    nrm = lambda k, shp: jax.random.normal(k, shp, f32)
    return {
        "x": nrm(ks[0], (BATCH, SEQ, D_MODEL)),
        "mem": nrm(ks[1], (BATCH, MEM_LEN, D_MODEL)),
        "w_in": nrm(ks[2], (L, D_MODEL, D_IN_PROJ)) * D_MODEL ** -0.5,
        "b_in": 0.02 * nrm(ks[3], (L, D_IN_PROJ)),
        "conv_w": nrm(ks[4], (L, CONV_WIDTH, D_CONV)) * CONV_WIDTH ** -0.5,
        "conv_b": 0.02 * nrm(ks[5], (L, D_CONV)),
        "conv_ln_g": 1.0 + 0.05 * nrm(ks[6], (L, D_CONV)),
        "conv_ln_b": 0.02 * nrm(ks[7], (L, D_CONV)),
        "gmlp_ln_g": 1.0 + 0.05 * nrm(ks[8], (L, D_GMLP)),
        "gmlp_ln_b": 0.02 * nrm(ks[9], (L, D_GMLP)),
        "gmlp_ws": nrm(ks[10], (L, N_GMLP_HEADS, CHUNK, CHUNK)) * CHUNK ** -0.5,
        "gmlp_bs": 1.0 + 0.05 * nrm(ks[11], (L, N_GMLP_HEADS, CHUNK)),
        "w_kv": nrm(ks[12], (L, D_MODEL, 2 * D_XATT)) * D_MODEL ** -0.5,
        "b_kv": 0.02 * nrm(ks[13], (L, 2 * D_XATT)),
        "w_out": nrm(ks[14], (L, D_MIX, D_MODEL)) * (D_MIX ** -0.5) * BETA,
        "b_out": 0.02 * nrm(ks[15], (L, D_MODEL)),
        "ln_g": 1.0 + 0.05 * nrm(ks[16], (L, D_MODEL)),
        "ln_b": 0.02 * nrm(ks[17], (L, D_MODEL)),
    }


def _fwd_reference(x, mem, w_in, b_in, conv_w, conv_b, conv_ln_g, conv_ln_b, gmlp_ln_g, gmlp_ln_b,
              gmlp_ws, gmlp_bs, w_kv, b_kv, w_out, b_out, ln_g, ln_b):
    split_pts = np.cumsum([D_CONV, D_CONV, D_CONV, D_GMLP, D_GMLP, D_GMLP, D_XATT])
    for l in range(DEPTH):
        h = jnp.einsum("bsd,de->bse", x, w_in[l]) + b_in[l]
        (c_a, c_glu, c_gate, g_u, g_v, g_gate, xq, x_gate) = jnp.split(
            h, [int(p) for p in split_pts], axis=-1)
        y_conv = _conv_branch(c_a, c_glu, conv_w[l], conv_b[l], conv_ln_g[l], conv_ln_b[l])
        y_gmlp = _gmlp_branch(jax.nn.gelu(g_u), jax.nn.gelu(g_v), gmlp_ln_g[l], gmlp_ln_b[l],
                              gmlp_ws[l], gmlp_bs[l])
        y_xatt = _mem_xattn(xq, mem, w_kv[l], b_kv[l])
        y = jnp.concatenate([y_conv * jax.nn.silu(c_gate),
                             y_gmlp * jax.nn.silu(g_gate),
                             y_xatt * jax.nn.silu(x_gate)], axis=-1)
        y = jnp.einsum("bse,ed->bsd", y, w_out[l]) + b_out[l]
        x = _layernorm(ALPHA * x + y, ln_g[l], ln_b[l])
    return x


import jax as _jax
import jax.numpy as _jnp

TWIN_FORMAT = 'train_step'
FWD_PARAMS = ['x', 'mem', 'w_in', 'b_in', 'conv_w', 'conv_b', 'conv_ln_g', 'conv_ln_b', 'gmlp_ln_g', 'gmlp_ln_b', 'gmlp_ws', 'gmlp_bs', 'w_kv', 'b_kv', 'w_out', 'b_out', 'ln_g', 'ln_b']
TWIN_WEIGHTS = ['w_in', 'b_in', 'conv_w', 'conv_b', 'conv_ln_g', 'conv_ln_b', 'gmlp_ln_g', 'gmlp_ln_b', 'gmlp_ws', 'gmlp_bs', 'w_kv', 'b_kv', 'w_out', 'b_out', 'ln_g', 'ln_b']
TWIN_DIFF_INPUT = 'x'
TWIN_INPUTS = ['x', 'mem', 'w_in', 'b_in', 'conv_w', 'conv_b', 'conv_ln_g', 'conv_ln_b', 'gmlp_ln_g', 'gmlp_ln_b', 'gmlp_ws', 'gmlp_bs', 'w_kv', 'b_kv', 'w_out', 'b_out', 'ln_g', 'ln_b', 'loss_target', 'm_w_in', 'm_b_in', 'm_conv_w', 'm_conv_b', 'm_conv_ln_g', 'm_conv_ln_b', 'm_gmlp_ln_g', 'm_gmlp_ln_b', 'm_gmlp_ws', 'm_gmlp_bs', 'm_w_kv', 'm_b_kv', 'm_w_out', 'm_b_out', 'm_ln_g', 'm_ln_b', 'v_w_in', 'v_b_in', 'v_conv_w', 'v_conv_b', 'v_conv_ln_g', 'v_conv_ln_b', 'v_gmlp_ln_g', 'v_gmlp_ln_b', 'v_gmlp_ws', 'v_gmlp_bs', 'v_w_kv', 'v_b_kv', 'v_w_out', 'v_b_out', 'v_ln_g', 'v_ln_b']
TWIN_OUTPUTS = ['loss', 'grad_x', 'grad_w_in', 'grad_b_in', 'grad_conv_w', 'grad_conv_b', 'grad_conv_ln_g', 'grad_conv_ln_b', 'grad_gmlp_ln_g', 'grad_gmlp_ln_b', 'grad_gmlp_ws', 'grad_gmlp_bs', 'grad_w_kv', 'grad_b_kv', 'grad_w_out', 'grad_b_out', 'grad_ln_g', 'grad_ln_b', 'delta_w_in', 'delta_b_in', 'delta_conv_w', 'delta_conv_b', 'delta_conv_ln_g', 'delta_conv_ln_b', 'delta_gmlp_ln_g', 'delta_gmlp_ln_b', 'delta_gmlp_ws', 'delta_gmlp_bs', 'delta_w_kv', 'delta_b_kv', 'delta_w_out', 'delta_b_out', 'delta_ln_g', 'delta_ln_b', 'new_m_w_in', 'new_m_b_in', 'new_m_conv_w', 'new_m_conv_b', 'new_m_conv_ln_g', 'new_m_conv_ln_b', 'new_m_gmlp_ln_g', 'new_m_gmlp_ln_b', 'new_m_gmlp_ws', 'new_m_gmlp_bs', 'new_m_w_kv', 'new_m_b_kv', 'new_m_w_out', 'new_m_b_out', 'new_m_ln_g', 'new_m_ln_b', 'new_v_w_in', 'new_v_b_in', 'new_v_conv_w', 'new_v_conv_b', 'new_v_conv_ln_g', 'new_v_conv_ln_b', 'new_v_gmlp_ln_g', 'new_v_gmlp_ln_b', 'new_v_gmlp_ws', 'new_v_gmlp_bs', 'new_v_w_kv', 'new_v_b_kv', 'new_v_w_out', 'new_v_b_out', 'new_v_ln_g', 'new_v_ln_b']
TWIN_LEAF_KINDS = {'loss': 'loss', 'grad_x': 'grad_x', 'grad_w_in': 'grad_w', 'grad_b_in': 'grad_w', 'grad_conv_w': 'grad_w', 'grad_conv_b': 'grad_w', 'grad_conv_ln_g': 'grad_w', 'grad_conv_ln_b': 'grad_w', 'grad_gmlp_ln_g': 'grad_w', 'grad_gmlp_ln_b': 'grad_w', 'grad_gmlp_ws': 'grad_w', 'grad_gmlp_bs': 'grad_w', 'grad_w_kv': 'grad_w', 'grad_b_kv': 'grad_w', 'grad_w_out': 'grad_w', 'grad_b_out': 'grad_w', 'grad_ln_g': 'grad_w', 'grad_ln_b': 'grad_w', 'delta_w_in': 'delta_w', 'delta_b_in': 'delta_w', 'delta_conv_w': 'delta_w', 'delta_conv_b': 'delta_w', 'delta_conv_ln_g': 'delta_w', 'delta_conv_ln_b': 'delta_w', 'delta_gmlp_ln_g': 'delta_w', 'delta_gmlp_ln_b': 'delta_w', 'delta_gmlp_ws': 'delta_w', 'delta_gmlp_bs': 'delta_w', 'delta_w_kv': 'delta_w', 'delta_b_kv': 'delta_w', 'delta_w_out': 'delta_w', 'delta_b_out': 'delta_w', 'delta_ln_g': 'delta_w', 'delta_ln_b': 'delta_w', 'new_m_w_in': 'new_m', 'new_m_b_in': 'new_m', 'new_m_conv_w': 'new_m', 'new_m_conv_b': 'new_m', 'new_m_conv_ln_g': 'new_m', 'new_m_conv_ln_b': 'new_m', 'new_m_gmlp_ln_g': 'new_m', 'new_m_gmlp_ln_b': 'new_m', 'new_m_gmlp_ws': 'new_m', 'new_m_gmlp_bs': 'new_m', 'new_m_w_kv': 'new_m', 'new_m_b_kv': 'new_m', 'new_m_w_out': 'new_m', 'new_m_b_out': 'new_m', 'new_m_ln_g': 'new_m', 'new_m_ln_b': 'new_m', 'new_v_w_in': 'new_v', 'new_v_b_in': 'new_v', 'new_v_conv_w': 'new_v', 'new_v_conv_b': 'new_v', 'new_v_conv_ln_g': 'new_v', 'new_v_conv_ln_b': 'new_v', 'new_v_gmlp_ln_g': 'new_v', 'new_v_gmlp_ln_b': 'new_v', 'new_v_gmlp_ws': 'new_v', 'new_v_gmlp_bs': 'new_v', 'new_v_w_kv': 'new_v', 'new_v_b_kv': 'new_v', 'new_v_w_out': 'new_v', 'new_v_b_out': 'new_v', 'new_v_ln_g': 'new_v', 'new_v_ln_b': 'new_v'}


def _forward(args):
    return _fwd_reference(*[args[k] for k in FWD_PARAMS])


def _output_shape():
    out = _jax.eval_shape(lambda: _forward(_fwd_setup_inputs(0)))
    return out.shape, out.dtype

N_MICROBATCH = 1
ADAM_LR = 0.001
ADAM_B1 = 0.9
ADAM_B2 = 0.999
ADAM_EPS = 1e-08
ADAM_WD = 0.01
ADAM_STEP = 10
PER_EXAMPLE_BATCH_AXIS = {'x': 0, 'mem': 0, 'loss_target': 0}
SHARED_INPUTS = []
_WEIGHT_DTYPES = {'w_in': _jnp.float32, 'b_in': _jnp.float32, 'conv_w': _jnp.float32, 'conv_b': _jnp.float32, 'conv_ln_g': _jnp.float32, 'conv_ln_b': _jnp.float32, 'gmlp_ln_g': _jnp.float32, 'gmlp_ln_b': _jnp.float32, 'gmlp_ws': _jnp.float32, 'gmlp_bs': _jnp.float32, 'w_kv': _jnp.float32, 'b_kv': _jnp.float32, 'w_out': _jnp.float32, 'b_out': _jnp.float32, 'ln_g': _jnp.float32, 'ln_b': _jnp.float32}
MOMENT_SCALE = {'w_in': 3.073500e-02, 'b_in': 3.410987e-02, 'conv_w': 3.288225e-02, 'conv_b': 7.301874e-02, 'conv_ln_g': 3.930500e-02, 'conv_ln_b': 3.537554e-02, 'gmlp_ln_g': 2.479232e-02, 'gmlp_ln_b': 2.834043e-02, 'gmlp_ws': 2.439870e-02, 'gmlp_bs': 3.506113e-02, 'w_kv': 5.602487e-03, 'b_kv': 4.783422e-02, 'w_out': 7.936151e-02, 'b_out': 6.097118e-01, 'ln_g': 6.412564e+01, 'ln_b': 1.377562e+00}


def _to_microbatches(a, axis):
    t = _jnp.moveaxis(a, axis, 0)
    t = t.reshape((N_MICROBATCH, t.shape[0] // N_MICROBATCH) + t.shape[1:])
    return _jnp.moveaxis(t, 1, axis + 1)


def setup_inputs(seed: int = 0) -> dict:
    inp = _fwd_setup_inputs(seed)
    key = _jax.random.fold_in(_jax.random.key(seed), 7919)
    shape, _ = _output_shape()
    out = dict(inp)
    out["loss_target"] = _jax.random.normal(_jax.random.fold_in(key, 0), shape, _jnp.float32)
    for i, name in enumerate(TWIN_WEIGHTS):
        w = inp[name].astype(_jnp.float32)
        if MOMENT_SCALE is None:
            s = _jnp.sqrt(_jnp.mean(_jnp.square(w)) + 1e-30)
        else:
            s = MOMENT_SCALE[name]
        km, kv = _jax.random.split(_jax.random.fold_in(key, i + 1))
        out[name] = w
        out["m_" + name] = s * _jax.random.normal(km, w.shape, _jnp.float32)
        out["v_" + name] = (s * s) * _jax.random.uniform(kv, w.shape, _jnp.float32, 0.5, 1.5)
    if N_MICROBATCH > 1:
        for name, axis in PER_EXAMPLE_BATCH_AXIS.items():
            out[name] = _to_microbatches(out[name], axis)
    return {'x': out['x'], 'mem': out['mem'], 'w_in': out['w_in'], 'b_in': out['b_in'], 'conv_w': out['conv_w'], 'conv_b': out['conv_b'], 'conv_ln_g': out['conv_ln_g'], 'conv_ln_b': out['conv_ln_b'], 'gmlp_ln_g': out['gmlp_ln_g'], 'gmlp_ln_b': out['gmlp_ln_b'], 'gmlp_ws': out['gmlp_ws'], 'gmlp_bs': out['gmlp_bs'], 'w_kv': out['w_kv'], 'b_kv': out['b_kv'], 'w_out': out['w_out'], 'b_out': out['b_out'], 'ln_g': out['ln_g'], 'ln_b': out['ln_b'], 'loss_target': out['loss_target'], 'm_w_in': out['m_w_in'], 'm_b_in': out['m_b_in'], 'm_conv_w': out['m_conv_w'], 'm_conv_b': out['m_conv_b'], 'm_conv_ln_g': out['m_conv_ln_g'], 'm_conv_ln_b': out['m_conv_ln_b'], 'm_gmlp_ln_g': out['m_gmlp_ln_g'], 'm_gmlp_ln_b': out['m_gmlp_ln_b'], 'm_gmlp_ws': out['m_gmlp_ws'], 'm_gmlp_bs': out['m_gmlp_bs'], 'm_w_kv': out['m_w_kv'], 'm_b_kv': out['m_b_kv'], 'm_w_out': out['m_w_out'], 'm_b_out': out['m_b_out'], 'm_ln_g': out['m_ln_g'], 'm_ln_b': out['m_ln_b'], 'v_w_in': out['v_w_in'], 'v_b_in': out['v_b_in'], 'v_conv_w': out['v_conv_w'], 'v_conv_b': out['v_conv_b'], 'v_conv_ln_g': out['v_conv_ln_g'], 'v_conv_ln_b': out['v_conv_ln_b'], 'v_gmlp_ln_g': out['v_gmlp_ln_g'], 'v_gmlp_ln_b': out['v_gmlp_ln_b'], 'v_gmlp_ws': out['v_gmlp_ws'], 'v_gmlp_bs': out['v_gmlp_bs'], 'v_w_kv': out['v_w_kv'], 'v_b_kv': out['v_b_kv'], 'v_w_out': out['v_w_out'], 'v_b_out': out['v_b_out'], 'v_ln_g': out['v_ln_g'], 'v_ln_b': out['v_ln_b']}


def _loss(weights, diff, rest, loss_target):
    with _jax.named_scope("forward"):
        args = {**rest, TWIN_DIFF_INPUT: diff, **{k: w.astype(_WEIGHT_DTYPES[k]) for k, w in weights.items()}}
        y = _forward(args)
    with _jax.named_scope("loss_head"):
        err = _jnp.square(y.astype(_jnp.float32) - loss_target)
        return 0.5 * _jnp.sum(_jnp.mean(err, axis=-1)) if err.ndim else 0.5 * err


def _adamw(w, g, m, v):
    m = ADAM_B1 * m + (1.0 - ADAM_B1) * g
    v = ADAM_B2 * v + (1.0 - ADAM_B2) * _jnp.square(g)
    m_hat = m / (1.0 - ADAM_B1 ** ADAM_STEP)
    v_hat = v / (1.0 - ADAM_B2 ** ADAM_STEP)
    delta = -ADAM_LR * (m_hat / (_jnp.sqrt(v_hat) + ADAM_EPS) + ADAM_WD * w)
    return delta, m, v


def reference(x, mem, w_in, b_in, conv_w, conv_b, conv_ln_g, conv_ln_b, gmlp_ln_g, gmlp_ln_b, gmlp_ws, gmlp_bs, w_kv, b_kv, w_out, b_out, ln_g, ln_b, loss_target, m_w_in, m_b_in, m_conv_w, m_conv_b, m_conv_ln_g, m_conv_ln_b, m_gmlp_ln_g, m_gmlp_ln_b, m_gmlp_ws, m_gmlp_bs, m_w_kv, m_b_kv, m_w_out, m_b_out, m_ln_g, m_ln_b, v_w_in, v_b_in, v_conv_w, v_conv_b, v_conv_ln_g, v_conv_ln_b, v_gmlp_ln_g, v_gmlp_ln_b, v_gmlp_ws, v_gmlp_bs, v_w_kv, v_b_kv, v_w_out, v_b_out, v_ln_g, v_ln_b):
    given = dict(x=x, mem=mem, w_in=w_in, b_in=b_in, conv_w=conv_w, conv_b=conv_b, conv_ln_g=conv_ln_g, conv_ln_b=conv_ln_b, gmlp_ln_g=gmlp_ln_g, gmlp_ln_b=gmlp_ln_b, gmlp_ws=gmlp_ws, gmlp_bs=gmlp_bs, w_kv=w_kv, b_kv=b_kv, w_out=w_out, b_out=b_out, ln_g=ln_g, ln_b=ln_b, loss_target=loss_target, m_w_in=m_w_in, m_b_in=m_b_in, m_conv_w=m_conv_w, m_conv_b=m_conv_b, m_conv_ln_g=m_conv_ln_g, m_conv_ln_b=m_conv_ln_b, m_gmlp_ln_g=m_gmlp_ln_g, m_gmlp_ln_b=m_gmlp_ln_b, m_gmlp_ws=m_gmlp_ws, m_gmlp_bs=m_gmlp_bs, m_w_kv=m_w_kv, m_b_kv=m_b_kv, m_w_out=m_w_out, m_b_out=m_b_out, m_ln_g=m_ln_g, m_ln_b=m_ln_b, v_w_in=v_w_in, v_b_in=v_b_in, v_conv_w=v_conv_w, v_conv_b=v_conv_b, v_conv_ln_g=v_conv_ln_g, v_conv_ln_b=v_conv_ln_b, v_gmlp_ln_g=v_gmlp_ln_g, v_gmlp_ln_b=v_gmlp_ln_b, v_gmlp_ws=v_gmlp_ws, v_gmlp_bs=v_gmlp_bs, v_w_kv=v_w_kv, v_b_kv=v_b_kv, v_w_out=v_w_out, v_b_out=v_b_out, v_ln_g=v_ln_g, v_ln_b=v_ln_b)
    weights = {n: given[n] for n in TWIN_WEIGHTS}
    shared = {n: given[n] for n in SHARED_INPUTS}
    per_example = {n: given[n] for n in ['x', 'mem']}
    grad_fn = _jax.value_and_grad(_loss, argnums=(0, 1))

    def one_microbatch(ex, loss_target):
        ex = dict(ex)
        diff = ex.pop(TWIN_DIFF_INPUT)
        return grad_fn(weights, diff, {**shared, **ex}, loss_target)

    if N_MICROBATCH == 1:
        loss, (grad_w, grad_x) = one_microbatch(per_example, given["loss_target"])
    else:
        def body(carry, xs):
            loss_sum, grad_sum = carry
            l_k, (gw_k, gx_k) = one_microbatch(xs[0], xs[1])
            with _jax.named_scope("update"):
                return (loss_sum + l_k, _jax.tree.map(_jnp.add, grad_sum, gw_k)), gx_k

        init = (_jnp.zeros((), _jnp.float32), _jax.tree.map(_jnp.zeros_like, weights))
        (loss, grad_w), grad_x = _jax.lax.scan(body, init, (per_example, given["loss_target"]))
    with _jax.named_scope("update"):
        delta_w, new_m, new_v = {}, {}, {}
        for n in TWIN_WEIGHTS:
            delta_w[n], new_m[n], new_v[n] = _adamw(weights[n], grad_w[n], given["m_" + n], given["v_" + n])
    return (loss, grad_x, *[grad_w[n] for n in TWIN_WEIGHTS], *[delta_w[n] for n in TWIN_WEIGHTS],
            *[new_m[n] for n in TWIN_WEIGHTS], *[new_v[n] for n in TWIN_WEIGHTS])
```

```python
import functools
import math

import jax
import jax.numpy as jnp
from jax import lax
from jax.experimental import pallas as pl
from jax.experimental.pallas import tpu as pltpu

F32 = jnp.float32
BF16 = jnp.bfloat16

D_MODEL = 1024
MEM_LEN = 256
D_MIX = 2048
D_CONV = 768
D_GMLP = 768
D_XATT = 512
N_XHEADS = 4
XHEAD = 128
CONV_WIDTH = 31
CHUNK = 128
N_GHEADS = 6
D_IN = 3 * D_CONV + 3 * D_GMLP + 2 * D_XATT
ALPHA = 2.0 ** 0.25
LN_EPS = 1e-5
N_CHIPS = 4
W_IN_SHARD = D_IN // N_CHIPS
W_OUT_SHARD = D_MIX // N_CHIPS
W_KV_SHARD = D_MODEL // N_CHIPS
CONV_SHARD = D_CONV // N_CHIPS
HALO = 32

C_A, C_GLU, C_GATE = 0, 768, 1536
G_U, G_V, G_GATE = 2304, 3072, 3840
X_Q, X_GATE = 4608, 5120

ADAM_LR = 0.001
ADAM_B1 = 0.9
ADAM_B2 = 0.999
ADAM_EPS = 1e-08
ADAM_WD = 0.01
ADAM_STEP = 10
BC1 = 1.0 - ADAM_B1 ** ADAM_STEP
BC2 = 1.0 - ADAM_B2 ** ADAM_STEP

VMEM_LIMIT = 56 * 1024 * 1024
MESH_ID = pl.DeviceIdType.MESH
ANY = pl.BlockSpec(memory_space=pl.ANY)

GELU_C0 = math.sqrt(2.0 / math.pi)
GELU_C1 = 0.044715


def _sigmoid(v):
    return 1.0 / (1.0 + jnp.exp(-v))


def _dsilu(v, s):
    return s * (1.0 + v * (1.0 - s))


def _gelu_and_grad(v):
    t = jnp.tanh(GELU_C0 * (v + GELU_C1 * v * v * v))
    g = 0.5 * v * (1.0 + t)
    dg = 0.5 * (1.0 + t) + 0.5 * v * (1.0 - t * t) * (GELU_C0 * (1.0 + 3.0 * GELU_C1 * v * v))
    return g, dg


def _gelu(v):
    return 0.5 * v * (1.0 + jnp.tanh(GELU_C0 * (v + GELU_C1 * v * v * v)))


def _ln_stats(v):
    mu = jnp.mean(v, axis=-1, keepdims=True)
    vc = v - mu
    var = jnp.mean(vc * vc, axis=-1, keepdims=True)
    rstd = lax.rsqrt(var + LN_EPS)
    return vc * rstd, rstd


def _ln_bwd(dvhat, vhat, rstd):
    m1 = jnp.mean(dvhat, axis=-1, keepdims=True)
    m2 = jnp.mean(dvhat * vhat, axis=-1, keepdims=True)
    return rstd * (dvhat - m1 - vhat * m2)


def _colsum(v):
    return jnp.sum(v, axis=0, keepdims=True)


def _dot(a, b):
    return jnp.dot(a, b, preferred_element_type=F32)


def _dot_nt(a, b):
    return lax.dot_general(a, b, (((1,), (1,)), ((), ())), preferred_element_type=F32)


def _dot_tn(a, b):
    return lax.dot_general(a, b, (((0,), (0,)), ((), ())), preferred_element_type=F32)


def _causal_ws(ws_ref):
    row = lax.broadcasted_iota(jnp.int32, (CHUNK, CHUNK), 0)
    col = lax.broadcasted_iota(jnp.int32, (CHUNK, CHUNK), 1)
    keep = col <= row
    return [jnp.where(keep, ws_ref[hd], 0.0).astype(BF16) for hd in range(N_GHEADS)], keep


def _params(sem):
    return pltpu.CompilerParams(dimension_semantics=sem, vmem_limit_bytes=VMEM_LIMIT)


def _gather_weights(win_s, wout_s, wkv_s, cw_s):
    def body(win_ref, wout_ref, wkv_ref, cw_ref, win_o, wout_o, wkv_o, cw_o, send_sems, recv_sems, loc_sems):
        x, y, c = lax.axis_index("x"), lax.axis_index("y"), lax.axis_index("c")
        chips = [(1 - x, y), (x, 1 - y), (1 - x, 1 - y)]
        srcs = [win_ref, wout_ref, wkv_ref, cw_ref]
        outs = [win_o, wout_o, wkv_o, cw_o]
        mine = 2 * x + y
        local = [pltpu.make_async_copy(srcs[a], outs[a].at[mine], loc_sems.at[a]) for a in range(4)]
        for cp in local:
            cp.start()

        def remote(p, a, slot):
            px, py = chips[p]
            return pltpu.make_async_remote_copy(
                src_ref=srcs[a], dst_ref=outs[a].at[slot], send_sem=send_sems.at[4 * p + a],
                recv_sem=recv_sems.at[4 * p + a], device_id=(px, py, c), device_id_type=MESH_ID)

        sends = [remote(p, a, mine) for p in range(3) for a in range(4)]
        for cp in sends:
            cp.start()
        for p in range(3):
            px, py = chips[p]
            for a in range(4):
                remote(p, a, 2 * px + py).wait_recv()
        for cp in sends:
            cp.wait_send()
        for cp in local:
            cp.wait()

    shapes = [jax.ShapeDtypeStruct((N_CHIPS,) + a.shape, a.dtype) for a in (win_s, wout_s, wkv_s, cw_s)]
    return pl.pallas_call(
        body, name="gather_weights", out_shape=shapes, in_specs=[ANY] * 4, out_specs=[ANY] * 4,
        scratch_shapes=[pltpu.SemaphoreType.DMA((12,)), pltpu.SemaphoreType.DMA((12,)), pltpu.SemaphoreType.DMA((4,))],
    )(win_s, wout_s, wkv_s, cw_s)


def _in_proj(x2, w_full, b_in, tm):
    n = x2.shape[0]

    def body(x_ref, w_ref, b_ref, h_ref):
        h_ref[...] = _dot(x_ref[...].astype(BF16), w_ref[...]) + b_ref[...]

    return pl.pallas_call(
        body, name="in_proj", grid=(N_CHIPS, n // tm),
        in_specs=[pl.BlockSpec((tm, D_MODEL), lambda j, i: (i, 0)),
                  pl.BlockSpec((None, D_MODEL, W_IN_SHARD), lambda j, i: (j, 0, 0)),
                  pl.BlockSpec((1, W_IN_SHARD), lambda j, i: (0, j))],
        out_specs=pl.BlockSpec((tm, W_IN_SHARD), lambda j, i: (i, j)),
        out_shape=jax.ShapeDtypeStruct((n, D_IN), F32),
        compiler_params=_params(("arbitrary", "arbitrary")),
    )(x2, w_full, b_in)


def _kv_proj(mem2, wkv_full, b_kv):
    m = mem2.shape[0]

    def body(m_ref, w_ref, b_ref, o_ref):
        o_ref[...] = _dot(m_ref[...].astype(BF16), w_ref[...]) + b_ref[...]

    return pl.pallas_call(
        body, name="kv_proj", grid=(m // MEM_LEN,),
        in_specs=[pl.BlockSpec((MEM_LEN, D_MODEL), lambda i: (i, 0)),
                  pl.BlockSpec((D_MODEL, D_MODEL), lambda i: (0, 0)),
                  pl.BlockSpec((1, D_MODEL), lambda i: (0, 0))],
        out_specs=pl.BlockSpec((MEM_LEN, D_MODEL), lambda i: (i, 0)),
        out_shape=jax.ShapeDtypeStruct((m, D_MODEL), F32),
        compiler_params=_params(("arbitrary",)),
    )(mem2, wkv_full, b_kv)


CONV_ROWS = 16


def _branch_fwd(h, kv, cw, cb, cg, cbeta, gg_, gb_, ws, bs_t, nb, seq, tm):
    nt = seq // tm
    n = nb * seq

    def body(h_ref, kv_ref, cw_ref, cb_ref, clg_ref, clb_ref, glg_ref, glb_ref, ws_ref, bst_ref,
             y_ref, z_ref, hcbuf):
        i = pl.program_id(1)

        @pl.when(i == 0)
        def _():
            hcbuf[0:HALO, :] = jnp.zeros((HALO, D_CONV), F32)

        @pl.when(i > 0)
        def _():
            hcbuf[0:HALO, :] = hcbuf[tm:tm + HALO, :]

        hcbuf[HALO:HALO + tm, :] = h_ref[:, C_A:C_A + D_CONV] * _sigmoid(h_ref[:, C_GLU:C_GLU + D_CONV])
        for r in range(tm // CONV_ROWS):
            base = r * CONV_ROWS
            acc = jnp.broadcast_to(cb_ref[...], (CONV_ROWS, D_CONV))
            for k in range(CONV_WIDTH):
                acc = acc + cw_ref[k:k + 1, :] * hcbuf[base + 2 + k:base + 2 + k + CONV_ROWS, :]
            z_ref[base:base + CONV_ROWS, :] = acc
        zhat, _ = _ln_stats(z_ref[...])
        zn = zhat * clg_ref[...] + clb_ref[...]
        cgate = h_ref[:, C_GATE:C_GATE + D_CONV]
        y_ref[:, 0:D_CONV] = (zn * _sigmoid(zn) * (cgate * _sigmoid(cgate))).astype(BF16)

        wsc, _ = _causal_ws(ws_ref)
        vhat, _ = _ln_stats(_gelu(h_ref[:, G_V:G_V + D_GMLP]))
        vn = (vhat * glg_ref[...] + glb_ref[...]).astype(BF16)
        for ch in range(tm // CHUNK):
            rows = slice(ch * CHUNK, (ch + 1) * CHUNK)
            for hd in range(N_GHEADS):
                cols = slice(hd * CHUNK, (hd + 1) * CHUNK)
                s = _dot(wsc[hd], vn[rows, cols]) + bst_ref[:, hd:hd + 1]
                u = _gelu(h_ref[rows, G_U + hd * CHUNK:G_U + (hd + 1) * CHUNK])
                gate = h_ref[rows, G_GATE + hd * CHUNK:G_GATE + (hd + 1) * CHUNK]
                y_ref[rows, D_CONV + hd * CHUNK:D_CONV + (hd + 1) * CHUNK] = (
                    u * s * (gate * _sigmoid(gate))).astype(BF16)

        scale = XHEAD ** -0.5
        for hd in range(N_XHEADS):
            cols = slice(hd * XHEAD, (hd + 1) * XHEAD)
            q = h_ref[:, X_Q + hd * XHEAD:X_Q + (hd + 1) * XHEAD].astype(BF16)
            k = kv_ref[:, hd * XHEAD:(hd + 1) * XHEAD].astype(BF16)
            v = kv_ref[:, D_XATT + hd * XHEAD:D_XATT + (hd + 1) * XHEAD].astype(BF16)
            s = _dot_nt(q, k) * scale
            e = jnp.exp(s - jnp.max(s, axis=-1, keepdims=True))
            p = e / jnp.sum(e, axis=-1, keepdims=True)
            o = _dot(p.astype(BF16), v)
            gate = h_ref[:, X_GATE + hd * XHEAD:X_GATE + (hd + 1) * XHEAD]
            y_ref[:, 2 * D_CONV + hd * XHEAD:2 * D_CONV + (hd + 1) * XHEAD] = (
                o * (gate * _sigmoid(gate))).astype(BF16)

    row = lambda b, i: (b * nt + i, 0)
    const2 = lambda b, i: (0, 0)
    vec = pl.BlockSpec((1, D_CONV), const2)
    return pl.pallas_call(
        body, name="branch_fwd", grid=(nb, nt),
        in_specs=[pl.BlockSpec((tm, D_IN), row),
                  pl.BlockSpec((MEM_LEN, D_MODEL), lambda b, i: (b, 0)),
                  pl.BlockSpec((CONV_WIDTH, D_CONV), const2), vec, vec, vec, vec, vec,
                  pl.BlockSpec((N_GHEADS, CHUNK, CHUNK), lambda b, i: (0, 0, 0)),
                  pl.BlockSpec((CHUNK, N_GHEADS), const2)],
        out_specs=[pl.BlockSpec((tm, D_MIX), row), pl.BlockSpec((tm, D_CONV), row)],
        out_shape=[jax.ShapeDtypeStruct((n, D_MIX), BF16), jax.ShapeDtypeStruct((n, D_CONV), F32)],
        scratch_shapes=[pltpu.VMEM((HALO + tm, D_CONV), F32)],
        compiler_params=_params(("arbitrary", "arbitrary")),
    )(h, kv, cw, cb, cg, cbeta, gg_, gb_, ws, bs_t)


def _out_proj_loss(ycat, wout_full, b_out, ln_g, ln_b, x2, tgt2, tm):
    n = x2.shape[0]

    def body(y_ref, w_ref, bo_ref, g_ref, b_ref, x_ref, t_ref, dr_ref, dy_ref, vec_ref, loss_ref):
        i = pl.program_id(0)

        @pl.when(i == 0)
        def _():
            vec_ref[...] = jnp.zeros_like(vec_ref)
            loss_ref[...] = jnp.zeros_like(loss_ref)

        r = ALPHA * x_ref[...] + _dot(y_ref[...], w_ref[...]) + bo_ref[...]
        rhat, rstd = _ln_stats(r)
        diff = rhat * g_ref[...] + b_ref[...] - t_ref[...]
        loss_ref[...] += 0.5 * jnp.sum(jnp.mean(diff * diff, axis=-1, keepdims=True), axis=0, keepdims=True)
        dout = diff * (1.0 / D_MODEL)
        dr = _ln_bwd(dout * g_ref[...], rhat, rstd)
        vec_ref[0:1, :] += _colsum(dr)
        vec_ref[1:2, :] += _colsum(dout * rhat)
        vec_ref[2:3, :] += _colsum(dout)
        dr_ref[...] = dr
        dy_ref[...] = _dot_nt(dr.astype(BF16), w_ref[...]).astype(BF16)

    row = lambda i: (i, 0)
    const = lambda i: (0, 0)
    vec = pl.BlockSpec((1, D_MODEL), const)
    return pl.pallas_call(
        body, name="out_proj_loss", grid=(n // tm,),
        in_specs=[pl.BlockSpec((tm, D_MIX), row), pl.BlockSpec((D_MIX, D_MODEL), const), vec, vec, vec,
                  pl.BlockSpec((tm, D_MODEL), row), pl.BlockSpec((tm, D_MODEL), row)],
        out_specs=[pl.BlockSpec((tm, D_MODEL), row), pl.BlockSpec((tm, D_MIX), row),
                   pl.BlockSpec((8, D_MODEL), const), pl.BlockSpec((1, 1), const)],
        out_shape=[jax.ShapeDtypeStruct((n, D_MODEL), F32), jax.ShapeDtypeStruct((n, D_MIX), BF16),
                   jax.ShapeDtypeStruct((8, D_MODEL), F32), jax.ShapeDtypeStruct((1, 1), F32)],
        compiler_params=_params(("arbitrary",)),
    )(ycat, wout_full, b_out, ln_g, ln_b, x2, tgt2)


ROW_CB, ROW_CLG, ROW_CLB, ROW_GLG, ROW_GLB = 32, 33, 34, 35, 36


def _branch_bwd(h, z, dy, kv, cw, clg, clb, glg, glb, ws, bs_t, nb, seq, tm):
    nt = seq // tm
    n = nb * seq
    hpt = tm // HALO

    def body(h_ref, ha_ref, hg_ref, z_ref, dy_ref, kv_ref, cw_ref, clg_ref, clb_ref, glg_ref, glb_ref,
             ws_ref, bst_ref, dh_ref, gbin_ref, g768_ref, gws_ref, gbst_ref, dkv_ref, gbkv_ref,
             hcbuf, dzbuf, dvnbuf):
        b = pl.program_id(0)
        i = pl.program_id(1)
        ri = nt - 1 - i

        @pl.when((b == 0) & (i == 0))
        def _():
            gbin_ref[...] = jnp.zeros_like(gbin_ref)
            g768_ref[...] = jnp.zeros_like(g768_ref)
            gws_ref[...] = jnp.zeros_like(gws_ref)
            gbst_ref[...] = jnp.zeros_like(gbst_ref)
            gbkv_ref[...] = jnp.zeros_like(gbkv_ref)

        @pl.when(i == 0)
        def _():
            dkv_ref[...] = jnp.zeros_like(dkv_ref)

        def emit(col, width, val):
            gbin_ref[:, col:col + width] += _colsum(val)
            dh_ref[:, col:col + width] = val.astype(BF16)

        d_c = dy_ref[:, 0:D_CONV].astype(F32)
        cgate = h_ref[:, C_GATE:C_GATE + D_CONV]
        sg = _sigmoid(cgate)
        zhat, zrstd = _ln_stats(z_ref[...])
        zn = zhat * clg_ref[...] + clb_ref[...]
        szn = _sigmoid(zn)
        emit(C_GATE, D_CONV, d_c * (zn * szn) * _dsilu(cgate, sg))
        dzn = d_c * (cgate * sg) * _dsilu(zn, szn)
        g768_ref[ROW_CLG:ROW_CLG + 1, :] += _colsum(dzn * zhat)
        g768_ref[ROW_CLB:ROW_CLB + 1, :] += _colsum(dzn)
        dz = _ln_bwd(dzn * clg_ref[...], zhat, zrstd)
        g768_ref[ROW_CB:ROW_CB + 1, :] += _colsum(dz)

        @pl.when(i == 0)
        def _():
            dzbuf[tm:tm + HALO, :] = jnp.zeros((HALO, D_CONV), F32)

        @pl.when(i > 0)
        def _():
            dzbuf[tm:tm + HALO, :] = dzbuf[0:HALO, :]

        dzbuf[0:tm, :] = dz
        a = h_ref[:, C_A:C_A + D_CONV]
        sgl = _sigmoid(h_ref[:, C_GLU:C_GLU + D_CONV])
        hcbuf[HALO:HALO + tm, :] = a * sgl

        @pl.when(ri == 0)
        def _():
            hcbuf[0:HALO, :] = jnp.zeros((HALO, D_CONV), F32)

        @pl.when(ri > 0)
        def _():
            hcbuf[0:HALO, :] = ha_ref[...] * _sigmoid(hg_ref[...])

        for r in range(tm // CONV_ROWS):
            base = r * CONV_ROWS
            acc = jnp.zeros((CONV_ROWS, D_CONV), F32)
            for k in range(CONV_WIDTH):
                acc = acc + cw_ref[k:k + 1, :] * dzbuf[base + 30 - k:base + 30 - k + CONV_ROWS, :]
            dvnbuf[base:base + CONV_ROWS, :] = acc
        dhc = dvnbuf[...]
        emit(C_A, D_CONV, dhc * sgl)
        emit(C_GLU, D_CONV, dhc * a * sgl * (1.0 - sgl))
        for k in range(CONV_WIDTH):
            g768_ref[k:k + 1, :] += _colsum(dzbuf[0:tm, :] * hcbuf[2 + k:2 + k + tm, :])

        wsc, _ = _causal_ws(ws_ref)
        v, dgelu_v = _gelu_and_grad(h_ref[:, G_V:G_V + D_GMLP])
        vhat, vrstd = _ln_stats(v)
        vn = (vhat * glg_ref[...] + glb_ref[...]).astype(BF16)
        for ch in range(tm // CHUNK):
            rows = slice(ch * CHUNK, (ch + 1) * CHUNK)
            for hd in range(N_GHEADS):
                cols = slice(hd * CHUNK, (hd + 1) * CHUNK)
                vn_blk = vn[rows, cols]
                s = _dot(wsc[hd], vn_blk) + bst_ref[:, hd:hd + 1]
                u, dgelu_u = _gelu_and_grad(h_ref[rows, G_U + hd * CHUNK:G_U + (hd + 1) * CHUNK])
                gate = h_ref[rows, G_GATE + hd * CHUNK:G_GATE + (hd + 1) * CHUNK]
                sgate = _sigmoid(gate)
                d_g = dy_ref[rows, D_CONV + hd * CHUNK:D_CONV + (hd + 1) * CHUNK].astype(F32)
                dgate = d_g * (u * s) * _dsilu(gate, sgate)
                gbin_ref[:, G_GATE + hd * CHUNK:G_GATE + (hd + 1) * CHUNK] += _colsum(dgate)
                dh_ref[rows, G_GATE + hd * CHUNK:G_GATE + (hd + 1) * CHUNK] = dgate.astype(BF16)
                dyg = d_g * (gate * sgate)
                du = dyg * s * dgelu_u
                gbin_ref[:, G_U + hd * CHUNK:G_U + (hd + 1) * CHUNK] += _colsum(du)
                dh_ref[rows, G_U + hd * CHUNK:G_U + (hd + 1) * CHUNK] = du.astype(BF16)
                ds = dyg * u
                dsb = ds.astype(BF16)
                gws_ref[hd] += _dot_nt(dsb, vn_blk)
                gbst_ref[:, hd:hd + 1] += jnp.sum(ds, axis=1, keepdims=True)
                dvnbuf[rows, cols] = _dot_tn(wsc[hd], dsb)
        dvn = dvnbuf[...]
        g768_ref[ROW_GLG:ROW_GLG + 1, :] += _colsum(dvn * vhat)
        g768_ref[ROW_GLB:ROW_GLB + 1, :] += _colsum(dvn)
        emit(G_V, D_GMLP, _ln_bwd(dvn * glg_ref[...], vhat, vrstd) * dgelu_v)

        scale = XHEAD ** -0.5
        for hd in range(N_XHEADS):
            q = h_ref[:, X_Q + hd * XHEAD:X_Q + (hd + 1) * XHEAD].astype(BF16)
            k = kv_ref[:, hd * XHEAD:(hd + 1) * XHEAD].astype(BF16)
            vv = kv_ref[:, D_XATT + hd * XHEAD:D_XATT + (hd + 1) * XHEAD].astype(BF16)
            s = _dot_nt(q, k) * scale
            e = jnp.exp(s - jnp.max(s, axis=-1, keepdims=True))
            p = e / jnp.sum(e, axis=-1, keepdims=True)
            pb = p.astype(BF16)
            o = _dot(pb, vv)
            gate = h_ref[:, X_GATE + hd * XHEAD:X_GATE + (hd + 1) * XHEAD]
            sgate = _sigmoid(gate)
            d_x = dy_ref[:, 2 * D_CONV + hd * XHEAD:2 * D_CONV + (hd + 1) * XHEAD].astype(F32)
            emit(X_GATE + hd * XHEAD, XHEAD, d_x * o * _dsilu(gate, sgate))
            do = (d_x * (gate * sgate)).astype(BF16)
            dp = _dot_nt(do, vv)
            dsc = (p * (dp - jnp.sum(dp * p, axis=-1, keepdims=True))).astype(BF16)
            emit(X_Q + hd * XHEAD, XHEAD, _dot(dsc, k) * scale)
            dkv_ref[:, hd * XHEAD:(hd + 1) * XHEAD] += _dot_tn(dsc, q) * scale
            dkv_ref[:, D_XATT + hd * XHEAD:D_XATT + (hd + 1) * XHEAD] += _dot_tn(pb, do)

        @pl.when(i == nt - 1)
        def _():
            gbkv_ref[...] += _colsum(dkv_ref[...])

        @pl.when((b == nb - 1) & (i == nt - 1))
        def _():
            _, keep = _causal_ws(ws_ref)
            for hd in range(N_GHEADS):
                gws_ref[hd] = jnp.where(keep, gws_ref[hd], 0.0)

    row = lambda b, i: (b * nt + nt - 1 - i, 0)
    const2 = lambda b, i: (0, 0)
    const3 = lambda b, i: (0, 0, 0)
    vec = pl.BlockSpec((1, D_CONV), const2)

    def halo(col_block):
        return pl.BlockSpec((HALO, D_CONV),
                            lambda b, i: (jnp.maximum((b * nt + nt - 1 - i) * hpt - 1, 0), col_block))

    return pl.pallas_call(
        body, name="branch_bwd", grid=(nb, nt),
        in_specs=[pl.BlockSpec((tm, D_IN), row), halo(0), halo(1),
                  pl.BlockSpec((tm, D_CONV), row), pl.BlockSpec((tm, D_MIX), row),
                  pl.BlockSpec((MEM_LEN, D_MODEL), lambda b, i: (b, 0)),
                  pl.BlockSpec((CONV_WIDTH, D_CONV), const2), vec, vec, vec, vec,
                  pl.BlockSpec((N_GHEADS, CHUNK, CHUNK), const3),
                  pl.BlockSpec((CHUNK, N_GHEADS), const2)],
        out_specs=[pl.BlockSpec((tm, D_IN), row),
                   pl.BlockSpec((1, D_IN), const2),
                   pl.BlockSpec((40, D_CONV), const2),
                   pl.BlockSpec((N_GHEADS, CHUNK, CHUNK), const3),
                   pl.BlockSpec((CHUNK, CHUNK), const2),
                   pl.BlockSpec((MEM_LEN, D_MODEL), lambda b, i: (b, 0)),
                   pl.BlockSpec((1, D_MODEL), const2)],
        out_shape=[jax.ShapeDtypeStruct((n, D_IN), BF16),
                   jax.ShapeDtypeStruct((1, D_IN), F32),
                   jax.ShapeDtypeStruct((40, D_CONV), F32),
                   jax.ShapeDtypeStruct((N_GHEADS, CHUNK, CHUNK), F32),
                   jax.ShapeDtypeStruct((CHUNK, CHUNK), F32),
                   jax.ShapeDtypeStruct((nb * MEM_LEN, D_MODEL), F32),
                   jax.ShapeDtypeStruct((1, D_MODEL), F32)],
        scratch_shapes=[pltpu.VMEM((HALO + tm, D_CONV), F32), pltpu.VMEM((tm + HALO, D_CONV), F32),
                        pltpu.VMEM((tm, D_CONV), F32)],
        compiler_params=_params(("arbitrary", "arbitrary")),
    )(h, h, h, z, dy, kv, cw, clg, clb, glg, glb, ws, bs_t)


def _grad_x(dh, w_full, dr, tm):
    n = dh.shape[0]

    def body(dh_ref, w_ref, dr_ref, o_ref):
        acc = ALPHA * dr_ref[...]
        for j in range(N_CHIPS):
            acc = acc + _dot_nt(dh_ref[:, j * W_IN_SHARD:(j + 1) * W_IN_SHARD], w_ref[j])
        o_ref[...] = acc

    return pl.pallas_call(
        body, name="grad_x", grid=(n // tm,),
        in_specs=[pl.BlockSpec((tm, D_IN), lambda i: (i, 0)),
                  pl.BlockSpec((N_CHIPS, D_MODEL, W_IN_SHARD), lambda i: (0, 0, 0)),
                  pl.BlockSpec((tm, D_MODEL), lambda i: (i, 0))],
        out_specs=pl.BlockSpec((tm, D_MODEL), lambda i: (i, 0)),
        out_shape=jax.ShapeDtypeStruct((n, D_MODEL), F32),
        compiler_params=_params(("arbitrary",)),
    )(dh, w_full, dr)


def _grad_w(a, b, tk, name):
    kdim, m = a.shape
    ncols = b.shape[1]
    shard = ncols // N_CHIPS
    nk = kdim // tk

    def body(a_ref, b_ref, o_ref, ob_ref):
        kk = pl.program_id(1)

        @pl.when(kk == 0)
        def _():
            o_ref[...] = jnp.zeros_like(o_ref)

        o_ref[...] += _dot_tn(a_ref[...].astype(BF16), b_ref[...].astype(BF16))

        @pl.when(kk == nk - 1)
        def _():
            ob_ref[...] = o_ref[...].astype(BF16)

    return pl.pallas_call(
        body, name=name, grid=(N_CHIPS, nk),
        in_specs=[pl.BlockSpec((tk, m), lambda j, kk: (kk, 0)), pl.BlockSpec((tk, shard), lambda j, kk: (kk, j))],
        out_specs=[pl.BlockSpec((None, m, shard), lambda j, kk: (j, 0, 0)),
                   pl.BlockSpec((None, m, shard), lambda j, kk: (j, 0, 0))],
        out_shape=[jax.ShapeDtypeStruct((N_CHIPS, m, shard), F32), jax.ShapeDtypeStruct((N_CHIPS, m, shard), BF16)],
        compiler_params=_params(("arbitrary", "arbitrary")),
    )(a, b)


def _grad_w_rows(a, b, tk, name):
    kdim, m = a.shape
    ncols = b.shape[1]
    shard = m // N_CHIPS
    nk = kdim // tk

    def body(a_ref, b_ref, o_ref, ob_ref):
        kk = pl.program_id(1)

        @pl.when(kk == 0)
        def _():
            o_ref[...] = jnp.zeros_like(o_ref)

        o_ref[...] += _dot_tn(a_ref[...].astype(BF16), b_ref[...].astype(BF16))

        @pl.when(kk == nk - 1)
        def _():
            ob_ref[...] = o_ref[...].astype(BF16)

    return pl.pallas_call(
        body, name=name, grid=(N_CHIPS, nk),
        in_specs=[pl.BlockSpec((tk, shard), lambda j, kk: (kk, j)), pl.BlockSpec((tk, ncols), lambda j, kk: (kk, 0))],
        out_specs=[pl.BlockSpec((None, shard, ncols), lambda j, kk: (j, 0, 0)),
                   pl.BlockSpec((None, shard, ncols), lambda j, kk: (j, 0, 0))],
        out_shape=[jax.ShapeDtypeStruct((N_CHIPS, shard, ncols), F32),
                   jax.ShapeDtypeStruct((N_CHIPS, shard, ncols), BF16)],
        compiler_params=_params(("arbitrary", "arbitrary")),
    )(a, b)


def _exchange_chips(f32_parts, bf16_parts, small):
    nbig = len(f32_parts)

    def body(*refs):
        f_refs = refs[0:nbig]
        b_refs = refs[nbig:2 * nbig]
        small_ref = refs[2 * nbig]
        own_refs = refs[2 * nbig + 1:3 * nbig + 1]
        got_refs = refs[3 * nbig + 1:4 * nbig + 1]
        gsmall_ref = refs[4 * nbig + 1]
        send_sems, recv_sems, loc_sems = refs[4 * nbig + 2:]
        x, y, c = lax.axis_index("x"), lax.axis_index("y"), lax.axis_index("c")
        chips = [(1 - x, y), (x, 1 - y), (1 - x, 1 - y)]
        mine = 2 * x + y
        local = [pltpu.make_async_copy(f_refs[a].at[mine], own_refs[a], loc_sems.at[a]) for a in range(nbig)]
        for cp in local:
            cp.start()
        na = nbig + 1

        def remote(p, a):
            px, py = chips[p]
            src = small_ref if a == nbig else b_refs[a].at[2 * px + py]
            dst = gsmall_ref.at[p] if a == nbig else got_refs[a].at[p]
            return pltpu.make_async_remote_copy(
                src_ref=src, dst_ref=dst, send_sem=send_sems.at[na * p + a], recv_sem=recv_sems.at[na * p + a],
                device_id=(px, py, c), device_id_type=MESH_ID)

        copies = [remote(p, a) for p in range(3) for a in range(na)]
        for cp in copies:
            cp.start()
        for cp in copies:
            cp.wait_recv()
        for cp in copies:
            cp.wait_send()
        for cp in local:
            cp.wait()

    out_shape = ([jax.ShapeDtypeStruct(f.shape[1:], F32) for f in f32_parts]
                 + [jax.ShapeDtypeStruct((3,) + bpart.shape[1:], BF16) for bpart in bf16_parts]
                 + [jax.ShapeDtypeStruct((3,) + small.shape, F32)])
    nsem = 3 * (nbig + 1)
    return pl.pallas_call(
        body, name="exchange_chips", out_shape=out_shape,
        in_specs=[ANY] * (2 * nbig + 1), out_specs=[ANY] * (2 * nbig + 1),
        scratch_shapes=[pltpu.SemaphoreType.DMA((nsem,)), pltpu.SemaphoreType.DMA((nsem,)),
                        pltpu.SemaphoreType.DMA((nbig,))],
    )(*f32_parts, *bf16_parts, small)


def _sum_chips(own, got, tr, name):
    r, ccols = own.shape

    def body(o_ref, g_ref, out_ref):
        out_ref[...] = (o_ref[...] + g_ref[1].astype(F32)) + (g_ref[0].astype(F32) + g_ref[2].astype(F32))

    return pl.pallas_call(
        body, name=name, grid=(r // tr,),
        in_specs=[pl.BlockSpec((tr, ccols), lambda i: (i, 0)), pl.BlockSpec((3, tr, ccols), lambda i: (0, i, 0))],
        out_specs=pl.BlockSpec((tr, ccols), lambda i: (i, 0)),
        out_shape=jax.ShapeDtypeStruct((r, ccols), F32),
        compiler_params=_params(("arbitrary",)),
    )(own, got)


def _exchange_cores(parts):
    npart = len(parts)

    def body(*refs):
        in_refs = refs[0:npart]
        out_refs = refs[npart:2 * npart]
        send_sems, recv_sems = refs[2 * npart:]
        sibling = (lax.axis_index("x"), lax.axis_index("y"), 1 - lax.axis_index("c"))
        copies = [pltpu.make_async_remote_copy(
            src_ref=in_refs[a], dst_ref=out_refs[a], send_sem=send_sems.at[a], recv_sem=recv_sems.at[a],
            device_id=sibling, device_id_type=MESH_ID) for a in range(npart)]
        for cp in copies:
            cp.start()
        for cp in copies:
            cp.wait_recv()
        for cp in copies:
            cp.wait_send()

    return pl.pallas_call(
        body, name="exchange_cores", out_shape=[jax.ShapeDtypeStruct(p.shape, p.dtype) for p in parts],
        in_specs=[ANY] * npart, out_specs=[ANY] * npart,
        scratch_shapes=[pltpu.SemaphoreType.DMA((npart,)), pltpu.SemaphoreType.DMA((npart,))],
    )(*parts)


def _adamw(a, b, w, m, v, tr, name):
    r, ccols = w.shape

    def body(a_ref, b_ref, w_ref, m_ref, v_ref, g_out, d_out, m_out, v_out):
        g = a_ref[...] + b_ref[...]
        mn = ADAM_B1 * m_ref[...] + (1.0 - ADAM_B1) * g
        vn = ADAM_B2 * v_ref[...] + (1.0 - ADAM_B2) * (g * g)
        g_out[...] = g
        m_out[...] = mn
        v_out[...] = vn
        d_out[...] = -ADAM_LR * ((mn / BC1) / (jnp.sqrt(vn / BC2) + ADAM_EPS) + ADAM_WD * w_ref[...])

    spec = pl.BlockSpec((tr, ccols), lambda i: (i, 0))
    shape = jax.ShapeDtypeStruct((r, ccols), F32)
    return pl.pallas_call(
        body, name=name, grid=(r // tr,), in_specs=[spec] * 5, out_specs=[spec] * 4, out_shape=[shape] * 4,
        compiler_params=_params(("arbitrary",)),
    )(a, b, w, m, v)


SMALL = [("b_in", (1, D_IN)), ("conv_b", (1, D_CONV)), ("conv_ln_g", (1, D_CONV)), ("conv_ln_b", (1, D_CONV)),
         ("gmlp_ln_g", (1, D_GMLP)), ("gmlp_ln_b", (1, D_GMLP)), ("gmlp_ws", (1, N_GHEADS, CHUNK, CHUNK)),
         ("gmlp_bs", (1, N_GHEADS, CHUNK)), ("b_kv", (1, D_MODEL)), ("b_out", (1, D_MODEL)),
         ("ln_g", (1, D_MODEL)), ("ln_b", (1, D_MODEL))]
CONV_W_ROWS = CONV_WIDTH * D_CONV // 128


def _pack(pieces, pad_to):
    rows = [p.reshape(-1, 128) for p in pieces]
    total = sum(r.shape[0] for r in rows)
    if pad_to > total:
        rows.append(jnp.zeros((pad_to - total, 128), F32))
    return jnp.concatenate(rows, axis=0)


def _round_up(v, mult):
    return (v + mult - 1) // mult * mult


def kernel(x, mem, w_in, b_in, conv_w, conv_b, conv_ln_g, conv_ln_b, gmlp_ln_g, gmlp_ln_b, gmlp_ws, gmlp_bs, w_kv, b_kv, w_out, b_out, ln_g, ln_b, loss_target, m_w_in, m_b_in, m_conv_w, m_conv_b, m_conv_ln_g, m_conv_ln_b, m_gmlp_ln_g, m_gmlp_ln_b, m_gmlp_ws, m_gmlp_bs, m_w_kv, m_b_kv, m_w_out, m_b_out, m_ln_g, m_ln_b, v_w_in, v_b_in, v_conv_w, v_conv_b, v_conv_ln_g, v_conv_ln_b, v_gmlp_ln_g, v_gmlp_ln_b, v_gmlp_ws, v_gmlp_bs, v_w_kv, v_b_kv, v_w_out, v_b_out, v_ln_g, v_ln_b):
    weights = dict(b_in=b_in, conv_b=conv_b, conv_ln_g=conv_ln_g, conv_ln_b=conv_ln_b, gmlp_ln_g=gmlp_ln_g,
                   gmlp_ln_b=gmlp_ln_b, gmlp_ws=gmlp_ws, gmlp_bs=gmlp_bs, b_kv=b_kv, b_out=b_out, ln_g=ln_g, ln_b=ln_b)
    mom_m = dict(b_in=m_b_in, conv_b=m_conv_b, conv_ln_g=m_conv_ln_g, conv_ln_b=m_conv_ln_b, gmlp_ln_g=m_gmlp_ln_g,
                 gmlp_ln_b=m_gmlp_ln_b, gmlp_ws=m_gmlp_ws, gmlp_bs=m_gmlp_bs, b_kv=m_b_kv, b_out=m_b_out,
                 ln_g=m_ln_g, ln_b=m_ln_b)
    mom_v = dict(b_in=v_b_in, conv_b=v_conv_b, conv_ln_g=v_conv_ln_g, conv_ln_b=v_conv_ln_b, gmlp_ln_g=v_gmlp_ln_g,
                 gmlp_ln_b=v_gmlp_ln_b, gmlp_ws=v_gmlp_ws, gmlp_bs=v_gmlp_bs, b_kv=v_b_kv, b_out=v_b_out,
                 ln_g=v_ln_g, ln_b=v_ln_b)
    nb, seq, _ = x.shape
    n = nb * seq
    tm = 256
    x2 = x.reshape(n, D_MODEL)
    tgt2 = loss_target.reshape(n, D_MODEL)
    mem2 = mem.reshape(nb * MEM_LEN, D_MODEL)
    chip = 2 * lax.axis_index("x") + lax.axis_index("y")

    win_g, wout_g, wkv_g, cw_g = _gather_weights(
        w_in[0].astype(BF16), w_out[0].astype(BF16), w_kv[0].astype(BF16), conv_w[0])
    wout_full = wout_g.reshape(D_MIX, D_MODEL)
    wkv_full = wkv_g.reshape(D_MODEL, D_MODEL)
    cw_full = jnp.transpose(cw_g, (1, 0, 2)).reshape(CONV_WIDTH, D_CONV)
    bs_t = jnp.transpose(gmlp_bs[0])

    h = _in_proj(x2, win_g, b_in, min(512, n))
    kv = _kv_proj(mem2, wkv_full, b_kv)
    ycat, z = _branch_fwd(h, kv, cw_full, conv_b, conv_ln_g, conv_ln_b, gmlp_ln_g, gmlp_ln_b, gmlp_ws[0], bs_t,
                          nb, seq, tm)
    dr, dycat, vec3, loss_part = _out_proj_loss(ycat, wout_full, b_out, ln_g, ln_b, x2, tgt2, tm)
    loss = lax.psum(loss_part[0, 0], ("x", "y", "c"))

    dh, gb_in, g768, gws, gbs_t, dkv, gb_kv = _branch_bwd(
        h, z, dycat, kv, cw_full, conv_ln_g, conv_ln_b, gmlp_ln_g, gmlp_ln_b, gmlp_ws[0], bs_t, nb, seq, tm)
    grad_x = _grad_x(dh, win_g, dr, tm).reshape(nb, seq, D_MODEL)
    gwin_f, gwin_b = _grad_w(x2, dh, min(1024, n), "grad_w_in")
    gwout_f, gwout_b = _grad_w_rows(ycat, dr, min(1024, n), "grad_w_out")
    gwkv_f, gwkv_b = _grad_w_rows(mem2, dkv, nb * MEM_LEN, "grad_w_kv")

    small_grads = dict(b_in=gb_in, conv_b=g768[ROW_CB], conv_ln_g=g768[ROW_CLG], conv_ln_b=g768[ROW_CLB],
                       gmlp_ln_g=g768[ROW_GLG], gmlp_ln_b=g768[ROW_GLB], gmlp_ws=gws,
                       gmlp_bs=jnp.transpose(gbs_t[:, 0:N_GHEADS]), b_kv=gb_kv, b_out=vec3[0], ln_g=vec3[1],
                       ln_b=vec3[2])
    small_rows = sum(math.prod(s) for _, s in SMALL) // 128 + CONV_W_ROWS
    small_pad = _round_up(small_rows, 16)
    small = _pack([small_grads[k] for k, _ in SMALL] + [g768[0:CONV_WIDTH]], small_pad)
    own_win, own_wout, own_wkv, got_win, got_wout, got_wkv, got_small = _exchange_chips(
        [gwin_f, gwout_f, gwkv_f], [gwin_b, gwout_b, gwkv_b], small)
    sum_win = _sum_chips(own_win, got_win, 256, "sum_w_in")
    sum_wout = _sum_chips(own_wout, got_wout, 256, "sum_w_out")
    sum_wkv = _sum_chips(own_wkv, got_wkv, 256, "sum_w_kv")
    sum_small = _sum_chips(small, got_small, small_pad, "sum_small")
    sib_win, sib_wout, sib_wkv, sib_small = _exchange_cores([sum_win, sum_wout, sum_wkv, sum_small])

    big = {}
    big["w_in"] = _adamw(sum_win, sib_win, w_in[0], m_w_in[0], v_w_in[0], 256, "adamw_w_in")
    big["w_out"] = _adamw(sum_wout, sib_wout, w_out[0], m_w_out[0], v_w_out[0], 256, "adamw_w_out")
    big["w_kv"] = _adamw(sum_wkv, sib_wkv, w_kv[0], m_w_kv[0], v_w_kv[0], 256, "adamw_w_kv")
    n_small = small_rows - CONV_W_ROWS
    n_small_pad = _round_up(n_small, 8)
    w_pack = _pack([weights[k] for k, _ in SMALL], n_small_pad)
    m_pack = _pack([mom_m[k] for k, _ in SMALL], n_small_pad)
    v_pack = _pack([mom_v[k] for k, _ in SMALL], n_small_pad)
    sm = _adamw(sum_small[0:n_small_pad], sib_small[0:n_small_pad], w_pack, m_pack, v_pack, n_small_pad,
                "adamw_small")
    cw_a = lax.dynamic_slice_in_dim(sum_small[n_small:small_rows].reshape(CONV_WIDTH, D_CONV),
                                    chip * CONV_SHARD, CONV_SHARD, axis=1)
    cw_b = lax.dynamic_slice_in_dim(sib_small[n_small:small_rows].reshape(CONV_WIDTH, D_CONV),
                                    chip * CONV_SHARD, CONV_SHARD, axis=1)
    cwp = ((0, 1), (0, 0))
    cw_out = _adamw(jnp.pad(cw_a, cwp), jnp.pad(cw_b, cwp), jnp.pad(conv_w[0], cwp), jnp.pad(m_conv_w[0], cwp),
                    jnp.pad(v_conv_w[0], cwp), CONV_WIDTH + 1, "adamw_conv_w")

    def small_out(which):
        outs, off = {}, 0
        for k, shape in SMALL:
            rows = math.prod(shape) // 128
            outs[k] = sm[which][off:off + rows].reshape(shape)
            off += rows
        return outs

    order = ["w_in", "b_in", "conv_w", "conv_b", "conv_ln_g", "conv_ln_b", "gmlp_ln_g", "gmlp_ln_b", "gmlp_ws",
             "gmlp_bs", "w_kv", "b_kv", "w_out", "b_out", "ln_g", "ln_b"]
    result = [loss, grad_x]
    for which in range(4):
        so = small_out(which)
        for k in order:
            if k in big:
                result.append(big[k][which][None])
            elif k == "conv_w":
                result.append(cw_out[which][0:CONV_WIDTH][None])
            else:
                result.append(so[k])
    return tuple(result)
```

```python
import functools
import math

import jax
import jax.numpy as jnp
from jax import lax
from jax.experimental import pallas as pl
from jax.experimental.pallas import tpu as pltpu

F32 = jnp.float32
BF16 = jnp.bfloat16

D_MODEL = 1024
MEM_LEN = 256
D_MIX = 2048
D_CONV = 768
D_GMLP = 768
D_XATT = 512
N_XHEADS = 4
XHEAD = 128
CONV_WIDTH = 31
CHUNK = 128
N_GHEADS = 6
D_IN = 3 * D_CONV + 3 * D_GMLP + 2 * D_XATT
ALPHA = 2.0 ** 0.25
LN_EPS = 1e-5
N_CHIPS = 4
W_IN_SHARD = D_IN // N_CHIPS
W_OUT_SHARD = D_MIX // N_CHIPS
W_KV_SHARD = D_MODEL // N_CHIPS
CONV_SHARD = D_CONV // N_CHIPS
HALO = 32

C_A, C_GLU, C_GATE = 0, 768, 1536
G_U, G_V, G_GATE = 2304, 3072, 3840
X_Q, X_GATE = 4608, 5120

ADAM_LR = 0.001
ADAM_B1 = 0.9
ADAM_B2 = 0.999
ADAM_EPS = 1e-08
ADAM_WD = 0.01
ADAM_STEP = 10
BC1 = 1.0 - ADAM_B1 ** ADAM_STEP
BC2 = 1.0 - ADAM_B2 ** ADAM_STEP

VMEM_LIMIT = 56 * 1024 * 1024
MESH_ID = pl.DeviceIdType.MESH
ANY = pl.BlockSpec(memory_space=pl.ANY)

GELU_C0 = math.sqrt(2.0 / math.pi)
GELU_C1 = 0.044715


def _sigmoid(v):
    return 1.0 / (1.0 + jnp.exp(-v))


def _dsilu(v, s):
    return s * (1.0 + v * (1.0 - s))


def _gelu_and_grad(v):
    t = jnp.tanh(GELU_C0 * (v + GELU_C1 * v * v * v))
    g = 0.5 * v * (1.0 + t)
    dg = 0.5 * (1.0 + t) + 0.5 * v * (1.0 - t * t) * (GELU_C0 * (1.0 + 3.0 * GELU_C1 * v * v))
    return g, dg


def _gelu(v):
    return 0.5 * v * (1.0 + jnp.tanh(GELU_C0 * (v + GELU_C1 * v * v * v)))


def _ln_stats(v):
    mu = jnp.mean(v, axis=-1, keepdims=True)
    vc = v - mu
    var = jnp.mean(vc * vc, axis=-1, keepdims=True)
    rstd = lax.rsqrt(var + LN_EPS)
    return vc * rstd, rstd


def _ln_bwd(dvhat, vhat, rstd):
    m1 = jnp.mean(dvhat, axis=-1, keepdims=True)
    m2 = jnp.mean(dvhat * vhat, axis=-1, keepdims=True)
    return rstd * (dvhat - m1 - vhat * m2)


def _colsum(v):
    return jnp.sum(v, axis=0, keepdims=True)


def _dot(a, b):
    return jnp.dot(a, b, preferred_element_type=F32)


def _dot_nt(a, b):
    return lax.dot_general(a, b, (((1,), (1,)), ((), ())), preferred_element_type=F32)


def _dot_tn(a, b):
    return lax.dot_general(a, b, (((0,), (0,)), ((), ())), preferred_element_type=F32)


def _causal_ws(ws_ref):
    row = lax.broadcasted_iota(jnp.int32, (CHUNK, CHUNK), 0)
    col = lax.broadcasted_iota(jnp.int32, (CHUNK, CHUNK), 1)
    keep = col <= row
    return [jnp.where(keep, ws_ref[hd], 0.0).astype(BF16) for hd in range(N_GHEADS)], keep


def _params(sem):
    return pltpu.CompilerParams(dimension_semantics=sem, vmem_limit_bytes=VMEM_LIMIT)


def _peer_chips():
    x, y, c = lax.axis_index("x"), lax.axis_index("y"), lax.axis_index("c")
    return [(1 - x, y), (x, 1 - y), (1 - x, 1 - y)], 2 * x + y, c


def _gather_shards(shard, name):
    def body(src_ref, out_ref, send_sems, recv_sems, loc_sem):
        chips, mine, c = _peer_chips()
        local = pltpu.make_async_copy(src_ref, out_ref.at[mine], loc_sem)
        local.start()

        def remote(p, slot):
            px, py = chips[p]
            return pltpu.make_async_remote_copy(
                src_ref=src_ref, dst_ref=out_ref.at[slot], send_sem=send_sems.at[p], recv_sem=recv_sems.at[p],
                device_id=(px, py, c), device_id_type=MESH_ID)

        sends = [remote(p, mine) for p in range(3)]
        for cp in sends:
            cp.start()
        for p in range(3):
            px, py = chips[p]
            remote(p, 2 * px + py).wait_recv()
        for cp in sends:
            cp.wait_send()
        local.wait()

    return pl.pallas_call(
        body, name=name, out_shape=jax.ShapeDtypeStruct((N_CHIPS,) + shard.shape, shard.dtype),
        in_specs=[ANY], out_specs=ANY,
        scratch_shapes=[pltpu.SemaphoreType.DMA((3,)), pltpu.SemaphoreType.DMA((3,)), pltpu.SemaphoreType.DMA],
    )(shard)


HBM = pl.BlockSpec(memory_space=pltpu.HBM)
SEM = pl.BlockSpec(memory_space=pltpu.SEMAPHORE)
EFFECT = pltpu.SideEffectType.DATAFLOW_SIDE_EFFECTING


def _exchange_copies(src_refs, land_refs, per_chip, send_sems, recv_sems):
    chips, _, c = _peer_chips()
    n = len(src_refs)
    copies = []
    for p, (px, py) in enumerate(chips):
        for a in range(n):
            src = src_refs[a].at[2 * px + py] if per_chip[a] else src_refs[a]
            copies.append(pltpu.make_async_remote_copy(
                src_ref=src, dst_ref=land_refs[a].at[p], send_sem=send_sems.at[n * p + a],
                recv_sem=recv_sems.at[n * p + a], device_id=(px, py, c), device_id_type=MESH_ID))
    return copies


def _start_exchange(name, srcs, per_chip):
    n = len(srcs)
    lands = [lax.empty((3,) + (s.shape[1:] if pc else s.shape), s.dtype) for s, pc in zip(srcs, per_chip)]

    def body(*refs):
        src_refs, land_refs = refs[0:n], refs[n:2 * n]
        send_sems, recv_sems = refs[2 * n], refs[2 * n + 1]
        token = refs[4 * n + 2]
        for cp in _exchange_copies(src_refs, land_refs, per_chip, send_sems, recv_sems):
            cp.start()
        token[...] = jnp.zeros_like(token)

    out = pl.pallas_call(
        body, name=name,
        out_shape=(pltpu.SemaphoreType.DMA((3 * n,)), pltpu.SemaphoreType.DMA((3 * n,)),
                   *[pltpu.HBM(a.shape, a.dtype) for a in srcs + lands], jax.ShapeDtypeStruct((8, 128), F32)),
        in_specs=[HBM] * (2 * n),
        out_specs=(SEM, SEM, *[HBM] * (2 * n), pl.BlockSpec(memory_space=pltpu.VMEM)),
        input_output_aliases={a: 2 + a for a in range(2 * n)},
        compiler_params=pltpu.CompilerParams(has_side_effects=EFFECT),
    )(*[pltpu.with_memory_space_constraint(a, pltpu.HBM) for a in srcs + lands])
    return dict(send=out[0], recv=out[1], thru=list(out[2:2 * n + 2]), token=out[2 * n + 2], per_chip=per_chip)


def _wait_exchange(name, started, after):
    thru, per_chip = started["thru"], started["per_chip"]
    n = len(thru) // 2

    def body(*refs):
        src_refs, land_refs = refs[0:n], refs[n:2 * n]
        send_sems, recv_sems = refs[2 * n], refs[2 * n + 1]
        for cp in _exchange_copies(src_refs, land_refs, per_chip, send_sems, recv_sems):
            cp.wait_send()
            cp.wait_recv()

    out = pl.pallas_call(
        body, name=name, out_shape=tuple(pltpu.HBM(a.shape, a.dtype) for a in thru),
        in_specs=[HBM] * (2 * n) + [SEM, SEM, ANY], out_specs=tuple([HBM] * (2 * n)),
        input_output_aliases={a: a for a in range(2 * n)},
        compiler_params=pltpu.CompilerParams(has_side_effects=EFFECT),
    )(*thru, started["send"], started["recv"], after)
    return list(out[n:2 * n])


def _place_shards(owns, landeds, name):
    n = len(owns)

    def body(*refs):
        own_refs, land_refs, out_refs, sems = refs[0:n], refs[n:2 * n], refs[2 * n:3 * n], refs[3 * n]
        chips, mine, _ = _peer_chips()
        copies = []
        for a in range(n):
            copies.append(pltpu.make_async_copy(own_refs[a], out_refs[a].at[mine], sems.at[4 * a + 3]))
            for p, (px, py) in enumerate(chips):
                copies.append(pltpu.make_async_copy(land_refs[a].at[p], out_refs[a].at[2 * px + py], sems.at[4 * a + p]))
        for cp in copies:
            cp.start()
        for cp in copies:
            cp.wait()

    return pl.pallas_call(
        body, name=name, out_shape=[jax.ShapeDtypeStruct((N_CHIPS,) + o.shape, o.dtype) for o in owns],
        in_specs=[ANY] * (2 * n), out_specs=[ANY] * n, scratch_shapes=[pltpu.SemaphoreType.DMA((4 * n,))],
    )(*owns, *landeds)


def _in_proj(x2, w_full, b_in, tm, after):
    n = x2.shape[0]

    def body(x_ref, w_ref, b_ref, after_ref, h_ref):
        h_ref[...] = _dot(x_ref[...].astype(BF16), w_ref[...]) + b_ref[...]

    return pl.pallas_call(
        body, name="in_proj", grid=(N_CHIPS, n // tm),
        in_specs=[pl.BlockSpec((tm, D_MODEL), lambda j, i: (i, 0)),
                  pl.BlockSpec((None, D_MODEL, W_IN_SHARD), lambda j, i: (j, 0, 0)),
                  pl.BlockSpec((1, W_IN_SHARD), lambda j, i: (0, j)), ANY],
        out_specs=pl.BlockSpec((tm, W_IN_SHARD), lambda j, i: (i, j)),
        out_shape=jax.ShapeDtypeStruct((n, D_IN), F32),
        compiler_params=_params(("arbitrary", "arbitrary")),
    )(x2, w_full, b_in, after)


def _kv_proj(mem2, wkv_full, b_kv):
    m = mem2.shape[0]

    def body(m_ref, w_ref, b_ref, o_ref):
        o_ref[...] = _dot(m_ref[...].astype(BF16), w_ref[...]) + b_ref[...]

    return pl.pallas_call(
        body, name="kv_proj", grid=(m // MEM_LEN,),
        in_specs=[pl.BlockSpec((MEM_LEN, D_MODEL), lambda i: (i, 0)),
                  pl.BlockSpec((D_MODEL, D_MODEL), lambda i: (0, 0)),
                  pl.BlockSpec((1, D_MODEL), lambda i: (0, 0))],
        out_specs=pl.BlockSpec((MEM_LEN, D_MODEL), lambda i: (i, 0)),
        out_shape=jax.ShapeDtypeStruct((m, D_MODEL), F32),
        compiler_params=_params(("arbitrary",)),
    )(mem2, wkv_full, b_kv)


CONV_ROWS = 16
SUBLANES = 8


def _shifted_planes(buf, tm):
    rows = tm + HALO - SUBLANES
    for s in range(1, SUBLANES):
        buf[s, 0:rows, :] = buf[0, s:s + rows, :]


def _window(buf, start, rows):
    s = start % SUBLANES
    return buf[s, start - s:start - s + rows, :]


def _branch_fwd(h, kv, cw, cb, cg, cbeta, gg_, gb_, ws, bs_t, nb, seq, tm):
    nt = seq // tm
    n = nb * seq

    def body(h_ref, kv_ref, cw_ref, cb_ref, clg_ref, clb_ref, glg_ref, glb_ref, ws_ref, bst_ref,
             y_ref, z_ref, hcbuf):
        i = pl.program_id(1)

        @pl.when(i == 0)
        def _():
            hcbuf[0, 0:HALO, :] = jnp.zeros((HALO, D_CONV), F32)

        @pl.when(i > 0)
        def _():
            hcbuf[0, 0:HALO, :] = hcbuf[0, tm:tm + HALO, :]

        hcbuf[0, HALO:HALO + tm, :] = h_ref[:, C_A:C_A + D_CONV] * _sigmoid(h_ref[:, C_GLU:C_GLU + D_CONV])
        _shifted_planes(hcbuf, tm)
        for r in range(tm // CONV_ROWS):
            base = r * CONV_ROWS
            acc = jnp.broadcast_to(cb_ref[...], (CONV_ROWS, D_CONV))
            for k in range(CONV_WIDTH):
                acc = acc + cw_ref[k:k + 1, :] * _window(hcbuf, base + 2 + k, CONV_ROWS)
            z_ref[base:base + CONV_ROWS, :] = acc
        zhat, _ = _ln_stats(z_ref[...])
        zn = zhat * clg_ref[...] + clb_ref[...]
        cgate = h_ref[:, C_GATE:C_GATE + D_CONV]
        y_ref[:, 0:D_CONV] = (zn * _sigmoid(zn) * (cgate * _sigmoid(cgate))).astype(BF16)

        wsc, _ = _causal_ws(ws_ref)
        vhat, _ = _ln_stats(_gelu(h_ref[:, G_V:G_V + D_GMLP]))
        vn = (vhat * glg_ref[...] + glb_ref[...]).astype(BF16)
        for ch in range(tm // CHUNK):
            rows = slice(ch * CHUNK, (ch + 1) * CHUNK)
            for hd in range(N_GHEADS):
                cols = slice(hd * CHUNK, (hd + 1) * CHUNK)
                s = _dot(wsc[hd], vn[rows, cols]) + bst_ref[:, hd:hd + 1]
                u = _gelu(h_ref[rows, G_U + hd * CHUNK:G_U + (hd + 1) * CHUNK])
                gate = h_ref[rows, G_GATE + hd * CHUNK:G_GATE + (hd + 1) * CHUNK]
                y_ref[rows, D_CONV + hd * CHUNK:D_CONV + (hd + 1) * CHUNK] = (
                    u * s * (gate * _sigmoid(gate))).astype(BF16)

        scale = XHEAD ** -0.5
        for hd in range(N_XHEADS):
            cols = slice(hd * XHEAD, (hd + 1) * XHEAD)
            q = h_ref[:, X_Q + hd * XHEAD:X_Q + (hd + 1) * XHEAD].astype(BF16)
            k = kv_ref[:, hd * XHEAD:(hd + 1) * XHEAD].astype(BF16)
            v = kv_ref[:, D_XATT + hd * XHEAD:D_XATT + (hd + 1) * XHEAD].astype(BF16)
            s = _dot_nt(q, k) * scale
            e = jnp.exp(s - jnp.max(s, axis=-1, keepdims=True))
            p = e / jnp.sum(e, axis=-1, keepdims=True)
            o = _dot(p.astype(BF16), v)
            gate = h_ref[:, X_GATE + hd * XHEAD:X_GATE + (hd + 1) * XHEAD]
            y_ref[:, 2 * D_CONV + hd * XHEAD:2 * D_CONV + (hd + 1) * XHEAD] = (
                o * (gate * _sigmoid(gate))).astype(BF16)

    row = lambda b, i: (b * nt + i, 0)
    const2 = lambda b, i: (0, 0)
    vec = pl.BlockSpec((1, D_CONV), const2)
    return pl.pallas_call(
        body, name="branch_fwd", grid=(nb, nt),
        in_specs=[pl.BlockSpec((tm, D_IN), row),
                  pl.BlockSpec((MEM_LEN, D_MODEL), lambda b, i: (b, 0)),
                  pl.BlockSpec((CONV_WIDTH, D_CONV), const2), vec, vec, vec, vec, vec,
                  pl.BlockSpec((N_GHEADS, CHUNK, CHUNK), lambda b, i: (0, 0, 0)),
                  pl.BlockSpec((CHUNK, N_GHEADS), const2)],
        out_specs=[pl.BlockSpec((tm, D_MIX), row), pl.BlockSpec((tm, D_CONV), row)],
        out_shape=[jax.ShapeDtypeStruct((n, D_MIX), BF16), jax.ShapeDtypeStruct((n, D_CONV), F32)],
        scratch_shapes=[pltpu.VMEM((SUBLANES, HALO + tm, D_CONV), F32)],
        compiler_params=_params(("arbitrary", "arbitrary")),
    )(h, kv, cw, cb, cg, cbeta, gg_, gb_, ws, bs_t)


def _out_proj_loss(ycat, wout_full, b_out, ln_g, ln_b, x2, tgt2, tm):
    n = x2.shape[0]

    def body(y_ref, w_ref, bo_ref, g_ref, b_ref, x_ref, t_ref, dr_ref, dy_ref, vec_ref, loss_ref):
        i = pl.program_id(0)

        @pl.when(i == 0)
        def _():
            vec_ref[...] = jnp.zeros_like(vec_ref)
            loss_ref[...] = jnp.zeros_like(loss_ref)

        r = ALPHA * x_ref[...] + _dot(y_ref[...], w_ref[...]) + bo_ref[...]
        rhat, rstd = _ln_stats(r)
        diff = rhat * g_ref[...] + b_ref[...] - t_ref[...]
        loss_ref[...] += 0.5 * jnp.sum(jnp.mean(diff * diff, axis=-1, keepdims=True), axis=0, keepdims=True)
        dout = diff * (1.0 / D_MODEL)
        dr = _ln_bwd(dout * g_ref[...], rhat, rstd)
        vec_ref[0:1, :] += _colsum(dr)
        vec_ref[1:2, :] += _colsum(dout * rhat)
        vec_ref[2:3, :] += _colsum(dout)
        dr_ref[...] = dr
        dy_ref[...] = _dot_nt(dr.astype(BF16), w_ref[...]).astype(BF16)

    row = lambda i: (i, 0)
    const = lambda i: (0, 0)
    vec = pl.BlockSpec((1, D_MODEL), const)
    return pl.pallas_call(
        body, name="out_proj_loss", grid=(n // tm,),
        in_specs=[pl.BlockSpec((tm, D_MIX), row), pl.BlockSpec((D_MIX, D_MODEL), const), vec, vec, vec,
                  pl.BlockSpec((tm, D_MODEL), row), pl.BlockSpec((tm, D_MODEL), row)],
        out_specs=[pl.BlockSpec((tm, D_MODEL), row), pl.BlockSpec((tm, D_MIX), row),
                   pl.BlockSpec((8, D_MODEL), const), pl.BlockSpec((1, 1), const)],
        out_shape=[jax.ShapeDtypeStruct((n, D_MODEL), F32), jax.ShapeDtypeStruct((n, D_MIX), BF16),
                   jax.ShapeDtypeStruct((8, D_MODEL), F32), jax.ShapeDtypeStruct((1, 1), F32)],
        compiler_params=_params(("arbitrary",)),
    )(ycat, wout_full, b_out, ln_g, ln_b, x2, tgt2)


ROW_CB, ROW_CLG, ROW_CLB, ROW_GLG, ROW_GLB = 32, 33, 34, 35, 36


def _branch_bwd(h, z, dy, kv, cw, clg, clb, glg, glb, ws, bs_t, nb, seq, tm, after):
    nt = seq // tm
    n = nb * seq
    hpt = tm // HALO

    def body(h_ref, ha_ref, hg_ref, z_ref, dy_ref, kv_ref, cw_ref, clg_ref, clb_ref, glg_ref, glb_ref,
             ws_ref, bst_ref, after_ref, dh_ref, gbin_ref, g768_ref, gws_ref, gbst_ref, dkv_ref, gbkv_ref,
             hcbuf, dzbuf, dvnbuf):
        b = pl.program_id(0)
        i = pl.program_id(1)
        ri = nt - 1 - i

        @pl.when((b == 0) & (i == 0))
        def _():
            gbin_ref[...] = jnp.zeros_like(gbin_ref)
            g768_ref[...] = jnp.zeros_like(g768_ref)
            gws_ref[...] = jnp.zeros_like(gws_ref)
            gbst_ref[...] = jnp.zeros_like(gbst_ref)
            gbkv_ref[...] = jnp.zeros_like(gbkv_ref)

        @pl.when(i == 0)
        def _():
            dkv_ref[...] = jnp.zeros_like(dkv_ref)

        def emit(col, width, val):
            gbin_ref[:, col:col + width] += _colsum(val)
            dh_ref[:, col:col + width] = val.astype(BF16)

        d_c = dy_ref[:, 0:D_CONV].astype(F32)
        cgate = h_ref[:, C_GATE:C_GATE + D_CONV]
        sg = _sigmoid(cgate)
        zhat, zrstd = _ln_stats(z_ref[...])
        zn = zhat * clg_ref[...] + clb_ref[...]
        szn = _sigmoid(zn)
        emit(C_GATE, D_CONV, d_c * (zn * szn) * _dsilu(cgate, sg))
        dzn = d_c * (cgate * sg) * _dsilu(zn, szn)
        g768_ref[ROW_CLG:ROW_CLG + 1, :] += _colsum(dzn * zhat)
        g768_ref[ROW_CLB:ROW_CLB + 1, :] += _colsum(dzn)
        dz = _ln_bwd(dzn * clg_ref[...], zhat, zrstd)
        g768_ref[ROW_CB:ROW_CB + 1, :] += _colsum(dz)

        @pl.when(i == 0)
        def _():
            dzbuf[0, tm:tm + HALO, :] = jnp.zeros((HALO, D_CONV), F32)

        @pl.when(i > 0)
        def _():
            dzbuf[0, tm:tm + HALO, :] = dzbuf[0, 0:HALO, :]

        dzbuf[0, 0:tm, :] = dz
        _shifted_planes(dzbuf, tm)
        a = h_ref[:, C_A:C_A + D_CONV]
        sgl = _sigmoid(h_ref[:, C_GLU:C_GLU + D_CONV])
        hcbuf[0, HALO:HALO + tm, :] = a * sgl

        @pl.when(ri == 0)
        def _():
            hcbuf[0, 0:HALO, :] = jnp.zeros((HALO, D_CONV), F32)

        @pl.when(ri > 0)
        def _():
            hcbuf[0, 0:HALO, :] = ha_ref[...] * _sigmoid(hg_ref[...])

        _shifted_planes(hcbuf, tm)

        for r in range(tm // CONV_ROWS):
            base = r * CONV_ROWS
            acc = jnp.zeros((CONV_ROWS, D_CONV), F32)
            for k in range(CONV_WIDTH):
                acc = acc + cw_ref[k:k + 1, :] * _window(dzbuf, base + 30 - k, CONV_ROWS)
            dvnbuf[base:base + CONV_ROWS, :] = acc
        dhc = dvnbuf[...]
        emit(C_A, D_CONV, dhc * sgl)
        emit(C_GLU, D_CONV, dhc * a * sgl * (1.0 - sgl))
        for k in range(CONV_WIDTH):
            g768_ref[k:k + 1, :] += _colsum(dzbuf[0, 0:tm, :] * _window(hcbuf, 2 + k, tm))

        wsc, _ = _causal_ws(ws_ref)
        v, dgelu_v = _gelu_and_grad(h_ref[:, G_V:G_V + D_GMLP])
        vhat, vrstd = _ln_stats(v)
        vn = (vhat * glg_ref[...] + glb_ref[...]).astype(BF16)
        for ch in range(tm // CHUNK):
            rows = slice(ch * CHUNK, (ch + 1) * CHUNK)
            for hd in range(N_GHEADS):
                cols = slice(hd * CHUNK, (hd + 1) * CHUNK)
                vn_blk = vn[rows, cols]
                s = _dot(wsc[hd], vn_blk) + bst_ref[:, hd:hd + 1]
                u, dgelu_u = _gelu_and_grad(h_ref[rows, G_U + hd * CHUNK:G_U + (hd + 1) * CHUNK])
                gate = h_ref[rows, G_GATE + hd * CHUNK:G_GATE + (hd + 1) * CHUNK]
                sgate = _sigmoid(gate)
                d_g = dy_ref[rows, D_CONV + hd * CHUNK:D_CONV + (hd + 1) * CHUNK].astype(F32)
                dgate = d_g * (u * s) * _dsilu(gate, sgate)
                gbin_ref[:, G_GATE + hd * CHUNK:G_GATE + (hd + 1) * CHUNK] += _colsum(dgate)
                dh_ref[rows, G_GATE + hd * CHUNK:G_GATE + (hd + 1) * CHUNK] = dgate.astype(BF16)
                dyg = d_g * (gate * sgate)
                du = dyg * s * dgelu_u
                gbin_ref[:, G_U + hd * CHUNK:G_U + (hd + 1) * CHUNK] += _colsum(du)
                dh_ref[rows, G_U + hd * CHUNK:G_U + (hd + 1) * CHUNK] = du.astype(BF16)
                ds = dyg * u
                dsb = ds.astype(BF16)
                gws_ref[hd] += _dot_nt(dsb, vn_blk)
                gbst_ref[:, hd:hd + 1] += jnp.sum(ds, axis=1, keepdims=True)
                dvnbuf[rows, cols] = _dot_tn(wsc[hd], dsb)
        dvn = dvnbuf[...]
        g768_ref[ROW_GLG:ROW_GLG + 1, :] += _colsum(dvn * vhat)
        g768_ref[ROW_GLB:ROW_GLB + 1, :] += _colsum(dvn)
        emit(G_V, D_GMLP, _ln_bwd(dvn * glg_ref[...], vhat, vrstd) * dgelu_v)

        scale = XHEAD ** -0.5
        for hd in range(N_XHEADS):
            q = h_ref[:, X_Q + hd * XHEAD:X_Q + (hd + 1) * XHEAD].astype(BF16)
            k = kv_ref[:, hd * XHEAD:(hd + 1) * XHEAD].astype(BF16)
            vv = kv_ref[:, D_XATT + hd * XHEAD:D_XATT + (hd + 1) * XHEAD].astype(BF16)
            s = _dot_nt(q, k) * scale
            e = jnp.exp(s - jnp.max(s, axis=-1, keepdims=True))
            p = e / jnp.sum(e, axis=-1, keepdims=True)
            pb = p.astype(BF16)
            o = _dot(pb, vv)
            gate = h_ref[:, X_GATE + hd * XHEAD:X_GATE + (hd + 1) * XHEAD]
            sgate = _sigmoid(gate)
            d_x = dy_ref[:, 2 * D_CONV + hd * XHEAD:2 * D_CONV + (hd + 1) * XHEAD].astype(F32)
            emit(X_GATE + hd * XHEAD, XHEAD, d_x * o * _dsilu(gate, sgate))
            do = (d_x * (gate * sgate)).astype(BF16)
            dp = _dot_nt(do, vv)
            dsc = (p * (dp - jnp.sum(dp * p, axis=-1, keepdims=True))).astype(BF16)
            emit(X_Q + hd * XHEAD, XHEAD, _dot(dsc, k) * scale)
            dkv_ref[:, hd * XHEAD:(hd + 1) * XHEAD] += _dot_tn(dsc, q) * scale
            dkv_ref[:, D_XATT + hd * XHEAD:D_XATT + (hd + 1) * XHEAD] += _dot_tn(pb, do)

        @pl.when(i == nt - 1)
        def _():
            gbkv_ref[...] += _colsum(dkv_ref[...])

        @pl.when((b == nb - 1) & (i == nt - 1))
        def _():
            _, keep = _causal_ws(ws_ref)
            for hd in range(N_GHEADS):
                gws_ref[hd] = jnp.where(keep, gws_ref[hd], 0.0)

    row = lambda b, i: (b * nt + nt - 1 - i, 0)
    const2 = lambda b, i: (0, 0)
    const3 = lambda b, i: (0, 0, 0)
    vec = pl.BlockSpec((1, D_CONV), const2)

    def halo(col_block):
        return pl.BlockSpec((HALO, D_CONV),
                            lambda b, i: (jnp.maximum((b * nt + nt - 1 - i) * hpt - 1, 0), col_block))

    return pl.pallas_call(
        body, name="branch_bwd", grid=(nb, nt),
        in_specs=[pl.BlockSpec((tm, D_IN), row), halo(0), halo(1),
                  pl.BlockSpec((tm, D_CONV), row), pl.BlockSpec((tm, D_MIX), row),
                  pl.BlockSpec((MEM_LEN, D_MODEL), lambda b, i: (b, 0)),
                  pl.BlockSpec((CONV_WIDTH, D_CONV), const2), vec, vec, vec, vec,
                  pl.BlockSpec((N_GHEADS, CHUNK, CHUNK), const3),
                  pl.BlockSpec((CHUNK, N_GHEADS), const2), ANY],
        out_specs=[pl.BlockSpec((tm, D_IN), row),
                   pl.BlockSpec((1, D_IN), const2),
                   pl.BlockSpec((40, D_CONV), const2),
                   pl.BlockSpec((N_GHEADS, CHUNK, CHUNK), const3),
                   pl.BlockSpec((CHUNK, CHUNK), const2),
                   pl.BlockSpec((MEM_LEN, D_MODEL), lambda b, i: (b, 0)),
                   pl.BlockSpec((1, D_MODEL), const2)],
        out_shape=[jax.ShapeDtypeStruct((n, D_IN), BF16),
                   jax.ShapeDtypeStruct((1, D_IN), F32),
                   jax.ShapeDtypeStruct((40, D_CONV), F32),
                   jax.ShapeDtypeStruct((N_GHEADS, CHUNK, CHUNK), F32),
                   jax.ShapeDtypeStruct((CHUNK, CHUNK), F32),
                   jax.ShapeDtypeStruct((nb * MEM_LEN, D_MODEL), F32),
                   jax.ShapeDtypeStruct((1, D_MODEL), F32)],
        scratch_shapes=[pltpu.VMEM((SUBLANES, HALO + tm, D_CONV), F32),
                        pltpu.VMEM((SUBLANES, tm + HALO, D_CONV), F32), pltpu.VMEM((tm, D_CONV), F32)],
        compiler_params=_params(("arbitrary", "arbitrary")),
    )(h, h, h, z, dy, kv, cw, clg, clb, glg, glb, ws, bs_t, after)


def _grad_x(dh, w_full, dr, tm, after):
    n = dh.shape[0]

    def body(dh_ref, w_ref, dr_ref, after_ref, o_ref):
        acc = ALPHA * dr_ref[...]
        for j in range(N_CHIPS):
            acc = acc + _dot_nt(dh_ref[:, j * W_IN_SHARD:(j + 1) * W_IN_SHARD], w_ref[j])
        o_ref[...] = acc

    return pl.pallas_call(
        body, name="grad_x", grid=(n // tm,),
        in_specs=[pl.BlockSpec((tm, D_IN), lambda i: (i, 0)),
                  pl.BlockSpec((N_CHIPS, D_MODEL, W_IN_SHARD), lambda i: (0, 0, 0)),
                  pl.BlockSpec((tm, D_MODEL), lambda i: (i, 0)), ANY],
        out_specs=pl.BlockSpec((tm, D_MODEL), lambda i: (i, 0)),
        out_shape=jax.ShapeDtypeStruct((n, D_MODEL), F32),
        compiler_params=_params(("arbitrary",)),
    )(dh, w_full, dr, after)


def _grad_w(a, b, tk, name, shard_rows, after):
    kdim, m = a.shape
    ncols = b.shape[1]
    nk = kdim // tk
    if shard_rows:
        shard = m // N_CHIPS
        oshape = (shard, ncols)
        a_spec = pl.BlockSpec((tk, shard), lambda j, kk: (kk, j))
        b_spec = pl.BlockSpec((tk, ncols), lambda j, kk: (kk, 0))
    else:
        shard = ncols // N_CHIPS
        oshape = (m, shard)
        a_spec = pl.BlockSpec((tk, m), lambda j, kk: (kk, 0))
        b_spec = pl.BlockSpec((tk, shard), lambda j, kk: (kk, j))

    def body(a_ref, b_ref, after_ref, own_ref, ob_ref, acc):
        j = pl.program_id(0)
        kk = pl.program_id(1)
        mine = 2 * lax.axis_index("x") + lax.axis_index("y")

        @pl.when(kk == 0)
        def _():
            acc[...] = jnp.zeros_like(acc)

        acc[...] += _dot_tn(a_ref[...].astype(BF16), b_ref[...].astype(BF16))

        @pl.when(kk == nk - 1)
        def _():
            ob_ref[...] = acc[...].astype(BF16)

        @pl.when((kk == nk - 1) & (j == mine))
        def _():
            own_ref[...] = acc[...]

    return pl.pallas_call(
        body, name=name, grid=(N_CHIPS, nk),
        in_specs=[a_spec, b_spec, ANY],
        out_specs=[pl.BlockSpec(oshape, lambda j, kk: (0, 0)),
                   pl.BlockSpec((None,) + oshape, lambda j, kk: (j, 0, 0))],
        out_shape=[jax.ShapeDtypeStruct(oshape, F32), jax.ShapeDtypeStruct((N_CHIPS,) + oshape, BF16)],
        scratch_shapes=[pltpu.VMEM(oshape, F32)],
        compiler_params=_params(("arbitrary", "arbitrary")),
    )(a, b, after)


def _sum_small(owns, gots):
    n = len(owns)

    def body(*refs):
        for a in range(n):
            o_ref, g_ref, out_ref = refs[a], refs[n + a], refs[2 * n + a]
            out_ref[...] = (o_ref[...] + g_ref[1]) + (g_ref[0] + g_ref[2])

    return pl.pallas_call(
        body, name="sum_small", out_shape=[jax.ShapeDtypeStruct(o.shape, F32) for o in owns],
        compiler_params=pltpu.CompilerParams(vmem_limit_bytes=VMEM_LIMIT),
    )(*owns, *gots)


def _sum_chips(own, got, tr, name):
    r, ccols = own.shape

    def body(o_ref, g_ref, out_ref):
        out_ref[...] = (o_ref[...] + g_ref[1].astype(F32)) + (g_ref[0].astype(F32) + g_ref[2].astype(F32))

    return pl.pallas_call(
        body, name=name, grid=(r // tr,),
        in_specs=[pl.BlockSpec((tr, ccols), lambda i: (i, 0)), pl.BlockSpec((3, tr, ccols), lambda i: (0, i, 0))],
        out_specs=pl.BlockSpec((tr, ccols), lambda i: (i, 0)),
        out_shape=jax.ShapeDtypeStruct((r, ccols), F32),
        compiler_params=_params(("arbitrary",)),
    )(own, got)


def _exchange_cores(parts):
    npart = len(parts)

    def body(*refs):
        in_refs = refs[0:npart]
        out_refs = refs[npart:2 * npart]
        send_sems, recv_sems = refs[2 * npart:]
        sibling = (lax.axis_index("x"), lax.axis_index("y"), 1 - lax.axis_index("c"))
        copies = [pltpu.make_async_remote_copy(
            src_ref=in_refs[a], dst_ref=out_refs[a], send_sem=send_sems.at[a], recv_sem=recv_sems.at[a],
            device_id=sibling, device_id_type=MESH_ID) for a in range(npart)]
        for cp in copies:
            cp.start()
        for cp in copies:
            cp.wait_recv()
        for cp in copies:
            cp.wait_send()

    return pl.pallas_call(
        body, name="exchange_cores", out_shape=[jax.ShapeDtypeStruct(p.shape, p.dtype) for p in parts],
        in_specs=[ANY] * npart, out_specs=[ANY] * npart,
        scratch_shapes=[pltpu.SemaphoreType.DMA((npart,)), pltpu.SemaphoreType.DMA((npart,))],
    )(*parts)


def _adam(g, w, m, v):
    mn = ADAM_B1 * m + (1.0 - ADAM_B1) * g
    vn = ADAM_B2 * v + (1.0 - ADAM_B2) * (g * g)
    return g, -ADAM_LR * ((mn / BC1) / (jnp.sqrt(vn / BC2) + ADAM_EPS) + ADAM_WD * w), mn, vn


def _adamw(a, b, w, m, v, tr, name):
    r, ccols = w.shape

    def body(a_ref, b_ref, w_ref, m_ref, v_ref, *outs):
        res = _adam(a_ref[...] + b_ref[...], w_ref[...], m_ref[...], v_ref[...])
        for which in range(4):
            outs[which][...] = res[which]

    spec = pl.BlockSpec((tr, ccols), lambda i: (i, 0))
    shape = jax.ShapeDtypeStruct((r, ccols), F32)
    return pl.pallas_call(
        body, name=name, grid=(r // tr,), in_specs=[spec] * 5, out_specs=[spec] * 4, out_shape=[shape] * 4,
        compiler_params=_params(("arbitrary",)),
    )(a, b, w, m, v)


SMALL = ["b_in", "conv_b", "conv_ln_g", "conv_ln_b", "gmlp_ln_g", "gmlp_ln_b", "gmlp_ws", "gmlp_bs", "b_kv", "b_out",
         "ln_g", "ln_b"]


def _adamw_small(a_parts, b_parts, params):
    nparts, nparams = len(a_parts), len(params)

    def body(*refs):
        a = refs[0:nparts]
        b = refs[nparts:2 * nparts]
        prm = refs[2 * nparts:2 * nparts + 3 * nparams]
        outs = refs[2 * nparts + 3 * nparams:]
        gb_in, g768, gws, gbs_t, gb_kv, vec3 = [a[q][...] + b[q][...] for q in range(nparts)]
        grads = [gb_in, g768[ROW_CB:ROW_CB + 1], g768[ROW_CLG:ROW_CLG + 1], g768[ROW_CLB:ROW_CLB + 1],
                 g768[ROW_GLG:ROW_GLG + 1], g768[ROW_GLB:ROW_GLB + 1], gws, jnp.transpose(gbs_t)[0:N_GHEADS, :],
                 gb_kv, vec3[0:1], vec3[1:2], vec3[2:3]]
        for q, g in enumerate(grads):
            res = _adam(g, prm[3 * q][...], prm[3 * q + 1][...], prm[3 * q + 2][...])
            for which in range(4):
                outs[4 * q + which][...] = res[which]

    flat = [t for p in params for t in p]
    out_shape = [jax.ShapeDtypeStruct(p[0].shape, F32) for p in params for _ in range(4)]
    return pl.pallas_call(
        body, name="adamw_small", out_shape=out_shape,
        compiler_params=pltpu.CompilerParams(vmem_limit_bytes=VMEM_LIMIT),
    )(*a_parts, *b_parts, *flat)


def kernel(x, mem, w_in, b_in, conv_w, conv_b, conv_ln_g, conv_ln_b, gmlp_ln_g, gmlp_ln_b, gmlp_ws, gmlp_bs, w_kv, b_kv, w_out, b_out, ln_g, ln_b, loss_target, m_w_in, m_b_in, m_conv_w, m_conv_b, m_conv_ln_g, m_conv_ln_b, m_gmlp_ln_g, m_gmlp_ln_b, m_gmlp_ws, m_gmlp_bs, m_w_kv, m_b_kv, m_w_out, m_b_out, m_ln_g, m_ln_b, v_w_in, v_b_in, v_conv_w, v_conv_b, v_conv_ln_g, v_conv_ln_b, v_gmlp_ln_g, v_gmlp_ln_b, v_gmlp_ws, v_gmlp_bs, v_w_kv, v_b_kv, v_w_out, v_b_out, v_ln_g, v_ln_b):
    weights = dict(b_in=b_in, conv_b=conv_b, conv_ln_g=conv_ln_g, conv_ln_b=conv_ln_b, gmlp_ln_g=gmlp_ln_g,
                   gmlp_ln_b=gmlp_ln_b, gmlp_ws=gmlp_ws, gmlp_bs=gmlp_bs, b_kv=b_kv, b_out=b_out, ln_g=ln_g, ln_b=ln_b)
    mom_m = dict(b_in=m_b_in, conv_b=m_conv_b, conv_ln_g=m_conv_ln_g, conv_ln_b=m_conv_ln_b, gmlp_ln_g=m_gmlp_ln_g,
                 gmlp_ln_b=m_gmlp_ln_b, gmlp_ws=m_gmlp_ws, gmlp_bs=m_gmlp_bs, b_kv=m_b_kv, b_out=m_b_out,
                 ln_g=m_ln_g, ln_b=m_ln_b)
    mom_v = dict(b_in=v_b_in, conv_b=v_conv_b, conv_ln_g=v_conv_ln_g, conv_ln_b=v_conv_ln_b, gmlp_ln_g=v_gmlp_ln_g,
                 gmlp_ln_b=v_gmlp_ln_b, gmlp_ws=v_gmlp_ws, gmlp_bs=v_gmlp_bs, b_kv=v_b_kv, b_out=v_b_out,
                 ln_g=v_ln_g, ln_b=v_ln_b)
    nb, seq, _ = x.shape
    n = nb * seq
    tm = 256
    tk = min(1024, n)
    x2 = x.reshape(n, D_MODEL)
    tgt2 = loss_target.reshape(n, D_MODEL)
    mem2 = mem.reshape(nb * MEM_LEN, D_MODEL)
    chip = 2 * lax.axis_index("x") + lax.axis_index("y")
    bs_t = jnp.transpose(gmlp_bs[0])

    win_g = _gather_shards(w_in[0].astype(BF16), "gather_w_in")
    own_w = [w_out[0].astype(BF16), w_kv[0].astype(BF16), conv_w[0]]
    rest = _start_exchange("gather_rest_start", own_w, [False] * 3)

    h = _in_proj(x2, win_g, b_in, min(512, n), rest["token"])
    wout_g, wkv_g, cw_g = _place_shards(own_w, _wait_exchange("gather_rest_wait", rest, h), "place_shards")
    wout_full = wout_g.reshape(D_MIX, D_MODEL)
    wkv_full = wkv_g.reshape(D_MODEL, D_MODEL)
    cw_full = jnp.transpose(cw_g, (1, 0, 2)).reshape(CONV_WIDTH, D_CONV)
    kv = _kv_proj(mem2, wkv_full, b_kv)
    ycat, z = _branch_fwd(h, kv, cw_full, conv_b, conv_ln_g, conv_ln_b, gmlp_ln_g, gmlp_ln_b, gmlp_ws[0], bs_t,
                          nb, seq, tm)
    dr, dycat, vec3, loss_part = _out_proj_loss(ycat, wout_full, b_out, ln_g, ln_b, x2, tgt2, tm)
    loss = lax.psum(loss_part[0, 0], ("x", "y", "c"))

    own_wout, gwout_b = _grad_w(ycat, dr, tk, "grad_w_out", True, dycat)
    ex1 = _start_exchange("exchange1_start", [gwout_b, vec3], [True, False])
    dh, gb_in, g768, gws, gbs_t, dkv, gb_kv = _branch_bwd(
        h, z, dycat, kv, cw_full, conv_ln_g, conv_ln_b, gmlp_ln_g, gmlp_ln_b, gmlp_ws[0], bs_t, nb, seq, tm,
        ex1["token"])
    own_wkv, gwkv_b = _grad_w(mem2, dkv, nb * MEM_LEN, "grad_w_kv", True, dkv)
    small2 = [gb_in, g768, gws, gbs_t, gb_kv]
    ex2 = _start_exchange("exchange2_start", [gwkv_b] + small2, [True] + [False] * 5)
    own_win, gwin_b = _grad_w(x2, dh, tk, "grad_w_in", False, ex2["token"])
    ex3 = _start_exchange("exchange3_start", [gwin_b], [True])
    grad_x2 = _grad_x(dh, win_g, dr, tm, ex3["token"])
    got_wout, got_vec3 = _wait_exchange("exchange1_wait", ex1, grad_x2)
    got2 = _wait_exchange("exchange2_wait", ex2, grad_x2)
    (got_win,) = _wait_exchange("exchange3_wait", ex3, grad_x2)

    sum_win = _sum_chips(own_win, got_win, 256, "sum_w_in")
    sum_wout = _sum_chips(own_wout, got_wout, 256, "sum_w_out")
    sum_wkv = _sum_chips(own_wkv, got2[0], 256, "sum_w_kv")
    sum_small = list(_sum_small(small2 + [vec3], got2[1:] + [got_vec3]))
    sib = list(_exchange_cores([sum_win, sum_wout, sum_wkv] + sum_small))

    big = {}
    big["w_in"] = _adamw(sum_win, sib[0], w_in[0], m_w_in[0], v_w_in[0], 256, "adamw_w_in")
    big["w_out"] = _adamw(sum_wout, sib[1], w_out[0], m_w_out[0], v_w_out[0], 256, "adamw_w_out")
    big["w_kv"] = _adamw(sum_wkv, sib[2], w_kv[0], m_w_kv[0], v_w_kv[0], 256, "adamw_w_kv")
    lead = {"gmlp_ws", "gmlp_bs"}
    strip = lambda k, t: t[0] if k in lead else t
    sm = _adamw_small(sum_small, sib[3:], [tuple(strip(k, t[k]) for t in (weights, mom_m, mom_v)) for k in SMALL])
    small_out = {k: [sm[4 * q + which][None] if k in lead else sm[4 * q + which] for which in range(4)]
                 for q, k in enumerate(SMALL)}
    cw_a = lax.dynamic_slice_in_dim(sum_small[1][0:CONV_WIDTH + 1], chip * CONV_SHARD, CONV_SHARD, axis=1)
    cw_b = lax.dynamic_slice_in_dim(sib[4][0:CONV_WIDTH + 1], chip * CONV_SHARD, CONV_SHARD, axis=1)
    cwp = ((0, 1), (0, 0))
    cw_out = _adamw(cw_a, cw_b, jnp.pad(conv_w[0], cwp), jnp.pad(m_conv_w[0], cwp), jnp.pad(v_conv_w[0], cwp),
                    CONV_WIDTH + 1, "adamw_conv_w")

    order = ["w_in", "b_in", "conv_w", "conv_b", "conv_ln_g", "conv_ln_b", "gmlp_ln_g", "gmlp_ln_b", "gmlp_ws",
             "gmlp_bs", "w_kv", "b_kv", "w_out", "b_out", "ln_g", "ln_b"]
    result = [loss, grad_x2.reshape(nb, seq, D_MODEL)]
    for which in range(4):
        for k in order:
            if k in big:
                result.append(big[k][which][None])
            elif k == "conv_w":
                result.append(cw_out[which][0:CONV_WIDTH][None])
            else:
                result.append(small_out[k][which])
    return tuple(result)
```

```python
import functools
import math

import jax
import jax.numpy as jnp
from jax import lax
from jax.experimental import pallas as pl
from jax.experimental.pallas import tpu as pltpu

F32 = jnp.float32
BF16 = jnp.bfloat16

D_MODEL = 1024
MEM_LEN = 256
D_MIX = 2048
D_CONV = 768
D_GMLP = 768
D_XATT = 512
N_XHEADS = 4
XHEAD = 128
CONV_WIDTH = 31
CHUNK = 128
N_GHEADS = 6
D_IN = 3 * D_CONV + 3 * D_GMLP + 2 * D_XATT
ALPHA = 2.0 ** 0.25
LN_EPS = 1e-5
N_CHIPS = 4
W_IN_SHARD = D_IN // N_CHIPS
W_OUT_SHARD = D_MIX // N_CHIPS
W_KV_SHARD = D_MODEL // N_CHIPS
CONV_SHARD = D_CONV // N_CHIPS
HALO = 32

C_A, C_GLU, C_GATE = 0, 768, 1536
G_U, G_V, G_GATE = 2304, 3072, 3840
X_Q, X_GATE = 4608, 5120

ADAM_LR = 0.001
ADAM_B1 = 0.9
ADAM_B2 = 0.999
ADAM_EPS = 1e-08
ADAM_WD = 0.01
ADAM_STEP = 10
BC1 = 1.0 - ADAM_B1 ** ADAM_STEP
BC2 = 1.0 - ADAM_B2 ** ADAM_STEP

VMEM_LIMIT = 56 * 1024 * 1024
MESH_ID = pl.DeviceIdType.MESH
ANY = pl.BlockSpec(memory_space=pl.ANY)

GELU_C0 = math.sqrt(2.0 / math.pi)
GELU_C1 = 0.044715


def _sigmoid(v):
    return 1.0 / (1.0 + jnp.exp(-v))


def _dsilu(v, s):
    return s * (1.0 + v * (1.0 - s))


def _gelu_and_grad(v):
    t = jnp.tanh(GELU_C0 * (v + GELU_C1 * v * v * v))
    g = 0.5 * v * (1.0 + t)
    dg = 0.5 * (1.0 + t) + 0.5 * v * (1.0 - t * t) * (GELU_C0 * (1.0 + 3.0 * GELU_C1 * v * v))
    return g, dg


def _gelu(v):
    return 0.5 * v * (1.0 + jnp.tanh(GELU_C0 * (v + GELU_C1 * v * v * v)))


def _ln_stats(v):
    mu = jnp.mean(v, axis=-1, keepdims=True)
    vc = v - mu
    var = jnp.mean(vc * vc, axis=-1, keepdims=True)
    rstd = lax.rsqrt(var + LN_EPS)
    return vc * rstd, rstd


def _ln_bwd(dvhat, vhat, rstd):
    m1 = jnp.mean(dvhat, axis=-1, keepdims=True)
    m2 = jnp.mean(dvhat * vhat, axis=-1, keepdims=True)
    return rstd * (dvhat - m1 - vhat * m2)


def _colsum(v):
    return jnp.sum(v, axis=0, keepdims=True)


def _dot(a, b):
    return jnp.dot(a, b, preferred_element_type=F32)


def _dot_nt(a, b):
    return lax.dot_general(a, b, (((1,), (1,)), ((), ())), preferred_element_type=F32)


def _dot_tn(a, b):
    return lax.dot_general(a, b, (((0,), (0,)), ((), ())), preferred_element_type=F32)


def _causal_ws(ws_ref):
    row = lax.broadcasted_iota(jnp.int32, (CHUNK, CHUNK), 0)
    col = lax.broadcasted_iota(jnp.int32, (CHUNK, CHUNK), 1)
    keep = col <= row
    return [jnp.where(keep, ws_ref[hd], 0.0).astype(BF16) for hd in range(N_GHEADS)], keep


def _params(sem):
    return pltpu.CompilerParams(dimension_semantics=sem, vmem_limit_bytes=VMEM_LIMIT)


def _peer_chips():
    x, y, c = lax.axis_index("x"), lax.axis_index("y"), lax.axis_index("c")
    return [(1 - x, y), (x, 1 - y), (1 - x, 1 - y)], 2 * x + y, c


def _gather_shards(shard, name):
    def body(src_ref, out_ref, send_sems, recv_sems, loc_sem):
        chips, mine, c = _peer_chips()
        local = pltpu.make_async_copy(src_ref, out_ref.at[mine], loc_sem)
        local.start()

        def remote(p, slot):
            px, py = chips[p]
            return pltpu.make_async_remote_copy(
                src_ref=src_ref, dst_ref=out_ref.at[slot], send_sem=send_sems.at[p], recv_sem=recv_sems.at[p],
                device_id=(px, py, c), device_id_type=MESH_ID)

        sends = [remote(p, mine) for p in range(3)]
        for cp in sends:
            cp.start()
        for p in range(3):
            px, py = chips[p]
            remote(p, 2 * px + py).wait_recv()
        for cp in sends:
            cp.wait_send()
        local.wait()

    return pl.pallas_call(
        body, name=name, out_shape=jax.ShapeDtypeStruct((N_CHIPS,) + shard.shape, shard.dtype),
        in_specs=[ANY], out_specs=ANY,
        scratch_shapes=[pltpu.SemaphoreType.DMA((3,)), pltpu.SemaphoreType.DMA((3,)), pltpu.SemaphoreType.DMA],
    )(shard)


HBM = pl.BlockSpec(memory_space=pltpu.HBM)
SEM = pl.BlockSpec(memory_space=pltpu.SEMAPHORE)
EFFECT = pltpu.SideEffectType.DATAFLOW_SIDE_EFFECTING


def _exchange_copies(src_refs, land_refs, per_chip, send_sems, recv_sems):
    chips, _, c = _peer_chips()
    n = len(src_refs)
    copies = []
    for p, (px, py) in enumerate(chips):
        for a in range(n):
            src = src_refs[a].at[2 * px + py] if per_chip[a] else src_refs[a]
            copies.append(pltpu.make_async_remote_copy(
                src_ref=src, dst_ref=land_refs[a].at[p], send_sem=send_sems.at[n * p + a],
                recv_sem=recv_sems.at[n * p + a], device_id=(px, py, c), device_id_type=MESH_ID))
    return copies


def _start_exchange(name, srcs, per_chip):
    n = len(srcs)
    lands = [lax.empty((3,) + (s.shape[1:] if pc else s.shape), s.dtype) for s, pc in zip(srcs, per_chip)]

    def body(*refs):
        src_refs, land_refs = refs[0:n], refs[n:2 * n]
        send_sems, recv_sems = refs[2 * n], refs[2 * n + 1]
        token = refs[4 * n + 2]
        for cp in _exchange_copies(src_refs, land_refs, per_chip, send_sems, recv_sems):
            cp.start()
        token[...] = jnp.zeros_like(token)

    out = pl.pallas_call(
        body, name=name,
        out_shape=(pltpu.SemaphoreType.DMA((3 * n,)), pltpu.SemaphoreType.DMA((3 * n,)),
                   *[pltpu.HBM(a.shape, a.dtype) for a in srcs + lands], jax.ShapeDtypeStruct((8, 128), F32)),
        in_specs=[HBM] * (2 * n),
        out_specs=(SEM, SEM, *[HBM] * (2 * n), pl.BlockSpec(memory_space=pltpu.VMEM)),
        input_output_aliases={a: 2 + a for a in range(2 * n)},
        compiler_params=pltpu.CompilerParams(has_side_effects=EFFECT),
    )(*[pltpu.with_memory_space_constraint(a, pltpu.HBM) for a in srcs + lands])
    return dict(send=out[0], recv=out[1], thru=list(out[2:2 * n + 2]), token=out[2 * n + 2], per_chip=per_chip)


def _wait_exchange(name, started, after):
    thru, per_chip = started["thru"], started["per_chip"]
    n = len(thru) // 2

    def body(*refs):
        src_refs, land_refs = refs[0:n], refs[n:2 * n]
        send_sems, recv_sems = refs[2 * n], refs[2 * n + 1]
        for cp in _exchange_copies(src_refs, land_refs, per_chip, send_sems, recv_sems):
            cp.wait_send()
            cp.wait_recv()

    out = pl.pallas_call(
        body, name=name, out_shape=tuple(pltpu.HBM(a.shape, a.dtype) for a in thru),
        in_specs=[HBM] * (2 * n) + [SEM, SEM, ANY], out_specs=tuple([HBM] * (2 * n)),
        input_output_aliases={a: a for a in range(2 * n)},
        compiler_params=pltpu.CompilerParams(has_side_effects=EFFECT),
    )(*thru, started["send"], started["recv"], after)
    return list(out[n:2 * n])


def _place_shards(owns, landeds, name):
    n = len(owns)
    mine = (2 * lax.axis_index("x") + lax.axis_index("y")).astype(jnp.int32).reshape(1)

    def body(mine_ref, *refs):
        own_refs, land_refs, out_refs = refs[0:n], refs[n:2 * n], refs[2 * n:3 * n]
        k = pl.program_id(0)
        for a in range(n):
            @pl.when(k == mine_ref[0])
            def _():
                out_refs[a][...] = own_refs[a][...]

            @pl.when(k != mine_ref[0])
            def _():
                out_refs[a][...] = land_refs[a][...]

    def slot(k, mine_ref):
        d = k ^ mine_ref[0]
        return jnp.where(d == 1, 1, jnp.where(d == 3, 2, 0))

    zeros = lambda o: (0,) * len(o.shape)
    grid_spec = pltpu.PrefetchScalarGridSpec(
        num_scalar_prefetch=1, grid=(N_CHIPS,),
        in_specs=[pl.BlockSpec(o.shape, lambda k, m, o=o: zeros(o)) for o in owns]
        + [pl.BlockSpec((None,) + o.shape, lambda k, m, o=o: (slot(k, m),) + zeros(o)) for o in owns],
        out_specs=[pl.BlockSpec((None,) + o.shape, lambda k, m, o=o: (k,) + zeros(o)) for o in owns])
    return pl.pallas_call(
        body, name=name, grid_spec=grid_spec,
        out_shape=[jax.ShapeDtypeStruct((N_CHIPS,) + o.shape, o.dtype) for o in owns],
        compiler_params=_params(("arbitrary",)),
    )(mine, *owns, *landeds)


def _in_proj(x2, w_full, b_in, tm, after):
    n = x2.shape[0]

    def body(x_ref, w_ref, b_ref, after_ref, h_ref):
        h_ref[...] = _dot(x_ref[...].astype(BF16), w_ref[...]) + b_ref[...]

    return pl.pallas_call(
        body, name="in_proj", grid=(N_CHIPS, n // tm),
        in_specs=[pl.BlockSpec((tm, D_MODEL), lambda j, i: (i, 0)),
                  pl.BlockSpec((None, D_MODEL, W_IN_SHARD), lambda j, i: (j, 0, 0)),
                  pl.BlockSpec((1, W_IN_SHARD), lambda j, i: (0, j)), ANY],
        out_specs=pl.BlockSpec((tm, W_IN_SHARD), lambda j, i: (i, j)),
        out_shape=jax.ShapeDtypeStruct((n, D_IN), F32),
        compiler_params=_params(("arbitrary", "arbitrary")),
    )(x2, w_full, b_in, after)


def _kv_proj(mem2, wkv_full, b_kv):
    m = mem2.shape[0]

    def body(m_ref, w_ref, b_ref, o_ref):
        o_ref[...] = _dot(m_ref[...].astype(BF16), w_ref[...]) + b_ref[...]

    return pl.pallas_call(
        body, name="kv_proj", grid=(m // MEM_LEN,),
        in_specs=[pl.BlockSpec((MEM_LEN, D_MODEL), lambda i: (i, 0)),
                  pl.BlockSpec((D_MODEL, D_MODEL), lambda i: (0, 0)),
                  pl.BlockSpec((1, D_MODEL), lambda i: (0, 0))],
        out_specs=pl.BlockSpec((MEM_LEN, D_MODEL), lambda i: (i, 0)),
        out_shape=jax.ShapeDtypeStruct((m, D_MODEL), F32),
        compiler_params=_params(("arbitrary",)),
    )(mem2, wkv_full, b_kv)


CONV_ROWS = 16
SUBLANES = 8


def _shifted_planes(buf, tm):
    rows = tm + HALO - SUBLANES
    for s in range(1, SUBLANES):
        buf[s, 0:rows, :] = buf[0, s:s + rows, :]


def _window(buf, start, rows):
    s = start % SUBLANES
    return buf[s, start - s:start - s + rows, :]


def _branch_fwd(h, kv, cw, cb, cg, cbeta, gg_, gb_, ws, bs_t, nb, seq, tm):
    nt = seq // tm
    n = nb * seq

    def body(h_ref, kv_ref, cw_ref, cb_ref, clg_ref, clb_ref, glg_ref, glb_ref, ws_ref, bst_ref,
             y_ref, z_ref, hcbuf):
        i = pl.program_id(1)

        @pl.when(i == 0)
        def _():
            hcbuf[0, 0:HALO, :] = jnp.zeros((HALO, D_CONV), F32)

        @pl.when(i > 0)
        def _():
            hcbuf[0, 0:HALO, :] = hcbuf[0, tm:tm + HALO, :]

        hcbuf[0, HALO:HALO + tm, :] = h_ref[:, C_A:C_A + D_CONV] * _sigmoid(h_ref[:, C_GLU:C_GLU + D_CONV])
        _shifted_planes(hcbuf, tm)
        for r in range(tm // CONV_ROWS):
            base = r * CONV_ROWS
            acc = jnp.broadcast_to(cb_ref[...], (CONV_ROWS, D_CONV))
            for k in range(CONV_WIDTH):
                acc = acc + cw_ref[k:k + 1, :] * _window(hcbuf, base + 2 + k, CONV_ROWS)
            z_ref[base:base + CONV_ROWS, :] = acc
        zhat, _ = _ln_stats(z_ref[...])
        zn = zhat * clg_ref[...] + clb_ref[...]
        cgate = h_ref[:, C_GATE:C_GATE + D_CONV]
        y_ref[:, 0:D_CONV] = (zn * _sigmoid(zn) * (cgate * _sigmoid(cgate))).astype(BF16)

        wsc, _ = _causal_ws(ws_ref)
        vhat, _ = _ln_stats(_gelu(h_ref[:, G_V:G_V + D_GMLP]))
        vn = (vhat * glg_ref[...] + glb_ref[...]).astype(BF16)
        for ch in range(tm // CHUNK):
            rows = slice(ch * CHUNK, (ch + 1) * CHUNK)
            for hd in range(N_GHEADS):
                cols = slice(hd * CHUNK, (hd + 1) * CHUNK)
                s = _dot(wsc[hd], vn[rows, cols]) + bst_ref[:, hd:hd + 1]
                u = _gelu(h_ref[rows, G_U + hd * CHUNK:G_U + (hd + 1) * CHUNK])
                gate = h_ref[rows, G_GATE + hd * CHUNK:G_GATE + (hd + 1) * CHUNK]
                y_ref[rows, D_CONV + hd * CHUNK:D_CONV + (hd + 1) * CHUNK] = (
                    u * s * (gate * _sigmoid(gate))).astype(BF16)

        scale = XHEAD ** -0.5
        for hd in range(N_XHEADS):
            cols = slice(hd * XHEAD, (hd + 1) * XHEAD)
            q = h_ref[:, X_Q + hd * XHEAD:X_Q + (hd + 1) * XHEAD].astype(BF16)
            k = kv_ref[:, hd * XHEAD:(hd + 1) * XHEAD].astype(BF16)
            v = kv_ref[:, D_XATT + hd * XHEAD:D_XATT + (hd + 1) * XHEAD].astype(BF16)
            s = _dot_nt(q, k) * scale
            e = jnp.exp(s - jnp.max(s, axis=-1, keepdims=True))
            p = e / jnp.sum(e, axis=-1, keepdims=True)
            o = _dot(p.astype(BF16), v)
            gate = h_ref[:, X_GATE + hd * XHEAD:X_GATE + (hd + 1) * XHEAD]
            y_ref[:, 2 * D_CONV + hd * XHEAD:2 * D_CONV + (hd + 1) * XHEAD] = (
                o * (gate * _sigmoid(gate))).astype(BF16)

    row = lambda b, i: (b * nt + i, 0)
    const2 = lambda b, i: (0, 0)
    vec = pl.BlockSpec((1, D_CONV), const2)
    return pl.pallas_call(
        body, name="branch_fwd", grid=(nb, nt),
        in_specs=[pl.BlockSpec((tm, D_IN), row),
                  pl.BlockSpec((MEM_LEN, D_MODEL), lambda b, i: (b, 0)),
                  pl.BlockSpec((CONV_WIDTH, D_CONV), const2), vec, vec, vec, vec, vec,
                  pl.BlockSpec((N_GHEADS, CHUNK, CHUNK), lambda b, i: (0, 0, 0)),
                  pl.BlockSpec((CHUNK, N_GHEADS), const2)],
        out_specs=[pl.BlockSpec((tm, D_MIX), row), pl.BlockSpec((tm, D_CONV), row)],
        out_shape=[jax.ShapeDtypeStruct((n, D_MIX), BF16), jax.ShapeDtypeStruct((n, D_CONV), F32)],
        scratch_shapes=[pltpu.VMEM((SUBLANES, HALO + tm, D_CONV), F32)],
        compiler_params=_params(("arbitrary", "arbitrary")),
    )(h, kv, cw, cb, cg, cbeta, gg_, gb_, ws, bs_t)


def _out_proj_loss(ycat, wout_full, b_out, ln_g, ln_b, x2, tgt2, tm):
    n = x2.shape[0]

    def body(y_ref, w_ref, bo_ref, g_ref, b_ref, x_ref, t_ref, dr_ref, dy_ref, vec_ref, loss_ref):
        i = pl.program_id(0)

        @pl.when(i == 0)
        def _():
            vec_ref[...] = jnp.zeros_like(vec_ref)
            loss_ref[...] = jnp.zeros_like(loss_ref)

        r = ALPHA * x_ref[...] + _dot(y_ref[...], w_ref[...]) + bo_ref[...]
        rhat, rstd = _ln_stats(r)
        diff = rhat * g_ref[...] + b_ref[...] - t_ref[...]
        loss_ref[...] += 0.5 * jnp.sum(jnp.mean(diff * diff, axis=-1, keepdims=True), axis=0, keepdims=True)
        dout = diff * (1.0 / D_MODEL)
        dr = _ln_bwd(dout * g_ref[...], rhat, rstd)
        vec_ref[0:1, :] += _colsum(dr)
        vec_ref[1:2, :] += _colsum(dout * rhat)
        vec_ref[2:3, :] += _colsum(dout)
        dr_ref[...] = dr
        dy_ref[...] = _dot_nt(dr.astype(BF16), w_ref[...]).astype(BF16)

    row = lambda i: (i, 0)
    const = lambda i: (0, 0)
    vec = pl.BlockSpec((1, D_MODEL), const)
    return pl.pallas_call(
        body, name="out_proj_loss", grid=(n // tm,),
        in_specs=[pl.BlockSpec((tm, D_MIX), row), pl.BlockSpec((D_MIX, D_MODEL), const), vec, vec, vec,
                  pl.BlockSpec((tm, D_MODEL), row), pl.BlockSpec((tm, D_MODEL), row)],
        out_specs=[pl.BlockSpec((tm, D_MODEL), row), pl.BlockSpec((tm, D_MIX), row),
                   pl.BlockSpec((8, D_MODEL), const), pl.BlockSpec((1, 1), const)],
        out_shape=[jax.ShapeDtypeStruct((n, D_MODEL), F32), jax.ShapeDtypeStruct((n, D_MIX), BF16),
                   jax.ShapeDtypeStruct((8, D_MODEL), F32), jax.ShapeDtypeStruct((1, 1), F32)],
        compiler_params=_params(("arbitrary",)),
    )(ycat, wout_full, b_out, ln_g, ln_b, x2, tgt2)


ROW_CB, ROW_CLG, ROW_CLB, ROW_GLG, ROW_GLB = 32, 33, 34, 35, 36


def _branch_bwd(h, z, dy, kv, cw, clg, clb, glg, glb, ws, bs_t, nb, seq, tm, after):
    nt = seq // tm
    n = nb * seq
    hpt = tm // HALO

    def body(h_ref, ha_ref, hg_ref, z_ref, dy_ref, kv_ref, cw_ref, clg_ref, clb_ref, glg_ref, glb_ref,
             ws_ref, bst_ref, after_ref, dh_ref, gbin_ref, g768_ref, gws_ref, gbst_ref, dkv_ref, gbkv_ref,
             hcbuf, dzbuf, dvnbuf):
        b = pl.program_id(0)
        i = pl.program_id(1)
        ri = nt - 1 - i

        @pl.when((b == 0) & (i == 0))
        def _():
            gbin_ref[...] = jnp.zeros_like(gbin_ref)
            g768_ref[...] = jnp.zeros_like(g768_ref)
            gws_ref[...] = jnp.zeros_like(gws_ref)
            gbst_ref[...] = jnp.zeros_like(gbst_ref)
            gbkv_ref[...] = jnp.zeros_like(gbkv_ref)

        @pl.when(i == 0)
        def _():
            dkv_ref[...] = jnp.zeros_like(dkv_ref)

        def emit(col, width, val):
            gbin_ref[:, col:col + width] += _colsum(val)
            dh_ref[:, col:col + width] = val.astype(BF16)

        d_c = dy_ref[:, 0:D_CONV].astype(F32)
        cgate = h_ref[:, C_GATE:C_GATE + D_CONV]
        sg = _sigmoid(cgate)
        zhat, zrstd = _ln_stats(z_ref[...])
        zn = zhat * clg_ref[...] + clb_ref[...]
        szn = _sigmoid(zn)
        emit(C_GATE, D_CONV, d_c * (zn * szn) * _dsilu(cgate, sg))
        dzn = d_c * (cgate * sg) * _dsilu(zn, szn)
        g768_ref[ROW_CLG:ROW_CLG + 1, :] += _colsum(dzn * zhat)
        g768_ref[ROW_CLB:ROW_CLB + 1, :] += _colsum(dzn)
        dz = _ln_bwd(dzn * clg_ref[...], zhat, zrstd)
        g768_ref[ROW_CB:ROW_CB + 1, :] += _colsum(dz)

        @pl.when(i == 0)
        def _():
            dzbuf[0, tm:tm + HALO, :] = jnp.zeros((HALO, D_CONV), F32)

        @pl.when(i > 0)
        def _():
            dzbuf[0, tm:tm + HALO, :] = dzbuf[0, 0:HALO, :]

        dzbuf[0, 0:tm, :] = dz
        _shifted_planes(dzbuf, tm)
        a = h_ref[:, C_A:C_A + D_CONV]
        sgl = _sigmoid(h_ref[:, C_GLU:C_GLU + D_CONV])
        hcbuf[0, HALO:HALO + tm, :] = a * sgl

        @pl.when(ri == 0)
        def _():
            hcbuf[0, 0:HALO, :] = jnp.zeros((HALO, D_CONV), F32)

        @pl.when(ri > 0)
        def _():
            hcbuf[0, 0:HALO, :] = ha_ref[...] * _sigmoid(hg_ref[...])

        _shifted_planes(hcbuf, tm)

        for r in range(tm // CONV_ROWS):
            base = r * CONV_ROWS
            acc = jnp.zeros((CONV_ROWS, D_CONV), F32)
            for k in range(CONV_WIDTH):
                acc = acc + cw_ref[k:k + 1, :] * _window(dzbuf, base + 30 - k, CONV_ROWS)
            dvnbuf[base:base + CONV_ROWS, :] = acc
        dhc = dvnbuf[...]
        emit(C_A, D_CONV, dhc * sgl)
        emit(C_GLU, D_CONV, dhc * a * sgl * (1.0 - sgl))
        for k in range(CONV_WIDTH):
            g768_ref[k:k + 1, :] += _colsum(dzbuf[0, 0:tm, :] * _window(hcbuf, 2 + k, tm))

        wsc, _ = _causal_ws(ws_ref)
        v, dgelu_v = _gelu_and_grad(h_ref[:, G_V:G_V + D_GMLP])
        vhat, vrstd = _ln_stats(v)
        vn = (vhat * glg_ref[...] + glb_ref[...]).astype(BF16)
        for ch in range(tm // CHUNK):
            rows = slice(ch * CHUNK, (ch + 1) * CHUNK)
            for hd in range(N_GHEADS):
                cols = slice(hd * CHUNK, (hd + 1) * CHUNK)
                vn_blk = vn[rows, cols]
                s = _dot(wsc[hd], vn_blk) + bst_ref[:, hd:hd + 1]
                u, dgelu_u = _gelu_and_grad(h_ref[rows, G_U + hd * CHUNK:G_U + (hd + 1) * CHUNK])
                gate = h_ref[rows, G_GATE + hd * CHUNK:G_GATE + (hd + 1) * CHUNK]
                sgate = _sigmoid(gate)
                d_g = dy_ref[rows, D_CONV + hd * CHUNK:D_CONV + (hd + 1) * CHUNK].astype(F32)
                dgate = d_g * (u * s) * _dsilu(gate, sgate)
                gbin_ref[:, G_GATE + hd * CHUNK:G_GATE + (hd + 1) * CHUNK] += _colsum(dgate)
                dh_ref[rows, G_GATE + hd * CHUNK:G_GATE + (hd + 1) * CHUNK] = dgate.astype(BF16)
                dyg = d_g * (gate * sgate)
                du = dyg * s * dgelu_u
                gbin_ref[:, G_U + hd * CHUNK:G_U + (hd + 1) * CHUNK] += _colsum(du)
                dh_ref[rows, G_U + hd * CHUNK:G_U + (hd + 1) * CHUNK] = du.astype(BF16)
                ds = dyg * u
                dsb = ds.astype(BF16)
                gws_ref[hd] += _dot_nt(dsb, vn_blk)
                gbst_ref[:, hd:hd + 1] += jnp.sum(ds, axis=1, keepdims=True)
                dvnbuf[rows, cols] = _dot_tn(wsc[hd], dsb)
        dvn = dvnbuf[...]
        g768_ref[ROW_GLG:ROW_GLG + 1, :] += _colsum(dvn * vhat)
        g768_ref[ROW_GLB:ROW_GLB + 1, :] += _colsum(dvn)
        emit(G_V, D_GMLP, _ln_bwd(dvn * glg_ref[...], vhat, vrstd) * dgelu_v)

        scale = XHEAD ** -0.5
        for hd in range(N_XHEADS):
            q = h_ref[:, X_Q + hd * XHEAD:X_Q + (hd + 1) * XHEAD].astype(BF16)
            k = kv_ref[:, hd * XHEAD:(hd + 1) * XHEAD].astype(BF16)
            vv = kv_ref[:, D_XATT + hd * XHEAD:D_XATT + (hd + 1) * XHEAD].astype(BF16)
            s = _dot_nt(q, k) * scale
            e = jnp.exp(s - jnp.max(s, axis=-1, keepdims=True))
            p = e / jnp.sum(e, axis=-1, keepdims=True)
            pb = p.astype(BF16)
            o = _dot(pb, vv)
            gate = h_ref[:, X_GATE + hd * XHEAD:X_GATE + (hd + 1) * XHEAD]
            sgate = _sigmoid(gate)
            d_x = dy_ref[:, 2 * D_CONV + hd * XHEAD:2 * D_CONV + (hd + 1) * XHEAD].astype(F32)
            emit(X_GATE + hd * XHEAD, XHEAD, d_x * o * _dsilu(gate, sgate))
            do = (d_x * (gate * sgate)).astype(BF16)
            dp = _dot_nt(do, vv)
            dsc = (p * (dp - jnp.sum(dp * p, axis=-1, keepdims=True))).astype(BF16)
            emit(X_Q + hd * XHEAD, XHEAD, _dot(dsc, k) * scale)
            dkv_ref[:, hd * XHEAD:(hd + 1) * XHEAD] += _dot_tn(dsc, q) * scale
            dkv_ref[:, D_XATT + hd * XHEAD:D_XATT + (hd + 1) * XHEAD] += _dot_tn(pb, do)

        @pl.when(i == nt - 1)
        def _():
            gbkv_ref[...] += _colsum(dkv_ref[...])

        @pl.when((b == nb - 1) & (i == nt - 1))
        def _():
            _, keep = _causal_ws(ws_ref)
            for hd in range(N_GHEADS):
                gws_ref[hd] = jnp.where(keep, gws_ref[hd], 0.0)

    row = lambda b, i: (b * nt + nt - 1 - i, 0)
    const2 = lambda b, i: (0, 0)
    const3 = lambda b, i: (0, 0, 0)
    vec = pl.BlockSpec((1, D_CONV), const2)

    def halo(col_block):
        return pl.BlockSpec((HALO, D_CONV),
                            lambda b, i: (jnp.maximum((b * nt + nt - 1 - i) * hpt - 1, 0), col_block))

    return pl.pallas_call(
        body, name="branch_bwd", grid=(nb, nt),
        in_specs=[pl.BlockSpec((tm, D_IN), row), halo(0), halo(1),
                  pl.BlockSpec((tm, D_CONV), row), pl.BlockSpec((tm, D_MIX), row),
                  pl.BlockSpec((MEM_LEN, D_MODEL), lambda b, i: (b, 0)),
                  pl.BlockSpec((CONV_WIDTH, D_CONV), const2), vec, vec, vec, vec,
                  pl.BlockSpec((N_GHEADS, CHUNK, CHUNK), const3),
                  pl.BlockSpec((CHUNK, N_GHEADS), const2), ANY],
        out_specs=[pl.BlockSpec((tm, D_IN), row),
                   pl.BlockSpec((1, D_IN), const2),
                   pl.BlockSpec((40, D_CONV), const2),
                   pl.BlockSpec((N_GHEADS, CHUNK, CHUNK), const3),
                   pl.BlockSpec((CHUNK, CHUNK), const2),
                   pl.BlockSpec((MEM_LEN, D_MODEL), lambda b, i: (b, 0)),
                   pl.BlockSpec((1, D_MODEL), const2)],
        out_shape=[jax.ShapeDtypeStruct((n, D_IN), BF16),
                   jax.ShapeDtypeStruct((1, D_IN), F32),
                   jax.ShapeDtypeStruct((40, D_CONV), F32),
                   jax.ShapeDtypeStruct((N_GHEADS, CHUNK, CHUNK), F32),
                   jax.ShapeDtypeStruct((CHUNK, CHUNK), F32),
                   jax.ShapeDtypeStruct((nb * MEM_LEN, D_MODEL), F32),
                   jax.ShapeDtypeStruct((1, D_MODEL), F32)],
        scratch_shapes=[pltpu.VMEM((SUBLANES, HALO + tm, D_CONV), F32),
                        pltpu.VMEM((SUBLANES, tm + HALO, D_CONV), F32), pltpu.VMEM((tm, D_CONV), F32)],
        compiler_params=_params(("arbitrary", "arbitrary")),
    )(h, h, h, z, dy, kv, cw, clg, clb, glg, glb, ws, bs_t, after)


def _grad_x(dh, w_full, dr, tm, after):
    n = dh.shape[0]

    def body(dh_ref, w_ref, dr_ref, after_ref, o_ref):
        acc = ALPHA * dr_ref[...]
        for j in range(N_CHIPS):
            acc = acc + _dot_nt(dh_ref[:, j * W_IN_SHARD:(j + 1) * W_IN_SHARD], w_ref[j])
        o_ref[...] = acc

    return pl.pallas_call(
        body, name="grad_x", grid=(n // tm,),
        in_specs=[pl.BlockSpec((tm, D_IN), lambda i: (i, 0)),
                  pl.BlockSpec((N_CHIPS, D_MODEL, W_IN_SHARD), lambda i: (0, 0, 0)),
                  pl.BlockSpec((tm, D_MODEL), lambda i: (i, 0)), ANY],
        out_specs=pl.BlockSpec((tm, D_MODEL), lambda i: (i, 0)),
        out_shape=jax.ShapeDtypeStruct((n, D_MODEL), F32),
        compiler_params=_params(("arbitrary",)),
    )(dh, w_full, dr, after)


def _grad_w(a, b, tk, name, shard_rows, after):
    kdim, m = a.shape
    ncols = b.shape[1]
    nk = kdim // tk
    if shard_rows:
        shard = m // N_CHIPS
        oshape = (shard, ncols)
        a_spec = pl.BlockSpec((tk, shard), lambda j, kk: (kk, j))
        b_spec = pl.BlockSpec((tk, ncols), lambda j, kk: (kk, 0))
    else:
        shard = ncols // N_CHIPS
        oshape = (m, shard)
        a_spec = pl.BlockSpec((tk, m), lambda j, kk: (kk, 0))
        b_spec = pl.BlockSpec((tk, shard), lambda j, kk: (kk, j))

    def body(a_ref, b_ref, after_ref, own_ref, ob_ref, acc):
        j = pl.program_id(0)
        kk = pl.program_id(1)
        mine = 2 * lax.axis_index("x") + lax.axis_index("y")

        @pl.when(kk == 0)
        def _():
            acc[...] = jnp.zeros_like(acc)

        acc[...] += _dot_tn(a_ref[...].astype(BF16), b_ref[...].astype(BF16))

        @pl.when(kk == nk - 1)
        def _():
            ob_ref[...] = acc[...].astype(BF16)

        @pl.when((kk == nk - 1) & (j == mine))
        def _():
            own_ref[...] = acc[...]

    return pl.pallas_call(
        body, name=name, grid=(N_CHIPS, nk),
        in_specs=[a_spec, b_spec, ANY],
        out_specs=[pl.BlockSpec(oshape, lambda j, kk: (0, 0)),
                   pl.BlockSpec((None,) + oshape, lambda j, kk: (j, 0, 0))],
        out_shape=[jax.ShapeDtypeStruct(oshape, F32), jax.ShapeDtypeStruct((N_CHIPS,) + oshape, BF16)],
        scratch_shapes=[pltpu.VMEM(oshape, F32)],
        compiler_params=_params(("arbitrary", "arbitrary")),
    )(a, b, after)


def _sum_small(owns, gots):
    n = len(owns)

    def body(*refs):
        for a in range(n):
            o_ref, g_ref, out_ref = refs[a], refs[n + a], refs[2 * n + a]
            out_ref[...] = (o_ref[...] + g_ref[1]) + (g_ref[0] + g_ref[2])

    return pl.pallas_call(
        body, name="sum_small", out_shape=[jax.ShapeDtypeStruct(o.shape, F32) for o in owns],
        compiler_params=pltpu.CompilerParams(vmem_limit_bytes=VMEM_LIMIT),
    )(*owns, *gots)


def _sum_chips(own, got, tr, name):
    r, ccols = own.shape

    def body(o_ref, g_ref, out_ref):
        out_ref[...] = (o_ref[...] + g_ref[1].astype(F32)) + (g_ref[0].astype(F32) + g_ref[2].astype(F32))

    return pl.pallas_call(
        body, name=name, grid=(r // tr,),
        in_specs=[pl.BlockSpec((tr, ccols), lambda i: (i, 0)), pl.BlockSpec((3, tr, ccols), lambda i: (0, i, 0))],
        out_specs=pl.BlockSpec((tr, ccols), lambda i: (i, 0)),
        out_shape=jax.ShapeDtypeStruct((r, ccols), F32),
        compiler_params=_params(("arbitrary",)),
    )(own, got)


def _exchange_cores(parts):
    npart = len(parts)

    def body(*refs):
        in_refs = refs[0:npart]
        out_refs = refs[npart:2 * npart]
        send_sems, recv_sems = refs[2 * npart:]
        sibling = (lax.axis_index("x"), lax.axis_index("y"), 1 - lax.axis_index("c"))
        copies = [pltpu.make_async_remote_copy(
            src_ref=in_refs[a], dst_ref=out_refs[a], send_sem=send_sems.at[a], recv_sem=recv_sems.at[a],
            device_id=sibling, device_id_type=MESH_ID) for a in range(npart)]
        for cp in copies:
            cp.start()
        for cp in copies:
            cp.wait_recv()
        for cp in copies:
            cp.wait_send()

    return pl.pallas_call(
        body, name="exchange_cores", out_shape=[jax.ShapeDtypeStruct(p.shape, p.dtype) for p in parts],
        in_specs=[ANY] * npart, out_specs=[ANY] * npart,
        scratch_shapes=[pltpu.SemaphoreType.DMA((npart,)), pltpu.SemaphoreType.DMA((npart,))],
    )(*parts)


def _adam(g, w, m, v):
    mn = ADAM_B1 * m + (1.0 - ADAM_B1) * g
    vn = ADAM_B2 * v + (1.0 - ADAM_B2) * (g * g)
    return g, -ADAM_LR * ((mn / BC1) / (jnp.sqrt(vn / BC2) + ADAM_EPS) + ADAM_WD * w), mn, vn


def _adamw(a, b, w, m, v, tr, name):
    r, ccols = w.shape

    def body(a_ref, b_ref, w_ref, m_ref, v_ref, *outs):
        res = _adam(a_ref[...] + b_ref[...], w_ref[...], m_ref[...], v_ref[...])
        for which in range(4):
            outs[which][...] = res[which]

    spec = pl.BlockSpec((tr, ccols), lambda i: (i, 0))
    shape = jax.ShapeDtypeStruct((r, ccols), F32)
    return pl.pallas_call(
        body, name=name, grid=(r // tr,), in_specs=[spec] * 5, out_specs=[spec] * 4, out_shape=[shape] * 4,
        compiler_params=_params(("arbitrary",)),
    )(a, b, w, m, v)


SMALL = ["b_in", "conv_b", "conv_ln_g", "conv_ln_b", "gmlp_ln_g", "gmlp_ln_b", "gmlp_ws", "gmlp_bs", "b_kv", "b_out",
         "ln_g", "ln_b"]


def _adamw_small(a_parts, b_parts, params):
    nparts, nparams = len(a_parts), len(params)

    def body(*refs):
        a = refs[0:nparts]
        b = refs[nparts:2 * nparts]
        prm = refs[2 * nparts:2 * nparts + 3 * nparams]
        outs = refs[2 * nparts + 3 * nparams:]
        gb_in, g768, gws, gbs_t, gb_kv, vec3 = [a[q][...] + b[q][...] for q in range(nparts)]
        grads = [gb_in, g768[ROW_CB:ROW_CB + 1], g768[ROW_CLG:ROW_CLG + 1], g768[ROW_CLB:ROW_CLB + 1],
                 g768[ROW_GLG:ROW_GLG + 1], g768[ROW_GLB:ROW_GLB + 1], gws, jnp.transpose(gbs_t)[0:N_GHEADS, :],
                 gb_kv, vec3[0:1], vec3[1:2], vec3[2:3]]
        for q, g in enumerate(grads):
            res = _adam(g, prm[3 * q][...], prm[3 * q + 1][...], prm[3 * q + 2][...])
            for which in range(4):
                outs[4 * q + which][...] = res[which]

    flat = [t for p in params for t in p]
    out_shape = [jax.ShapeDtypeStruct(p[0].shape, F32) for p in params for _ in range(4)]
    return pl.pallas_call(
        body, name="adamw_small", out_shape=out_shape,
        compiler_params=pltpu.CompilerParams(vmem_limit_bytes=VMEM_LIMIT),
    )(*a_parts, *b_parts, *flat)


def kernel(x, mem, w_in, b_in, conv_w, conv_b, conv_ln_g, conv_ln_b, gmlp_ln_g, gmlp_ln_b, gmlp_ws, gmlp_bs, w_kv, b_kv, w_out, b_out, ln_g, ln_b, loss_target, m_w_in, m_b_in, m_conv_w, m_conv_b, m_conv_ln_g, m_conv_ln_b, m_gmlp_ln_g, m_gmlp_ln_b, m_gmlp_ws, m_gmlp_bs, m_w_kv, m_b_kv, m_w_out, m_b_out, m_ln_g, m_ln_b, v_w_in, v_b_in, v_conv_w, v_conv_b, v_conv_ln_g, v_conv_ln_b, v_gmlp_ln_g, v_gmlp_ln_b, v_gmlp_ws, v_gmlp_bs, v_w_kv, v_b_kv, v_w_out, v_b_out, v_ln_g, v_ln_b):
    weights = dict(b_in=b_in, conv_b=conv_b, conv_ln_g=conv_ln_g, conv_ln_b=conv_ln_b, gmlp_ln_g=gmlp_ln_g,
                   gmlp_ln_b=gmlp_ln_b, gmlp_ws=gmlp_ws, gmlp_bs=gmlp_bs, b_kv=b_kv, b_out=b_out, ln_g=ln_g, ln_b=ln_b)
    mom_m = dict(b_in=m_b_in, conv_b=m_conv_b, conv_ln_g=m_conv_ln_g, conv_ln_b=m_conv_ln_b, gmlp_ln_g=m_gmlp_ln_g,
                 gmlp_ln_b=m_gmlp_ln_b, gmlp_ws=m_gmlp_ws, gmlp_bs=m_gmlp_bs, b_kv=m_b_kv, b_out=m_b_out,
                 ln_g=m_ln_g, ln_b=m_ln_b)
    mom_v = dict(b_in=v_b_in, conv_b=v_conv_b, conv_ln_g=v_conv_ln_g, conv_ln_b=v_conv_ln_b, gmlp_ln_g=v_gmlp_ln_g,
                 gmlp_ln_b=v_gmlp_ln_b, gmlp_ws=v_gmlp_ws, gmlp_bs=v_gmlp_bs, b_kv=v_b_kv, b_out=v_b_out,
                 ln_g=v_ln_g, ln_b=v_ln_b)
    nb, seq, _ = x.shape
    n = nb * seq
    tm = 256
    tk = min(1024, n)
    x2 = x.reshape(n, D_MODEL)
    tgt2 = loss_target.reshape(n, D_MODEL)
    mem2 = mem.reshape(nb * MEM_LEN, D_MODEL)
    chip = 2 * lax.axis_index("x") + lax.axis_index("y")
    bs_t = jnp.transpose(gmlp_bs[0])

    win_g = _gather_shards(w_in[0].astype(BF16), "gather_w_in")
    own_w = [w_out[0].astype(BF16), w_kv[0].astype(BF16), conv_w[0]]
    rest = _start_exchange("gather_rest_start", own_w, [False] * 3)

    h = _in_proj(x2, win_g, b_in, min(512, n), rest["token"])
    wout_g, wkv_g, cw_g = _place_shards(own_w, _wait_exchange("gather_rest_wait", rest, h), "place_shards")
    wout_full = wout_g.reshape(D_MIX, D_MODEL)
    wkv_full = wkv_g.reshape(D_MODEL, D_MODEL)
    cw_full = jnp.transpose(cw_g, (1, 0, 2)).reshape(CONV_WIDTH, D_CONV)
    kv = _kv_proj(mem2, wkv_full, b_kv)
    ycat, z = _branch_fwd(h, kv, cw_full, conv_b, conv_ln_g, conv_ln_b, gmlp_ln_g, gmlp_ln_b, gmlp_ws[0], bs_t,
                          nb, seq, tm)
    dr, dycat, vec3, loss_part = _out_proj_loss(ycat, wout_full, b_out, ln_g, ln_b, x2, tgt2, tm)
    loss = lax.psum(loss_part[0, 0], ("x", "y", "c"))

    own_wout, gwout_b = _grad_w(ycat, dr, tk, "grad_w_out", True, dycat)
    ex1 = _start_exchange("exchange1_start", [gwout_b, vec3], [True, False])
    dh, gb_in, g768, gws, gbs_t, dkv, gb_kv = _branch_bwd(
        h, z, dycat, kv, cw_full, conv_ln_g, conv_ln_b, gmlp_ln_g, gmlp_ln_b, gmlp_ws[0], bs_t, nb, seq, tm,
        ex1["token"])
    own_wkv, gwkv_b = _grad_w(mem2, dkv, nb * MEM_LEN, "grad_w_kv", True, dkv)
    small2 = [gb_in, g768, gws, gbs_t, gb_kv]
    ex2 = _start_exchange("exchange2_start", [gwkv_b] + small2, [True] + [False] * 5)
    own_win, gwin_b = _grad_w(x2, dh, tk, "grad_w_in", False, ex2["token"])
    ex3 = _start_exchange("exchange3_start", [gwin_b], [True])
    grad_x2 = _grad_x(dh, win_g, dr, tm, ex3["token"])
    got_wout, got_vec3 = _wait_exchange("exchange1_wait", ex1, grad_x2)
    got2 = _wait_exchange("exchange2_wait", ex2, grad_x2)
    (got_win,) = _wait_exchange("exchange3_wait", ex3, grad_x2)

    sum_win = _sum_chips(own_win, got_win, 256, "sum_w_in")
    sum_wout = _sum_chips(own_wout, got_wout, 256, "sum_w_out")
    sum_wkv = _sum_chips(own_wkv, got2[0], 256, "sum_w_kv")
    sum_small = list(_sum_small(small2 + [vec3], got2[1:] + [got_vec3]))
    sib = list(_exchange_cores([sum_win, sum_wout, sum_wkv] + sum_small))

    big = {}
    big["w_in"] = _adamw(sum_win, sib[0], w_in[0], m_w_in[0], v_w_in[0], 256, "adamw_w_in")
    big["w_out"] = _adamw(sum_wout, sib[1], w_out[0], m_w_out[0], v_w_out[0], 256, "adamw_w_out")
    big["w_kv"] = _adamw(sum_wkv, sib[2], w_kv[0], m_w_kv[0], v_w_kv[0], 256, "adamw_w_kv")
    lead = {"gmlp_ws", "gmlp_bs"}
    strip = lambda k, t: t[0] if k in lead else t
    sm = _adamw_small(sum_small, sib[3:], [tuple(strip(k, t[k]) for t in (weights, mom_m, mom_v)) for k in SMALL])
    small_out = {k: [sm[4 * q + which][None] if k in lead else sm[4 * q + which] for which in range(4)]
                 for q, k in enumerate(SMALL)}
    cw_a = lax.dynamic_slice_in_dim(sum_small[1][0:CONV_WIDTH + 1], chip * CONV_SHARD, CONV_SHARD, axis=1)
    cw_b = lax.dynamic_slice_in_dim(sib[4][0:CONV_WIDTH + 1], chip * CONV_SHARD, CONV_SHARD, axis=1)
    cwp = ((0, 1), (0, 0))
    cw_out = _adamw(cw_a, cw_b, jnp.pad(conv_w[0], cwp), jnp.pad(m_conv_w[0], cwp), jnp.pad(v_conv_w[0], cwp),
                    CONV_WIDTH + 1, "adamw_conv_w")

    order = ["w_in", "b_in", "conv_w", "conv_b", "conv_ln_g", "conv_ln_b", "gmlp_ln_g", "gmlp_ln_b", "gmlp_ws",
             "gmlp_bs", "w_kv", "b_kv", "w_out", "b_out", "ln_g", "ln_b"]
    result = [loss, grad_x2.reshape(nb, seq, D_MODEL)]
    for which in range(4):
        for k in order:
            if k in big:
                result.append(big[k][which][None])
            elif k == "conv_w":
                result.append(cw_out[which][0:CONV_WIDTH][None])
            else:
                result.append(small_out[k][which])
    return tuple(result)
```

```python
import functools
import math

import jax
import jax.numpy as jnp
from jax import lax
from jax.experimental import pallas as pl
from jax.experimental.pallas import tpu as pltpu

F32 = jnp.float32
BF16 = jnp.bfloat16

D_MODEL = 1024
MEM_LEN = 256
D_MIX = 2048
D_CONV = 768
D_GMLP = 768
D_XATT = 512
N_XHEADS = 4
XHEAD = 128
CONV_WIDTH = 31
CHUNK = 128
N_GHEADS = 6
D_IN = 3 * D_CONV + 3 * D_GMLP + 2 * D_XATT
ALPHA = 2.0 ** 0.25
LN_EPS = 1e-5
N_CHIPS = 4
W_IN_SHARD = D_IN // N_CHIPS
W_OUT_SHARD = D_MIX // N_CHIPS
W_KV_SHARD = D_MODEL // N_CHIPS
CONV_SHARD = D_CONV // N_CHIPS
HALO = 32

C_A, C_GLU, C_GATE = 0, 768, 1536
G_U, G_V, G_GATE = 2304, 3072, 3840
X_Q, X_GATE = 4608, 5120

ADAM_LR = 0.001
ADAM_B1 = 0.9
ADAM_B2 = 0.999
ADAM_EPS = 1e-08
ADAM_WD = 0.01
ADAM_STEP = 10
BC1 = 1.0 - ADAM_B1 ** ADAM_STEP
BC2 = 1.0 - ADAM_B2 ** ADAM_STEP

VMEM_LIMIT = 56 * 1024 * 1024
MESH_ID = pl.DeviceIdType.MESH
ANY = pl.BlockSpec(memory_space=pl.ANY)

GELU_C0 = math.sqrt(2.0 / math.pi)
GELU_C1 = 0.044715


def _sigmoid(v):
    return 0.5 + 0.5 * jnp.tanh(0.5 * v)


def _f32(ref, rows, col, width):
    return ref[rows, col:col + width].astype(F32)


def _dsilu(v, s):
    return s * (1.0 + v * (1.0 - s))


def _gelu_and_grad(v):
    t = jnp.tanh(GELU_C0 * (v + GELU_C1 * v * v * v))
    g = 0.5 * v * (1.0 + t)
    dg = 0.5 * (1.0 + t) + 0.5 * v * (1.0 - t * t) * (GELU_C0 * (1.0 + 3.0 * GELU_C1 * v * v))
    return g, dg


def _gelu(v):
    return 0.5 * v * (1.0 + jnp.tanh(GELU_C0 * (v + GELU_C1 * v * v * v)))


def _ln_stats(v):
    mu = jnp.mean(v, axis=-1, keepdims=True)
    vc = v - mu
    var = jnp.mean(vc * vc, axis=-1, keepdims=True)
    rstd = lax.rsqrt(var + LN_EPS)
    return vc * rstd, rstd


def _ln_bwd(dvhat, vhat, rstd):
    m1 = jnp.mean(dvhat, axis=-1, keepdims=True)
    m2 = jnp.mean(dvhat * vhat, axis=-1, keepdims=True)
    return rstd * (dvhat - m1 - vhat * m2)


def _colsum(v):
    return jnp.sum(v, axis=0, keepdims=True)


def _dot(a, b):
    return jnp.dot(a, b, preferred_element_type=F32)


def _dot_nt(a, b):
    return lax.dot_general(a, b, (((1,), (1,)), ((), ())), preferred_element_type=F32)


def _dot_tn(a, b):
    return lax.dot_general(a, b, (((0,), (0,)), ((), ())), preferred_element_type=F32)


def _causal_ws(ws_ref):
    row = lax.broadcasted_iota(jnp.int32, (CHUNK, CHUNK), 0)
    col = lax.broadcasted_iota(jnp.int32, (CHUNK, CHUNK), 1)
    keep = col <= row
    return [jnp.where(keep, ws_ref[hd], 0.0).astype(BF16) for hd in range(N_GHEADS)], keep


def _params(sem):
    return pltpu.CompilerParams(dimension_semantics=sem, vmem_limit_bytes=VMEM_LIMIT)


def _peer_chips():
    x, y, c = lax.axis_index("x"), lax.axis_index("y"), lax.axis_index("c")
    return [(1 - x, y), (x, 1 - y), (1 - x, 1 - y)], 2 * x + y, c


HBM = pl.BlockSpec(memory_space=pltpu.HBM)
SEM = pl.BlockSpec(memory_space=pltpu.SEMAPHORE)
EFFECT = pltpu.SideEffectType.DATAFLOW_SIDE_EFFECTING


def _exchange_copies(src_refs, land_refs, per_chip, send_sems, recv_sems):
    chips, _, c = _peer_chips()
    n = len(src_refs)
    copies = []
    for p, (px, py) in enumerate(chips):
        for a in range(n):
            src = src_refs[a].at[2 * px + py] if per_chip[a] else src_refs[a]
            copies.append(pltpu.make_async_remote_copy(
                src_ref=src, dst_ref=land_refs[a].at[p], send_sem=send_sems.at[n * p + a],
                recv_sem=recv_sems.at[n * p + a], device_id=(px, py, c), device_id_type=MESH_ID))
    return copies


def _start_exchange(name, srcs, per_chip):
    n = len(srcs)
    lands = [lax.empty((3,) + (s.shape[1:] if pc else s.shape), s.dtype) for s, pc in zip(srcs, per_chip)]

    def body(*refs):
        src_refs, land_refs = refs[0:n], refs[n:2 * n]
        send_sems, recv_sems = refs[2 * n], refs[2 * n + 1]
        token = refs[4 * n + 2]
        for cp in _exchange_copies(src_refs, land_refs, per_chip, send_sems, recv_sems):
            cp.start()
        token[...] = jnp.zeros_like(token)

    out = pl.pallas_call(
        body, name=name,
        out_shape=(pltpu.SemaphoreType.DMA((3 * n,)), pltpu.SemaphoreType.DMA((3 * n,)),
                   *[pltpu.HBM(a.shape, a.dtype) for a in srcs + lands], jax.ShapeDtypeStruct((8, 128), F32)),
        in_specs=[HBM] * (2 * n),
        out_specs=(SEM, SEM, *[HBM] * (2 * n), pl.BlockSpec(memory_space=pltpu.VMEM)),
        input_output_aliases={a: 2 + a for a in range(2 * n)},
        compiler_params=pltpu.CompilerParams(has_side_effects=EFFECT),
    )(*[pltpu.with_memory_space_constraint(a, pltpu.HBM) for a in srcs + lands])
    return dict(send=out[0], recv=out[1], thru=list(out[2:2 * n + 2]), token=out[2 * n + 2], per_chip=per_chip)


def _wait_exchange(name, started, after):
    thru, per_chip = started["thru"], started["per_chip"]
    n = len(thru) // 2

    def body(*refs):
        src_refs, land_refs = refs[0:n], refs[n:2 * n]
        send_sems, recv_sems = refs[2 * n], refs[2 * n + 1]
        for cp in _exchange_copies(src_refs, land_refs, per_chip, send_sems, recv_sems):
            cp.wait_send()
            cp.wait_recv()

    out = pl.pallas_call(
        body, name=name, out_shape=tuple(pltpu.HBM(a.shape, a.dtype) for a in thru),
        in_specs=[HBM] * (2 * n) + [SEM, SEM, ANY], out_specs=tuple([HBM] * (2 * n)),
        input_output_aliases={a: a for a in range(2 * n)},
        compiler_params=pltpu.CompilerParams(has_side_effects=EFFECT),
    )(*thru, started["send"], started["recv"], after)
    return list(out[n:2 * n])


def _place_shards(owns, landeds, name):
    n = len(owns)
    mine = (2 * lax.axis_index("x") + lax.axis_index("y")).astype(jnp.int32).reshape(1)

    def body(mine_ref, *refs):
        own_refs, land_refs, out_refs = refs[0:n], refs[n:2 * n], refs[2 * n:3 * n]
        k = pl.program_id(0)
        for a in range(n):
            @pl.when(k == mine_ref[0])
            def _():
                out_refs[a][...] = own_refs[a][...]

            @pl.when(k != mine_ref[0])
            def _():
                out_refs[a][...] = land_refs[a][...]

    def slot(k, mine_ref):
        d = k ^ mine_ref[0]
        return jnp.where(d == 1, 1, jnp.where(d == 3, 2, 0))

    zeros = lambda o: (0,) * len(o.shape)
    grid_spec = pltpu.PrefetchScalarGridSpec(
        num_scalar_prefetch=1, grid=(N_CHIPS,),
        in_specs=[pl.BlockSpec(o.shape, lambda k, m, o=o: zeros(o)) for o in owns]
        + [pl.BlockSpec((None,) + o.shape, lambda k, m, o=o: (slot(k, m),) + zeros(o)) for o in owns],
        out_specs=[pl.BlockSpec((None,) + o.shape, lambda k, m, o=o: (k,) + zeros(o)) for o in owns])
    return pl.pallas_call(
        body, name=name, grid_spec=grid_spec,
        out_shape=[jax.ShapeDtypeStruct((N_CHIPS,) + o.shape, o.dtype) for o in owns],
        compiler_params=_params(("arbitrary",)),
    )(mine, *owns, *landeds)


def _in_proj(x2, w_shard, b_in, tm, after):
    n = x2.shape[0]
    nt = n // tm
    chip = 2 * lax.axis_index("x") + lax.axis_index("y")
    order = jnp.stack([chip, chip ^ 1, chip ^ 2, chip ^ 3]).astype(jnp.int32)
    flip_of_step = {1: 1, 2: 0, 3: 2}

    def body(order_ref, x_ref, w_ref, b_ref, after_ref, h_ref, wg_ref, wbuf, send_sems, recv_sems, loc_sems):
        s = pl.program_id(0)
        i = pl.program_id(1)
        chips, mine, c = _peer_chips()

        def remote(p, slot):
            px, py = chips[p]
            return pltpu.make_async_remote_copy(
                src_ref=w_ref, dst_ref=wg_ref.at[slot], send_sem=send_sems.at[p], recv_sem=recv_sems.at[p],
                device_id=(px, py, c), device_id_type=MESH_ID)

        own = pltpu.make_async_copy(w_ref, wg_ref.at[mine], loc_sems.at[2])

        @pl.when((s == 0) & (i == 0))
        def _():
            remote(1, mine).start()
            remote(0, mine).start()
            own.start()
            first = pltpu.make_async_copy(w_ref, wbuf.at[0], loc_sems.at[0])
            first.start()
            first.wait()

        for step, p in flip_of_step.items():
            @pl.when((s == step) & (i == 0))
            def _():
                px, py = chips[p]
                remote(p, 2 * px + py).wait_recv()
                if step == 1:
                    remote(2, mine).start()
                load = pltpu.make_async_copy(wg_ref.at[2 * px + py], wbuf.at[step % 2], loc_sems.at[step % 2])
                load.start()
                load.wait()

        h_ref[...] = (_dot(x_ref[...].astype(BF16), wbuf[s % 2]) + b_ref[...]).astype(BF16)

        @pl.when((s == N_CHIPS - 1) & (i == nt - 1))
        def _():
            for p in range(3):
                remote(p, mine).wait_send()
            own.wait()

    grid_spec = pltpu.PrefetchScalarGridSpec(
        num_scalar_prefetch=1, grid=(N_CHIPS, nt),
        in_specs=[pl.BlockSpec((tm, D_MODEL), lambda s, i, o: (i, 0)), ANY,
                  pl.BlockSpec((1, W_IN_SHARD), lambda s, i, o: (0, o[s])), ANY],
        out_specs=[pl.BlockSpec((tm, W_IN_SHARD), lambda s, i, o: (i, o[s])), ANY],
        scratch_shapes=[pltpu.VMEM((2, D_MODEL, W_IN_SHARD), BF16), pltpu.SemaphoreType.DMA((3,)),
                        pltpu.SemaphoreType.DMA((3,)), pltpu.SemaphoreType.DMA((3,))])
    return pl.pallas_call(
        body, name="in_proj", grid_spec=grid_spec,
        out_shape=[jax.ShapeDtypeStruct((n, D_IN), BF16),
                   jax.ShapeDtypeStruct((N_CHIPS, D_MODEL, W_IN_SHARD), BF16)],
        compiler_params=_params(("arbitrary", "arbitrary")),
    )(order, x2, w_shard, b_in, after)


def _kv_proj(mem2, wkv_full, b_kv):
    m = mem2.shape[0]

    def body(m_ref, w_ref, b_ref, o_ref):
        o_ref[...] = _dot(m_ref[...].astype(BF16), w_ref[...]) + b_ref[...]

    return pl.pallas_call(
        body, name="kv_proj", grid=(m // MEM_LEN,),
        in_specs=[pl.BlockSpec((MEM_LEN, D_MODEL), lambda i: (i, 0)),
                  pl.BlockSpec((D_MODEL, D_MODEL), lambda i: (0, 0)),
                  pl.BlockSpec((1, D_MODEL), lambda i: (0, 0))],
        out_specs=pl.BlockSpec((MEM_LEN, D_MODEL), lambda i: (i, 0)),
        out_shape=jax.ShapeDtypeStruct((m, D_MODEL), F32),
        compiler_params=_params(("arbitrary",)),
    )(mem2, wkv_full, b_kv)


CONV_ROWS = 16
SUBLANES = 8


def _shifted_planes(buf, tm):
    rows = tm + HALO - SUBLANES
    for s in range(1, SUBLANES):
        buf[s, 0:rows, :] = buf[0, s:s + rows, :]


def _window(buf, start, rows):
    s = start % SUBLANES
    return buf[s, start - s:start - s + rows, :]


def _branch_fwd(h, kv, cw, cb, cg, cbeta, gg_, gb_, ws, bs_t, nb, seq, tm, after):
    nt = seq // tm
    n = nb * seq

    def body(h_ref, kv_ref, cw_ref, cb_ref, clg_ref, clb_ref, glg_ref, glb_ref, ws_ref, bst_ref, after_ref,
             y_ref, z_ref, hcbuf):
        i = pl.program_id(1)

        @pl.when(i == 0)
        def _():
            hcbuf[0, 0:HALO, :] = jnp.zeros((HALO, D_CONV), F32)

        @pl.when(i > 0)
        def _():
            hcbuf[0, 0:HALO, :] = hcbuf[0, tm:tm + HALO, :]

        every = slice(None)
        hcbuf[0, HALO:HALO + tm, :] = _f32(h_ref, every, C_A, D_CONV) * _sigmoid(_f32(h_ref, every, C_GLU, D_CONV))
        _shifted_planes(hcbuf, tm)
        for r in range(tm // CONV_ROWS):
            base = r * CONV_ROWS
            acc = jnp.broadcast_to(cb_ref[...], (CONV_ROWS, D_CONV))
            for k in range(CONV_WIDTH):
                acc = acc + cw_ref[k:k + 1, :] * _window(hcbuf, base + 2 + k, CONV_ROWS)
            z_ref[base:base + CONV_ROWS, :] = acc
        zhat, _ = _ln_stats(z_ref[...])
        zn = zhat * clg_ref[...] + clb_ref[...]
        cgate = _f32(h_ref, every, C_GATE, D_CONV)
        y_ref[:, 0:D_CONV] = (zn * _sigmoid(zn) * (cgate * _sigmoid(cgate))).astype(BF16)

        wsc, _ = _causal_ws(ws_ref)
        vhat, _ = _ln_stats(_gelu(_f32(h_ref, every, G_V, D_GMLP)))
        vn = (vhat * glg_ref[...] + glb_ref[...]).astype(BF16)
        for ch in range(tm // CHUNK):
            rows = slice(ch * CHUNK, (ch + 1) * CHUNK)
            for hd in range(N_GHEADS):
                cols = slice(hd * CHUNK, (hd + 1) * CHUNK)
                s = _dot(wsc[hd], vn[rows, cols]) + bst_ref[:, hd:hd + 1]
                u = _gelu(_f32(h_ref, rows, G_U + hd * CHUNK, CHUNK))
                gate = _f32(h_ref, rows, G_GATE + hd * CHUNK, CHUNK)
                y_ref[rows, D_CONV + hd * CHUNK:D_CONV + (hd + 1) * CHUNK] = (
                    u * s * (gate * _sigmoid(gate))).astype(BF16)

        scale = XHEAD ** -0.5
        for hd in range(N_XHEADS):
            q = h_ref[:, X_Q + hd * XHEAD:X_Q + (hd + 1) * XHEAD]
            k = kv_ref[:, hd * XHEAD:(hd + 1) * XHEAD].astype(BF16)
            v = kv_ref[:, D_XATT + hd * XHEAD:D_XATT + (hd + 1) * XHEAD].astype(BF16)
            s = _dot_nt(q, k) * scale
            e = jnp.exp(s - jnp.max(s, axis=-1, keepdims=True))
            p = e * (1.0 / jnp.sum(e, axis=-1, keepdims=True))
            o = _dot(p.astype(BF16), v)
            gate = _f32(h_ref, every, X_GATE + hd * XHEAD, XHEAD)
            y_ref[:, 2 * D_CONV + hd * XHEAD:2 * D_CONV + (hd + 1) * XHEAD] = (
                o * (gate * _sigmoid(gate))).astype(BF16)

    row = lambda b, i: (b * nt + i, 0)
    const2 = lambda b, i: (0, 0)
    vec = pl.BlockSpec((1, D_CONV), const2)
    return pl.pallas_call(
        body, name="branch_fwd", grid=(nb, nt),
        in_specs=[pl.BlockSpec((tm, D_IN), row),
                  pl.BlockSpec((MEM_LEN, D_MODEL), lambda b, i: (b, 0)),
                  pl.BlockSpec((CONV_WIDTH, D_CONV), const2), vec, vec, vec, vec, vec,
                  pl.BlockSpec((N_GHEADS, CHUNK, CHUNK), lambda b, i: (0, 0, 0)),
                  pl.BlockSpec((CHUNK, N_GHEADS), const2), ANY],
        out_specs=[pl.BlockSpec((tm, D_MIX), row), pl.BlockSpec((tm, D_CONV), row)],
        out_shape=[jax.ShapeDtypeStruct((n, D_MIX), BF16), jax.ShapeDtypeStruct((n, D_CONV), F32)],
        scratch_shapes=[pltpu.VMEM((SUBLANES, HALO + tm, D_CONV), F32)],
        compiler_params=_params(("arbitrary", "arbitrary")),
    )(h, kv, cw, cb, cg, cbeta, gg_, gb_, ws, bs_t, after)


def _out_proj_loss(ycat, wout_full, b_out, ln_g, ln_b, x2, tgt2, tm):
    n = x2.shape[0]

    def body(y_ref, w_ref, bo_ref, g_ref, b_ref, x_ref, t_ref, dr_ref, dy_ref, vec_ref, loss_ref):
        i = pl.program_id(0)

        @pl.when(i == 0)
        def _():
            vec_ref[...] = jnp.zeros_like(vec_ref)
            loss_ref[...] = jnp.zeros_like(loss_ref)

        r = ALPHA * x_ref[...] + _dot(y_ref[...], w_ref[...]) + bo_ref[...]
        rhat, rstd = _ln_stats(r)
        diff = rhat * g_ref[...] + b_ref[...] - t_ref[...]
        loss_ref[...] += 0.5 * jnp.sum(jnp.mean(diff * diff, axis=-1, keepdims=True), axis=0, keepdims=True)
        dout = diff * (1.0 / D_MODEL)
        dr = _ln_bwd(dout * g_ref[...], rhat, rstd)
        vec_ref[0:1, :] += _colsum(dr)
        vec_ref[1:2, :] += _colsum(dout * rhat)
        vec_ref[2:3, :] += _colsum(dout)
        dr_ref[...] = dr
        dy_ref[...] = _dot_nt(dr.astype(BF16), w_ref[...]).astype(BF16)

    row = lambda i: (i, 0)
    const = lambda i: (0, 0)
    vec = pl.BlockSpec((1, D_MODEL), const)
    return pl.pallas_call(
        body, name="out_proj_loss", grid=(n // tm,),
        in_specs=[pl.BlockSpec((tm, D_MIX), row), pl.BlockSpec((D_MIX, D_MODEL), const), vec, vec, vec,
                  pl.BlockSpec((tm, D_MODEL), row), pl.BlockSpec((tm, D_MODEL), row)],
        out_specs=[pl.BlockSpec((tm, D_MODEL), row), pl.BlockSpec((tm, D_MIX), row),
                   pl.BlockSpec((8, D_MODEL), const), pl.BlockSpec((1, 1), const)],
        out_shape=[jax.ShapeDtypeStruct((n, D_MODEL), F32), jax.ShapeDtypeStruct((n, D_MIX), BF16),
                   jax.ShapeDtypeStruct((8, D_MODEL), F32), jax.ShapeDtypeStruct((1, 1), F32)],
        compiler_params=_params(("arbitrary",)),
    )(ycat, wout_full, b_out, ln_g, ln_b, x2, tgt2)


ROW_CB, ROW_CLG, ROW_CLB, ROW_GLG, ROW_GLB = 32, 33, 34, 35, 36


def _branch_bwd(h, z, dy, kv, cw, clg, clb, glg, glb, ws, bs_t, nb, seq, tm, after):
    nt = seq // tm
    n = nb * seq

    def body(h_ref, z_ref, dy_ref, kv_ref, cw_ref, clg_ref, clb_ref, glg_ref, glb_ref,
             ws_ref, bst_ref, after_ref, dh_ref, gbin_ref, g768_ref, gws_ref, gbst_ref, dkv_ref, gbkv_ref,
             dzbuf, dvnbuf):
        b = pl.program_id(0)
        i = pl.program_id(1)
        every = slice(None)

        @pl.when((b == 0) & (i == 0))
        def _():
            gbin_ref[...] = jnp.zeros_like(gbin_ref)
            g768_ref[...] = jnp.zeros_like(g768_ref)
            gws_ref[...] = jnp.zeros_like(gws_ref)
            gbst_ref[...] = jnp.zeros_like(gbst_ref)
            gbkv_ref[...] = jnp.zeros_like(gbkv_ref)

        @pl.when(i == 0)
        def _():
            dkv_ref[...] = jnp.zeros_like(dkv_ref)

        def emit(col, width, val):
            gbin_ref[:, col:col + width] += _colsum(val)
            dh_ref[:, col:col + width] = val.astype(BF16)

        d_c = dy_ref[:, 0:D_CONV].astype(F32)
        cgate = _f32(h_ref, every, C_GATE, D_CONV)
        sg = _sigmoid(cgate)
        zhat, zrstd = _ln_stats(z_ref[...])
        zn = zhat * clg_ref[...] + clb_ref[...]
        szn = _sigmoid(zn)
        emit(C_GATE, D_CONV, d_c * (zn * szn) * _dsilu(cgate, sg))
        dzn = d_c * (cgate * sg) * _dsilu(zn, szn)
        g768_ref[ROW_CLG:ROW_CLG + 1, :] += _colsum(dzn * zhat)
        g768_ref[ROW_CLB:ROW_CLB + 1, :] += _colsum(dzn)
        dz = _ln_bwd(dzn * clg_ref[...], zhat, zrstd)
        g768_ref[ROW_CB:ROW_CB + 1, :] += _colsum(dz)

        @pl.when(i == 0)
        def _():
            dzbuf[0, tm:tm + HALO, :] = jnp.zeros((HALO, D_CONV), F32)

        @pl.when(i > 0)
        def _():
            dzbuf[0, tm:tm + HALO, :] = dzbuf[0, 0:HALO, :]

        dzbuf[0, 0:tm, :] = dz
        _shifted_planes(dzbuf, tm)
        a = _f32(h_ref, every, C_A, D_CONV)
        sgl = _sigmoid(_f32(h_ref, every, C_GLU, D_CONV))
        hc = a * sgl

        for r in range(tm // CONV_ROWS):
            base = r * CONV_ROWS
            acc = jnp.zeros((CONV_ROWS, D_CONV), F32)
            for k in range(CONV_WIDTH):
                acc = acc + cw_ref[k:k + 1, :] * _window(dzbuf, base + 30 - k, CONV_ROWS)
            dvnbuf[base:base + CONV_ROWS, :] = acc
        dhc = dvnbuf[...]
        emit(C_A, D_CONV, dhc * sgl)
        emit(C_GLU, D_CONV, dhc * a * sgl * (1.0 - sgl))
        for k in range(CONV_WIDTH):
            g768_ref[k:k + 1, :] += _colsum(hc * _window(dzbuf, 30 - k, tm))

        wsc, _ = _causal_ws(ws_ref)
        v, dgelu_v = _gelu_and_grad(_f32(h_ref, every, G_V, D_GMLP))
        vhat, vrstd = _ln_stats(v)
        vn = (vhat * glg_ref[...] + glb_ref[...]).astype(BF16)
        for ch in range(tm // CHUNK):
            rows = slice(ch * CHUNK, (ch + 1) * CHUNK)
            for hd in range(N_GHEADS):
                cols = slice(hd * CHUNK, (hd + 1) * CHUNK)
                vn_blk = vn[rows, cols]
                s = _dot(wsc[hd], vn_blk) + bst_ref[:, hd:hd + 1]
                u, dgelu_u = _gelu_and_grad(_f32(h_ref, rows, G_U + hd * CHUNK, CHUNK))
                gate = _f32(h_ref, rows, G_GATE + hd * CHUNK, CHUNK)
                sgate = _sigmoid(gate)
                d_g = dy_ref[rows, D_CONV + hd * CHUNK:D_CONV + (hd + 1) * CHUNK].astype(F32)
                dgate = d_g * (u * s) * _dsilu(gate, sgate)
                gbin_ref[:, G_GATE + hd * CHUNK:G_GATE + (hd + 1) * CHUNK] += _colsum(dgate)
                dh_ref[rows, G_GATE + hd * CHUNK:G_GATE + (hd + 1) * CHUNK] = dgate.astype(BF16)
                dyg = d_g * (gate * sgate)
                du = dyg * s * dgelu_u
                gbin_ref[:, G_U + hd * CHUNK:G_U + (hd + 1) * CHUNK] += _colsum(du)
                dh_ref[rows, G_U + hd * CHUNK:G_U + (hd + 1) * CHUNK] = du.astype(BF16)
                ds = dyg * u
                dsb = ds.astype(BF16)
                gws_ref[hd] += _dot_nt(dsb, vn_blk)
                gbst_ref[:, hd:hd + 1] += jnp.sum(ds, axis=1, keepdims=True)
                dvnbuf[rows, cols] = _dot_tn(wsc[hd], dsb)
        dvn = dvnbuf[...]
        g768_ref[ROW_GLG:ROW_GLG + 1, :] += _colsum(dvn * vhat)
        g768_ref[ROW_GLB:ROW_GLB + 1, :] += _colsum(dvn)
        emit(G_V, D_GMLP, _ln_bwd(dvn * glg_ref[...], vhat, vrstd) * dgelu_v)

        scale = XHEAD ** -0.5
        for hd in range(N_XHEADS):
            q = h_ref[:, X_Q + hd * XHEAD:X_Q + (hd + 1) * XHEAD]
            k = kv_ref[:, hd * XHEAD:(hd + 1) * XHEAD].astype(BF16)
            vv = kv_ref[:, D_XATT + hd * XHEAD:D_XATT + (hd + 1) * XHEAD].astype(BF16)
            s = _dot_nt(q, k) * scale
            e = jnp.exp(s - jnp.max(s, axis=-1, keepdims=True))
            p = e * (1.0 / jnp.sum(e, axis=-1, keepdims=True))
            pb = p.astype(BF16)
            o = _dot(pb, vv)
            gate = _f32(h_ref, every, X_GATE + hd * XHEAD, XHEAD)
            sgate = _sigmoid(gate)
            d_x = dy_ref[:, 2 * D_CONV + hd * XHEAD:2 * D_CONV + (hd + 1) * XHEAD].astype(F32)
            emit(X_GATE + hd * XHEAD, XHEAD, d_x * o * _dsilu(gate, sgate))
            do = (d_x * (gate * sgate)).astype(BF16)
            dp = _dot_nt(do, vv)
            dsc = (p * (dp - jnp.sum(dp * p, axis=-1, keepdims=True))).astype(BF16)
            emit(X_Q + hd * XHEAD, XHEAD, _dot(dsc, k) * scale)
            dkv_ref[:, hd * XHEAD:(hd + 1) * XHEAD] += _dot_tn(dsc, q) * scale
            dkv_ref[:, D_XATT + hd * XHEAD:D_XATT + (hd + 1) * XHEAD] += _dot_tn(pb, do)

        @pl.when(i == nt - 1)
        def _():
            gbkv_ref[...] += _colsum(dkv_ref[...])

        @pl.when((b == nb - 1) & (i == nt - 1))
        def _():
            _, keep = _causal_ws(ws_ref)
            for hd in range(N_GHEADS):
                gws_ref[hd] = jnp.where(keep, gws_ref[hd], 0.0)

    row = lambda b, i: (b * nt + nt - 1 - i, 0)
    const2 = lambda b, i: (0, 0)
    const3 = lambda b, i: (0, 0, 0)
    vec = pl.BlockSpec((1, D_CONV), const2)

    return pl.pallas_call(
        body, name="branch_bwd", grid=(nb, nt),
        in_specs=[pl.BlockSpec((tm, D_IN), row),
                  pl.BlockSpec((tm, D_CONV), row), pl.BlockSpec((tm, D_MIX), row),
                  pl.BlockSpec((MEM_LEN, D_MODEL), lambda b, i: (b, 0)),
                  pl.BlockSpec((CONV_WIDTH, D_CONV), const2), vec, vec, vec, vec,
                  pl.BlockSpec((N_GHEADS, CHUNK, CHUNK), const3),
                  pl.BlockSpec((CHUNK, N_GHEADS), const2), ANY],
        out_specs=[pl.BlockSpec((tm, D_IN), row),
                   pl.BlockSpec((1, D_IN), const2),
                   pl.BlockSpec((40, D_CONV), const2),
                   pl.BlockSpec((N_GHEADS, CHUNK, CHUNK), const3),
                   pl.BlockSpec((CHUNK, CHUNK), const2),
                   pl.BlockSpec((MEM_LEN, D_MODEL), lambda b, i: (b, 0)),
                   pl.BlockSpec((1, D_MODEL), const2)],
        out_shape=[jax.ShapeDtypeStruct((n, D_IN), BF16),
                   jax.ShapeDtypeStruct((1, D_IN), F32),
                   jax.ShapeDtypeStruct((40, D_CONV), F32),
                   jax.ShapeDtypeStruct((N_GHEADS, CHUNK, CHUNK), F32),
                   jax.ShapeDtypeStruct((CHUNK, CHUNK), F32),
                   jax.ShapeDtypeStruct((nb * MEM_LEN, D_MODEL), F32),
                   jax.ShapeDtypeStruct((1, D_MODEL), F32)],
        scratch_shapes=[pltpu.VMEM((SUBLANES, tm + HALO, D_CONV), F32), pltpu.VMEM((tm, D_CONV), F32)],
        compiler_params=_params(("arbitrary", "arbitrary")),
    )(h, z, dy, kv, cw, clg, clb, glg, glb, ws, bs_t, after)


def _grad_x(dh, w_full, dr, tm, after):
    n = dh.shape[0]

    def body(dh_ref, w_ref, dr_ref, after_ref, o_ref):
        acc = ALPHA * dr_ref[...]
        for j in range(N_CHIPS):
            acc = acc + _dot_nt(dh_ref[:, j * W_IN_SHARD:(j + 1) * W_IN_SHARD], w_ref[j])
        o_ref[...] = acc

    return pl.pallas_call(
        body, name="grad_x", grid=(n // tm,),
        in_specs=[pl.BlockSpec((tm, D_IN), lambda i: (i, 0)),
                  pl.BlockSpec((N_CHIPS, D_MODEL, W_IN_SHARD), lambda i: (0, 0, 0)),
                  pl.BlockSpec((tm, D_MODEL), lambda i: (i, 0)), ANY],
        out_specs=pl.BlockSpec((tm, D_MODEL), lambda i: (i, 0)),
        out_shape=jax.ShapeDtypeStruct((n, D_MODEL), F32),
        compiler_params=_params(("arbitrary",)),
    )(dh, w_full, dr, after)


def _grad_w(a, b, tk, name, shard_rows, after):
    kdim, m = a.shape
    ncols = b.shape[1]
    nk = kdim // tk
    if shard_rows:
        shard = m // N_CHIPS
        oshape = (shard, ncols)
        a_spec = pl.BlockSpec((tk, shard), lambda j, kk: (kk, j))
        b_spec = pl.BlockSpec((tk, ncols), lambda j, kk: (kk, 0))
    else:
        shard = ncols // N_CHIPS
        oshape = (m, shard)
        a_spec = pl.BlockSpec((tk, m), lambda j, kk: (kk, 0))
        b_spec = pl.BlockSpec((tk, shard), lambda j, kk: (kk, j))

    def body(a_ref, b_ref, after_ref, own_ref, ob_ref, acc):
        j = pl.program_id(0)
        kk = pl.program_id(1)
        mine = 2 * lax.axis_index("x") + lax.axis_index("y")

        @pl.when(kk == 0)
        def _():
            acc[...] = jnp.zeros_like(acc)

        acc[...] += _dot_tn(a_ref[...].astype(BF16), b_ref[...].astype(BF16))

        @pl.when(kk == nk - 1)
        def _():
            ob_ref[...] = acc[...].astype(BF16)

        @pl.when((kk == nk - 1) & (j == mine))
        def _():
            own_ref[...] = acc[...]

    return pl.pallas_call(
        body, name=name, grid=(N_CHIPS, nk),
        in_specs=[a_spec, b_spec, ANY],
        out_specs=[pl.BlockSpec(oshape, lambda j, kk: (0, 0)),
                   pl.BlockSpec((None,) + oshape, lambda j, kk: (j, 0, 0))],
        out_shape=[jax.ShapeDtypeStruct(oshape, F32), jax.ShapeDtypeStruct((N_CHIPS,) + oshape, BF16)],
        scratch_shapes=[pltpu.VMEM(oshape, F32)],
        compiler_params=_params(("arbitrary", "arbitrary")),
    )(a, b, after)


def _sum_small(owns, gots):
    n = len(owns)

    def body(*refs):
        for a in range(n):
            o_ref, g_ref, out_ref = refs[a], refs[n + a], refs[2 * n + a]
            out_ref[...] = (o_ref[...] + g_ref[1]) + (g_ref[0] + g_ref[2])

    return pl.pallas_call(
        body, name="sum_small", out_shape=[jax.ShapeDtypeStruct(o.shape, F32) for o in owns],
        compiler_params=pltpu.CompilerParams(vmem_limit_bytes=VMEM_LIMIT),
    )(*owns, *gots)


def _sum_chips(own, got, tr, name):
    r, ccols = own.shape

    def body(o_ref, g_ref, out_ref):
        out_ref[...] = (o_ref[...] + g_ref[1].astype(F32)) + (g_ref[0].astype(F32) + g_ref[2].astype(F32))

    return pl.pallas_call(
        body, name=name, grid=(r // tr,),
        in_specs=[pl.BlockSpec((tr, ccols), lambda i: (i, 0)), pl.BlockSpec((3, tr, ccols), lambda i: (0, i, 0))],
        out_specs=pl.BlockSpec((tr, ccols), lambda i: (i, 0)),
        out_shape=jax.ShapeDtypeStruct((r, ccols), F32),
        compiler_params=_params(("arbitrary",)),
    )(own, got)


def _exchange_cores(parts):
    npart = len(parts)

    def body(*refs):
        in_refs = refs[0:npart]
        out_refs = refs[npart:2 * npart]
        send_sems, recv_sems = refs[2 * npart:]
        sibling = (lax.axis_index("x"), lax.axis_index("y"), 1 - lax.axis_index("c"))
        copies = [pltpu.make_async_remote_copy(
            src_ref=in_refs[a], dst_ref=out_refs[a], send_sem=send_sems.at[a], recv_sem=recv_sems.at[a],
            device_id=sibling, device_id_type=MESH_ID) for a in range(npart)]
        for cp in copies:
            cp.start()
        for cp in copies:
            cp.wait_recv()
        for cp in copies:
            cp.wait_send()

    return pl.pallas_call(
        body, name="exchange_cores", out_shape=[jax.ShapeDtypeStruct(p.shape, p.dtype) for p in parts],
        in_specs=[ANY] * npart, out_specs=[ANY] * npart,
        scratch_shapes=[pltpu.SemaphoreType.DMA((npart,)), pltpu.SemaphoreType.DMA((npart,))],
    )(*parts)


def _adam(g, w, m, v):
    mn = ADAM_B1 * m + (1.0 - ADAM_B1) * g
    vn = ADAM_B2 * v + (1.0 - ADAM_B2) * (g * g)
    return g, -ADAM_LR * ((mn / BC1) / (jnp.sqrt(vn / BC2) + ADAM_EPS) + ADAM_WD * w), mn, vn


def _adamw(a, b, w, m, v, tr, name):
    r, ccols = w.shape

    def body(a_ref, b_ref, w_ref, m_ref, v_ref, *outs):
        res = _adam(a_ref[...] + b_ref[...], w_ref[...], m_ref[...], v_ref[...])
        for which in range(4):
            outs[which][...] = res[which]

    spec = pl.BlockSpec((tr, ccols), lambda i: (i, 0))
    shape = jax.ShapeDtypeStruct((r, ccols), F32)
    return pl.pallas_call(
        body, name=name, grid=(r // tr,), in_specs=[spec] * 5, out_specs=[spec] * 4, out_shape=[shape] * 4,
        compiler_params=_params(("arbitrary",)),
    )(a, b, w, m, v)


SMALL = ["b_in", "conv_b", "conv_ln_g", "conv_ln_b", "gmlp_ln_g", "gmlp_ln_b", "gmlp_ws", "gmlp_bs", "b_kv", "b_out",
         "ln_g", "ln_b"]


def _adamw_small(a_parts, b_parts, params):
    nparts, nparams = len(a_parts), len(params)

    def body(*refs):
        a = refs[0:nparts]
        b = refs[nparts:2 * nparts]
        prm = refs[2 * nparts:2 * nparts + 3 * nparams]
        outs = refs[2 * nparts + 3 * nparams:]
        gb_in, g768, gws, gbs_t, gb_kv, vec3 = [a[q][...] + b[q][...] for q in range(nparts)]
        grads = [gb_in, g768[ROW_CB:ROW_CB + 1], g768[ROW_CLG:ROW_CLG + 1], g768[ROW_CLB:ROW_CLB + 1],
                 g768[ROW_GLG:ROW_GLG + 1], g768[ROW_GLB:ROW_GLB + 1], gws, jnp.transpose(gbs_t)[0:N_GHEADS, :],
                 gb_kv, vec3[0:1], vec3[1:2], vec3[2:3]]
        for q, g in enumerate(grads):
            res = _adam(g, prm[3 * q][...], prm[3 * q + 1][...], prm[3 * q + 2][...])
            for which in range(4):
                outs[4 * q + which][...] = res[which]

    flat = [t for p in params for t in p]
    out_shape = [jax.ShapeDtypeStruct(p[0].shape, F32) for p in params for _ in range(4)]
    return pl.pallas_call(
        body, name="adamw_small", out_shape=out_shape,
        compiler_params=pltpu.CompilerParams(vmem_limit_bytes=VMEM_LIMIT),
    )(*a_parts, *b_parts, *flat)


def kernel(x, mem, w_in, b_in, conv_w, conv_b, conv_ln_g, conv_ln_b, gmlp_ln_g, gmlp_ln_b, gmlp_ws, gmlp_bs, w_kv, b_kv, w_out, b_out, ln_g, ln_b, loss_target, m_w_in, m_b_in, m_conv_w, m_conv_b, m_conv_ln_g, m_conv_ln_b, m_gmlp_ln_g, m_gmlp_ln_b, m_gmlp_ws, m_gmlp_bs, m_w_kv, m_b_kv, m_w_out, m_b_out, m_ln_g, m_ln_b, v_w_in, v_b_in, v_conv_w, v_conv_b, v_conv_ln_g, v_conv_ln_b, v_gmlp_ln_g, v_gmlp_ln_b, v_gmlp_ws, v_gmlp_bs, v_w_kv, v_b_kv, v_w_out, v_b_out, v_ln_g, v_ln_b):
    weights = dict(b_in=b_in, conv_b=conv_b, conv_ln_g=conv_ln_g, conv_ln_b=conv_ln_b, gmlp_ln_g=gmlp_ln_g,
                   gmlp_ln_b=gmlp_ln_b, gmlp_ws=gmlp_ws, gmlp_bs=gmlp_bs, b_kv=b_kv, b_out=b_out, ln_g=ln_g, ln_b=ln_b)
    mom_m = dict(b_in=m_b_in, conv_b=m_conv_b, conv_ln_g=m_conv_ln_g, conv_ln_b=m_conv_ln_b, gmlp_ln_g=m_gmlp_ln_g,
                 gmlp_ln_b=m_gmlp_ln_b, gmlp_ws=m_gmlp_ws, gmlp_bs=m_gmlp_bs, b_kv=m_b_kv, b_out=m_b_out,
                 ln_g=m_ln_g, ln_b=m_ln_b)
    mom_v = dict(b_in=v_b_in, conv_b=v_conv_b, conv_ln_g=v_conv_ln_g, conv_ln_b=v_conv_ln_b, gmlp_ln_g=v_gmlp_ln_g,
                 gmlp_ln_b=v_gmlp_ln_b, gmlp_ws=v_gmlp_ws, gmlp_bs=v_gmlp_bs, b_kv=v_b_kv, b_out=v_b_out,
                 ln_g=v_ln_g, ln_b=v_ln_b)
    nb, seq, _ = x.shape
    n = nb * seq
    tm = 256
    tk = min(1024, n)
    x2 = x.reshape(n, D_MODEL)
    tgt2 = loss_target.reshape(n, D_MODEL)
    mem2 = mem.reshape(nb * MEM_LEN, D_MODEL)
    chip = 2 * lax.axis_index("x") + lax.axis_index("y")
    bs_t = jnp.transpose(gmlp_bs[0])

    own_kv = [w_kv[0].astype(BF16), conv_w[0]]
    own_out = [w_out[0].astype(BF16)]
    ga = _start_exchange("gather_kv_start", own_kv, [False] * 2)
    h, win_g = _in_proj(x2, w_in[0].astype(BF16), b_in, min(512, n), ga["token"])
    gb = _start_exchange("gather_out_start", own_out, [False])
    wkv_g, cw_g = _place_shards(own_kv, _wait_exchange("gather_kv_wait", ga, h), "place_kv")
    wkv_full = wkv_g.reshape(D_MODEL, D_MODEL)
    cw_full = jnp.transpose(cw_g, (1, 0, 2)).reshape(CONV_WIDTH, D_CONV)
    kv = _kv_proj(mem2, wkv_full, b_kv)
    ycat, z = _branch_fwd(h, kv, cw_full, conv_b, conv_ln_g, conv_ln_b, gmlp_ln_g, gmlp_ln_b, gmlp_ws[0], bs_t,
                          nb, seq, tm, gb["token"])
    (wout_g,) = _place_shards(own_out, _wait_exchange("gather_out_wait", gb, ycat), "place_out")
    wout_full = wout_g.reshape(D_MIX, D_MODEL)
    dr, dycat, vec3, loss_part = _out_proj_loss(ycat, wout_full, b_out, ln_g, ln_b, x2, tgt2, tm)
    loss = lax.psum(loss_part[0, 0], ("x", "y", "c"))

    own_wout, gwout_b = _grad_w(ycat, dr, tk, "grad_w_out", True, dycat)
    ex1 = _start_exchange("exchange1_start", [gwout_b, vec3], [True, False])
    dh, gb_in, g768, gws, gbs_t, dkv, gb_kv = _branch_bwd(
        h, z, dycat, kv, cw_full, conv_ln_g, conv_ln_b, gmlp_ln_g, gmlp_ln_b, gmlp_ws[0], bs_t, nb, seq, tm,
        ex1["token"])
    own_wkv, gwkv_b = _grad_w(mem2, dkv, nb * MEM_LEN, "grad_w_kv", True, dkv)
    small2 = [gb_in, g768, gws, gbs_t, gb_kv]
    ex2 = _start_exchange("exchange2_start", [gwkv_b] + small2, [True] + [False] * 5)
    own_win, gwin_b = _grad_w(x2, dh, tk, "grad_w_in", False, ex2["token"])
    ex3 = _start_exchange("exchange3_start", [gwin_b], [True])
    grad_x2 = _grad_x(dh, win_g, dr, tm, ex3["token"])
    got_wout, got_vec3 = _wait_exchange("exchange1_wait", ex1, grad_x2)
    got2 = _wait_exchange("exchange2_wait", ex2, grad_x2)
    (got_win,) = _wait_exchange("exchange3_wait", ex3, grad_x2)

    sum_win = _sum_chips(own_win, got_win, 256, "sum_w_in")
    sum_wout = _sum_chips(own_wout, got_wout, 256, "sum_w_out")
    sum_wkv = _sum_chips(own_wkv, got2[0], 256, "sum_w_kv")
    sum_small = list(_sum_small(small2 + [vec3], got2[1:] + [got_vec3]))
    sib = list(_exchange_cores([sum_win, sum_wout, sum_wkv] + sum_small))

    big = {}
    big["w_in"] = _adamw(sum_win, sib[0], w_in[0], m_w_in[0], v_w_in[0], 256, "adamw_w_in")
    big["w_out"] = _adamw(sum_wout, sib[1], w_out[0], m_w_out[0], v_w_out[0], 256, "adamw_w_out")
    big["w_kv"] = _adamw(sum_wkv, sib[2], w_kv[0], m_w_kv[0], v_w_kv[0], 256, "adamw_w_kv")
    lead = {"gmlp_ws", "gmlp_bs"}
    strip = lambda k, t: t[0] if k in lead else t
    sm = _adamw_small(sum_small, sib[3:], [tuple(strip(k, t[k]) for t in (weights, mom_m, mom_v)) for k in SMALL])
    small_out = {k: [sm[4 * q + which][None] if k in lead else sm[4 * q + which] for which in range(4)]
                 for q, k in enumerate(SMALL)}
    cw_a = lax.dynamic_slice_in_dim(sum_small[1][0:CONV_WIDTH + 1], chip * CONV_SHARD, CONV_SHARD, axis=1)
    cw_b = lax.dynamic_slice_in_dim(sib[4][0:CONV_WIDTH + 1], chip * CONV_SHARD, CONV_SHARD, axis=1)
    cwp = ((0, 1), (0, 0))
    cw_out = _adamw(cw_a, cw_b, jnp.pad(conv_w[0], cwp), jnp.pad(m_conv_w[0], cwp), jnp.pad(v_conv_w[0], cwp),
                    CONV_WIDTH + 1, "adamw_conv_w")

    order = ["w_in", "b_in", "conv_w", "conv_b", "conv_ln_g", "conv_ln_b", "gmlp_ln_g", "gmlp_ln_b", "gmlp_ws",
             "gmlp_bs", "w_kv", "b_kv", "w_out", "b_out", "ln_g", "ln_b"]
    result = [loss, grad_x2.reshape(nb, seq, D_MODEL)]
    for which in range(4):
        for k in order:
            if k in big:
                result.append(big[k][which][None])
            elif k == "conv_w":
                result.append(cw_out[which][0:CONV_WIDTH][None])
            else:
                result.append(small_out[k][which])
    return tuple(result)
```

```python
import functools
import math

import jax
import jax.numpy as jnp
from jax import lax
from jax.experimental import pallas as pl
from jax.experimental.pallas import tpu as pltpu

F32 = jnp.float32
BF16 = jnp.bfloat16

D_MODEL = 1024
MEM_LEN = 256
D_MIX = 2048
D_CONV = 768
D_GMLP = 768
D_XATT = 512
N_XHEADS = 4
XHEAD = 128
CONV_WIDTH = 31
CHUNK = 128
N_GHEADS = 6
D_IN = 3 * D_CONV + 3 * D_GMLP + 2 * D_XATT
ALPHA = 2.0 ** 0.25
LN_EPS = 1e-5
N_CHIPS = 4
W_IN_SHARD = D_IN // N_CHIPS
W_OUT_SHARD = D_MIX // N_CHIPS
W_KV_SHARD = D_MODEL // N_CHIPS
CONV_SHARD = D_CONV // N_CHIPS
HALO = 32

C_A, C_GLU, C_GATE = 0, 768, 1536
G_U, G_V, G_GATE = 2304, 3072, 3840
X_Q, X_GATE = 4608, 5120

ADAM_LR = 0.001
ADAM_B1 = 0.9
ADAM_B2 = 0.999
ADAM_EPS = 1e-08
ADAM_WD = 0.01
ADAM_STEP = 10
BC1 = 1.0 - ADAM_B1 ** ADAM_STEP
BC2 = 1.0 - ADAM_B2 ** ADAM_STEP

VMEM_LIMIT = 56 * 1024 * 1024
MESH_ID = pl.DeviceIdType.MESH
ANY = pl.BlockSpec(memory_space=pl.ANY)

GELU_C0 = math.sqrt(2.0 / math.pi)
GELU_C1 = 0.044715


def _sigmoid(v):
    return 0.5 + 0.5 * jnp.tanh(0.5 * v)


def _f32(ref, rows, col, width):
    return ref[rows, col:col + width].astype(F32)


def _dsilu(v, s):
    return s * (1.0 + v * (1.0 - s))


def _gelu_and_grad(v):
    t = jnp.tanh(GELU_C0 * (v + GELU_C1 * v * v * v))
    g = 0.5 * v * (1.0 + t)
    dg = 0.5 * (1.0 + t) + 0.5 * v * (1.0 - t * t) * (GELU_C0 * (1.0 + 3.0 * GELU_C1 * v * v))
    return g, dg


def _gelu(v):
    return 0.5 * v * (1.0 + jnp.tanh(GELU_C0 * (v + GELU_C1 * v * v * v)))


def _ln_stats(v):
    mu = jnp.mean(v, axis=-1, keepdims=True)
    vc = v - mu
    var = jnp.mean(vc * vc, axis=-1, keepdims=True)
    rstd = lax.rsqrt(var + LN_EPS)
    return vc * rstd, rstd


def _ln_bwd(dvhat, vhat, rstd):
    m1 = jnp.mean(dvhat, axis=-1, keepdims=True)
    m2 = jnp.mean(dvhat * vhat, axis=-1, keepdims=True)
    return rstd * (dvhat - m1 - vhat * m2)


def _colsum(v):
    return jnp.sum(v, axis=0, keepdims=True)


def _dot(a, b):
    return jnp.dot(a, b, preferred_element_type=F32)


def _dot_nt(a, b):
    return lax.dot_general(a, b, (((1,), (1,)), ((), ())), preferred_element_type=F32)


def _dot_tn(a, b):
    return lax.dot_general(a, b, (((0,), (0,)), ((), ())), preferred_element_type=F32)


def _causal_ws(ws_ref):
    row = lax.broadcasted_iota(jnp.int32, (CHUNK, CHUNK), 0)
    col = lax.broadcasted_iota(jnp.int32, (CHUNK, CHUNK), 1)
    keep = col <= row
    return [jnp.where(keep, ws_ref[hd], 0.0).astype(BF16) for hd in range(N_GHEADS)], keep


def _params(sem):
    return pltpu.CompilerParams(dimension_semantics=sem, vmem_limit_bytes=VMEM_LIMIT)


def _peer_chips():
    x, y, c = lax.axis_index("x"), lax.axis_index("y"), lax.axis_index("c")
    return [(1 - x, y), (x, 1 - y), (1 - x, 1 - y)], 2 * x + y, c


HBM = pl.BlockSpec(memory_space=pltpu.HBM)
SEM = pl.BlockSpec(memory_space=pltpu.SEMAPHORE)
EFFECT = pltpu.SideEffectType.DATAFLOW_SIDE_EFFECTING
ALL_FLIPS = (0, 1, 2)


def _exchange_copies(src_refs, land_refs, per_chip, flips, send_sems, recv_sems):
    chips, _, c = _peer_chips()
    n = len(src_refs)
    copies = []
    for q, p in enumerate(flips):
        px, py = chips[p]
        for a in range(n):
            src = src_refs[a].at[2 * px + py] if per_chip[a] else src_refs[a]
            copies.append(pltpu.make_async_remote_copy(
                src_ref=src, dst_ref=land_refs[a].at[q], send_sem=send_sems.at[n * q + a],
                recv_sem=recv_sems.at[n * q + a], device_id=(px, py, c), device_id_type=MESH_ID))
    return copies


def _start_exchange(name, srcs, per_chip, flips=ALL_FLIPS):
    n = len(srcs)
    nf = len(flips)
    lands = [lax.empty((nf,) + (s.shape[1:] if pc else s.shape), s.dtype) for s, pc in zip(srcs, per_chip)]

    def body(*refs):
        src_refs, land_refs = refs[0:n], refs[n:2 * n]
        send_sems, recv_sems = refs[2 * n], refs[2 * n + 1]
        token = refs[4 * n + 2]
        for cp in _exchange_copies(src_refs, land_refs, per_chip, flips, send_sems, recv_sems):
            cp.start()
        token[...] = jnp.zeros_like(token)

    out = pl.pallas_call(
        body, name=name,
        out_shape=(pltpu.SemaphoreType.DMA((nf * n,)), pltpu.SemaphoreType.DMA((nf * n,)),
                   *[pltpu.HBM(a.shape, a.dtype) for a in srcs + lands], jax.ShapeDtypeStruct((8, 128), F32)),
        in_specs=[HBM] * (2 * n),
        out_specs=(SEM, SEM, *[HBM] * (2 * n), pl.BlockSpec(memory_space=pltpu.VMEM)),
        input_output_aliases={a: 2 + a for a in range(2 * n)},
        compiler_params=pltpu.CompilerParams(has_side_effects=EFFECT),
    )(*[pltpu.with_memory_space_constraint(a, pltpu.HBM) for a in srcs + lands])
    return dict(send=out[0], recv=out[1], thru=list(out[2:2 * n + 2]), token=out[2 * n + 2], per_chip=per_chip,
                flips=flips)


def _wait_exchange(name, started, after):
    thru, per_chip, flips = started["thru"], started["per_chip"], started["flips"]
    n = len(thru) // 2

    def body(*refs):
        src_refs, land_refs = refs[0:n], refs[n:2 * n]
        send_sems, recv_sems = refs[2 * n], refs[2 * n + 1]
        for cp in _exchange_copies(src_refs, land_refs, per_chip, flips, send_sems, recv_sems):
            cp.wait_send()
            cp.wait_recv()

    out = pl.pallas_call(
        body, name=name, out_shape=tuple(pltpu.HBM(a.shape, a.dtype) for a in thru),
        in_specs=[HBM] * (2 * n) + [SEM, SEM, ANY], out_specs=tuple([HBM] * (2 * n)),
        input_output_aliases={a: a for a in range(2 * n)},
        compiler_params=pltpu.CompilerParams(has_side_effects=EFFECT),
    )(*thru, started["send"], started["recv"], after)
    return list(out[n:2 * n])


def _place_shards(owns, landeds, name):
    n = len(owns)
    mine = (2 * lax.axis_index("x") + lax.axis_index("y")).astype(jnp.int32).reshape(1)

    def body(mine_ref, *refs):
        own_refs, land_refs, out_refs = refs[0:n], refs[n:2 * n], refs[2 * n:3 * n]
        k = pl.program_id(0)
        for a in range(n):
            @pl.when(k == mine_ref[0])
            def _():
                out_refs[a][...] = own_refs[a][...]

            @pl.when(k != mine_ref[0])
            def _():
                out_refs[a][...] = land_refs[a][...]

    def slot(k, mine_ref):
        d = k ^ mine_ref[0]
        return jnp.where(d == 1, 1, jnp.where(d == 3, 2, 0))

    zeros = lambda o: (0,) * len(o.shape)
    grid_spec = pltpu.PrefetchScalarGridSpec(
        num_scalar_prefetch=1, grid=(N_CHIPS,),
        in_specs=[pl.BlockSpec(o.shape, lambda k, m, o=o: zeros(o)) for o in owns]
        + [pl.BlockSpec((None,) + o.shape, lambda k, m, o=o: (slot(k, m),) + zeros(o)) for o in owns],
        out_specs=[pl.BlockSpec((None,) + o.shape, lambda k, m, o=o: (k,) + zeros(o)) for o in owns])
    return pl.pallas_call(
        body, name=name, grid_spec=grid_spec,
        out_shape=[jax.ShapeDtypeStruct((N_CHIPS,) + o.shape, o.dtype) for o in owns],
        compiler_params=_params(("arbitrary",)),
    )(mine, *owns, *landeds)


def _in_proj(x2, w_shard, b_in, tm, after):
    n = x2.shape[0]
    nt = n // tm
    chip = 2 * lax.axis_index("x") + lax.axis_index("y")
    order = jnp.stack([chip, chip ^ 1, chip ^ 2, chip ^ 3]).astype(jnp.int32)
    flip_of_step = {1: 1, 2: 0, 3: 2}

    def body(order_ref, x_ref, w_ref, b_ref, after_ref, h_ref, wg_ref, wbuf, send_sems, recv_sems, loc_sems):
        s = pl.program_id(0)
        i = pl.program_id(1)
        chips, mine, c = _peer_chips()

        def remote(p, slot):
            px, py = chips[p]
            return pltpu.make_async_remote_copy(
                src_ref=w_ref, dst_ref=wg_ref.at[slot], send_sem=send_sems.at[p], recv_sem=recv_sems.at[p],
                device_id=(px, py, c), device_id_type=MESH_ID)

        own = pltpu.make_async_copy(w_ref, wg_ref.at[mine], loc_sems.at[2])

        @pl.when((s == 0) & (i == 0))
        def _():
            remote(1, mine).start()
            remote(0, mine).start()
            own.start()
            first = pltpu.make_async_copy(w_ref, wbuf.at[0], loc_sems.at[0])
            first.start()
            first.wait()

        for step, p in flip_of_step.items():
            @pl.when((s == step) & (i == 0))
            def _():
                px, py = chips[p]
                remote(p, 2 * px + py).wait_recv()
                if step == 1:
                    remote(2, mine).start()
                load = pltpu.make_async_copy(wg_ref.at[2 * px + py], wbuf.at[step % 2], loc_sems.at[step % 2])
                load.start()
                load.wait()

        h_ref[...] = (_dot(x_ref[...].astype(BF16), wbuf[s % 2]) + b_ref[...]).astype(BF16)

        @pl.when((s == N_CHIPS - 1) & (i == nt - 1))
        def _():
            for p in range(3):
                remote(p, mine).wait_send()
            own.wait()

    grid_spec = pltpu.PrefetchScalarGridSpec(
        num_scalar_prefetch=1, grid=(N_CHIPS, nt),
        in_specs=[pl.BlockSpec((tm, D_MODEL), lambda s, i, o: (i, 0)), ANY,
                  pl.BlockSpec((1, W_IN_SHARD), lambda s, i, o: (0, o[s])), ANY],
        out_specs=[pl.BlockSpec((tm, W_IN_SHARD), lambda s, i, o: (i, o[s])), ANY],
        scratch_shapes=[pltpu.VMEM((2, D_MODEL, W_IN_SHARD), BF16), pltpu.SemaphoreType.DMA((3,)),
                        pltpu.SemaphoreType.DMA((3,)), pltpu.SemaphoreType.DMA((3,))])
    return pl.pallas_call(
        body, name="in_proj", grid_spec=grid_spec,
        out_shape=[jax.ShapeDtypeStruct((n, D_IN), BF16),
                   jax.ShapeDtypeStruct((N_CHIPS, D_MODEL, W_IN_SHARD), BF16)],
        compiler_params=_params(("arbitrary", "arbitrary")),
    )(order, x2, w_shard, b_in, after)


def _kv_proj(mem2, wkv_full, b_kv):
    m = mem2.shape[0]

    def body(m_ref, w_ref, b_ref, o_ref):
        o_ref[...] = (_dot(m_ref[...].astype(BF16), w_ref[...]) + b_ref[...]).astype(BF16)

    return pl.pallas_call(
        body, name="kv_proj", grid=(m // MEM_LEN,),
        in_specs=[pl.BlockSpec((MEM_LEN, D_MODEL), lambda i: (i, 0)),
                  pl.BlockSpec((D_MODEL, D_MODEL), lambda i: (0, 0)),
                  pl.BlockSpec((1, D_MODEL), lambda i: (0, 0))],
        out_specs=pl.BlockSpec((MEM_LEN, D_MODEL), lambda i: (i, 0)),
        out_shape=jax.ShapeDtypeStruct((m, D_MODEL), BF16),
        compiler_params=_params(("arbitrary",)),
    )(mem2, wkv_full, b_kv)


CONV_ROWS = 16
SUBLANES = 8


def _shifted_planes(buf, tm):
    rows = tm + HALO - SUBLANES
    for s in range(1, SUBLANES):
        buf[s, 0:rows, :] = buf[0, s:s + rows, :]


def _window(buf, start, rows):
    s = start % SUBLANES
    return buf[s, start - s:start - s + rows, :]


def _branch_fwd(h, kv, cw, cb, cg, cbeta, gg_, gb_, ws, bs_t, nb, seq, tm, after):
    nt = seq // tm
    n = nb * seq

    def body(h_ref, kv_ref, cw_ref, cb_ref, clg_ref, clb_ref, glg_ref, glb_ref, ws_ref, bst_ref, after_ref,
             y_ref, z_ref, hcbuf):
        i = pl.program_id(1)

        @pl.when(i == 0)
        def _():
            hcbuf[0, 0:HALO, :] = jnp.zeros((HALO, D_CONV), F32)

        @pl.when(i > 0)
        def _():
            hcbuf[0, 0:HALO, :] = hcbuf[0, tm:tm + HALO, :]

        every = slice(None)
        hcbuf[0, HALO:HALO + tm, :] = _f32(h_ref, every, C_A, D_CONV) * _sigmoid(_f32(h_ref, every, C_GLU, D_CONV))
        _shifted_planes(hcbuf, tm)
        for r in range(tm // CONV_ROWS):
            base = r * CONV_ROWS
            acc = jnp.broadcast_to(cb_ref[...], (CONV_ROWS, D_CONV))
            for k in range(CONV_WIDTH):
                acc = acc + cw_ref[k:k + 1, :] * _window(hcbuf, base + 2 + k, CONV_ROWS)
            z_ref[base:base + CONV_ROWS, :] = acc
        zhat, _ = _ln_stats(z_ref[...])
        zn = zhat * clg_ref[...] + clb_ref[...]
        cgate = _f32(h_ref, every, C_GATE, D_CONV)
        y_ref[:, 0:D_CONV] = (zn * _sigmoid(zn) * (cgate * _sigmoid(cgate))).astype(BF16)

        wsc, _ = _causal_ws(ws_ref)
        vhat, _ = _ln_stats(_gelu(_f32(h_ref, every, G_V, D_GMLP)))
        vn = (vhat * glg_ref[...] + glb_ref[...]).astype(BF16)
        for ch in range(tm // CHUNK):
            rows = slice(ch * CHUNK, (ch + 1) * CHUNK)
            for hd in range(N_GHEADS):
                cols = slice(hd * CHUNK, (hd + 1) * CHUNK)
                s = _dot(wsc[hd], vn[rows, cols]) + bst_ref[:, hd:hd + 1]
                u = _gelu(_f32(h_ref, rows, G_U + hd * CHUNK, CHUNK))
                gate = _f32(h_ref, rows, G_GATE + hd * CHUNK, CHUNK)
                y_ref[rows, D_CONV + hd * CHUNK:D_CONV + (hd + 1) * CHUNK] = (
                    u * s * (gate * _sigmoid(gate))).astype(BF16)

        scale = XHEAD ** -0.5
        for hd in range(N_XHEADS):
            q = h_ref[:, X_Q + hd * XHEAD:X_Q + (hd + 1) * XHEAD]
            k = kv_ref[:, hd * XHEAD:(hd + 1) * XHEAD]
            v = kv_ref[:, D_XATT + hd * XHEAD:D_XATT + (hd + 1) * XHEAD]
            s = _dot_nt(q, k) * scale
            e = jnp.exp(s - jnp.max(s, axis=-1, keepdims=True))
            p = e * (1.0 / jnp.sum(e, axis=-1, keepdims=True))
            o = _dot(p.astype(BF16), v)
            gate = _f32(h_ref, every, X_GATE + hd * XHEAD, XHEAD)
            y_ref[:, 2 * D_CONV + hd * XHEAD:2 * D_CONV + (hd + 1) * XHEAD] = (
                o * (gate * _sigmoid(gate))).astype(BF16)

    row = lambda b, i: (b * nt + i, 0)
    const2 = lambda b, i: (0, 0)
    vec = pl.BlockSpec((1, D_CONV), const2)
    return pl.pallas_call(
        body, name="branch_fwd", grid=(nb, nt),
        in_specs=[pl.BlockSpec((tm, D_IN), row),
                  pl.BlockSpec((MEM_LEN, D_MODEL), lambda b, i: (b, 0)),
                  pl.BlockSpec((CONV_WIDTH, D_CONV), const2), vec, vec, vec, vec, vec,
                  pl.BlockSpec((N_GHEADS, CHUNK, CHUNK), lambda b, i: (0, 0, 0)),
                  pl.BlockSpec((CHUNK, N_GHEADS), const2), ANY],
        out_specs=[pl.BlockSpec((tm, D_MIX), row), pl.BlockSpec((tm, D_CONV), row)],
        out_shape=[jax.ShapeDtypeStruct((n, D_MIX), BF16), jax.ShapeDtypeStruct((n, D_CONV), F32)],
        scratch_shapes=[pltpu.VMEM((SUBLANES, HALO + tm, D_CONV), F32)],
        compiler_params=_params(("arbitrary", "arbitrary")),
    )(h, kv, cw, cb, cg, cbeta, gg_, gb_, ws, bs_t, after)


def _out_proj_loss(ycat, wout_full, b_out, ln_g, ln_b, x2, tgt2, tm):
    n = x2.shape[0]

    def body(y_ref, w_ref, bo_ref, g_ref, b_ref, x_ref, t_ref, dr_ref, dy_ref, vec_ref, loss_ref):
        i = pl.program_id(0)

        @pl.when(i == 0)
        def _():
            vec_ref[...] = jnp.zeros_like(vec_ref)
            loss_ref[...] = jnp.zeros_like(loss_ref)

        r = ALPHA * x_ref[...] + _dot(y_ref[...], w_ref[...]) + bo_ref[...]
        rhat, rstd = _ln_stats(r)
        diff = rhat * g_ref[...] + b_ref[...] - t_ref[...]
        loss_ref[...] += 0.5 * jnp.sum(jnp.mean(diff * diff, axis=-1, keepdims=True), axis=0, keepdims=True)
        dout = diff * (1.0 / D_MODEL)
        dr = _ln_bwd(dout * g_ref[...], rhat, rstd)
        vec_ref[0:1, :] += _colsum(dr)
        vec_ref[1:2, :] += _colsum(dout * rhat)
        vec_ref[2:3, :] += _colsum(dout)
        dr_ref[...] = dr
        dy_ref[...] = _dot_nt(dr.astype(BF16), w_ref[...]).astype(BF16)

    row = lambda i: (i, 0)
    const = lambda i: (0, 0)
    vec = pl.BlockSpec((1, D_MODEL), const)
    return pl.pallas_call(
        body, name="out_proj_loss", grid=(n // tm,),
        in_specs=[pl.BlockSpec((tm, D_MIX), row), pl.BlockSpec((D_MIX, D_MODEL), const), vec, vec, vec,
                  pl.BlockSpec((tm, D_MODEL), row), pl.BlockSpec((tm, D_MODEL), row)],
        out_specs=[pl.BlockSpec((tm, D_MODEL), row), pl.BlockSpec((tm, D_MIX), row),
                   pl.BlockSpec((8, D_MODEL), const), pl.BlockSpec((1, 1), const)],
        out_shape=[jax.ShapeDtypeStruct((n, D_MODEL), F32), jax.ShapeDtypeStruct((n, D_MIX), BF16),
                   jax.ShapeDtypeStruct((8, D_MODEL), F32), jax.ShapeDtypeStruct((1, 1), F32)],
        compiler_params=_params(("arbitrary",)),
    )(ycat, wout_full, b_out, ln_g, ln_b, x2, tgt2)


ROW_CB, ROW_CLG, ROW_CLB, ROW_GLG, ROW_GLB = 32, 33, 34, 35, 36


def _branch_bwd(h, z, dy, kv, cw, clg, clb, glg, glb, ws, bs_t, w_full, dr, nb, seq, tm, after):
    nt = seq // tm
    n = nb * seq
    tiles = nb * nt

    def place(s):
        sc = jnp.minimum(s, tiles - 1)
        return sc // nt, nt - 1 - sc % nt

    def body(h_ref, z_ref, dy_ref, kv_ref, cw_ref, clg_ref, clb_ref, glg_ref, glb_ref,
             ws_ref, bst_ref, w_ref, dr_ref, after_ref,
             dh_ref, gbin_ref, g768_ref, gws_ref, gbst_ref, dkv_ref, gbkv_ref, gx_ref,
             dzbuf, dvnbuf, dhkeep):
        step = pl.program_id(0)
        live = step < tiles
        i = jnp.minimum(step, tiles - 1) % nt
        every = slice(None)
        keep_now = step % 2
        gain = jnp.where(live, 1.0, 0.0).astype(F32)

        @pl.when(step == 0)
        def _():
            gbin_ref[...] = jnp.zeros_like(gbin_ref)
            g768_ref[...] = jnp.zeros_like(g768_ref)
            gws_ref[...] = jnp.zeros_like(gws_ref)
            gbst_ref[...] = jnp.zeros_like(gbst_ref)
            gbkv_ref[...] = jnp.zeros_like(gbkv_ref)
            dhkeep[1] = jnp.zeros((tm, D_IN), BF16)

        @pl.when(live & (i == 0))
        def _():
            dkv_ref[...] = jnp.zeros_like(dkv_ref)

        prev = 1 - keep_now
        dx = ALPHA * dr_ref[...]
        for j in range(N_CHIPS):
            dx = dx + _dot_nt(dhkeep[prev, :, j * W_IN_SHARD:(j + 1) * W_IN_SHARD], w_ref[j])
        gx_ref[...] = dx

        def put(rows, col, width, val):
            vb = val.astype(BF16)
            dh_ref[rows, col:col + width] = vb
            dhkeep[keep_now, rows, col:col + width] = vb

        def emit(col, width, val):
            gbin_ref[:, col:col + width] += _colsum(val)
            put(every, col, width, val)

        d_c = dy_ref[:, 0:D_CONV].astype(F32) * gain
        cgate = _f32(h_ref, every, C_GATE, D_CONV)
        sg = _sigmoid(cgate)
        zhat, zrstd = _ln_stats(z_ref[...])
        zn = zhat * clg_ref[...] + clb_ref[...]
        szn = _sigmoid(zn)
        emit(C_GATE, D_CONV, d_c * (zn * szn) * _dsilu(cgate, sg))
        dzn = d_c * (cgate * sg) * _dsilu(zn, szn)
        g768_ref[ROW_CLG:ROW_CLG + 1, :] += _colsum(dzn * zhat)
        g768_ref[ROW_CLB:ROW_CLB + 1, :] += _colsum(dzn)
        dz = _ln_bwd(dzn * clg_ref[...], zhat, zrstd)
        g768_ref[ROW_CB:ROW_CB + 1, :] += _colsum(dz)

        follows = live & (i > 0)

        @pl.when(jnp.logical_not(follows))
        def _():
            dzbuf[0, tm:tm + HALO, :] = jnp.zeros((HALO, D_CONV), F32)

        @pl.when(follows)
        def _():
            dzbuf[0, tm:tm + HALO, :] = dzbuf[0, 0:HALO, :]

        dzbuf[0, 0:tm, :] = dz
        _shifted_planes(dzbuf, tm)
        a = _f32(h_ref, every, C_A, D_CONV)
        sgl = _sigmoid(_f32(h_ref, every, C_GLU, D_CONV))
        hc = a * sgl

        for r in range(tm // CONV_ROWS):
            base = r * CONV_ROWS
            acc = jnp.zeros((CONV_ROWS, D_CONV), F32)
            for k in range(CONV_WIDTH):
                acc = acc + cw_ref[k:k + 1, :] * _window(dzbuf, base + 30 - k, CONV_ROWS)
            dvnbuf[base:base + CONV_ROWS, :] = acc
        dhc = dvnbuf[...]
        emit(C_A, D_CONV, dhc * sgl)
        emit(C_GLU, D_CONV, dhc * a * sgl * (1.0 - sgl))
        for k in range(CONV_WIDTH):
            g768_ref[k:k + 1, :] += _colsum(hc * _window(dzbuf, 30 - k, tm))

        wsc, _ = _causal_ws(ws_ref)
        v, dgelu_v = _gelu_and_grad(_f32(h_ref, every, G_V, D_GMLP))
        vhat, vrstd = _ln_stats(v)
        vn = (vhat * glg_ref[...] + glb_ref[...]).astype(BF16)
        for ch in range(tm // CHUNK):
            rows = slice(ch * CHUNK, (ch + 1) * CHUNK)
            for hd in range(N_GHEADS):
                cols = slice(hd * CHUNK, (hd + 1) * CHUNK)
                vn_blk = vn[rows, cols]
                s = _dot(wsc[hd], vn_blk) + bst_ref[:, hd:hd + 1]
                u, dgelu_u = _gelu_and_grad(_f32(h_ref, rows, G_U + hd * CHUNK, CHUNK))
                gate = _f32(h_ref, rows, G_GATE + hd * CHUNK, CHUNK)
                sgate = _sigmoid(gate)
                d_g = dy_ref[rows, D_CONV + hd * CHUNK:D_CONV + (hd + 1) * CHUNK].astype(F32) * gain
                dgate = d_g * (u * s) * _dsilu(gate, sgate)
                gbin_ref[:, G_GATE + hd * CHUNK:G_GATE + (hd + 1) * CHUNK] += _colsum(dgate)
                put(rows, G_GATE + hd * CHUNK, CHUNK, dgate)
                dyg = d_g * (gate * sgate)
                du = dyg * s * dgelu_u
                gbin_ref[:, G_U + hd * CHUNK:G_U + (hd + 1) * CHUNK] += _colsum(du)
                put(rows, G_U + hd * CHUNK, CHUNK, du)
                ds = dyg * u
                dsb = ds.astype(BF16)
                gws_ref[hd] += _dot_nt(dsb, vn_blk)
                gbst_ref[:, hd:hd + 1] += jnp.sum(ds, axis=1, keepdims=True)
                dvnbuf[rows, cols] = _dot_tn(wsc[hd], dsb)
        dvn = dvnbuf[...]
        g768_ref[ROW_GLG:ROW_GLG + 1, :] += _colsum(dvn * vhat)
        g768_ref[ROW_GLB:ROW_GLB + 1, :] += _colsum(dvn)
        emit(G_V, D_GMLP, _ln_bwd(dvn * glg_ref[...], vhat, vrstd) * dgelu_v)

        scale = XHEAD ** -0.5
        for hd in range(N_XHEADS):
            q = h_ref[:, X_Q + hd * XHEAD:X_Q + (hd + 1) * XHEAD]
            k = kv_ref[:, hd * XHEAD:(hd + 1) * XHEAD]
            vv = kv_ref[:, D_XATT + hd * XHEAD:D_XATT + (hd + 1) * XHEAD]
            s = _dot_nt(q, k) * scale
            e = jnp.exp(s - jnp.max(s, axis=-1, keepdims=True))
            p = e * (1.0 / jnp.sum(e, axis=-1, keepdims=True))
            pb = p.astype(BF16)
            o = _dot(pb, vv)
            gate = _f32(h_ref, every, X_GATE + hd * XHEAD, XHEAD)
            sgate = _sigmoid(gate)
            d_x = dy_ref[:, 2 * D_CONV + hd * XHEAD:2 * D_CONV + (hd + 1) * XHEAD].astype(F32) * gain
            emit(X_GATE + hd * XHEAD, XHEAD, d_x * o * _dsilu(gate, sgate))
            do = (d_x * (gate * sgate)).astype(BF16)
            dp = _dot_nt(do, vv)
            dsc = (p * (dp - jnp.sum(dp * p, axis=-1, keepdims=True))).astype(BF16)
            emit(X_Q + hd * XHEAD, XHEAD, _dot(dsc, k) * scale)
            dkv_ref[:, hd * XHEAD:(hd + 1) * XHEAD] += _dot_tn(dsc, q) * scale
            dkv_ref[:, D_XATT + hd * XHEAD:D_XATT + (hd + 1) * XHEAD] += _dot_tn(pb, do)

        @pl.when(live & (i == nt - 1))
        def _():
            gbkv_ref[...] += _colsum(dkv_ref[...])

        @pl.when(step == tiles)
        def _():
            _, keep = _causal_ws(ws_ref)
            for hd in range(N_GHEADS):
                gws_ref[hd] = jnp.where(keep, gws_ref[hd], 0.0)

    def row(s):
        b, ri = place(s)
        return b * nt + ri, 0

    def row_dh(s):
        b, ri = place(s)
        return jnp.where(s < tiles, b * nt + ri, tiles), 0

    def row_prev(s):
        b, ri = place(jnp.maximum(s - 1, 0))
        return b * nt + ri, 0

    example = lambda s: (place(s)[0], 0)
    const2 = lambda s: (0, 0)
    const3 = lambda s: (0, 0, 0)
    vec = pl.BlockSpec((1, D_CONV), const2)

    return pl.pallas_call(
        body, name="branch_bwd", grid=(tiles + 1,),
        in_specs=[pl.BlockSpec((tm, D_IN), row),
                  pl.BlockSpec((tm, D_CONV), row), pl.BlockSpec((tm, D_MIX), row),
                  pl.BlockSpec((MEM_LEN, D_MODEL), example),
                  pl.BlockSpec((CONV_WIDTH, D_CONV), const2), vec, vec, vec, vec,
                  pl.BlockSpec((N_GHEADS, CHUNK, CHUNK), const3),
                  pl.BlockSpec((CHUNK, N_GHEADS), const2),
                  pl.BlockSpec((N_CHIPS, D_MODEL, W_IN_SHARD), const3, pipeline_mode=pl.Buffered(1)),
                  pl.BlockSpec((tm, D_MODEL), row_prev), ANY],
        out_specs=[pl.BlockSpec((tm, D_IN), row_dh),
                   pl.BlockSpec((1, D_IN), const2),
                   pl.BlockSpec((40, D_CONV), const2),
                   pl.BlockSpec((N_GHEADS, CHUNK, CHUNK), const3),
                   pl.BlockSpec((CHUNK, CHUNK), const2),
                   pl.BlockSpec((MEM_LEN, D_MODEL), example),
                   pl.BlockSpec((1, D_MODEL), const2),
                   pl.BlockSpec((tm, D_MODEL), row_prev)],
        out_shape=[jax.ShapeDtypeStruct((n + tm, D_IN), BF16),
                   jax.ShapeDtypeStruct((1, D_IN), F32),
                   jax.ShapeDtypeStruct((40, D_CONV), F32),
                   jax.ShapeDtypeStruct((N_GHEADS, CHUNK, CHUNK), F32),
                   jax.ShapeDtypeStruct((CHUNK, CHUNK), F32),
                   jax.ShapeDtypeStruct((nb * MEM_LEN, D_MODEL), F32),
                   jax.ShapeDtypeStruct((1, D_MODEL), F32),
                   jax.ShapeDtypeStruct((n, D_MODEL), F32)],
        scratch_shapes=[pltpu.VMEM((SUBLANES, tm + HALO, D_CONV), F32), pltpu.VMEM((tm, D_CONV), F32),
                        pltpu.VMEM((2, tm, D_IN), BF16)],
        compiler_params=_params(("arbitrary",)),
    )(h, z, dy, kv, cw, clg, clb, glg, glb, ws, bs_t, w_full, dr, after)


def _grad_w(a, b, tk, name):
    kdim, m = a.shape
    ncols = b.shape[1]
    nk = kdim // tk
    shard = m // N_CHIPS
    oshape = (shard, ncols)
    a_spec = pl.BlockSpec((tk, shard), lambda j, kk: (kk, j))
    b_spec = pl.BlockSpec((tk, ncols), lambda j, kk: (kk, 0))

    def body(a_ref, b_ref, own_ref, ob_ref, acc):
        j = pl.program_id(0)
        kk = pl.program_id(1)
        mine = 2 * lax.axis_index("x") + lax.axis_index("y")

        @pl.when(kk == 0)
        def _():
            acc[...] = jnp.zeros_like(acc)

        acc[...] += _dot_tn(a_ref[...].astype(BF16), b_ref[...].astype(BF16))

        @pl.when(kk == nk - 1)
        def _():
            ob_ref[...] = acc[...].astype(BF16)

        @pl.when((kk == nk - 1) & (j == mine))
        def _():
            own_ref[...] = acc[...]

    return pl.pallas_call(
        body, name=name, grid=(N_CHIPS, nk),
        in_specs=[a_spec, b_spec],
        out_specs=[pl.BlockSpec(oshape, lambda j, kk: (0, 0)),
                   pl.BlockSpec((None,) + oshape, lambda j, kk: (j, 0, 0))],
        out_shape=[jax.ShapeDtypeStruct(oshape, F32), jax.ShapeDtypeStruct((N_CHIPS,) + oshape, BF16)],
        scratch_shapes=[pltpu.VMEM(oshape, F32)],
        compiler_params=_params(("arbitrary", "arbitrary")),
    )(a, b)


def _grad_w_block(a, b, block, tk, name, dtype, after):
    kdim, m = a.shape
    shard = b.shape[1] // N_CHIPS
    nk = kdim // tk

    def body(block_ref, a_ref, b_ref, after_ref, o_ref, acc):
        kk = pl.program_id(0)

        @pl.when(kk == 0)
        def _():
            acc[...] = jnp.zeros_like(acc)

        acc[...] += _dot_tn(a_ref[...].astype(BF16), b_ref[...])

        @pl.when(kk == nk - 1)
        def _():
            o_ref[...] = acc[...].astype(dtype)

    grid_spec = pltpu.PrefetchScalarGridSpec(
        num_scalar_prefetch=1, grid=(nk,),
        in_specs=[pl.BlockSpec((tk, m), lambda kk, blk: (kk, 0)),
                  pl.BlockSpec((tk, shard), lambda kk, blk: (kk, blk[0])), ANY],
        out_specs=pl.BlockSpec((m, shard), lambda kk, blk: (0, 0)),
        scratch_shapes=[pltpu.VMEM((m, shard), F32)])
    return pl.pallas_call(
        body, name=name, grid_spec=grid_spec, out_shape=jax.ShapeDtypeStruct((m, shard), dtype),
        compiler_params=_params(("arbitrary",)),
    )(block, a, b, after)


def _sum_small(owns, gots):
    n = len(owns)

    def body(*refs):
        for a in range(n):
            o_ref, g_ref, out_ref = refs[a], refs[n + a], refs[2 * n + a]
            out_ref[...] = (o_ref[...] + g_ref[1]) + (g_ref[0] + g_ref[2])

    return pl.pallas_call(
        body, name="sum_small", out_shape=[jax.ShapeDtypeStruct(o.shape, F32) for o in owns],
        compiler_params=pltpu.CompilerParams(vmem_limit_bytes=VMEM_LIMIT),
    )(*owns, *gots)


def _sum_chips(own, gots, tr, name):
    r, ccols = own.shape

    def body(o_ref, gx_ref, gy_ref, gxy_ref, out_ref):
        out_ref[...] = (o_ref[...] + gy_ref[...].astype(F32)) + (gx_ref[...].astype(F32) + gxy_ref[...].astype(F32))

    return pl.pallas_call(
        body, name=name, grid=(r // tr,),
        in_specs=[pl.BlockSpec((tr, ccols), lambda i: (i, 0))]
        + [pl.BlockSpec((None, tr, ccols), lambda i, slot=slot: (slot, i, 0)) for _, slot in gots],
        out_specs=pl.BlockSpec((tr, ccols), lambda i: (i, 0)),
        out_shape=jax.ShapeDtypeStruct((r, ccols), F32),
        compiler_params=_params(("arbitrary",)),
    )(own, *[g for g, _ in gots])


def _exchange_cores(parts):
    npart = len(parts)

    def body(*refs):
        in_refs = refs[0:npart]
        out_refs = refs[npart:2 * npart]
        send_sems, recv_sems = refs[2 * npart:]
        sibling = (lax.axis_index("x"), lax.axis_index("y"), 1 - lax.axis_index("c"))
        copies = [pltpu.make_async_remote_copy(
            src_ref=in_refs[a], dst_ref=out_refs[a], send_sem=send_sems.at[a], recv_sem=recv_sems.at[a],
            device_id=sibling, device_id_type=MESH_ID) for a in range(npart)]
        for cp in copies:
            cp.start()
        for cp in copies:
            cp.wait_recv()
        for cp in copies:
            cp.wait_send()

    return pl.pallas_call(
        body, name="exchange_cores", out_shape=[jax.ShapeDtypeStruct(p.shape, p.dtype) for p in parts],
        in_specs=[ANY] * npart, out_specs=[ANY] * npart,
        scratch_shapes=[pltpu.SemaphoreType.DMA((npart,)), pltpu.SemaphoreType.DMA((npart,))],
    )(*parts)


def _adam(g, w, m, v):
    mn = ADAM_B1 * m + (1.0 - ADAM_B1) * g
    vn = ADAM_B2 * v + (1.0 - ADAM_B2) * (g * g)
    return g, -ADAM_LR * ((mn / BC1) / (jnp.sqrt(vn / BC2) + ADAM_EPS) + ADAM_WD * w), mn, vn


def _adamw(a, b, w, m, v, tr, name):
    r, ccols = w.shape

    def body(a_ref, b_ref, w_ref, m_ref, v_ref, *outs):
        res = _adam(a_ref[...] + b_ref[...], w_ref[...], m_ref[...], v_ref[...])
        for which in range(4):
            outs[which][...] = res[which]

    spec = pl.BlockSpec((tr, ccols), lambda i: (i, 0))
    shape = jax.ShapeDtypeStruct((r, ccols), F32)
    return pl.pallas_call(
        body, name=name, grid=(r // tr,), in_specs=[spec] * 5, out_specs=[spec] * 4, out_shape=[shape] * 4,
        compiler_params=_params(("arbitrary",)),
    )(a, b, w, m, v)


SMALL = ["b_in", "conv_b", "conv_ln_g", "conv_ln_b", "gmlp_ln_g", "gmlp_ln_b", "gmlp_ws", "gmlp_bs", "b_kv", "b_out",
         "ln_g", "ln_b"]


def _adamw_small(a_parts, b_parts, params):
    nparts, nparams = len(a_parts), len(params)

    def body(*refs):
        a = refs[0:nparts]
        b = refs[nparts:2 * nparts]
        prm = refs[2 * nparts:2 * nparts + 3 * nparams]
        outs = refs[2 * nparts + 3 * nparams:]
        gb_in, g768, gws, gbs_t, gb_kv, vec3 = [a[q][...] + b[q][...] for q in range(nparts)]
        grads = [gb_in, g768[ROW_CB:ROW_CB + 1], g768[ROW_CLG:ROW_CLG + 1], g768[ROW_CLB:ROW_CLB + 1],
                 g768[ROW_GLG:ROW_GLG + 1], g768[ROW_GLB:ROW_GLB + 1], gws, jnp.transpose(gbs_t)[0:N_GHEADS, :],
                 gb_kv, vec3[0:1], vec3[1:2], vec3[2:3]]
        for q, g in enumerate(grads):
            res = _adam(g, prm[3 * q][...], prm[3 * q + 1][...], prm[3 * q + 2][...])
            for which in range(4):
                outs[4 * q + which][...] = res[which]

    flat = [t for p in params for t in p]
    out_shape = [jax.ShapeDtypeStruct(p[0].shape, F32) for p in params for _ in range(4)]
    return pl.pallas_call(
        body, name="adamw_small", out_shape=out_shape,
        compiler_params=pltpu.CompilerParams(vmem_limit_bytes=VMEM_LIMIT),
    )(*a_parts, *b_parts, *flat)


def kernel(x, mem, w_in, b_in, conv_w, conv_b, conv_ln_g, conv_ln_b, gmlp_ln_g, gmlp_ln_b, gmlp_ws, gmlp_bs, w_kv, b_kv, w_out, b_out, ln_g, ln_b, loss_target, m_w_in, m_b_in, m_conv_w, m_conv_b, m_conv_ln_g, m_conv_ln_b, m_gmlp_ln_g, m_gmlp_ln_b, m_gmlp_ws, m_gmlp_bs, m_w_kv, m_b_kv, m_w_out, m_b_out, m_ln_g, m_ln_b, v_w_in, v_b_in, v_conv_w, v_conv_b, v_conv_ln_g, v_conv_ln_b, v_gmlp_ln_g, v_gmlp_ln_b, v_gmlp_ws, v_gmlp_bs, v_w_kv, v_b_kv, v_w_out, v_b_out, v_ln_g, v_ln_b):
    weights = dict(b_in=b_in, conv_b=conv_b, conv_ln_g=conv_ln_g, conv_ln_b=conv_ln_b, gmlp_ln_g=gmlp_ln_g,
                   gmlp_ln_b=gmlp_ln_b, gmlp_ws=gmlp_ws, gmlp_bs=gmlp_bs, b_kv=b_kv, b_out=b_out, ln_g=ln_g, ln_b=ln_b)
    mom_m = dict(b_in=m_b_in, conv_b=m_conv_b, conv_ln_g=m_conv_ln_g, conv_ln_b=m_conv_ln_b, gmlp_ln_g=m_gmlp_ln_g,
                 gmlp_ln_b=m_gmlp_ln_b, gmlp_ws=m_gmlp_ws, gmlp_bs=m_gmlp_bs, b_kv=m_b_kv, b_out=m_b_out,
                 ln_g=m_ln_g, ln_b=m_ln_b)
    mom_v = dict(b_in=v_b_in, conv_b=v_conv_b, conv_ln_g=v_conv_ln_g, conv_ln_b=v_conv_ln_b, gmlp_ln_g=v_gmlp_ln_g,
                 gmlp_ln_b=v_gmlp_ln_b, gmlp_ws=v_gmlp_ws, gmlp_bs=v_gmlp_bs, b_kv=v_b_kv, b_out=v_b_out,
                 ln_g=v_ln_g, ln_b=v_ln_b)
    nb, seq, _ = x.shape
    n = nb * seq
    tm = 256
    tk = min(1024, n)
    x2 = x.reshape(n, D_MODEL)
    tgt2 = loss_target.reshape(n, D_MODEL)
    mem2 = mem.reshape(nb * MEM_LEN, D_MODEL)
    chip = 2 * lax.axis_index("x") + lax.axis_index("y")
    bs_t = jnp.transpose(gmlp_bs[0])

    own_kv = [w_kv[0].astype(BF16), conv_w[0]]
    own_out = [w_out[0].astype(BF16)]
    ga = _start_exchange("gather_kv_start", own_kv, [False] * 2)
    h, win_g = _in_proj(x2, w_in[0].astype(BF16), b_in, min(512, n), ga["token"])
    gb = _start_exchange("gather_out_start", own_out, [False])
    wkv_g, cw_g = _place_shards(own_kv, _wait_exchange("gather_kv_wait", ga, h), "place_kv")
    wkv_full = wkv_g.reshape(D_MODEL, D_MODEL)
    cw_full = jnp.transpose(cw_g, (1, 0, 2)).reshape(CONV_WIDTH, D_CONV)
    kv = _kv_proj(mem2, wkv_full, b_kv)
    ycat, z = _branch_fwd(h, kv, cw_full, conv_b, conv_ln_g, conv_ln_b, gmlp_ln_g, gmlp_ln_b, gmlp_ws[0], bs_t,
                          nb, seq, tm, gb["token"])
    (wout_g,) = _place_shards(own_out, _wait_exchange("gather_out_wait", gb, ycat), "place_out")
    wout_full = wout_g.reshape(D_MIX, D_MODEL)
    dr, dycat, vec3, loss_part = _out_proj_loss(ycat, wout_full, b_out, ln_g, ln_b, x2, tgt2, tm)
    loss = lax.psum(loss_part[0, 0], ("x", "y", "c"))

    own_wout, gwout_b = _grad_w(ycat, dr, tk, "grad_w_out")
    ex1 = _start_exchange("exchange1_start", [gwout_b, vec3], [True, False])
    dh, gb_in, g768, gws, gbs_t, dkv, gb_kv, grad_x2 = _branch_bwd(
        h, z, dycat, kv, cw_full, conv_ln_g, conv_ln_b, gmlp_ln_g, gmlp_ln_b, gmlp_ws[0], bs_t, win_g, dr,
        nb, seq, tm, ex1["token"])
    own_wkv, gwkv_b = _grad_w(mem2, dkv, nb * MEM_LEN, "grad_w_kv")
    small2 = [gb_in, g768, gws, gbs_t, gb_kv]
    ex2 = _start_exchange("exchange2_start", [gwkv_b] + small2, [True] + [False] * 5)
    block_of = lambda flip_bits: (chip ^ flip_bits).astype(jnp.int32).reshape(1)
    after, ex3 = ex2["token"], {}
    for flip, bits in ((2, 3), (1, 1), (0, 2)):
        part = _grad_w_block(x2, dh, block_of(bits), tk, f"grad_w_in_{flip}", BF16, after)
        ex3[flip] = _start_exchange(f"exchange3{flip}_start", [part], [False], (flip,))
        after = ex3[flip]["token"]
    own_win = _grad_w_block(x2, dh, block_of(0), tk, "grad_w_in_own", F32, after)
    got_wout, got_vec3 = _wait_exchange("exchange1_wait", ex1, own_win)
    got2 = _wait_exchange("exchange2_wait", ex2, own_win)
    got_win = {flip: _wait_exchange(f"exchange3{flip}_wait", ex3[flip], own_win)[0] for flip in (2, 1, 0)}

    sum_win = _sum_chips(own_win, [(got_win[0], 0), (got_win[1], 0), (got_win[2], 0)], 256, "sum_w_in")
    sum_wout = _sum_chips(own_wout, [(got_wout, 0), (got_wout, 1), (got_wout, 2)], 256, "sum_w_out")
    sum_wkv = _sum_chips(own_wkv, [(got2[0], 0), (got2[0], 1), (got2[0], 2)], 256, "sum_w_kv")
    sum_small = list(_sum_small(small2 + [vec3], got2[1:] + [got_vec3]))
    sib = list(_exchange_cores([sum_win, sum_wout, sum_wkv] + sum_small))

    big = {}
    big["w_in"] = _adamw(sum_win, sib[0], w_in[0], m_w_in[0], v_w_in[0], 256, "adamw_w_in")
    big["w_out"] = _adamw(sum_wout, sib[1], w_out[0], m_w_out[0], v_w_out[0], 256, "adamw_w_out")
    big["w_kv"] = _adamw(sum_wkv, sib[2], w_kv[0], m_w_kv[0], v_w_kv[0], 256, "adamw_w_kv")
    lead = {"gmlp_ws", "gmlp_bs"}
    strip = lambda k, t: t[0] if k in lead else t
    sm = _adamw_small(sum_small, sib[3:], [tuple(strip(k, t[k]) for t in (weights, mom_m, mom_v)) for k in SMALL])
    small_out = {k: [sm[4 * q + which][None] if k in lead else sm[4 * q + which] for which in range(4)]
                 for q, k in enumerate(SMALL)}
    cw_a = lax.dynamic_slice_in_dim(sum_small[1][0:CONV_WIDTH + 1], chip * CONV_SHARD, CONV_SHARD, axis=1)
    cw_b = lax.dynamic_slice_in_dim(sib[4][0:CONV_WIDTH + 1], chip * CONV_SHARD, CONV_SHARD, axis=1)
    cwp = ((0, 1), (0, 0))
    cw_out = _adamw(cw_a, cw_b, jnp.pad(conv_w[0], cwp), jnp.pad(m_conv_w[0], cwp), jnp.pad(v_conv_w[0], cwp),
                    CONV_WIDTH + 1, "adamw_conv_w")

    order = ["w_in", "b_in", "conv_w", "conv_b", "conv_ln_g", "conv_ln_b", "gmlp_ln_g", "gmlp_ln_b", "gmlp_ws",
             "gmlp_bs", "w_kv", "b_kv", "w_out", "b_out", "ln_g", "ln_b"]
    result = [loss, grad_x2.reshape(nb, seq, D_MODEL)]
    for which in range(4):
        for k in order:
            if k in big:
                result.append(big[k][which][None])
            elif k == "conv_w":
                result.append(cw_out[which][0:CONV_WIDTH][None])
            else:
                result.append(small_out[k][which])
    return tuple(result)
```

```python
import functools
import math

import jax
import jax.numpy as jnp
from jax import lax
from jax.experimental import pallas as pl
from jax.experimental.pallas import tpu as pltpu

F32 = jnp.float32
BF16 = jnp.bfloat16

D_MODEL = 1024
MEM_LEN = 256
D_MIX = 2048
D_CONV = 768
D_GMLP = 768
D_XATT = 512
N_XHEADS = 4
XHEAD = 128
CONV_WIDTH = 31
CHUNK = 128
N_GHEADS = 6
D_IN = 3 * D_CONV + 3 * D_GMLP + 2 * D_XATT
ALPHA = 2.0 ** 0.25
LN_EPS = 1e-5
N_CHIPS = 4
W_IN_SHARD = D_IN // N_CHIPS
W_OUT_SHARD = D_MIX // N_CHIPS
W_KV_SHARD = D_MODEL // N_CHIPS
CONV_SHARD = D_CONV // N_CHIPS
HALO = 32

C_A, C_GLU, C_GATE = 0, 768, 1536
G_U, G_V, G_GATE = 2304, 3072, 3840
X_Q, X_GATE = 4608, 5120

ADAM_LR = 0.001
ADAM_B1 = 0.9
ADAM_B2 = 0.999
ADAM_EPS = 1e-08
ADAM_WD = 0.01
ADAM_STEP = 10
BC1 = 1.0 - ADAM_B1 ** ADAM_STEP
BC2 = 1.0 - ADAM_B2 ** ADAM_STEP

VMEM_LIMIT = 56 * 1024 * 1024
MESH_ID = pl.DeviceIdType.MESH
ANY = pl.BlockSpec(memory_space=pl.ANY)

GELU_C0 = math.sqrt(2.0 / math.pi)
GELU_C1 = 0.044715


def _sigmoid(v):
    return 0.5 + 0.5 * jnp.tanh(0.5 * v)


def _f32(ref, rows, col, width):
    return ref[rows, col:col + width].astype(F32)


def _dsilu(v, s):
    return s * (1.0 + v * (1.0 - s))


def _gelu_and_grad(v):
    t = jnp.tanh(GELU_C0 * (v + GELU_C1 * v * v * v))
    g = 0.5 * v * (1.0 + t)
    dg = 0.5 * (1.0 + t) + 0.5 * v * (1.0 - t * t) * (GELU_C0 * (1.0 + 3.0 * GELU_C1 * v * v))
    return g, dg


def _gelu(v):
    return 0.5 * v * (1.0 + jnp.tanh(GELU_C0 * (v + GELU_C1 * v * v * v)))


def _ln_stats(v):
    mu = jnp.mean(v, axis=-1, keepdims=True)
    vc = v - mu
    var = jnp.mean(vc * vc, axis=-1, keepdims=True)
    rstd = lax.rsqrt(var + LN_EPS)
    return vc * rstd, rstd


def _ln_bwd(dvhat, vhat, rstd):
    m1 = jnp.mean(dvhat, axis=-1, keepdims=True)
    m2 = jnp.mean(dvhat * vhat, axis=-1, keepdims=True)
    return rstd * (dvhat - m1 - vhat * m2)


def _colsum(v):
    return jnp.sum(v, axis=0, keepdims=True)


def _dot(a, b):
    return jnp.dot(a, b, preferred_element_type=F32)


def _dot_nt(a, b):
    return lax.dot_general(a, b, (((1,), (1,)), ((), ())), preferred_element_type=F32)


def _dot_tn(a, b):
    return lax.dot_general(a, b, (((0,), (0,)), ((), ())), preferred_element_type=F32)


def _causal_ws(ws_ref):
    row = lax.broadcasted_iota(jnp.int32, (CHUNK, CHUNK), 0)
    col = lax.broadcasted_iota(jnp.int32, (CHUNK, CHUNK), 1)
    keep = col <= row
    return [jnp.where(keep, ws_ref[hd], 0.0).astype(BF16) for hd in range(N_GHEADS)], keep


def _params(sem):
    return pltpu.CompilerParams(dimension_semantics=sem, vmem_limit_bytes=VMEM_LIMIT)


def _peer_chips():
    x, y, c = lax.axis_index("x"), lax.axis_index("y"), lax.axis_index("c")
    return [(1 - x, y), (x, 1 - y), (1 - x, 1 - y)], 2 * x + y, c


HBM = pl.BlockSpec(memory_space=pltpu.HBM)
SEM = pl.BlockSpec(memory_space=pltpu.SEMAPHORE)
EFFECT = pltpu.SideEffectType.DATAFLOW_SIDE_EFFECTING
ALL_FLIPS = (0, 1, 2)


def _exchange_copies(src_refs, land_refs, per_chip, flips, send_sems, recv_sems):
    chips, _, c = _peer_chips()
    n = len(src_refs)
    copies = []
    for q, p in enumerate(flips):
        px, py = chips[p]
        for a in range(n):
            src = src_refs[a].at[2 * px + py] if per_chip[a] else src_refs[a]
            copies.append(pltpu.make_async_remote_copy(
                src_ref=src, dst_ref=land_refs[a].at[q], send_sem=send_sems.at[n * q + a],
                recv_sem=recv_sems.at[n * q + a], device_id=(px, py, c), device_id_type=MESH_ID))
    return copies


def _start_exchange(name, srcs, per_chip, flips=ALL_FLIPS):
    n = len(srcs)
    nf = len(flips)
    lands = [lax.empty((nf,) + (s.shape[1:] if pc else s.shape), s.dtype) for s, pc in zip(srcs, per_chip)]

    def body(*refs):
        src_refs, land_refs = refs[0:n], refs[n:2 * n]
        send_sems, recv_sems = refs[2 * n], refs[2 * n + 1]
        token = refs[4 * n + 2]
        for cp in _exchange_copies(src_refs, land_refs, per_chip, flips, send_sems, recv_sems):
            cp.start()
        token[...] = jnp.zeros_like(token)

    out = pl.pallas_call(
        body, name=name,
        out_shape=(pltpu.SemaphoreType.DMA((nf * n,)), pltpu.SemaphoreType.DMA((nf * n,)),
                   *[pltpu.HBM(a.shape, a.dtype) for a in srcs + lands], jax.ShapeDtypeStruct((8, 128), F32)),
        in_specs=[HBM] * (2 * n),
        out_specs=(SEM, SEM, *[HBM] * (2 * n), pl.BlockSpec(memory_space=pltpu.VMEM)),
        input_output_aliases={a: 2 + a for a in range(2 * n)},
        compiler_params=pltpu.CompilerParams(has_side_effects=EFFECT),
    )(*[pltpu.with_memory_space_constraint(a, pltpu.HBM) for a in srcs + lands])
    return dict(send=out[0], recv=out[1], thru=list(out[2:2 * n + 2]), token=out[2 * n + 2], per_chip=per_chip,
                flips=flips)


def _wait_exchange(name, started, after):
    thru, per_chip, flips = started["thru"], started["per_chip"], started["flips"]
    n = len(thru) // 2

    def body(*refs):
        src_refs, land_refs = refs[0:n], refs[n:2 * n]
        send_sems, recv_sems = refs[2 * n], refs[2 * n + 1]
        for cp in _exchange_copies(src_refs, land_refs, per_chip, flips, send_sems, recv_sems):
            cp.wait_send()
            cp.wait_recv()

    out = pl.pallas_call(
        body, name=name, out_shape=tuple(pltpu.HBM(a.shape, a.dtype) for a in thru),
        in_specs=[HBM] * (2 * n) + [SEM, SEM, ANY], out_specs=tuple([HBM] * (2 * n)),
        input_output_aliases={a: a for a in range(2 * n)},
        compiler_params=pltpu.CompilerParams(has_side_effects=EFFECT),
    )(*thru, started["send"], started["recv"], after)
    return list(out[n:2 * n])


def _place_shards(owns, landeds, name):
    n = len(owns)
    mine = (2 * lax.axis_index("x") + lax.axis_index("y")).astype(jnp.int32).reshape(1)

    def body(mine_ref, *refs):
        own_refs, land_refs, out_refs = refs[0:n], refs[n:2 * n], refs[2 * n:3 * n]
        k = pl.program_id(0)
        for a in range(n):
            @pl.when(k == mine_ref[0])
            def _():
                out_refs[a][...] = own_refs[a][...]

            @pl.when(k != mine_ref[0])
            def _():
                out_refs[a][...] = land_refs[a][...]

    def slot(k, mine_ref):
        d = k ^ mine_ref[0]
        return jnp.where(d == 1, 1, jnp.where(d == 3, 2, 0))

    zeros = lambda o: (0,) * len(o.shape)
    grid_spec = pltpu.PrefetchScalarGridSpec(
        num_scalar_prefetch=1, grid=(N_CHIPS,),
        in_specs=[pl.BlockSpec(o.shape, lambda k, m, o=o: zeros(o)) for o in owns]
        + [pl.BlockSpec((None,) + o.shape, lambda k, m, o=o: (slot(k, m),) + zeros(o)) for o in owns],
        out_specs=[pl.BlockSpec((None,) + o.shape, lambda k, m, o=o: (k,) + zeros(o)) for o in owns])
    return pl.pallas_call(
        body, name=name, grid_spec=grid_spec,
        out_shape=[jax.ShapeDtypeStruct((N_CHIPS,) + o.shape, o.dtype) for o in owns],
        compiler_params=_params(("arbitrary",)),
    )(mine, *owns, *landeds)


def _in_proj(x2, w_shard, b_in, tm, after):
    n = x2.shape[0]
    nt = n // tm
    chip = 2 * lax.axis_index("x") + lax.axis_index("y")
    order = jnp.stack([chip, chip ^ 1, chip ^ 2, chip ^ 3]).astype(jnp.int32)
    flip_of_step = {1: 1, 2: 0, 3: 2}

    def body(order_ref, x_ref, w_ref, b_ref, after_ref, h_ref, wg_ref, wbuf, send_sems, recv_sems, loc_sems):
        s = pl.program_id(0)
        i = pl.program_id(1)
        chips, mine, c = _peer_chips()

        def remote(p, slot):
            px, py = chips[p]
            return pltpu.make_async_remote_copy(
                src_ref=w_ref, dst_ref=wg_ref.at[slot], send_sem=send_sems.at[p], recv_sem=recv_sems.at[p],
                device_id=(px, py, c), device_id_type=MESH_ID)

        own = pltpu.make_async_copy(w_ref, wg_ref.at[mine], loc_sems.at[2])

        @pl.when((s == 0) & (i == 0))
        def _():
            remote(1, mine).start()
            remote(0, mine).start()
            own.start()
            first = pltpu.make_async_copy(w_ref, wbuf.at[0], loc_sems.at[0])
            first.start()
            first.wait()

        for step, p in flip_of_step.items():
            @pl.when((s == step) & (i == 0))
            def _():
                px, py = chips[p]
                remote(p, 2 * px + py).wait_recv()
                if step == 1:
                    remote(2, mine).start()
                load = pltpu.make_async_copy(wg_ref.at[2 * px + py], wbuf.at[step % 2], loc_sems.at[step % 2])
                load.start()
                load.wait()

        h_ref[...] = (_dot(x_ref[...].astype(BF16), wbuf[s % 2]) + b_ref[...]).astype(BF16)

        @pl.when((s == N_CHIPS - 1) & (i == nt - 1))
        def _():
            for p in range(3):
                remote(p, mine).wait_send()
            own.wait()

    grid_spec = pltpu.PrefetchScalarGridSpec(
        num_scalar_prefetch=1, grid=(N_CHIPS, nt),
        in_specs=[pl.BlockSpec((tm, D_MODEL), lambda s, i, o: (i, 0)), ANY,
                  pl.BlockSpec((1, W_IN_SHARD), lambda s, i, o: (0, o[s])), ANY],
        out_specs=[pl.BlockSpec((tm, W_IN_SHARD), lambda s, i, o: (i, o[s])), ANY],
        scratch_shapes=[pltpu.VMEM((2, D_MODEL, W_IN_SHARD), BF16), pltpu.SemaphoreType.DMA((3,)),
                        pltpu.SemaphoreType.DMA((3,)), pltpu.SemaphoreType.DMA((3,))])
    return pl.pallas_call(
        body, name="in_proj", grid_spec=grid_spec,
        out_shape=[jax.ShapeDtypeStruct((n, D_IN), BF16),
                   jax.ShapeDtypeStruct((N_CHIPS, D_MODEL, W_IN_SHARD), BF16)],
        compiler_params=_params(("arbitrary", "arbitrary")),
    )(order, x2, w_shard, b_in, after)


def _kv_proj(mem2, wkv_full, b_kv):
    m = mem2.shape[0]

    def body(m_ref, w_ref, b_ref, o_ref):
        o_ref[...] = (_dot(m_ref[...].astype(BF16), w_ref[...]) + b_ref[...]).astype(BF16)

    return pl.pallas_call(
        body, name="kv_proj", grid=(m // MEM_LEN,),
        in_specs=[pl.BlockSpec((MEM_LEN, D_MODEL), lambda i: (i, 0)),
                  pl.BlockSpec((D_MODEL, D_MODEL), lambda i: (0, 0)),
                  pl.BlockSpec((1, D_MODEL), lambda i: (0, 0))],
        out_specs=pl.BlockSpec((MEM_LEN, D_MODEL), lambda i: (i, 0)),
        out_shape=jax.ShapeDtypeStruct((m, D_MODEL), BF16),
        compiler_params=_params(("arbitrary",)),
    )(mem2, wkv_full, b_kv)


CONV_ROWS = 16
SUBLANES = 8


def _shifted_planes(buf, tm):
    rows = tm + HALO - SUBLANES
    for s in range(1, SUBLANES):
        buf[s, 0:rows, :] = buf[0, s:s + rows, :]


def _window(buf, start, rows):
    s = start % SUBLANES
    return buf[s, start - s:start - s + rows, :]


def _branch_fwd(h, kv, cw, cb, cg, cbeta, gg_, gb_, ws, bs_t, nb, seq, tm, after):
    nt = seq // tm
    n = nb * seq

    def body(h_ref, kv_ref, cw_ref, cb_ref, clg_ref, clb_ref, glg_ref, glb_ref, ws_ref, bst_ref, after_ref,
             y_ref, z_ref, hcbuf):
        i = pl.program_id(1)

        @pl.when(i == 0)
        def _():
            hcbuf[0, 0:HALO, :] = jnp.zeros((HALO, D_CONV), F32)

        @pl.when(i > 0)
        def _():
            hcbuf[0, 0:HALO, :] = hcbuf[0, tm:tm + HALO, :]

        every = slice(None)
        hcbuf[0, HALO:HALO + tm, :] = _f32(h_ref, every, C_A, D_CONV) * _sigmoid(_f32(h_ref, every, C_GLU, D_CONV))
        _shifted_planes(hcbuf, tm)
        for r in range(tm // CONV_ROWS):
            base = r * CONV_ROWS
            acc = jnp.broadcast_to(cb_ref[...], (CONV_ROWS, D_CONV))
            for k in range(CONV_WIDTH):
                acc = acc + cw_ref[k:k + 1, :] * _window(hcbuf, base + 2 + k, CONV_ROWS)
            z_ref[base:base + CONV_ROWS, :] = acc
        zhat, _ = _ln_stats(z_ref[...])
        zn = zhat * clg_ref[...] + clb_ref[...]
        cgate = _f32(h_ref, every, C_GATE, D_CONV)
        y_ref[:, 0:D_CONV] = (zn * _sigmoid(zn) * (cgate * _sigmoid(cgate))).astype(BF16)

        wsc, _ = _causal_ws(ws_ref)
        vhat, _ = _ln_stats(_gelu(_f32(h_ref, every, G_V, D_GMLP)))
        vn = (vhat * glg_ref[...] + glb_ref[...]).astype(BF16)
        for ch in range(tm // CHUNK):
            rows = slice(ch * CHUNK, (ch + 1) * CHUNK)
            for hd in range(N_GHEADS):
                cols = slice(hd * CHUNK, (hd + 1) * CHUNK)
                s = _dot(wsc[hd], vn[rows, cols]) + bst_ref[:, hd:hd + 1]
                u = _gelu(_f32(h_ref, rows, G_U + hd * CHUNK, CHUNK))
                gate = _f32(h_ref, rows, G_GATE + hd * CHUNK, CHUNK)
                y_ref[rows, D_CONV + hd * CHUNK:D_CONV + (hd + 1) * CHUNK] = (
                    u * s * (gate * _sigmoid(gate))).astype(BF16)

        scale = XHEAD ** -0.5
        for hd in range(N_XHEADS):
            q = h_ref[:, X_Q + hd * XHEAD:X_Q + (hd + 1) * XHEAD]
            k = kv_ref[:, hd * XHEAD:(hd + 1) * XHEAD]
            v = kv_ref[:, D_XATT + hd * XHEAD:D_XATT + (hd + 1) * XHEAD]
            s = _dot_nt(q, k) * scale
            e = jnp.exp(s - jnp.max(s, axis=-1, keepdims=True))
            p = e * (1.0 / jnp.sum(e, axis=-1, keepdims=True))
            o = _dot(p.astype(BF16), v)
            gate = _f32(h_ref, every, X_GATE + hd * XHEAD, XHEAD)
            y_ref[:, 2 * D_CONV + hd * XHEAD:2 * D_CONV + (hd + 1) * XHEAD] = (
                o * (gate * _sigmoid(gate))).astype(BF16)

    row = lambda b, i: (b * nt + i, 0)
    const2 = lambda b, i: (0, 0)
    vec = pl.BlockSpec((1, D_CONV), const2)
    return pl.pallas_call(
        body, name="branch_fwd", grid=(nb, nt),
        in_specs=[pl.BlockSpec((tm, D_IN), row),
                  pl.BlockSpec((MEM_LEN, D_MODEL), lambda b, i: (b, 0)),
                  pl.BlockSpec((CONV_WIDTH, D_CONV), const2), vec, vec, vec, vec, vec,
                  pl.BlockSpec((N_GHEADS, CHUNK, CHUNK), lambda b, i: (0, 0, 0)),
                  pl.BlockSpec((CHUNK, N_GHEADS), const2), ANY],
        out_specs=[pl.BlockSpec((tm, D_MIX), row), pl.BlockSpec((tm, D_CONV), row)],
        out_shape=[jax.ShapeDtypeStruct((n, D_MIX), BF16), jax.ShapeDtypeStruct((n, D_CONV), F32)],
        scratch_shapes=[pltpu.VMEM((SUBLANES, HALO + tm, D_CONV), F32)],
        compiler_params=_params(("arbitrary", "arbitrary")),
    )(h, kv, cw, cb, cg, cbeta, gg_, gb_, ws, bs_t, after)


def _out_proj_loss(ycat, wout_full, b_out, ln_g, ln_b, x2, tgt2, tm):
    n = x2.shape[0]

    def body(y_ref, w_ref, bo_ref, g_ref, b_ref, x_ref, t_ref, dr_ref, dy_ref, vec_ref, loss_ref):
        i = pl.program_id(0)

        @pl.when(i == 0)
        def _():
            vec_ref[...] = jnp.zeros_like(vec_ref)
            loss_ref[...] = jnp.zeros_like(loss_ref)

        r = ALPHA * x_ref[...] + _dot(y_ref[...], w_ref[...]) + bo_ref[...]
        rhat, rstd = _ln_stats(r)
        diff = rhat * g_ref[...] + b_ref[...] - t_ref[...]
        loss_ref[...] += 0.5 * jnp.sum(jnp.mean(diff * diff, axis=-1, keepdims=True), axis=0, keepdims=True)
        dout = diff * (1.0 / D_MODEL)
        dr = _ln_bwd(dout * g_ref[...], rhat, rstd)
        vec_ref[0:1, :] += _colsum(dr)
        vec_ref[1:2, :] += _colsum(dout * rhat)
        vec_ref[2:3, :] += _colsum(dout)
        dr_ref[...] = dr
        dy_ref[...] = _dot_nt(dr.astype(BF16), w_ref[...]).astype(BF16)

    row = lambda i: (i, 0)
    const = lambda i: (0, 0)
    vec = pl.BlockSpec((1, D_MODEL), const)
    return pl.pallas_call(
        body, name="out_proj_loss", grid=(n // tm,),
        in_specs=[pl.BlockSpec((tm, D_MIX), row), pl.BlockSpec((D_MIX, D_MODEL), const), vec, vec, vec,
                  pl.BlockSpec((tm, D_MODEL), row), pl.BlockSpec((tm, D_MODEL), row)],
        out_specs=[pl.BlockSpec((tm, D_MODEL), row), pl.BlockSpec((tm, D_MIX), row),
                   pl.BlockSpec((8, D_MODEL), const), pl.BlockSpec((1, 1), const)],
        out_shape=[jax.ShapeDtypeStruct((n, D_MODEL), F32), jax.ShapeDtypeStruct((n, D_MIX), BF16),
                   jax.ShapeDtypeStruct((8, D_MODEL), F32), jax.ShapeDtypeStruct((1, 1), F32)],
        compiler_params=_params(("arbitrary",)),
    )(ycat, wout_full, b_out, ln_g, ln_b, x2, tgt2)


ROW_CB, ROW_CLG, ROW_CLB, ROW_GLG, ROW_GLB = 32, 33, 34, 35, 36


def _branch_bwd(h, z, dy, kv, cw, clg, clb, glg, glb, ws, bs_t, w_full, dr, nb, seq, tm, after):
    nt = seq // tm
    n = nb * seq
    tiles = nb * nt

    def place(s):
        sc = jnp.minimum(s, tiles - 1)
        return sc // nt, nt - 1 - sc % nt

    def body(h_ref, z_ref, dy_ref, kv_ref, cw_ref, clg_ref, clb_ref, glg_ref, glb_ref,
             ws_ref, bst_ref, w_ref, dr_ref, after_ref,
             dh_ref, gbin_ref, g768_ref, gws_ref, gbst_ref, dkv_ref, gbkv_ref, gx_ref,
             dzbuf, dvnbuf, dhkeep):
        step = pl.program_id(0)
        live = step < tiles
        i = jnp.minimum(step, tiles - 1) % nt
        every = slice(None)
        keep_now = step % 2
        gain = jnp.where(live, 1.0, 0.0).astype(F32)

        @pl.when(step == 0)
        def _():
            gbin_ref[...] = jnp.zeros_like(gbin_ref)
            g768_ref[...] = jnp.zeros_like(g768_ref)
            gws_ref[...] = jnp.zeros_like(gws_ref)
            gbst_ref[...] = jnp.zeros_like(gbst_ref)
            gbkv_ref[...] = jnp.zeros_like(gbkv_ref)
            dhkeep[1] = jnp.zeros((tm, D_IN), BF16)

        @pl.when(live & (i == 0))
        def _():
            dkv_ref[...] = jnp.zeros_like(dkv_ref)

        prev = 1 - keep_now
        gx_ref[...] = ALPHA * dr_ref[...]

        def dx_piece(j):
            gx_ref[...] += _dot_nt(dhkeep[prev, :, j * W_IN_SHARD:(j + 1) * W_IN_SHARD], w_ref[j])

        def put(rows, col, width, val):
            vb = val.astype(BF16)
            dh_ref[rows, col:col + width] = vb
            dhkeep[keep_now, rows, col:col + width] = vb

        def emit(col, width, val):
            gbin_ref[:, col:col + width] += _colsum(val)
            put(every, col, width, val)

        d_c = dy_ref[:, 0:D_CONV].astype(F32) * gain
        cgate = _f32(h_ref, every, C_GATE, D_CONV)
        sg = _sigmoid(cgate)
        zhat, zrstd = _ln_stats(z_ref[...])
        zn = zhat * clg_ref[...] + clb_ref[...]
        szn = _sigmoid(zn)
        emit(C_GATE, D_CONV, d_c * (zn * szn) * _dsilu(cgate, sg))
        dzn = d_c * (cgate * sg) * _dsilu(zn, szn)
        g768_ref[ROW_CLG:ROW_CLG + 1, :] += _colsum(dzn * zhat)
        g768_ref[ROW_CLB:ROW_CLB + 1, :] += _colsum(dzn)
        dz = _ln_bwd(dzn * clg_ref[...], zhat, zrstd)
        g768_ref[ROW_CB:ROW_CB + 1, :] += _colsum(dz)

        follows = live & (i > 0)

        @pl.when(jnp.logical_not(follows))
        def _():
            dzbuf[0, tm:tm + HALO, :] = jnp.zeros((HALO, D_CONV), F32)

        @pl.when(follows)
        def _():
            dzbuf[0, tm:tm + HALO, :] = dzbuf[0, 0:HALO, :]

        dzbuf[0, 0:tm, :] = dz
        _shifted_planes(dzbuf, tm)
        a = _f32(h_ref, every, C_A, D_CONV)
        sgl = _sigmoid(_f32(h_ref, every, C_GLU, D_CONV))
        hc = a * sgl

        for r in range(tm // CONV_ROWS):
            base = r * CONV_ROWS
            acc = jnp.zeros((CONV_ROWS, D_CONV), F32)
            for k in range(CONV_WIDTH):
                acc = acc + cw_ref[k:k + 1, :] * _window(dzbuf, base + 30 - k, CONV_ROWS)
            dvnbuf[base:base + CONV_ROWS, :] = acc
            if r % 4 == 1:
                dx_piece(r // 4)
        dhc = dvnbuf[...]
        emit(C_A, D_CONV, dhc * sgl)
        emit(C_GLU, D_CONV, dhc * a * sgl * (1.0 - sgl))
        for k in range(CONV_WIDTH):
            g768_ref[k:k + 1, :] += _colsum(hc * _window(dzbuf, 30 - k, tm))

        wsc, _ = _causal_ws(ws_ref)
        v, dgelu_v = _gelu_and_grad(_f32(h_ref, every, G_V, D_GMLP))
        vhat, vrstd = _ln_stats(v)
        vn = (vhat * glg_ref[...] + glb_ref[...]).astype(BF16)
        for ch in range(tm // CHUNK):
            rows = slice(ch * CHUNK, (ch + 1) * CHUNK)
            for hd in range(N_GHEADS):
                cols = slice(hd * CHUNK, (hd + 1) * CHUNK)
                vn_blk = vn[rows, cols]
                s = _dot(wsc[hd], vn_blk) + bst_ref[:, hd:hd + 1]
                u, dgelu_u = _gelu_and_grad(_f32(h_ref, rows, G_U + hd * CHUNK, CHUNK))
                gate = _f32(h_ref, rows, G_GATE + hd * CHUNK, CHUNK)
                sgate = _sigmoid(gate)
                d_g = dy_ref[rows, D_CONV + hd * CHUNK:D_CONV + (hd + 1) * CHUNK].astype(F32) * gain
                dgate = d_g * (u * s) * _dsilu(gate, sgate)
                gbin_ref[:, G_GATE + hd * CHUNK:G_GATE + (hd + 1) * CHUNK] += _colsum(dgate)
                put(rows, G_GATE + hd * CHUNK, CHUNK, dgate)
                dyg = d_g * (gate * sgate)
                du = dyg * s * dgelu_u
                gbin_ref[:, G_U + hd * CHUNK:G_U + (hd + 1) * CHUNK] += _colsum(du)
                put(rows, G_U + hd * CHUNK, CHUNK, du)
                ds = dyg * u
                dsb = ds.astype(BF16)
                gws_ref[hd] += _dot_nt(dsb, vn_blk)
                gbst_ref[:, hd:hd + 1] += jnp.sum(ds, axis=1, keepdims=True)
                dvnbuf[rows, cols] = _dot_tn(wsc[hd], dsb)
        dvn = dvnbuf[...]
        g768_ref[ROW_GLG:ROW_GLG + 1, :] += _colsum(dvn * vhat)
        g768_ref[ROW_GLB:ROW_GLB + 1, :] += _colsum(dvn)
        emit(G_V, D_GMLP, _ln_bwd(dvn * glg_ref[...], vhat, vrstd) * dgelu_v)

        scale = XHEAD ** -0.5
        for hd in range(N_XHEADS):
            q = h_ref[:, X_Q + hd * XHEAD:X_Q + (hd + 1) * XHEAD]
            k = kv_ref[:, hd * XHEAD:(hd + 1) * XHEAD]
            vv = kv_ref[:, D_XATT + hd * XHEAD:D_XATT + (hd + 1) * XHEAD]
            s = _dot_nt(q, k) * scale
            e = jnp.exp(s - jnp.max(s, axis=-1, keepdims=True))
            p = e * (1.0 / jnp.sum(e, axis=-1, keepdims=True))
            pb = p.astype(BF16)
            o = _dot(pb, vv)
            gate = _f32(h_ref, every, X_GATE + hd * XHEAD, XHEAD)
            sgate = _sigmoid(gate)
            d_x = dy_ref[:, 2 * D_CONV + hd * XHEAD:2 * D_CONV + (hd + 1) * XHEAD].astype(F32) * gain
            emit(X_GATE + hd * XHEAD, XHEAD, d_x * o * _dsilu(gate, sgate))
            do = (d_x * (gate * sgate)).astype(BF16)
            dp = _dot_nt(do, vv)
            dsc = (p * (dp - jnp.sum(dp * p, axis=-1, keepdims=True))).astype(BF16)
            emit(X_Q + hd * XHEAD, XHEAD, _dot(dsc, k) * scale)
            dkv_ref[:, hd * XHEAD:(hd + 1) * XHEAD] += _dot_tn(dsc, q) * scale
            dkv_ref[:, D_XATT + hd * XHEAD:D_XATT + (hd + 1) * XHEAD] += _dot_tn(pb, do)

        @pl.when(live & (i == nt - 1))
        def _():
            gbkv_ref[...] += _colsum(dkv_ref[...])

        @pl.when(step == tiles)
        def _():
            _, keep = _causal_ws(ws_ref)
            for hd in range(N_GHEADS):
                gws_ref[hd] = jnp.where(keep, gws_ref[hd], 0.0)

    def row(s):
        b, ri = place(s)
        return b * nt + ri, 0

    def row_dh(s):
        b, ri = place(s)
        return jnp.where(s < tiles, b * nt + ri, tiles), 0

    def row_prev(s):
        b, ri = place(jnp.maximum(s - 1, 0))
        return b * nt + ri, 0

    example = lambda s: (place(s)[0], 0)
    const2 = lambda s: (0, 0)
    const3 = lambda s: (0, 0, 0)
    vec = pl.BlockSpec((1, D_CONV), const2)

    return pl.pallas_call(
        body, name="branch_bwd", grid=(tiles + 1,),
        in_specs=[pl.BlockSpec((tm, D_IN), row),
                  pl.BlockSpec((tm, D_CONV), row), pl.BlockSpec((tm, D_MIX), row),
                  pl.BlockSpec((MEM_LEN, D_MODEL), example),
                  pl.BlockSpec((CONV_WIDTH, D_CONV), const2), vec, vec, vec, vec,
                  pl.BlockSpec((N_GHEADS, CHUNK, CHUNK), const3),
                  pl.BlockSpec((CHUNK, N_GHEADS), const2),
                  pl.BlockSpec((N_CHIPS, D_MODEL, W_IN_SHARD), const3, pipeline_mode=pl.Buffered(1)),
                  pl.BlockSpec((tm, D_MODEL), row_prev), ANY],
        out_specs=[pl.BlockSpec((tm, D_IN), row_dh),
                   pl.BlockSpec((1, D_IN), const2),
                   pl.BlockSpec((40, D_CONV), const2),
                   pl.BlockSpec((N_GHEADS, CHUNK, CHUNK), const3),
                   pl.BlockSpec((CHUNK, CHUNK), const2),
                   pl.BlockSpec((MEM_LEN, D_MODEL), example),
                   pl.BlockSpec((1, D_MODEL), const2),
                   pl.BlockSpec((tm, D_MODEL), row_prev)],
        out_shape=[jax.ShapeDtypeStruct((n + tm, D_IN), BF16),
                   jax.ShapeDtypeStruct((1, D_IN), F32),
                   jax.ShapeDtypeStruct((40, D_CONV), F32),
                   jax.ShapeDtypeStruct((N_GHEADS, CHUNK, CHUNK), F32),
                   jax.ShapeDtypeStruct((CHUNK, CHUNK), F32),
                   jax.ShapeDtypeStruct((nb * MEM_LEN, D_MODEL), F32),
                   jax.ShapeDtypeStruct((1, D_MODEL), F32),
                   jax.ShapeDtypeStruct((n, D_MODEL), F32)],
        scratch_shapes=[pltpu.VMEM((SUBLANES, tm + HALO, D_CONV), F32), pltpu.VMEM((tm, D_CONV), F32),
                        pltpu.VMEM((2, tm, D_IN), BF16)],
        compiler_params=_params(("arbitrary",)),
    )(h, z, dy, kv, cw, clg, clb, glg, glb, ws, bs_t, w_full, dr, after)


def _grad_w(a, b, tk, name):
    kdim, m = a.shape
    ncols = b.shape[1]
    nk = kdim // tk
    shard = m // N_CHIPS
    oshape = (shard, ncols)
    a_spec = pl.BlockSpec((tk, shard), lambda j, kk: (kk, j))
    b_spec = pl.BlockSpec((tk, ncols), lambda j, kk: (kk, 0))

    def body(a_ref, b_ref, own_ref, ob_ref, acc):
        j = pl.program_id(0)
        kk = pl.program_id(1)
        mine = 2 * lax.axis_index("x") + lax.axis_index("y")

        @pl.when(kk == 0)
        def _():
            acc[...] = jnp.zeros_like(acc)

        acc[...] += _dot_tn(a_ref[...].astype(BF16), b_ref[...].astype(BF16))

        @pl.when(kk == nk - 1)
        def _():
            ob_ref[...] = acc[...].astype(BF16)

        @pl.when((kk == nk - 1) & (j == mine))
        def _():
            own_ref[...] = acc[...]

    return pl.pallas_call(
        body, name=name, grid=(N_CHIPS, nk),
        in_specs=[a_spec, b_spec],
        out_specs=[pl.BlockSpec(oshape, lambda j, kk: (0, 0)),
                   pl.BlockSpec((None,) + oshape, lambda j, kk: (j, 0, 0))],
        out_shape=[jax.ShapeDtypeStruct(oshape, F32), jax.ShapeDtypeStruct((N_CHIPS,) + oshape, BF16)],
        scratch_shapes=[pltpu.VMEM(oshape, F32)],
        compiler_params=_params(("arbitrary", "arbitrary")),
    )(a, b)


def _grad_w_block(a, b, block, tk, name, dtype, after):
    kdim, m = a.shape
    shard = b.shape[1] // N_CHIPS
    nk = kdim // tk

    def body(block_ref, a_ref, b_ref, after_ref, o_ref, acc):
        kk = pl.program_id(0)

        @pl.when(kk == 0)
        def _():
            acc[...] = jnp.zeros_like(acc)

        acc[...] += _dot_tn(a_ref[...].astype(BF16), b_ref[...])

        @pl.when(kk == nk - 1)
        def _():
            o_ref[...] = acc[...].astype(dtype)

    grid_spec = pltpu.PrefetchScalarGridSpec(
        num_scalar_prefetch=1, grid=(nk,),
        in_specs=[pl.BlockSpec((tk, m), lambda kk, blk: (kk, 0)),
                  pl.BlockSpec((tk, shard), lambda kk, blk: (kk, blk[0])), ANY],
        out_specs=pl.BlockSpec((m, shard), lambda kk, blk: (0, 0)),
        scratch_shapes=[pltpu.VMEM((m, shard), F32)])
    return pl.pallas_call(
        body, name=name, grid_spec=grid_spec, out_shape=jax.ShapeDtypeStruct((m, shard), dtype),
        compiler_params=_params(("arbitrary",)),
    )(block, a, b, after)


def _sum_small(owns, gots):
    n = len(owns)

    def body(*refs):
        for a in range(n):
            o_ref, g_ref, out_ref = refs[a], refs[n + a], refs[2 * n + a]
            out_ref[...] = (o_ref[...] + g_ref[1]) + (g_ref[0] + g_ref[2])

    return pl.pallas_call(
        body, name="sum_small", out_shape=[jax.ShapeDtypeStruct(o.shape, F32) for o in owns],
        compiler_params=pltpu.CompilerParams(vmem_limit_bytes=VMEM_LIMIT),
    )(*owns, *gots)


def _sum_chips(own, gots, tr, name):
    r, ccols = own.shape

    def body(o_ref, gx_ref, gy_ref, gxy_ref, out_ref):
        out_ref[...] = (o_ref[...] + gy_ref[...].astype(F32)) + (gx_ref[...].astype(F32) + gxy_ref[...].astype(F32))

    return pl.pallas_call(
        body, name=name, grid=(r // tr,),
        in_specs=[pl.BlockSpec((tr, ccols), lambda i: (i, 0))]
        + [pl.BlockSpec((None, tr, ccols), lambda i, slot=slot: (slot, i, 0)) for _, slot in gots],
        out_specs=pl.BlockSpec((tr, ccols), lambda i: (i, 0)),
        out_shape=jax.ShapeDtypeStruct((r, ccols), F32),
        compiler_params=_params(("arbitrary",)),
    )(own, *[g for g, _ in gots])


def _exchange_cores(parts):
    npart = len(parts)

    def body(*refs):
        in_refs = refs[0:npart]
        out_refs = refs[npart:2 * npart]
        send_sems, recv_sems = refs[2 * npart:]
        sibling = (lax.axis_index("x"), lax.axis_index("y"), 1 - lax.axis_index("c"))
        copies = [pltpu.make_async_remote_copy(
            src_ref=in_refs[a], dst_ref=out_refs[a], send_sem=send_sems.at[a], recv_sem=recv_sems.at[a],
            device_id=sibling, device_id_type=MESH_ID) for a in range(npart)]
        for cp in copies:
            cp.start()
        for cp in copies:
            cp.wait_recv()
        for cp in copies:
            cp.wait_send()

    return pl.pallas_call(
        body, name="exchange_cores", out_shape=[jax.ShapeDtypeStruct(p.shape, p.dtype) for p in parts],
        in_specs=[ANY] * npart, out_specs=[ANY] * npart,
        scratch_shapes=[pltpu.SemaphoreType.DMA((npart,)), pltpu.SemaphoreType.DMA((npart,))],
    )(*parts)


def _adam(g, w, m, v):
    mn = ADAM_B1 * m + (1.0 - ADAM_B1) * g
    vn = ADAM_B2 * v + (1.0 - ADAM_B2) * (g * g)
    return g, -ADAM_LR * ((mn / BC1) / (jnp.sqrt(vn / BC2) + ADAM_EPS) + ADAM_WD * w), mn, vn


def _adamw(a, b, w, m, v, tr, name):
    r, ccols = w.shape

    def body(a_ref, b_ref, w_ref, m_ref, v_ref, *outs):
        res = _adam(a_ref[...] + b_ref[...], w_ref[...], m_ref[...], v_ref[...])
        for which in range(4):
            outs[which][...] = res[which]

    spec = pl.BlockSpec((tr, ccols), lambda i: (i, 0))
    shape = jax.ShapeDtypeStruct((r, ccols), F32)
    return pl.pallas_call(
        body, name=name, grid=(r // tr,), in_specs=[spec] * 5, out_specs=[spec] * 4, out_shape=[shape] * 4,
        compiler_params=_params(("arbitrary",)),
    )(a, b, w, m, v)


SMALL = ["b_in", "conv_b", "conv_ln_g", "conv_ln_b", "gmlp_ln_g", "gmlp_ln_b", "gmlp_ws", "gmlp_bs", "b_kv", "b_out",
         "ln_g", "ln_b"]


def _adamw_small(a_parts, b_parts, params):
    nparts, nparams = len(a_parts), len(params)

    def body(*refs):
        a = refs[0:nparts]
        b = refs[nparts:2 * nparts]
        prm = refs[2 * nparts:2 * nparts + 3 * nparams]
        outs = refs[2 * nparts + 3 * nparams:]
        gb_in, g768, gws, gbs_t, gb_kv, vec3 = [a[q][...] + b[q][...] for q in range(nparts)]
        grads = [gb_in, g768[ROW_CB:ROW_CB + 1], g768[ROW_CLG:ROW_CLG + 1], g768[ROW_CLB:ROW_CLB + 1],
                 g768[ROW_GLG:ROW_GLG + 1], g768[ROW_GLB:ROW_GLB + 1], gws, jnp.transpose(gbs_t)[0:N_GHEADS, :],
                 gb_kv, vec3[0:1], vec3[1:2], vec3[2:3]]
        for q, g in enumerate(grads):
            res = _adam(g, prm[3 * q][...], prm[3 * q + 1][...], prm[3 * q + 2][...])
            for which in range(4):
                outs[4 * q + which][...] = res[which]

    flat = [t for p in params for t in p]
    out_shape = [jax.ShapeDtypeStruct(p[0].shape, F32) for p in params for _ in range(4)]
    return pl.pallas_call(
        body, name="adamw_small", out_shape=out_shape,
        compiler_params=pltpu.CompilerParams(vmem_limit_bytes=VMEM_LIMIT),
    )(*a_parts, *b_parts, *flat)


def kernel(x, mem, w_in, b_in, conv_w, conv_b, conv_ln_g, conv_ln_b, gmlp_ln_g, gmlp_ln_b, gmlp_ws, gmlp_bs, w_kv, b_kv, w_out, b_out, ln_g, ln_b, loss_target, m_w_in, m_b_in, m_conv_w, m_conv_b, m_conv_ln_g, m_conv_ln_b, m_gmlp_ln_g, m_gmlp_ln_b, m_gmlp_ws, m_gmlp_bs, m_w_kv, m_b_kv, m_w_out, m_b_out, m_ln_g, m_ln_b, v_w_in, v_b_in, v_conv_w, v_conv_b, v_conv_ln_g, v_conv_ln_b, v_gmlp_ln_g, v_gmlp_ln_b, v_gmlp_ws, v_gmlp_bs, v_w_kv, v_b_kv, v_w_out, v_b_out, v_ln_g, v_ln_b):
    weights = dict(b_in=b_in, conv_b=conv_b, conv_ln_g=conv_ln_g, conv_ln_b=conv_ln_b, gmlp_ln_g=gmlp_ln_g,
                   gmlp_ln_b=gmlp_ln_b, gmlp_ws=gmlp_ws, gmlp_bs=gmlp_bs, b_kv=b_kv, b_out=b_out, ln_g=ln_g, ln_b=ln_b)
    mom_m = dict(b_in=m_b_in, conv_b=m_conv_b, conv_ln_g=m_conv_ln_g, conv_ln_b=m_conv_ln_b, gmlp_ln_g=m_gmlp_ln_g,
                 gmlp_ln_b=m_gmlp_ln_b, gmlp_ws=m_gmlp_ws, gmlp_bs=m_gmlp_bs, b_kv=m_b_kv, b_out=m_b_out,
                 ln_g=m_ln_g, ln_b=m_ln_b)
    mom_v = dict(b_in=v_b_in, conv_b=v_conv_b, conv_ln_g=v_conv_ln_g, conv_ln_b=v_conv_ln_b, gmlp_ln_g=v_gmlp_ln_g,
                 gmlp_ln_b=v_gmlp_ln_b, gmlp_ws=v_gmlp_ws, gmlp_bs=v_gmlp_bs, b_kv=v_b_kv, b_out=v_b_out,
                 ln_g=v_ln_g, ln_b=v_ln_b)
    nb, seq, _ = x.shape
    n = nb * seq
    tm = 256
    tk = min(1024, n)
    x2 = x.reshape(n, D_MODEL)
    tgt2 = loss_target.reshape(n, D_MODEL)
    mem2 = mem.reshape(nb * MEM_LEN, D_MODEL)
    chip = 2 * lax.axis_index("x") + lax.axis_index("y")
    bs_t = jnp.transpose(gmlp_bs[0])

    own_kv = [w_kv[0].astype(BF16), conv_w[0]]
    own_out = [w_out[0].astype(BF16)]
    ga = _start_exchange("gather_kv_start", own_kv, [False] * 2)
    h, win_g = _in_proj(x2, w_in[0].astype(BF16), b_in, min(512, n), ga["token"])
    gb = _start_exchange("gather_out_start", own_out, [False])
    wkv_g, cw_g = _place_shards(own_kv, _wait_exchange("gather_kv_wait", ga, h), "place_kv")
    wkv_full = wkv_g.reshape(D_MODEL, D_MODEL)
    cw_full = jnp.transpose(cw_g, (1, 0, 2)).reshape(CONV_WIDTH, D_CONV)
    kv = _kv_proj(mem2, wkv_full, b_kv)
    ycat, z = _branch_fwd(h, kv, cw_full, conv_b, conv_ln_g, conv_ln_b, gmlp_ln_g, gmlp_ln_b, gmlp_ws[0], bs_t,
                          nb, seq, tm, gb["token"])
    (wout_g,) = _place_shards(own_out, _wait_exchange("gather_out_wait", gb, ycat), "place_out")
    wout_full = wout_g.reshape(D_MIX, D_MODEL)
    dr, dycat, vec3, loss_part = _out_proj_loss(ycat, wout_full, b_out, ln_g, ln_b, x2, tgt2, tm)
    loss = lax.psum(loss_part[0, 0], ("x", "y", "c"))

    own_wout, gwout_b = _grad_w(ycat, dr, tk, "grad_w_out")
    ex1 = _start_exchange("exchange1_start", [gwout_b, vec3], [True, False])
    dh, gb_in, g768, gws, gbs_t, dkv, gb_kv, grad_x2 = _branch_bwd(
        h, z, dycat, kv, cw_full, conv_ln_g, conv_ln_b, gmlp_ln_g, gmlp_ln_b, gmlp_ws[0], bs_t, win_g, dr,
        nb, seq, tm, ex1["token"])
    own_wkv, gwkv_b = _grad_w(mem2, dkv, nb * MEM_LEN, "grad_w_kv")
    small2 = [gb_in, g768, gws, gbs_t, gb_kv]
    ex2 = _start_exchange("exchange2_start", [gwkv_b] + small2, [True] + [False] * 5)
    block_of = lambda flip_bits: (chip ^ flip_bits).astype(jnp.int32).reshape(1)
    after, ex3 = ex2["token"], {}
    for flip, bits in ((2, 3), (1, 1), (0, 2)):
        part = _grad_w_block(x2, dh, block_of(bits), tk, f"grad_w_in_{flip}", BF16, after)
        ex3[flip] = _start_exchange(f"exchange3{flip}_start", [part], [False], (flip,))
        after = ex3[flip]["token"]
    own_win = _grad_w_block(x2, dh, block_of(0), tk, "grad_w_in_own", F32, after)
    got_wout, got_vec3 = _wait_exchange("exchange1_wait", ex1, own_win)
    got2 = _wait_exchange("exchange2_wait", ex2, own_win)
    got_win = {flip: _wait_exchange(f"exchange3{flip}_wait", ex3[flip], own_win)[0] for flip in (2, 1, 0)}

    sum_win = _sum_chips(own_win, [(got_win[0], 0), (got_win[1], 0), (got_win[2], 0)], 256, "sum_w_in")
    sum_wout = _sum_chips(own_wout, [(got_wout, 0), (got_wout, 1), (got_wout, 2)], 256, "sum_w_out")
    sum_wkv = _sum_chips(own_wkv, [(got2[0], 0), (got2[0], 1), (got2[0], 2)], 256, "sum_w_kv")
    sum_small = list(_sum_small(small2 + [vec3], got2[1:] + [got_vec3]))
    sib = list(_exchange_cores([sum_win, sum_wout, sum_wkv] + sum_small))

    big = {}
    big["w_in"] = _adamw(sum_win, sib[0], w_in[0], m_w_in[0], v_w_in[0], 256, "adamw_w_in")
    big["w_out"] = _adamw(sum_wout, sib[1], w_out[0], m_w_out[0], v_w_out[0], 256, "adamw_w_out")
    big["w_kv"] = _adamw(sum_wkv, sib[2], w_kv[0], m_w_kv[0], v_w_kv[0], 256, "adamw_w_kv")
    lead = {"gmlp_ws", "gmlp_bs"}
    strip = lambda k, t: t[0] if k in lead else t
    sm = _adamw_small(sum_small, sib[3:], [tuple(strip(k, t[k]) for t in (weights, mom_m, mom_v)) for k in SMALL])
    small_out = {k: [sm[4 * q + which][None] if k in lead else sm[4 * q + which] for which in range(4)]
                 for q, k in enumerate(SMALL)}
    cw_a = lax.dynamic_slice_in_dim(sum_small[1][0:CONV_WIDTH + 1], chip * CONV_SHARD, CONV_SHARD, axis=1)
    cw_b = lax.dynamic_slice_in_dim(sib[4][0:CONV_WIDTH + 1], chip * CONV_SHARD, CONV_SHARD, axis=1)
    cwp = ((0, 1), (0, 0))
    cw_out = _adamw(cw_a, cw_b, jnp.pad(conv_w[0], cwp), jnp.pad(m_conv_w[0], cwp), jnp.pad(v_conv_w[0], cwp),
                    CONV_WIDTH + 1, "adamw_conv_w")

    order = ["w_in", "b_in", "conv_w", "conv_b", "conv_ln_g", "conv_ln_b", "gmlp_ln_g", "gmlp_ln_b", "gmlp_ws",
             "gmlp_bs", "w_kv", "b_kv", "w_out", "b_out", "ln_g", "ln_b"]
    result = [loss, grad_x2.reshape(nb, seq, D_MODEL)]
    for which in range(4):
        for k in order:
            if k in big:
                result.append(big[k][which][None])
            elif k == "conv_w":
                result.append(cw_out[which][0:CONV_WIDTH][None])
            else:
                result.append(small_out[k][which])
    return tuple(result)
```

```python
import functools
import math

import jax
import jax.numpy as jnp
from jax import lax
from jax.experimental import pallas as pl
from jax.experimental.pallas import tpu as pltpu

F32 = jnp.float32
BF16 = jnp.bfloat16

D_MODEL = 1024
MEM_LEN = 256
D_MIX = 2048
D_CONV = 768
D_GMLP = 768
D_XATT = 512
N_XHEADS = 4
XHEAD = 128
CONV_WIDTH = 31
CHUNK = 128
N_GHEADS = 6
D_IN = 3 * D_CONV + 3 * D_GMLP + 2 * D_XATT
ALPHA = 2.0 ** 0.25
LN_EPS = 1e-5
N_CHIPS = 4
W_IN_SHARD = D_IN // N_CHIPS
W_OUT_SHARD = D_MIX // N_CHIPS
W_KV_SHARD = D_MODEL // N_CHIPS
CONV_SHARD = D_CONV // N_CHIPS
HALO = 32

C_A, C_GLU, C_GATE = 0, 768, 1536
G_U, G_V, G_GATE = 2304, 3072, 3840
X_Q, X_GATE = 4608, 5120

ADAM_LR = 0.001
ADAM_B1 = 0.9
ADAM_B2 = 0.999
ADAM_EPS = 1e-08
ADAM_WD = 0.01
ADAM_STEP = 10
BC1 = 1.0 - ADAM_B1 ** ADAM_STEP
BC2 = 1.0 - ADAM_B2 ** ADAM_STEP

VMEM_LIMIT = 56 * 1024 * 1024
MESH_ID = pl.DeviceIdType.MESH
ANY = pl.BlockSpec(memory_space=pl.ANY)

GELU_C0 = math.sqrt(2.0 / math.pi)
GELU_C1 = 0.044715


def _sigmoid(v):
    return 0.5 + 0.5 * jnp.tanh(0.5 * v)


def _f32(ref, rows, col, width):
    return ref[rows, col:col + width].astype(F32)


def _dsilu(v, s):
    return s * (1.0 + v * (1.0 - s))


def _gelu_and_grad(v):
    t = jnp.tanh(GELU_C0 * (v + GELU_C1 * v * v * v))
    g = 0.5 * v * (1.0 + t)
    dg = 0.5 * (1.0 + t) + 0.5 * v * (1.0 - t * t) * (GELU_C0 * (1.0 + 3.0 * GELU_C1 * v * v))
    return g, dg


def _gelu(v):
    return 0.5 * v * (1.0 + jnp.tanh(GELU_C0 * (v + GELU_C1 * v * v * v)))


def _ln_stats(v):
    mu = jnp.mean(v, axis=-1, keepdims=True)
    vc = v - mu
    var = jnp.mean(vc * vc, axis=-1, keepdims=True)
    rstd = lax.rsqrt(var + LN_EPS)
    return vc * rstd, rstd


def _ln_bwd(dvhat, vhat, rstd):
    m1 = jnp.mean(dvhat, axis=-1, keepdims=True)
    m2 = jnp.mean(dvhat * vhat, axis=-1, keepdims=True)
    return rstd * (dvhat - m1 - vhat * m2)


def _colsum(v):
    return jnp.sum(v, axis=0, keepdims=True)


def _dot(a, b):
    return jnp.dot(a, b, preferred_element_type=F32)


def _dot_nt(a, b):
    return lax.dot_general(a, b, (((1,), (1,)), ((), ())), preferred_element_type=F32)


def _dot_tn(a, b):
    return lax.dot_general(a, b, (((0,), (0,)), ((), ())), preferred_element_type=F32)


def _causal_ws(ws_ref):
    row = lax.broadcasted_iota(jnp.int32, (CHUNK, CHUNK), 0)
    col = lax.broadcasted_iota(jnp.int32, (CHUNK, CHUNK), 1)
    keep = col <= row
    return [jnp.where(keep, ws_ref[hd], 0.0).astype(BF16) for hd in range(N_GHEADS)], keep


def _params(sem):
    return pltpu.CompilerParams(dimension_semantics=sem, vmem_limit_bytes=VMEM_LIMIT)


def _peer_chips():
    x, y, c = lax.axis_index("x"), lax.axis_index("y"), lax.axis_index("c")
    return [(1 - x, y), (x, 1 - y), (1 - x, 1 - y)], 2 * x + y, c


HBM = pl.BlockSpec(memory_space=pltpu.HBM)
SEM = pl.BlockSpec(memory_space=pltpu.SEMAPHORE)
EFFECT = pltpu.SideEffectType.DATAFLOW_SIDE_EFFECTING
ALL_FLIPS = (0, 1, 2)


def _exchange_copies(src_refs, land_refs, per_chip, flips, send_sems, recv_sems):
    chips, _, c = _peer_chips()
    n = len(src_refs)
    copies = []
    for q, p in enumerate(flips):
        px, py = chips[p]
        for a in range(n):
            src = src_refs[a].at[2 * px + py] if per_chip[a] else src_refs[a]
            copies.append(pltpu.make_async_remote_copy(
                src_ref=src, dst_ref=land_refs[a].at[q], send_sem=send_sems.at[n * q + a],
                recv_sem=recv_sems.at[n * q + a], device_id=(px, py, c), device_id_type=MESH_ID))
    return copies


def _start_exchange(name, srcs, per_chip, flips=ALL_FLIPS):
    n = len(srcs)
    nf = len(flips)
    lands = [lax.empty((nf,) + (s.shape[1:] if pc else s.shape), s.dtype) for s, pc in zip(srcs, per_chip)]

    def body(*refs):
        src_refs, land_refs = refs[0:n], refs[n:2 * n]
        send_sems, recv_sems = refs[2 * n], refs[2 * n + 1]
        token = refs[4 * n + 2]
        for cp in _exchange_copies(src_refs, land_refs, per_chip, flips, send_sems, recv_sems):
            cp.start()
        token[...] = jnp.zeros_like(token)

    out = pl.pallas_call(
        body, name=name,
        out_shape=(pltpu.SemaphoreType.DMA((nf * n,)), pltpu.SemaphoreType.DMA((nf * n,)),
                   *[pltpu.HBM(a.shape, a.dtype) for a in srcs + lands], jax.ShapeDtypeStruct((8, 128), F32)),
        in_specs=[HBM] * (2 * n),
        out_specs=(SEM, SEM, *[HBM] * (2 * n), pl.BlockSpec(memory_space=pltpu.VMEM)),
        input_output_aliases={a: 2 + a for a in range(2 * n)},
        compiler_params=pltpu.CompilerParams(has_side_effects=EFFECT),
    )(*[pltpu.with_memory_space_constraint(a, pltpu.HBM) for a in srcs + lands])
    return dict(send=out[0], recv=out[1], thru=list(out[2:2 * n + 2]), token=out[2 * n + 2], per_chip=per_chip,
                flips=flips)


def _wait_exchange(name, started, after):
    thru, per_chip, flips = started["thru"], started["per_chip"], started["flips"]
    n = len(thru) // 2

    def body(*refs):
        src_refs, land_refs = refs[0:n], refs[n:2 * n]
        send_sems, recv_sems = refs[2 * n], refs[2 * n + 1]
        for cp in _exchange_copies(src_refs, land_refs, per_chip, flips, send_sems, recv_sems):
            cp.wait_send()
            cp.wait_recv()

    out = pl.pallas_call(
        body, name=name, out_shape=tuple(pltpu.HBM(a.shape, a.dtype) for a in thru),
        in_specs=[HBM] * (2 * n) + [SEM, SEM, ANY], out_specs=tuple([HBM] * (2 * n)),
        input_output_aliases={a: a for a in range(2 * n)},
        compiler_params=pltpu.CompilerParams(has_side_effects=EFFECT),
    )(*thru, started["send"], started["recv"], after)
    return list(out[n:2 * n])


def _place_shards(owns, landeds, name):
    n = len(owns)
    mine = (2 * lax.axis_index("x") + lax.axis_index("y")).astype(jnp.int32).reshape(1)

    def body(mine_ref, *refs):
        own_refs, land_refs, out_refs = refs[0:n], refs[n:2 * n], refs[2 * n:3 * n]
        k = pl.program_id(0)
        for a in range(n):
            @pl.when(k == mine_ref[0])
            def _():
                out_refs[a][...] = own_refs[a][...]

            @pl.when(k != mine_ref[0])
            def _():
                out_refs[a][...] = land_refs[a][...]

    def slot(k, mine_ref):
        d = k ^ mine_ref[0]
        return jnp.where(d == 1, 1, jnp.where(d == 3, 2, 0))

    zeros = lambda o: (0,) * len(o.shape)
    grid_spec = pltpu.PrefetchScalarGridSpec(
        num_scalar_prefetch=1, grid=(N_CHIPS,),
        in_specs=[pl.BlockSpec(o.shape, lambda k, m, o=o: zeros(o)) for o in owns]
        + [pl.BlockSpec((None,) + o.shape, lambda k, m, o=o: (slot(k, m),) + zeros(o)) for o in owns],
        out_specs=[pl.BlockSpec((None,) + o.shape, lambda k, m, o=o: (k,) + zeros(o)) for o in owns])
    return pl.pallas_call(
        body, name=name, grid_spec=grid_spec,
        out_shape=[jax.ShapeDtypeStruct((N_CHIPS,) + o.shape, o.dtype) for o in owns],
        compiler_params=_params(("arbitrary",)),
    )(mine, *owns, *landeds)


def _gather_w_in(w_shard, after):
    half = D_MODEL // 2

    def body(w_ref, after_ref, out_ref, ici_send, ici_recv, d2d_send, d2d_recv, loc_sem):
        chips, mine, c = _peer_chips()
        sibling = (lax.axis_index("x"), lax.axis_index("y"), 1 - c)
        my_rows = pl.ds(pl.multiple_of(c * half, half), half)
        their_rows = pl.ds(pl.multiple_of((1 - c) * half, half), half)
        own = pltpu.make_async_copy(w_ref, out_ref.at[mine], loc_sem)
        own.start()

        def over_ici(p, slot):
            px, py = chips[p]
            return pltpu.make_async_remote_copy(
                src_ref=w_ref.at[my_rows, :], dst_ref=out_ref.at[slot, my_rows, :], send_sem=ici_send.at[p],
                recv_sem=ici_recv.at[p], device_id=(px, py, c), device_id_type=MESH_ID)

        def over_d2d(p, rows):
            px, py = chips[p]
            part = out_ref.at[2 * px + py, rows, :]
            return pltpu.make_async_remote_copy(
                src_ref=part, dst_ref=part, send_sem=d2d_send.at[p], recv_sem=d2d_recv.at[p],
                device_id=sibling, device_id_type=MESH_ID)

        for p in (1, 0, 2):
            over_ici(p, mine).start()
        for p in (1, 0, 2):
            px, py = chips[p]
            over_ici(p, 2 * px + py).wait_recv()
            over_d2d(p, my_rows).start()
        for p in (1, 0, 2):
            over_d2d(p, their_rows).wait_recv()
        for p in range(3):
            over_ici(p, mine).wait_send()
            over_d2d(p, my_rows).wait_send()
        own.wait()

    return pl.pallas_call(
        body, name="gather_w_in", out_shape=jax.ShapeDtypeStruct((N_CHIPS,) + w_shard.shape, w_shard.dtype),
        in_specs=[ANY, ANY], out_specs=ANY,
        scratch_shapes=[pltpu.SemaphoreType.DMA((3,)), pltpu.SemaphoreType.DMA((3,)), pltpu.SemaphoreType.DMA((3,)),
                        pltpu.SemaphoreType.DMA((3,)), pltpu.SemaphoreType.DMA],
    )(w_shard, after)


def _kv_proj(mem2, wkv_full, b_kv):
    m = mem2.shape[0]

    def body(m_ref, w_ref, b_ref, o_ref):
        o_ref[...] = (_dot(m_ref[...].astype(BF16), w_ref[...]) + b_ref[...]).astype(BF16)

    return pl.pallas_call(
        body, name="kv_proj", grid=(m // MEM_LEN,),
        in_specs=[pl.BlockSpec((MEM_LEN, D_MODEL), lambda i: (i, 0)),
                  pl.BlockSpec((D_MODEL, D_MODEL), lambda i: (0, 0)),
                  pl.BlockSpec((1, D_MODEL), lambda i: (0, 0))],
        out_specs=pl.BlockSpec((MEM_LEN, D_MODEL), lambda i: (i, 0)),
        out_shape=jax.ShapeDtypeStruct((m, D_MODEL), BF16),
        compiler_params=_params(("arbitrary",)),
    )(mem2, wkv_full, b_kv)


CONV_ROWS = 16
SUBLANES = 8


def _shifted_planes(buf, tm):
    rows = tm + HALO - SUBLANES
    for s in range(1, SUBLANES):
        buf[s, 0:rows, :] = buf[0, s:s + rows, :]


def _window(buf, start, rows):
    s = start % SUBLANES
    return buf[s, start - s:start - s + rows, :]


def _forward(x2, w_full, b_in, kv, cw, cb, cg, cbeta, gg_, gb_, ws, bs_t, nb, seq, tm, after):
    nt = seq // tm
    n = nb * seq
    tiles = nb * nt

    def body(x_ref, w_ref, b_ref, kv_ref, cw_ref, cb_ref, clg_ref, clb_ref, glg_ref, glb_ref, ws_ref, bst_ref,
             after_ref, hout_ref, y_ref, z_ref, hcbuf, hkeep):
        step = pl.program_id(0)
        i = jnp.maximum(step - 1, 0) % nt
        keep_now = step % 2
        h_ref = hkeep.at[1 - keep_now]

        @pl.when(step == 0)
        def _():
            hkeep[1] = jnp.zeros((tm, D_IN), BF16)

        xb = x_ref[...].astype(BF16)

        def h_piece(j):
            cols = slice(j * W_IN_SHARD, (j + 1) * W_IN_SHARD)
            hj = (_dot(xb, w_ref[j]) + b_ref[:, cols]).astype(BF16)
            hout_ref[:, cols] = hj
            hkeep[keep_now, :, cols] = hj

        starts = (step <= 1) | (i == 0)

        @pl.when(starts)
        def _():
            hcbuf[0, 0:HALO, :] = jnp.zeros((HALO, D_CONV), F32)

        @pl.when(jnp.logical_not(starts))
        def _():
            hcbuf[0, 0:HALO, :] = hcbuf[0, tm:tm + HALO, :]

        every = slice(None)
        hcbuf[0, HALO:HALO + tm, :] = _f32(h_ref, every, C_A, D_CONV) * _sigmoid(_f32(h_ref, every, C_GLU, D_CONV))
        _shifted_planes(hcbuf, tm)
        for r in range(tm // CONV_ROWS):
            base = r * CONV_ROWS
            acc = jnp.broadcast_to(cb_ref[...], (CONV_ROWS, D_CONV))
            for k in range(CONV_WIDTH):
                acc = acc + cw_ref[k:k + 1, :] * _window(hcbuf, base + 2 + k, CONV_ROWS)
            z_ref[base:base + CONV_ROWS, :] = acc
            if r % 4 == 1:
                h_piece(r // 4)
        zhat, _ = _ln_stats(z_ref[...])
        zn = zhat * clg_ref[...] + clb_ref[...]
        cgate = _f32(h_ref, every, C_GATE, D_CONV)
        y_ref[:, 0:D_CONV] = (zn * _sigmoid(zn) * (cgate * _sigmoid(cgate))).astype(BF16)

        wsc, _ = _causal_ws(ws_ref)
        vhat, _ = _ln_stats(_gelu(_f32(h_ref, every, G_V, D_GMLP)))
        vn = (vhat * glg_ref[...] + glb_ref[...]).astype(BF16)
        for ch in range(tm // CHUNK):
            rows = slice(ch * CHUNK, (ch + 1) * CHUNK)
            for hd in range(N_GHEADS):
                cols = slice(hd * CHUNK, (hd + 1) * CHUNK)
                s = _dot(wsc[hd], vn[rows, cols]) + bst_ref[:, hd:hd + 1]
                u = _gelu(_f32(h_ref, rows, G_U + hd * CHUNK, CHUNK))
                gate = _f32(h_ref, rows, G_GATE + hd * CHUNK, CHUNK)
                y_ref[rows, D_CONV + hd * CHUNK:D_CONV + (hd + 1) * CHUNK] = (
                    u * s * (gate * _sigmoid(gate))).astype(BF16)

        scale = XHEAD ** -0.5
        for hd in range(N_XHEADS):
            q = h_ref[:, X_Q + hd * XHEAD:X_Q + (hd + 1) * XHEAD]
            k = kv_ref[:, hd * XHEAD:(hd + 1) * XHEAD]
            v = kv_ref[:, D_XATT + hd * XHEAD:D_XATT + (hd + 1) * XHEAD]
            s = _dot_nt(q, k) * scale
            e = jnp.exp(s - jnp.max(s, axis=-1, keepdims=True))
            p = e * (1.0 / jnp.sum(e, axis=-1, keepdims=True))
            o = _dot(p.astype(BF16), v)
            gate = _f32(h_ref, every, X_GATE + hd * XHEAD, XHEAD)
            y_ref[:, 2 * D_CONV + hd * XHEAD:2 * D_CONV + (hd + 1) * XHEAD] = (
                o * (gate * _sigmoid(gate))).astype(BF16)

    ahead = lambda s: (jnp.minimum(s, tiles - 1), 0)
    behind = lambda s: (jnp.where(s == 0, tiles, s - 1), 0)
    example = lambda s: (jnp.maximum(s - 1, 0) // nt, 0)
    const2 = lambda s: (0, 0)
    const3 = lambda s: (0, 0, 0)
    vec = pl.BlockSpec((1, D_CONV), const2)
    return pl.pallas_call(
        body, name="forward", grid=(tiles + 1,),
        in_specs=[pl.BlockSpec((tm, D_MODEL), ahead),
                  pl.BlockSpec((N_CHIPS, D_MODEL, W_IN_SHARD), const3, pipeline_mode=pl.Buffered(1)),
                  pl.BlockSpec((1, D_IN), const2),
                  pl.BlockSpec((MEM_LEN, D_MODEL), example),
                  pl.BlockSpec((CONV_WIDTH, D_CONV), const2), vec, vec, vec, vec, vec,
                  pl.BlockSpec((N_GHEADS, CHUNK, CHUNK), const3),
                  pl.BlockSpec((CHUNK, N_GHEADS), const2), ANY],
        out_specs=[pl.BlockSpec((tm, D_IN), ahead), pl.BlockSpec((tm, D_MIX), behind),
                   pl.BlockSpec((tm, D_CONV), behind)],
        out_shape=[jax.ShapeDtypeStruct((n, D_IN), BF16), jax.ShapeDtypeStruct((n + tm, D_MIX), BF16),
                   jax.ShapeDtypeStruct((n + tm, D_CONV), F32)],
        scratch_shapes=[pltpu.VMEM((SUBLANES, HALO + tm, D_CONV), F32), pltpu.VMEM((2, tm, D_IN), BF16)],
        compiler_params=_params(("arbitrary",)),
    )(x2, w_full, b_in, kv, cw, cb, cg, cbeta, gg_, gb_, ws, bs_t, after)


ROW_LOSS = 3


def _out_proj_loss(ycat, wout_full, b_out, ln_g, ln_b, x2, tgt2, tm):
    n = x2.shape[0]

    def body(y_ref, w_ref, bo_ref, g_ref, b_ref, x_ref, t_ref, dr_ref, dy_ref, vec_ref):
        i = pl.program_id(0)

        @pl.when(i == 0)
        def _():
            vec_ref[...] = jnp.zeros_like(vec_ref)

        r = ALPHA * x_ref[...] + _dot(y_ref[...], w_ref[...]) + bo_ref[...]
        rhat, rstd = _ln_stats(r)
        diff = rhat * g_ref[...] + b_ref[...] - t_ref[...]
        loss = 0.5 * jnp.sum(jnp.mean(diff * diff, axis=-1, keepdims=True), axis=0, keepdims=True)
        dout = diff * (1.0 / D_MODEL)
        dr = _ln_bwd(dout * g_ref[...], rhat, rstd)
        vec_ref[0:1, :] += _colsum(dr)
        vec_ref[1:2, :] += _colsum(dout * rhat)
        vec_ref[2:3, :] += _colsum(dout)
        vec_ref[ROW_LOSS:ROW_LOSS + 1, :] += jnp.broadcast_to(loss, (1, D_MODEL))
        dr_ref[...] = dr
        dy_ref[...] = _dot_nt(dr.astype(BF16), w_ref[...]).astype(BF16)

    row = lambda i: (i, 0)
    const = lambda i: (0, 0)
    vec = pl.BlockSpec((1, D_MODEL), const)
    return pl.pallas_call(
        body, name="out_proj_loss", grid=(n // tm,),
        in_specs=[pl.BlockSpec((tm, D_MIX), row), pl.BlockSpec((D_MIX, D_MODEL), const), vec, vec, vec,
                  pl.BlockSpec((tm, D_MODEL), row), pl.BlockSpec((tm, D_MODEL), row)],
        out_specs=[pl.BlockSpec((tm, D_MODEL), row), pl.BlockSpec((tm, D_MIX), row),
                   pl.BlockSpec((8, D_MODEL), const)],
        out_shape=[jax.ShapeDtypeStruct((n, D_MODEL), F32), jax.ShapeDtypeStruct((n, D_MIX), BF16),
                   jax.ShapeDtypeStruct((8, D_MODEL), F32)],
        compiler_params=_params(("arbitrary",)),
    )(ycat, wout_full, b_out, ln_g, ln_b, x2, tgt2)


ROW_CB, ROW_CLG, ROW_CLB, ROW_GLG, ROW_GLB = 32, 33, 34, 35, 36


def _branch_bwd(h, z, dy, kv, cw, clg, clb, glg, glb, ws, bs_t, w_full, dr, nb, seq, tm, after):
    nt = seq // tm
    n = nb * seq
    tiles = nb * nt

    def place(s):
        sc = jnp.minimum(s, tiles - 1)
        return sc // nt, nt - 1 - sc % nt

    def body(h_ref, z_ref, dy_ref, kv_ref, cw_ref, clg_ref, clb_ref, glg_ref, glb_ref,
             ws_ref, bst_ref, w_ref, dr_ref, after_ref,
             dh_ref, gbin_ref, g768_ref, gws_ref, gbst_ref, dkv_ref, gbkv_ref, gx_ref,
             dzbuf, dvnbuf, dhkeep):
        step = pl.program_id(0)
        live = step < tiles
        i = jnp.minimum(step, tiles - 1) % nt
        every = slice(None)
        keep_now = step % 2
        gain = jnp.where(live, 1.0, 0.0).astype(F32)

        @pl.when(step == 0)
        def _():
            gbin_ref[...] = jnp.zeros_like(gbin_ref)
            g768_ref[...] = jnp.zeros_like(g768_ref)
            gws_ref[...] = jnp.zeros_like(gws_ref)
            gbst_ref[...] = jnp.zeros_like(gbst_ref)
            gbkv_ref[...] = jnp.zeros_like(gbkv_ref)
            dhkeep[1] = jnp.zeros((tm, D_IN), BF16)

        @pl.when(live & (i == 0))
        def _():
            dkv_ref[...] = jnp.zeros_like(dkv_ref)

        prev = 1 - keep_now
        gx_ref[...] = ALPHA * dr_ref[...]

        def dx_piece(j):
            gx_ref[...] += _dot_nt(dhkeep[prev, :, j * W_IN_SHARD:(j + 1) * W_IN_SHARD], w_ref[j])

        def put(rows, col, width, val):
            vb = val.astype(BF16)
            dh_ref[rows, col:col + width] = vb
            dhkeep[keep_now, rows, col:col + width] = vb

        def emit(col, width, val):
            gbin_ref[:, col:col + width] += _colsum(val)
            put(every, col, width, val)

        d_c = dy_ref[:, 0:D_CONV].astype(F32) * gain
        cgate = _f32(h_ref, every, C_GATE, D_CONV)
        sg = _sigmoid(cgate)
        zhat, zrstd = _ln_stats(z_ref[...])
        zn = zhat * clg_ref[...] + clb_ref[...]
        szn = _sigmoid(zn)
        emit(C_GATE, D_CONV, d_c * (zn * szn) * _dsilu(cgate, sg))
        dzn = d_c * (cgate * sg) * _dsilu(zn, szn)
        g768_ref[ROW_CLG:ROW_CLG + 1, :] += _colsum(dzn * zhat)
        g768_ref[ROW_CLB:ROW_CLB + 1, :] += _colsum(dzn)
        dz = _ln_bwd(dzn * clg_ref[...], zhat, zrstd)
        g768_ref[ROW_CB:ROW_CB + 1, :] += _colsum(dz)

        follows = live & (i > 0)

        @pl.when(jnp.logical_not(follows))
        def _():
            dzbuf[0, tm:tm + HALO, :] = jnp.zeros((HALO, D_CONV), F32)

        @pl.when(follows)
        def _():
            dzbuf[0, tm:tm + HALO, :] = dzbuf[0, 0:HALO, :]

        dzbuf[0, 0:tm, :] = dz
        _shifted_planes(dzbuf, tm)
        a = _f32(h_ref, every, C_A, D_CONV)
        sgl = _sigmoid(_f32(h_ref, every, C_GLU, D_CONV))
        hc = a * sgl

        for r in range(tm // CONV_ROWS):
            base = r * CONV_ROWS
            acc = jnp.zeros((CONV_ROWS, D_CONV), F32)
            for k in range(CONV_WIDTH):
                acc = acc + cw_ref[k:k + 1, :] * _window(dzbuf, base + 30 - k, CONV_ROWS)
            dvnbuf[base:base + CONV_ROWS, :] = acc
            if r % 4 == 1:
                dx_piece(r // 4)
        dhc = dvnbuf[...]
        emit(C_A, D_CONV, dhc * sgl)
        emit(C_GLU, D_CONV, dhc * a * sgl * (1.0 - sgl))
        for k in range(CONV_WIDTH):
            g768_ref[k:k + 1, :] += _colsum(hc * _window(dzbuf, 30 - k, tm))

        wsc, _ = _causal_ws(ws_ref)
        v, dgelu_v = _gelu_and_grad(_f32(h_ref, every, G_V, D_GMLP))
        vhat, vrstd = _ln_stats(v)
        vn = (vhat * glg_ref[...] + glb_ref[...]).astype(BF16)
        for ch in range(tm // CHUNK):
            rows = slice(ch * CHUNK, (ch + 1) * CHUNK)
            for hd in range(N_GHEADS):
                cols = slice(hd * CHUNK, (hd + 1) * CHUNK)
                vn_blk = vn[rows, cols]
                s = _dot(wsc[hd], vn_blk) + bst_ref[:, hd:hd + 1]
                u, dgelu_u = _gelu_and_grad(_f32(h_ref, rows, G_U + hd * CHUNK, CHUNK))
                gate = _f32(h_ref, rows, G_GATE + hd * CHUNK, CHUNK)
                sgate = _sigmoid(gate)
                d_g = dy_ref[rows, D_CONV + hd * CHUNK:D_CONV + (hd + 1) * CHUNK].astype(F32) * gain
                dgate = d_g * (u * s) * _dsilu(gate, sgate)
                gbin_ref[:, G_GATE + hd * CHUNK:G_GATE + (hd + 1) * CHUNK] += _colsum(dgate)
                put(rows, G_GATE + hd * CHUNK, CHUNK, dgate)
                dyg = d_g * (gate * sgate)
                du = dyg * s * dgelu_u
                gbin_ref[:, G_U + hd * CHUNK:G_U + (hd + 1) * CHUNK] += _colsum(du)
                put(rows, G_U + hd * CHUNK, CHUNK, du)
                ds = dyg * u
                dsb = ds.astype(BF16)
                gws_ref[hd] += _dot_nt(dsb, vn_blk)
                gbst_ref[:, hd:hd + 1] += jnp.sum(ds, axis=1, keepdims=True)
                dvnbuf[rows, cols] = _dot_tn(wsc[hd], dsb)
        dvn = dvnbuf[...]
        g768_ref[ROW_GLG:ROW_GLG + 1, :] += _colsum(dvn * vhat)
        g768_ref[ROW_GLB:ROW_GLB + 1, :] += _colsum(dvn)
        emit(G_V, D_GMLP, _ln_bwd(dvn * glg_ref[...], vhat, vrstd) * dgelu_v)

        scale = XHEAD ** -0.5
        for hd in range(N_XHEADS):
            q = h_ref[:, X_Q + hd * XHEAD:X_Q + (hd + 1) * XHEAD]
            k = kv_ref[:, hd * XHEAD:(hd + 1) * XHEAD]
            vv = kv_ref[:, D_XATT + hd * XHEAD:D_XATT + (hd + 1) * XHEAD]
            s = _dot_nt(q, k) * scale
            e = jnp.exp(s - jnp.max(s, axis=-1, keepdims=True))
            p = e * (1.0 / jnp.sum(e, axis=-1, keepdims=True))
            pb = p.astype(BF16)
            o = _dot(pb, vv)
            gate = _f32(h_ref, every, X_GATE + hd * XHEAD, XHEAD)
            sgate = _sigmoid(gate)
            d_x = dy_ref[:, 2 * D_CONV + hd * XHEAD:2 * D_CONV + (hd + 1) * XHEAD].astype(F32) * gain
            emit(X_GATE + hd * XHEAD, XHEAD, d_x * o * _dsilu(gate, sgate))
            do = (d_x * (gate * sgate)).astype(BF16)
            dp = _dot_nt(do, vv)
            dsc = (p * (dp - jnp.sum(dp * p, axis=-1, keepdims=True))).astype(BF16)
            emit(X_Q + hd * XHEAD, XHEAD, _dot(dsc, k) * scale)
            dkv_ref[:, hd * XHEAD:(hd + 1) * XHEAD] += _dot_tn(dsc, q) * scale
            dkv_ref[:, D_XATT + hd * XHEAD:D_XATT + (hd + 1) * XHEAD] += _dot_tn(pb, do)

        @pl.when(live & (i == nt - 1))
        def _():
            gbkv_ref[...] += _colsum(dkv_ref[...])

        @pl.when(step == tiles)
        def _():
            _, keep = _causal_ws(ws_ref)
            for hd in range(N_GHEADS):
                gws_ref[hd] = jnp.where(keep, gws_ref[hd], 0.0)

    def row(s):
        b, ri = place(s)
        return b * nt + ri, 0

    def row_dh(s):
        b, ri = place(s)
        return jnp.where(s < tiles, b * nt + ri, tiles), 0

    def row_prev(s):
        b, ri = place(jnp.maximum(s - 1, 0))
        return b * nt + ri, 0

    example = lambda s: (place(s)[0], 0)
    const2 = lambda s: (0, 0)
    const3 = lambda s: (0, 0, 0)
    vec = pl.BlockSpec((1, D_CONV), const2)

    return pl.pallas_call(
        body, name="branch_bwd", grid=(tiles + 1,),
        in_specs=[pl.BlockSpec((tm, D_IN), row),
                  pl.BlockSpec((tm, D_CONV), row), pl.BlockSpec((tm, D_MIX), row),
                  pl.BlockSpec((MEM_LEN, D_MODEL), example),
                  pl.BlockSpec((CONV_WIDTH, D_CONV), const2), vec, vec, vec, vec,
                  pl.BlockSpec((N_GHEADS, CHUNK, CHUNK), const3),
                  pl.BlockSpec((CHUNK, N_GHEADS), const2),
                  pl.BlockSpec((N_CHIPS, D_MODEL, W_IN_SHARD), const3, pipeline_mode=pl.Buffered(1)),
                  pl.BlockSpec((tm, D_MODEL), row_prev), ANY],
        out_specs=[pl.BlockSpec((tm, D_IN), row_dh),
                   pl.BlockSpec((1, D_IN), const2),
                   pl.BlockSpec((40, D_CONV), const2),
                   pl.BlockSpec((N_GHEADS, CHUNK, CHUNK), const3),
                   pl.BlockSpec((CHUNK, CHUNK), const2),
                   pl.BlockSpec((MEM_LEN, D_MODEL), example),
                   pl.BlockSpec((1, D_MODEL), const2),
                   pl.BlockSpec((tm, D_MODEL), row_prev)],
        out_shape=[jax.ShapeDtypeStruct((n + tm, D_IN), BF16),
                   jax.ShapeDtypeStruct((1, D_IN), F32),
                   jax.ShapeDtypeStruct((40, D_CONV), F32),
                   jax.ShapeDtypeStruct((N_GHEADS, CHUNK, CHUNK), F32),
                   jax.ShapeDtypeStruct((CHUNK, CHUNK), F32),
                   jax.ShapeDtypeStruct((nb * MEM_LEN, D_MODEL), F32),
                   jax.ShapeDtypeStruct((1, D_MODEL), F32),
                   jax.ShapeDtypeStruct((n, D_MODEL), F32)],
        scratch_shapes=[pltpu.VMEM((SUBLANES, tm + HALO, D_CONV), F32), pltpu.VMEM((tm, D_CONV), F32),
                        pltpu.VMEM((2, tm, D_IN), BF16)],
        compiler_params=_params(("arbitrary",)),
    )(h, z, dy, kv, cw, clg, clb, glg, glb, ws, bs_t, w_full, dr, after)


def _grad_w(a, b, tk, name):
    m = a.shape[1]
    kdim, ncols = b.shape
    nk = kdim // tk
    shard = m // N_CHIPS
    oshape = (shard, ncols)
    a_spec = pl.BlockSpec((tk, shard), lambda j, kk: (kk, j))
    b_spec = pl.BlockSpec((tk, ncols), lambda j, kk: (kk, 0))

    def body(a_ref, b_ref, own_ref, ob_ref, acc):
        j = pl.program_id(0)
        kk = pl.program_id(1)
        mine = 2 * lax.axis_index("x") + lax.axis_index("y")

        @pl.when(kk == 0)
        def _():
            acc[...] = jnp.zeros_like(acc)

        acc[...] += _dot_tn(a_ref[...].astype(BF16), b_ref[...].astype(BF16))

        @pl.when(kk == nk - 1)
        def _():
            ob_ref[...] = acc[...].astype(BF16)

        @pl.when((kk == nk - 1) & (j == mine))
        def _():
            own_ref[...] = acc[...]

    return pl.pallas_call(
        body, name=name, grid=(N_CHIPS, nk),
        in_specs=[a_spec, b_spec],
        out_specs=[pl.BlockSpec(oshape, lambda j, kk: (0, 0)),
                   pl.BlockSpec((None,) + oshape, lambda j, kk: (j, 0, 0))],
        out_shape=[jax.ShapeDtypeStruct(oshape, F32), jax.ShapeDtypeStruct((N_CHIPS,) + oshape, BF16)],
        scratch_shapes=[pltpu.VMEM(oshape, F32)],
        compiler_params=_params(("arbitrary", "arbitrary")),
    )(a, b)


def _grad_w_block(a, b, block, tk, name, dtype, after):
    kdim, m = a.shape
    shard = b.shape[1] // N_CHIPS
    nk = kdim // tk

    def body(block_ref, a_ref, b_ref, after_ref, o_ref, acc):
        kk = pl.program_id(0)

        @pl.when(kk == 0)
        def _():
            acc[...] = jnp.zeros_like(acc)

        acc[...] += _dot_tn(a_ref[...].astype(BF16), b_ref[...])

        @pl.when(kk == nk - 1)
        def _():
            o_ref[...] = acc[...].astype(dtype)

    grid_spec = pltpu.PrefetchScalarGridSpec(
        num_scalar_prefetch=1, grid=(nk,),
        in_specs=[pl.BlockSpec((tk, m), lambda kk, blk: (kk, 0)),
                  pl.BlockSpec((tk, shard), lambda kk, blk: (kk, blk[0])), ANY],
        out_specs=pl.BlockSpec((m, shard), lambda kk, blk: (0, 0)),
        scratch_shapes=[pltpu.VMEM((m, shard), F32)])
    return pl.pallas_call(
        body, name=name, grid_spec=grid_spec, out_shape=jax.ShapeDtypeStruct((m, shard), dtype),
        compiler_params=_params(("arbitrary",)),
    )(block, a, b, after)


def _sum_small(owns, gots):
    n = len(owns)

    def body(*refs):
        for a in range(n):
            o_ref, g_ref, out_ref = refs[a], refs[n + a], refs[2 * n + a]
            out_ref[...] = (o_ref[...] + g_ref[1]) + (g_ref[0] + g_ref[2])

    return pl.pallas_call(
        body, name="sum_small", out_shape=[jax.ShapeDtypeStruct(o.shape, F32) for o in owns],
        compiler_params=pltpu.CompilerParams(vmem_limit_bytes=VMEM_LIMIT),
    )(*owns, *gots)


def _sum_chips(own, gots, tr, name):
    r, ccols = own.shape

    def body(o_ref, gx_ref, gy_ref, gxy_ref, out_ref):
        out_ref[...] = (o_ref[...] + gy_ref[...].astype(F32)) + (gx_ref[...].astype(F32) + gxy_ref[...].astype(F32))

    return pl.pallas_call(
        body, name=name, grid=(r // tr,),
        in_specs=[pl.BlockSpec((tr, ccols), lambda i: (i, 0))]
        + [pl.BlockSpec((None, tr, ccols), lambda i, slot=slot: (slot, i, 0)) for _, slot in gots],
        out_specs=pl.BlockSpec((tr, ccols), lambda i: (i, 0)),
        out_shape=jax.ShapeDtypeStruct((r, ccols), F32),
        compiler_params=_params(("arbitrary",)),
    )(own, *[g for g, _ in gots])


def _exchange_cores(parts):
    npart = len(parts)

    def body(*refs):
        in_refs = refs[0:npart]
        out_refs = refs[npart:2 * npart]
        send_sems, recv_sems = refs[2 * npart:]
        sibling = (lax.axis_index("x"), lax.axis_index("y"), 1 - lax.axis_index("c"))
        copies = [pltpu.make_async_remote_copy(
            src_ref=in_refs[a], dst_ref=out_refs[a], send_sem=send_sems.at[a], recv_sem=recv_sems.at[a],
            device_id=sibling, device_id_type=MESH_ID) for a in range(npart)]
        for cp in copies:
            cp.start()
        for cp in copies:
            cp.wait_recv()
        for cp in copies:
            cp.wait_send()

    return pl.pallas_call(
        body, name="exchange_cores", out_shape=[jax.ShapeDtypeStruct(p.shape, p.dtype) for p in parts],
        in_specs=[ANY] * npart, out_specs=[ANY] * npart,
        scratch_shapes=[pltpu.SemaphoreType.DMA((npart,)), pltpu.SemaphoreType.DMA((npart,))],
    )(*parts)


def _adam(g, w, m, v):
    mn = ADAM_B1 * m + (1.0 - ADAM_B1) * g
    vn = ADAM_B2 * v + (1.0 - ADAM_B2) * (g * g)
    return g, -ADAM_LR * ((mn / BC1) / (jnp.sqrt(vn / BC2) + ADAM_EPS) + ADAM_WD * w), mn, vn


def _adamw(a, b, w, m, v, tr, name):
    r, ccols = w.shape

    def body(a_ref, b_ref, w_ref, m_ref, v_ref, *outs):
        res = _adam(a_ref[...] + b_ref[...], w_ref[...], m_ref[...], v_ref[...])
        for which in range(4):
            outs[which][...] = res[which]

    spec = pl.BlockSpec((tr, ccols), lambda i: (i, 0))
    shape = jax.ShapeDtypeStruct((r, ccols), F32)
    return pl.pallas_call(
        body, name=name, grid=(r // tr,), in_specs=[spec] * 5, out_specs=[spec] * 4, out_shape=[shape] * 4,
        compiler_params=_params(("arbitrary",)),
    )(a, b, w, m, v)


SMALL = ["b_in", "conv_b", "conv_ln_g", "conv_ln_b", "gmlp_ln_g", "gmlp_ln_b", "gmlp_ws", "gmlp_bs", "b_kv", "b_out",
         "ln_g", "ln_b"]


def _adamw_small(a_parts, b_parts, params):
    nparts, nparams = len(a_parts), len(params)

    def body(*refs):
        a = refs[0:nparts]
        b = refs[nparts:2 * nparts]
        prm = refs[2 * nparts:2 * nparts + 3 * nparams]
        outs = refs[2 * nparts + 3 * nparams:]
        gb_in, g768, gws, gbs_t, gb_kv, vec3 = [a[q][...] + b[q][...] for q in range(nparts)]
        grads = [gb_in, g768[ROW_CB:ROW_CB + 1], g768[ROW_CLG:ROW_CLG + 1], g768[ROW_CLB:ROW_CLB + 1],
                 g768[ROW_GLG:ROW_GLG + 1], g768[ROW_GLB:ROW_GLB + 1], gws, jnp.transpose(gbs_t)[0:N_GHEADS, :],
                 gb_kv, vec3[0:1], vec3[1:2], vec3[2:3]]
        for q, g in enumerate(grads):
            res = _adam(g, prm[3 * q][...], prm[3 * q + 1][...], prm[3 * q + 2][...])
            for which in range(4):
                outs[4 * q + which][...] = res[which]

    flat = [t for p in params for t in p]
    out_shape = [jax.ShapeDtypeStruct(p[0].shape, F32) for p in params for _ in range(4)]
    return pl.pallas_call(
        body, name="adamw_small", out_shape=out_shape,
        compiler_params=pltpu.CompilerParams(vmem_limit_bytes=VMEM_LIMIT),
    )(*a_parts, *b_parts, *flat)


def kernel(x, mem, w_in, b_in, conv_w, conv_b, conv_ln_g, conv_ln_b, gmlp_ln_g, gmlp_ln_b, gmlp_ws, gmlp_bs, w_kv, b_kv, w_out, b_out, ln_g, ln_b, loss_target, m_w_in, m_b_in, m_conv_w, m_conv_b, m_conv_ln_g, m_conv_ln_b, m_gmlp_ln_g, m_gmlp_ln_b, m_gmlp_ws, m_gmlp_bs, m_w_kv, m_b_kv, m_w_out, m_b_out, m_ln_g, m_ln_b, v_w_in, v_b_in, v_conv_w, v_conv_b, v_conv_ln_g, v_conv_ln_b, v_gmlp_ln_g, v_gmlp_ln_b, v_gmlp_ws, v_gmlp_bs, v_w_kv, v_b_kv, v_w_out, v_b_out, v_ln_g, v_ln_b):
    weights = dict(b_in=b_in, conv_b=conv_b, conv_ln_g=conv_ln_g, conv_ln_b=conv_ln_b, gmlp_ln_g=gmlp_ln_g,
                   gmlp_ln_b=gmlp_ln_b, gmlp_ws=gmlp_ws, gmlp_bs=gmlp_bs, b_kv=b_kv, b_out=b_out, ln_g=ln_g, ln_b=ln_b)
    mom_m = dict(b_in=m_b_in, conv_b=m_conv_b, conv_ln_g=m_conv_ln_g, conv_ln_b=m_conv_ln_b, gmlp_ln_g=m_gmlp_ln_g,
                 gmlp_ln_b=m_gmlp_ln_b, gmlp_ws=m_gmlp_ws, gmlp_bs=m_gmlp_bs, b_kv=m_b_kv, b_out=m_b_out,
                 ln_g=m_ln_g, ln_b=m_ln_b)
    mom_v = dict(b_in=v_b_in, conv_b=v_conv_b, conv_ln_g=v_conv_ln_g, conv_ln_b=v_conv_ln_b, gmlp_ln_g=v_gmlp_ln_g,
                 gmlp_ln_b=v_gmlp_ln_b, gmlp_ws=v_gmlp_ws, gmlp_bs=v_gmlp_bs, b_kv=v_b_kv, b_out=v_b_out,
                 ln_g=v_ln_g, ln_b=v_ln_b)
    nb, seq, _ = x.shape
    n = nb * seq
    tm = 256
    tk = min(1024, n)
    x2 = x.reshape(n, D_MODEL)
    tgt2 = loss_target.reshape(n, D_MODEL)
    mem2 = mem.reshape(nb * MEM_LEN, D_MODEL)
    chip = 2 * lax.axis_index("x") + lax.axis_index("y")
    bs_t = jnp.transpose(gmlp_bs[0])

    own_kv = [w_kv[0].astype(BF16), conv_w[0]]
    own_out = [w_out[0].astype(BF16)]
    ga = _start_exchange("gather_kv_start", own_kv, [False] * 2)
    win_g = _gather_w_in(w_in[0].astype(BF16), ga["token"])
    wkv_g, cw_g = _place_shards(own_kv, _wait_exchange("gather_kv_wait", ga, win_g), "place_kv")
    wkv_full = wkv_g.reshape(D_MODEL, D_MODEL)
    cw_full = jnp.transpose(cw_g, (1, 0, 2)).reshape(CONV_WIDTH, D_CONV)
    kv = _kv_proj(mem2, wkv_full, b_kv)
    gb = _start_exchange("gather_out_start", own_out, [False])
    h, ycat, z = _forward(x2, win_g, b_in, kv, cw_full, conv_b, conv_ln_g, conv_ln_b, gmlp_ln_g, gmlp_ln_b,
                          gmlp_ws[0], bs_t, nb, seq, tm, gb["token"])
    (wout_g,) = _place_shards(own_out, _wait_exchange("gather_out_wait", gb, ycat), "place_out")
    wout_full = wout_g.reshape(D_MIX, D_MODEL)
    dr, dycat, vec3 = _out_proj_loss(ycat, wout_full, b_out, ln_g, ln_b, x2, tgt2, min(512, n))

    own_wout, gwout_b = _grad_w(ycat, dr, tk, "grad_w_out")
    ex1 = _start_exchange("exchange1_start", [gwout_b, vec3], [True, False])
    dh, gb_in, g768, gws, gbs_t, dkv, gb_kv, grad_x2 = _branch_bwd(
        h, z, dycat, kv, cw_full, conv_ln_g, conv_ln_b, gmlp_ln_g, gmlp_ln_b, gmlp_ws[0], bs_t, win_g, dr,
        nb, seq, tm, ex1["token"])
    own_wkv, gwkv_b = _grad_w(mem2, dkv, nb * MEM_LEN, "grad_w_kv")
    small2 = [gb_in, g768, gws, gbs_t, gb_kv]
    ex2 = _start_exchange("exchange2_start", [gwkv_b] + small2, [True] + [False] * 5)
    block_of = lambda flip_bits: (chip ^ flip_bits).astype(jnp.int32).reshape(1)
    after, ex3 = ex2["token"], {}
    for flip, bits in ((2, 3), (1, 1), (0, 2)):
        part = _grad_w_block(x2, dh, block_of(bits), tk, f"grad_w_in_{flip}", BF16, after)
        ex3[flip] = _start_exchange(f"exchange3{flip}_start", [part], [False], (flip,))
        after = ex3[flip]["token"]
    own_win = _grad_w_block(x2, dh, block_of(0), tk, "grad_w_in_own", F32, after)
    got_wout, got_vec3 = _wait_exchange("exchange1_wait", ex1, own_win)
    got2 = _wait_exchange("exchange2_wait", ex2, own_win)
    got_win = {flip: _wait_exchange(f"exchange3{flip}_wait", ex3[flip], own_win)[0] for flip in (2, 1, 0)}

    sum_win = _sum_chips(own_win, [(got_win[0], 0), (got_win[1], 0), (got_win[2], 0)], 256, "sum_w_in")
    sum_wout = _sum_chips(own_wout, [(got_wout, 0), (got_wout, 1), (got_wout, 2)], 256, "sum_w_out")
    sum_wkv = _sum_chips(own_wkv, [(got2[0], 0), (got2[0], 1), (got2[0], 2)], 256, "sum_w_kv")
    sum_small = list(_sum_small(small2 + [vec3], got2[1:] + [got_vec3]))
    sib = list(_exchange_cores([sum_win, sum_wout, sum_wkv] + sum_small))
    loss = sum_small[5][ROW_LOSS, 0] + sib[8][ROW_LOSS, 0]

    big = {}
    big["w_in"] = _adamw(sum_win, sib[0], w_in[0], m_w_in[0], v_w_in[0], 256, "adamw_w_in")
    big["w_out"] = _adamw(sum_wout, sib[1], w_out[0], m_w_out[0], v_w_out[0], 256, "adamw_w_out")
    big["w_kv"] = _adamw(sum_wkv, sib[2], w_kv[0], m_w_kv[0], v_w_kv[0], 256, "adamw_w_kv")
    lead = {"gmlp_ws", "gmlp_bs"}
    strip = lambda k, t: t[0] if k in lead else t
    sm = _adamw_small(sum_small, sib[3:], [tuple(strip(k, t[k]) for t in (weights, mom_m, mom_v)) for k in SMALL])
    small_out = {k: [sm[4 * q + which][None] if k in lead else sm[4 * q + which] for which in range(4)]
                 for q, k in enumerate(SMALL)}
    cw_a = lax.dynamic_slice_in_dim(sum_small[1][0:CONV_WIDTH + 1], chip * CONV_SHARD, CONV_SHARD, axis=1)
    cw_b = lax.dynamic_slice_in_dim(sib[4][0:CONV_WIDTH + 1], chip * CONV_SHARD, CONV_SHARD, axis=1)
    cwp = ((0, 1), (0, 0))
    cw_out = _adamw(cw_a, cw_b, jnp.pad(conv_w[0], cwp), jnp.pad(m_conv_w[0], cwp), jnp.pad(v_conv_w[0], cwp),
                    CONV_WIDTH + 1, "adamw_conv_w")

    order = ["w_in", "b_in", "conv_w", "conv_b", "conv_ln_g", "conv_ln_b", "gmlp_ln_g", "gmlp_ln_b", "gmlp_ws",
             "gmlp_bs", "w_kv", "b_kv", "w_out", "b_out", "ln_g", "ln_b"]
    result = [loss, grad_x2.reshape(nb, seq, D_MODEL)]
    for which in range(4):
        for k in order:
            if k in big:
                result.append(big[k][which][None])
            elif k == "conv_w":
                result.append(cw_out[which][0:CONV_WIDTH][None])
            else:
                result.append(small_out[k][which])
    return tuple(result)
```

```python
import functools
import math

import jax
import jax.numpy as jnp
from jax import lax
from jax.experimental import pallas as pl
from jax.experimental.pallas import tpu as pltpu

F32 = jnp.float32
BF16 = jnp.bfloat16

D_MODEL = 1024
MEM_LEN = 256
D_MIX = 2048
D_CONV = 768
D_GMLP = 768
D_XATT = 512
N_XHEADS = 4
XHEAD = 128
CONV_WIDTH = 31
CHUNK = 128
N_GHEADS = 6
D_IN = 3 * D_CONV + 3 * D_GMLP + 2 * D_XATT
ALPHA = 2.0 ** 0.25
LN_EPS = 1e-5
N_CHIPS = 4
W_IN_SHARD = D_IN // N_CHIPS
W_OUT_SHARD = D_MIX // N_CHIPS
W_KV_SHARD = D_MODEL // N_CHIPS
CONV_SHARD = D_CONV // N_CHIPS
HALO = 32

C_A, C_GLU, C_GATE = 0, 768, 1536
G_U, G_V, G_GATE = 2304, 3072, 3840
X_Q, X_GATE = 4608, 5120

ADAM_LR = 0.001
ADAM_B1 = 0.9
ADAM_B2 = 0.999
ADAM_EPS = 1e-08
ADAM_WD = 0.01
ADAM_STEP = 10
BC1 = 1.0 - ADAM_B1 ** ADAM_STEP
BC2 = 1.0 - ADAM_B2 ** ADAM_STEP

VMEM_LIMIT = 56 * 1024 * 1024
MESH_ID = pl.DeviceIdType.MESH
ANY = pl.BlockSpec(memory_space=pl.ANY)

GELU_C0 = math.sqrt(2.0 / math.pi)
GELU_C1 = 0.044715


def _sigmoid(v):
    return 0.5 + 0.5 * jnp.tanh(0.5 * v)


def _f32(ref, rows, col, width):
    return ref[rows, col:col + width].astype(F32)


def _dsilu(v, s):
    return s * (1.0 + v * (1.0 - s))


def _gelu_and_grad(v):
    t = jnp.tanh(GELU_C0 * (v + GELU_C1 * v * v * v))
    g = 0.5 * v * (1.0 + t)
    dg = 0.5 * (1.0 + t) + 0.5 * v * (1.0 - t * t) * (GELU_C0 * (1.0 + 3.0 * GELU_C1 * v * v))
    return g, dg


def _gelu(v):
    return 0.5 * v * (1.0 + jnp.tanh(GELU_C0 * (v + GELU_C1 * v * v * v)))


def _ln_stats(v):
    mu = jnp.mean(v, axis=-1, keepdims=True)
    vc = v - mu
    var = jnp.mean(vc * vc, axis=-1, keepdims=True)
    rstd = lax.rsqrt(var + LN_EPS)
    return vc * rstd, rstd


def _ln_bwd(dvhat, vhat, rstd):
    m1 = jnp.mean(dvhat, axis=-1, keepdims=True)
    m2 = jnp.mean(dvhat * vhat, axis=-1, keepdims=True)
    return rstd * (dvhat - m1 - vhat * m2)


def _colsum(v):
    return jnp.sum(v, axis=0, keepdims=True)


def _dot(a, b):
    return jnp.dot(a, b, preferred_element_type=F32)


def _dot_nt(a, b):
    return lax.dot_general(a, b, (((1,), (1,)), ((), ())), preferred_element_type=F32)


def _dot_tn(a, b):
    return lax.dot_general(a, b, (((0,), (0,)), ((), ())), preferred_element_type=F32)


def _causal_ws(ws_ref):
    row = lax.broadcasted_iota(jnp.int32, (CHUNK, CHUNK), 0)
    col = lax.broadcasted_iota(jnp.int32, (CHUNK, CHUNK), 1)
    keep = col <= row
    return [jnp.where(keep, ws_ref[hd], 0.0).astype(BF16) for hd in range(N_GHEADS)], keep


def _params(sem):
    return pltpu.CompilerParams(dimension_semantics=sem, vmem_limit_bytes=VMEM_LIMIT)


def _peer_chips():
    x, y, c = lax.axis_index("x"), lax.axis_index("y"), lax.axis_index("c")
    return [(1 - x, y), (x, 1 - y), (1 - x, 1 - y)], 2 * x + y, c


HBM = pl.BlockSpec(memory_space=pltpu.HBM)
SEM = pl.BlockSpec(memory_space=pltpu.SEMAPHORE)
EFFECT = pltpu.SideEffectType.DATAFLOW_SIDE_EFFECTING
ALL_FLIPS = (0, 1, 2)


def _exchange_copies(src_refs, land_refs, per_chip, flips, send_sems, recv_sems):
    chips, _, c = _peer_chips()
    n = len(src_refs)
    copies = []
    for q, p in enumerate(flips):
        px, py = chips[p]
        for a in range(n):
            src = src_refs[a].at[2 * px + py] if per_chip[a] else src_refs[a]
            copies.append(pltpu.make_async_remote_copy(
                src_ref=src, dst_ref=land_refs[a].at[q], send_sem=send_sems.at[n * q + a],
                recv_sem=recv_sems.at[n * q + a], device_id=(px, py, c), device_id_type=MESH_ID))
    return copies


def _start_exchange(name, srcs, per_chip, flips=ALL_FLIPS):
    n = len(srcs)
    nf = len(flips)
    lands = [lax.empty((nf,) + (s.shape[1:] if pc else s.shape), s.dtype) for s, pc in zip(srcs, per_chip)]

    def body(*refs):
        src_refs, land_refs = refs[0:n], refs[n:2 * n]
        send_sems, recv_sems = refs[2 * n], refs[2 * n + 1]
        token = refs[4 * n + 2]
        for cp in _exchange_copies(src_refs, land_refs, per_chip, flips, send_sems, recv_sems):
            cp.start()
        token[...] = jnp.zeros_like(token)

    out = pl.pallas_call(
        body, name=name,
        out_shape=(pltpu.SemaphoreType.DMA((nf * n,)), pltpu.SemaphoreType.DMA((nf * n,)),
                   *[pltpu.HBM(a.shape, a.dtype) for a in srcs + lands], jax.ShapeDtypeStruct((8, 128), F32)),
        in_specs=[HBM] * (2 * n),
        out_specs=(SEM, SEM, *[HBM] * (2 * n), pl.BlockSpec(memory_space=pltpu.VMEM)),
        input_output_aliases={a: 2 + a for a in range(2 * n)},
        compiler_params=pltpu.CompilerParams(has_side_effects=EFFECT),
    )(*[pltpu.with_memory_space_constraint(a, pltpu.HBM) for a in srcs + lands])
    return dict(send=out[0], recv=out[1], thru=list(out[2:2 * n + 2]), token=out[2 * n + 2], per_chip=per_chip,
                flips=flips)


def _wait_exchange(name, started, after):
    thru, per_chip, flips = started["thru"], started["per_chip"], started["flips"]
    n = len(thru) // 2

    def body(*refs):
        src_refs, land_refs = refs[0:n], refs[n:2 * n]
        send_sems, recv_sems = refs[2 * n], refs[2 * n + 1]
        for cp in _exchange_copies(src_refs, land_refs, per_chip, flips, send_sems, recv_sems):
            cp.wait_send()
            cp.wait_recv()

    out = pl.pallas_call(
        body, name=name, out_shape=tuple(pltpu.HBM(a.shape, a.dtype) for a in thru),
        in_specs=[HBM] * (2 * n) + [SEM, SEM, ANY], out_specs=tuple([HBM] * (2 * n)),
        input_output_aliases={a: a for a in range(2 * n)},
        compiler_params=pltpu.CompilerParams(has_side_effects=EFFECT),
    )(*thru, started["send"], started["recv"], after)
    return list(out[n:2 * n])


def _place_shards(owns, landeds, name):
    n = len(owns)
    mine = (2 * lax.axis_index("x") + lax.axis_index("y")).astype(jnp.int32).reshape(1)

    def body(mine_ref, *refs):
        own_refs, land_refs, out_refs = refs[0:n], refs[n:2 * n], refs[2 * n:3 * n]
        k = pl.program_id(0)
        for a in range(n):
            @pl.when(k == mine_ref[0])
            def _():
                out_refs[a][...] = own_refs[a][...]

            @pl.when(k != mine_ref[0])
            def _():
                out_refs[a][...] = land_refs[a][...]

    def slot(k, mine_ref):
        d = k ^ mine_ref[0]
        return jnp.where(d == 1, 1, jnp.where(d == 3, 2, 0))

    zeros = lambda o: (0,) * len(o.shape)
    grid_spec = pltpu.PrefetchScalarGridSpec(
        num_scalar_prefetch=1, grid=(N_CHIPS,),
        in_specs=[pl.BlockSpec(o.shape, lambda k, m, o=o: zeros(o)) for o in owns]
        + [pl.BlockSpec((None,) + o.shape, lambda k, m, o=o: (slot(k, m),) + zeros(o)) for o in owns],
        out_specs=[pl.BlockSpec((None,) + o.shape, lambda k, m, o=o: (k,) + zeros(o)) for o in owns])
    return pl.pallas_call(
        body, name=name, grid_spec=grid_spec,
        out_shape=[jax.ShapeDtypeStruct((N_CHIPS,) + o.shape, o.dtype) for o in owns],
        compiler_params=_params(("arbitrary",)),
    )(mine, *owns, *landeds)


def _gather_w_in(w_shard, after):
    half = D_MODEL // 2

    def body(w_ref, after_ref, out_ref, ici_send, ici_recv, d2d_send, d2d_recv, loc_sem):
        chips, mine, c = _peer_chips()
        sibling = (lax.axis_index("x"), lax.axis_index("y"), 1 - c)
        my_rows = pl.ds(pl.multiple_of(c * half, half), half)
        their_rows = pl.ds(pl.multiple_of((1 - c) * half, half), half)
        own = pltpu.make_async_copy(w_ref, out_ref.at[mine], loc_sem)
        own.start()

        def over_ici(p, slot):
            px, py = chips[p]
            return pltpu.make_async_remote_copy(
                src_ref=w_ref.at[my_rows, :], dst_ref=out_ref.at[slot, my_rows, :], send_sem=ici_send.at[p],
                recv_sem=ici_recv.at[p], device_id=(px, py, c), device_id_type=MESH_ID)

        def over_d2d(p, rows):
            px, py = chips[p]
            part = out_ref.at[2 * px + py, rows, :]
            return pltpu.make_async_remote_copy(
                src_ref=part, dst_ref=part, send_sem=d2d_send.at[p], recv_sem=d2d_recv.at[p],
                device_id=sibling, device_id_type=MESH_ID)

        for p in (1, 0, 2):
            over_ici(p, mine).start()
        for p in (1, 0, 2):
            px, py = chips[p]
            over_ici(p, 2 * px + py).wait_recv()
            over_d2d(p, my_rows).start()
        for p in (1, 0, 2):
            over_d2d(p, their_rows).wait_recv()
        for p in range(3):
            over_ici(p, mine).wait_send()
            over_d2d(p, my_rows).wait_send()
        own.wait()

    return pl.pallas_call(
        body, name="gather_w_in", out_shape=jax.ShapeDtypeStruct((N_CHIPS,) + w_shard.shape, w_shard.dtype),
        in_specs=[ANY, ANY], out_specs=ANY,
        scratch_shapes=[pltpu.SemaphoreType.DMA((3,)), pltpu.SemaphoreType.DMA((3,)), pltpu.SemaphoreType.DMA((3,)),
                        pltpu.SemaphoreType.DMA((3,)), pltpu.SemaphoreType.DMA],
    )(w_shard, after)


def _kv_proj(mem2, wkv_full, b_kv):
    m = mem2.shape[0]

    def body(m_ref, w_ref, b_ref, o_ref):
        o_ref[...] = (_dot(m_ref[...].astype(BF16), w_ref[...]) + b_ref[...]).astype(BF16)

    return pl.pallas_call(
        body, name="kv_proj", grid=(m // MEM_LEN,),
        in_specs=[pl.BlockSpec((MEM_LEN, D_MODEL), lambda i: (i, 0)),
                  pl.BlockSpec((D_MODEL, D_MODEL), lambda i: (0, 0)),
                  pl.BlockSpec((1, D_MODEL), lambda i: (0, 0))],
        out_specs=pl.BlockSpec((MEM_LEN, D_MODEL), lambda i: (i, 0)),
        out_shape=jax.ShapeDtypeStruct((m, D_MODEL), BF16),
        compiler_params=_params(("arbitrary",)),
    )(mem2, wkv_full, b_kv)


CONV_ROWS = 16
SUBLANES = 8


def _shifted_planes(buf, tm):
    rows = tm + HALO - SUBLANES
    for s in range(1, SUBLANES):
        buf[s, 0:rows, :] = buf[0, s:s + rows, :]


def _window(buf, start, rows):
    s = start % SUBLANES
    return buf[s, start - s:start - s + rows, :]


def _forward(x2, w_full, b_in, kv, cw, cb, cg, cbeta, gg_, gb_, ws, bs_t, nb, seq, tm, after):
    nt = seq // tm
    n = nb * seq
    tiles = nb * nt

    def body(x_ref, w_ref, b_ref, kv_ref, cw_ref, cb_ref, clg_ref, clb_ref, glg_ref, glb_ref, ws_ref, bst_ref,
             after_ref, hout_ref, y_ref, z_ref, xb_ref, hcbuf, hkeep):
        step = pl.program_id(0)
        i = jnp.maximum(step - 1, 0) % nt
        keep_now = step % 2
        h_ref = hkeep.at[1 - keep_now]

        @pl.when(step == 0)
        def _():
            hkeep[1] = jnp.zeros((tm, D_IN), BF16)

        xb = x_ref[...].astype(BF16)
        xb_ref[...] = xb

        def h_piece(j):
            cols = slice(j * W_IN_SHARD, (j + 1) * W_IN_SHARD)
            hj = (_dot(xb, w_ref[j]) + b_ref[:, cols]).astype(BF16)
            hout_ref[:, cols] = hj
            hkeep[keep_now, :, cols] = hj

        starts = (step <= 1) | (i == 0)

        @pl.when(starts)
        def _():
            hcbuf[0, 0:HALO, :] = jnp.zeros((HALO, D_CONV), F32)

        @pl.when(jnp.logical_not(starts))
        def _():
            hcbuf[0, 0:HALO, :] = hcbuf[0, tm:tm + HALO, :]

        every = slice(None)
        hcbuf[0, HALO:HALO + tm, :] = _f32(h_ref, every, C_A, D_CONV) * _sigmoid(_f32(h_ref, every, C_GLU, D_CONV))
        _shifted_planes(hcbuf, tm)
        for r in range(tm // CONV_ROWS):
            base = r * CONV_ROWS
            acc = jnp.broadcast_to(cb_ref[...], (CONV_ROWS, D_CONV))
            for k in range(CONV_WIDTH):
                acc = acc + cw_ref[k:k + 1, :] * _window(hcbuf, base + 2 + k, CONV_ROWS)
            z_ref[base:base + CONV_ROWS, :] = acc
            if r % 4 == 1:
                h_piece(r // 4)
        zhat, _ = _ln_stats(z_ref[...])
        zn = zhat * clg_ref[...] + clb_ref[...]
        cgate = _f32(h_ref, every, C_GATE, D_CONV)
        y_ref[:, 0:D_CONV] = (zn * _sigmoid(zn) * (cgate * _sigmoid(cgate))).astype(BF16)

        wsc, _ = _causal_ws(ws_ref)
        vhat, _ = _ln_stats(_gelu(_f32(h_ref, every, G_V, D_GMLP)))
        vn = (vhat * glg_ref[...] + glb_ref[...]).astype(BF16)
        for ch in range(tm // CHUNK):
            rows = slice(ch * CHUNK, (ch + 1) * CHUNK)
            for hd in range(N_GHEADS):
                cols = slice(hd * CHUNK, (hd + 1) * CHUNK)
                s = _dot(wsc[hd], vn[rows, cols]) + bst_ref[:, hd:hd + 1]
                u = _gelu(_f32(h_ref, rows, G_U + hd * CHUNK, CHUNK))
                gate = _f32(h_ref, rows, G_GATE + hd * CHUNK, CHUNK)
                y_ref[rows, D_CONV + hd * CHUNK:D_CONV + (hd + 1) * CHUNK] = (
                    u * s * (gate * _sigmoid(gate))).astype(BF16)

        scale = XHEAD ** -0.5
        for hd in range(N_XHEADS):
            q = h_ref[:, X_Q + hd * XHEAD:X_Q + (hd + 1) * XHEAD]
            k = kv_ref[:, hd * XHEAD:(hd + 1) * XHEAD]
            v = kv_ref[:, D_XATT + hd * XHEAD:D_XATT + (hd + 1) * XHEAD]
            s = _dot_nt(q, k) * scale
            e = jnp.exp(s - jnp.max(s, axis=-1, keepdims=True))
            p = e * (1.0 / jnp.sum(e, axis=-1, keepdims=True))
            o = _dot(p.astype(BF16), v)
            gate = _f32(h_ref, every, X_GATE + hd * XHEAD, XHEAD)
            y_ref[:, 2 * D_CONV + hd * XHEAD:2 * D_CONV + (hd + 1) * XHEAD] = (
                o * (gate * _sigmoid(gate))).astype(BF16)

    ahead = lambda s: (jnp.minimum(s, tiles - 1), 0)
    behind = lambda s: (jnp.where(s == 0, tiles, s - 1), 0)
    example = lambda s: (jnp.maximum(s - 1, 0) // nt, 0)
    const2 = lambda s: (0, 0)
    const3 = lambda s: (0, 0, 0)
    vec = pl.BlockSpec((1, D_CONV), const2)
    return pl.pallas_call(
        body, name="forward", grid=(tiles + 1,),
        in_specs=[pl.BlockSpec((tm, D_MODEL), ahead),
                  pl.BlockSpec((N_CHIPS, D_MODEL, W_IN_SHARD), const3, pipeline_mode=pl.Buffered(1)),
                  pl.BlockSpec((1, D_IN), const2),
                  pl.BlockSpec((MEM_LEN, D_MODEL), example),
                  pl.BlockSpec((CONV_WIDTH, D_CONV), const2), vec, vec, vec, vec, vec,
                  pl.BlockSpec((N_GHEADS, CHUNK, CHUNK), const3),
                  pl.BlockSpec((CHUNK, N_GHEADS), const2), ANY],
        out_specs=[pl.BlockSpec((tm, D_IN), ahead), pl.BlockSpec((tm, D_MIX), behind),
                   pl.BlockSpec((tm, D_CONV), behind), pl.BlockSpec((tm, D_MODEL), ahead)],
        out_shape=[jax.ShapeDtypeStruct((n, D_IN), BF16), jax.ShapeDtypeStruct((n + tm, D_MIX), BF16),
                   jax.ShapeDtypeStruct((n + tm, D_CONV), F32), jax.ShapeDtypeStruct((n, D_MODEL), BF16)],
        scratch_shapes=[pltpu.VMEM((SUBLANES, HALO + tm, D_CONV), F32), pltpu.VMEM((2, tm, D_IN), BF16)],
        compiler_params=_params(("arbitrary",)),
    )(x2, w_full, b_in, kv, cw, cb, cg, cbeta, gg_, gb_, ws, bs_t, after)


ROW_LOSS = 3


def _out_proj_loss(ycat, wout_full, b_out, ln_g, ln_b, x2, tgt2, tm):
    n = x2.shape[0]

    def body(y_ref, w_ref, bo_ref, g_ref, b_ref, x_ref, t_ref, dr_ref, drb_ref, dy_ref, vec_ref):
        i = pl.program_id(0)

        @pl.when(i == 0)
        def _():
            vec_ref[...] = jnp.zeros_like(vec_ref)

        r = ALPHA * x_ref[...] + _dot(y_ref[...], w_ref[...]) + bo_ref[...]
        rhat, rstd = _ln_stats(r)
        diff = rhat * g_ref[...] + b_ref[...] - t_ref[...]
        loss = 0.5 * jnp.sum(jnp.mean(diff * diff, axis=-1, keepdims=True), axis=0, keepdims=True)
        dout = diff * (1.0 / D_MODEL)
        dr = _ln_bwd(dout * g_ref[...], rhat, rstd)
        vec_ref[0:1, :] += _colsum(dr)
        vec_ref[1:2, :] += _colsum(dout * rhat)
        vec_ref[2:3, :] += _colsum(dout)
        vec_ref[ROW_LOSS:ROW_LOSS + 1, :] += jnp.broadcast_to(loss, (1, D_MODEL))
        dr_ref[...] = dr
        drb = dr.astype(BF16)
        drb_ref[...] = drb
        dy_ref[...] = _dot_nt(drb, w_ref[...]).astype(BF16)

    row = lambda i: (i, 0)
    const = lambda i: (0, 0)
    vec = pl.BlockSpec((1, D_MODEL), const)
    return pl.pallas_call(
        body, name="out_proj_loss", grid=(n // tm,),
        in_specs=[pl.BlockSpec((tm, D_MIX), row), pl.BlockSpec((D_MIX, D_MODEL), const), vec, vec, vec,
                  pl.BlockSpec((tm, D_MODEL), row), pl.BlockSpec((tm, D_MODEL), row)],
        out_specs=[pl.BlockSpec((tm, D_MODEL), row), pl.BlockSpec((tm, D_MODEL), row), pl.BlockSpec((tm, D_MIX), row),
                   pl.BlockSpec((8, D_MODEL), const)],
        out_shape=[jax.ShapeDtypeStruct((n, D_MODEL), F32), jax.ShapeDtypeStruct((n, D_MODEL), BF16),
                   jax.ShapeDtypeStruct((n, D_MIX), BF16), jax.ShapeDtypeStruct((8, D_MODEL), F32)],
        compiler_params=_params(("arbitrary",)),
    )(ycat, wout_full, b_out, ln_g, ln_b, x2, tgt2)


ROW_CB, ROW_CLG, ROW_CLB, ROW_GLG, ROW_GLB = 32, 33, 34, 35, 36


def _branch_bwd(h, z, dy, kv, cw, clg, clb, glg, glb, ws, bs_t, w_full, dr, nb, seq, tm, after):
    nt = seq // tm
    n = nb * seq
    tiles = nb * nt

    def place(s):
        sc = jnp.minimum(s, tiles - 1)
        return sc // nt, nt - 1 - sc % nt

    def body(h_ref, z_ref, dy_ref, kv_ref, cw_ref, clg_ref, clb_ref, glg_ref, glb_ref,
             ws_ref, bst_ref, w_ref, dr_ref, after_ref,
             dh_ref, gbin_ref, g768_ref, gws_ref, gbst_ref, dkv_ref, gbkv_ref, gx_ref,
             dzbuf, dvnbuf, dhkeep):
        step = pl.program_id(0)
        live = step < tiles
        i = jnp.minimum(step, tiles - 1) % nt
        every = slice(None)
        keep_now = step % 2
        gain = jnp.where(live, 1.0, 0.0).astype(F32)

        @pl.when(step == 0)
        def _():
            gbin_ref[...] = jnp.zeros_like(gbin_ref)
            g768_ref[...] = jnp.zeros_like(g768_ref)
            gws_ref[...] = jnp.zeros_like(gws_ref)
            gbst_ref[...] = jnp.zeros_like(gbst_ref)
            gbkv_ref[...] = jnp.zeros_like(gbkv_ref)
            dhkeep[1] = jnp.zeros((tm, D_IN), BF16)

        @pl.when(live & (i == 0))
        def _():
            dkv_ref[...] = jnp.zeros_like(dkv_ref)

        prev = 1 - keep_now
        gx_ref[...] = ALPHA * dr_ref[...]

        def dx_piece(j):
            gx_ref[...] += _dot_nt(dhkeep[prev, :, j * W_IN_SHARD:(j + 1) * W_IN_SHARD], w_ref[j])

        def put(rows, col, width, val):
            vb = val.astype(BF16)
            dh_ref[rows, col:col + width] = vb
            dhkeep[keep_now, rows, col:col + width] = vb

        def emit(col, width, val):
            gbin_ref[:, col:col + width] += _colsum(val)
            put(every, col, width, val)

        d_c = dy_ref[:, 0:D_CONV].astype(F32) * gain
        cgate = _f32(h_ref, every, C_GATE, D_CONV)
        sg = _sigmoid(cgate)
        zhat, zrstd = _ln_stats(z_ref[...])
        zn = zhat * clg_ref[...] + clb_ref[...]
        szn = _sigmoid(zn)
        emit(C_GATE, D_CONV, d_c * (zn * szn) * _dsilu(cgate, sg))
        dzn = d_c * (cgate * sg) * _dsilu(zn, szn)
        g768_ref[ROW_CLG:ROW_CLG + 1, :] += _colsum(dzn * zhat)
        g768_ref[ROW_CLB:ROW_CLB + 1, :] += _colsum(dzn)
        dz = _ln_bwd(dzn * clg_ref[...], zhat, zrstd)
        g768_ref[ROW_CB:ROW_CB + 1, :] += _colsum(dz)

        follows = live & (i > 0)

        @pl.when(jnp.logical_not(follows))
        def _():
            dzbuf[0, tm:tm + HALO, :] = jnp.zeros((HALO, D_CONV), F32)

        @pl.when(follows)
        def _():
            dzbuf[0, tm:tm + HALO, :] = dzbuf[0, 0:HALO, :]

        dzbuf[0, 0:tm, :] = dz
        _shifted_planes(dzbuf, tm)
        a = _f32(h_ref, every, C_A, D_CONV)
        sgl = _sigmoid(_f32(h_ref, every, C_GLU, D_CONV))
        hc = a * sgl

        for r in range(tm // CONV_ROWS):
            base = r * CONV_ROWS
            acc = jnp.zeros((CONV_ROWS, D_CONV), F32)
            for k in range(CONV_WIDTH):
                acc = acc + cw_ref[k:k + 1, :] * _window(dzbuf, base + 30 - k, CONV_ROWS)
            dvnbuf[base:base + CONV_ROWS, :] = acc
            if r % 4 == 1:
                dx_piece(r // 4)
        dhc = dvnbuf[...]
        emit(C_A, D_CONV, dhc * sgl)
        emit(C_GLU, D_CONV, dhc * a * sgl * (1.0 - sgl))
        for k in range(CONV_WIDTH):
            g768_ref[k:k + 1, :] += _colsum(hc * _window(dzbuf, 30 - k, tm))

        wsc, _ = _causal_ws(ws_ref)
        v, dgelu_v = _gelu_and_grad(_f32(h_ref, every, G_V, D_GMLP))
        vhat, vrstd = _ln_stats(v)
        vn = (vhat * glg_ref[...] + glb_ref[...]).astype(BF16)
        for ch in range(tm // CHUNK):
            rows = slice(ch * CHUNK, (ch + 1) * CHUNK)
            for hd in range(N_GHEADS):
                cols = slice(hd * CHUNK, (hd + 1) * CHUNK)
                vn_blk = vn[rows, cols]
                s = _dot(wsc[hd], vn_blk) + bst_ref[:, hd:hd + 1]
                u, dgelu_u = _gelu_and_grad(_f32(h_ref, rows, G_U + hd * CHUNK, CHUNK))
                gate = _f32(h_ref, rows, G_GATE + hd * CHUNK, CHUNK)
                sgate = _sigmoid(gate)
                d_g = dy_ref[rows, D_CONV + hd * CHUNK:D_CONV + (hd + 1) * CHUNK].astype(F32) * gain
                dgate = d_g * (u * s) * _dsilu(gate, sgate)
                gbin_ref[:, G_GATE + hd * CHUNK:G_GATE + (hd + 1) * CHUNK] += _colsum(dgate)
                put(rows, G_GATE + hd * CHUNK, CHUNK, dgate)
                dyg = d_g * (gate * sgate)
                du = dyg * s * dgelu_u
                gbin_ref[:, G_U + hd * CHUNK:G_U + (hd + 1) * CHUNK] += _colsum(du)
                put(rows, G_U + hd * CHUNK, CHUNK, du)
                ds = dyg * u
                dsb = ds.astype(BF16)
                gws_ref[hd] += _dot_nt(dsb, vn_blk)
                gbst_ref[:, hd:hd + 1] += jnp.sum(ds, axis=1, keepdims=True)
                dvnbuf[rows, cols] = _dot_tn(wsc[hd], dsb)
        dvn = dvnbuf[...]
        g768_ref[ROW_GLG:ROW_GLG + 1, :] += _colsum(dvn * vhat)
        g768_ref[ROW_GLB:ROW_GLB + 1, :] += _colsum(dvn)
        emit(G_V, D_GMLP, _ln_bwd(dvn * glg_ref[...], vhat, vrstd) * dgelu_v)

        scale = XHEAD ** -0.5
        for hd in range(N_XHEADS):
            q = h_ref[:, X_Q + hd * XHEAD:X_Q + (hd + 1) * XHEAD]
            k = kv_ref[:, hd * XHEAD:(hd + 1) * XHEAD]
            vv = kv_ref[:, D_XATT + hd * XHEAD:D_XATT + (hd + 1) * XHEAD]
            s = _dot_nt(q, k) * scale
            e = jnp.exp(s - jnp.max(s, axis=-1, keepdims=True))
            p = e * (1.0 / jnp.sum(e, axis=-1, keepdims=True))
            pb = p.astype(BF16)
            o = _dot(pb, vv)
            gate = _f32(h_ref, every, X_GATE + hd * XHEAD, XHEAD)
            sgate = _sigmoid(gate)
            d_x = dy_ref[:, 2 * D_CONV + hd * XHEAD:2 * D_CONV + (hd + 1) * XHEAD].astype(F32) * gain
            emit(X_GATE + hd * XHEAD, XHEAD, d_x * o * _dsilu(gate, sgate))
            do = (d_x * (gate * sgate)).astype(BF16)
            dp = _dot_nt(do, vv)
            dsc = (p * (dp - jnp.sum(dp * p, axis=-1, keepdims=True))).astype(BF16)
            emit(X_Q + hd * XHEAD, XHEAD, _dot(dsc, k) * scale)
            dkv_ref[:, hd * XHEAD:(hd + 1) * XHEAD] += _dot_tn(dsc, q) * scale
            dkv_ref[:, D_XATT + hd * XHEAD:D_XATT + (hd + 1) * XHEAD] += _dot_tn(pb, do)

        @pl.when(live & (i == nt - 1))
        def _():
            gbkv_ref[...] += _colsum(dkv_ref[...])

        @pl.when(step == tiles)
        def _():
            _, keep = _causal_ws(ws_ref)
            for hd in range(N_GHEADS):
                gws_ref[hd] = jnp.where(keep, gws_ref[hd], 0.0)

    def row(s):
        b, ri = place(s)
        return b * nt + ri, 0

    def row_dh(s):
        b, ri = place(s)
        return jnp.where(s < tiles, b * nt + ri, tiles), 0

    def row_prev(s):
        b, ri = place(jnp.maximum(s - 1, 0))
        return b * nt + ri, 0

    example = lambda s: (place(s)[0], 0)
    const2 = lambda s: (0, 0)
    const3 = lambda s: (0, 0, 0)
    vec = pl.BlockSpec((1, D_CONV), const2)

    return pl.pallas_call(
        body, name="branch_bwd", grid=(tiles + 1,),
        in_specs=[pl.BlockSpec((tm, D_IN), row),
                  pl.BlockSpec((tm, D_CONV), row), pl.BlockSpec((tm, D_MIX), row),
                  pl.BlockSpec((MEM_LEN, D_MODEL), example),
                  pl.BlockSpec((CONV_WIDTH, D_CONV), const2), vec, vec, vec, vec,
                  pl.BlockSpec((N_GHEADS, CHUNK, CHUNK), const3),
                  pl.BlockSpec((CHUNK, N_GHEADS), const2),
                  pl.BlockSpec((N_CHIPS, D_MODEL, W_IN_SHARD), const3, pipeline_mode=pl.Buffered(1)),
                  pl.BlockSpec((tm, D_MODEL), row_prev), ANY],
        out_specs=[pl.BlockSpec((tm, D_IN), row_dh),
                   pl.BlockSpec((1, D_IN), const2),
                   pl.BlockSpec((40, D_CONV), const2),
                   pl.BlockSpec((N_GHEADS, CHUNK, CHUNK), const3),
                   pl.BlockSpec((CHUNK, CHUNK), const2),
                   pl.BlockSpec((MEM_LEN, D_MODEL), example),
                   pl.BlockSpec((1, D_MODEL), const2),
                   pl.BlockSpec((tm, D_MODEL), row_prev)],
        out_shape=[jax.ShapeDtypeStruct((n + tm, D_IN), BF16),
                   jax.ShapeDtypeStruct((1, D_IN), F32),
                   jax.ShapeDtypeStruct((40, D_CONV), F32),
                   jax.ShapeDtypeStruct((N_GHEADS, CHUNK, CHUNK), F32),
                   jax.ShapeDtypeStruct((CHUNK, CHUNK), F32),
                   jax.ShapeDtypeStruct((nb * MEM_LEN, D_MODEL), F32),
                   jax.ShapeDtypeStruct((1, D_MODEL), F32),
                   jax.ShapeDtypeStruct((n, D_MODEL), F32)],
        scratch_shapes=[pltpu.VMEM((SUBLANES, tm + HALO, D_CONV), F32), pltpu.VMEM((tm, D_CONV), F32),
                        pltpu.VMEM((2, tm, D_IN), BF16)],
        compiler_params=_params(("arbitrary",)),
    )(h, z, dy, kv, cw, clg, clb, glg, glb, ws, bs_t, w_full, dr, after)


def _grad_w(a, b, tk, name):
    m = a.shape[1]
    kdim, ncols = b.shape
    nk = kdim // tk
    shard = m // N_CHIPS
    oshape = (shard, ncols)
    a_spec = pl.BlockSpec((tk, shard), lambda j, kk: (kk, j))
    b_spec = pl.BlockSpec((tk, ncols), lambda j, kk: (kk, 0))

    def body(a_ref, b_ref, own_ref, ob_ref, acc):
        j = pl.program_id(0)
        kk = pl.program_id(1)
        mine = 2 * lax.axis_index("x") + lax.axis_index("y")

        @pl.when(kk == 0)
        def _():
            acc[...] = jnp.zeros_like(acc)

        acc[...] += _dot_tn(a_ref[...].astype(BF16), b_ref[...].astype(BF16))

        @pl.when(kk == nk - 1)
        def _():
            ob_ref[...] = acc[...].astype(BF16)

        @pl.when((kk == nk - 1) & (j == mine))
        def _():
            own_ref[...] = acc[...]

    return pl.pallas_call(
        body, name=name, grid=(N_CHIPS, nk),
        in_specs=[a_spec, b_spec],
        out_specs=[pl.BlockSpec(oshape, lambda j, kk: (0, 0)),
                   pl.BlockSpec((None,) + oshape, lambda j, kk: (j, 0, 0))],
        out_shape=[jax.ShapeDtypeStruct(oshape, F32), jax.ShapeDtypeStruct((N_CHIPS,) + oshape, BF16)],
        scratch_shapes=[pltpu.VMEM(oshape, F32)],
        compiler_params=_params(("arbitrary", "arbitrary")),
    )(a, b)


def _grad_w_block(a, b, block, tk, name, dtype, after):
    kdim, m = a.shape
    shard = b.shape[1] // N_CHIPS
    nk = kdim // tk

    def body(block_ref, a_ref, b_ref, after_ref, o_ref, acc):
        kk = pl.program_id(0)

        @pl.when(kk == 0)
        def _():
            acc[...] = jnp.zeros_like(acc)

        acc[...] += _dot_tn(a_ref[...].astype(BF16), b_ref[...])

        @pl.when(kk == nk - 1)
        def _():
            o_ref[...] = acc[...].astype(dtype)

    grid_spec = pltpu.PrefetchScalarGridSpec(
        num_scalar_prefetch=1, grid=(nk,),
        in_specs=[pl.BlockSpec((tk, m), lambda kk, blk: (kk, 0)),
                  pl.BlockSpec((tk, shard), lambda kk, blk: (kk, blk[0])), ANY],
        out_specs=pl.BlockSpec((m, shard), lambda kk, blk: (0, 0)),
        scratch_shapes=[pltpu.VMEM((m, shard), F32)])
    return pl.pallas_call(
        body, name=name, grid_spec=grid_spec, out_shape=jax.ShapeDtypeStruct((m, shard), dtype),
        compiler_params=_params(("arbitrary",)),
    )(block, a, b, after)


def _sum_small(owns, gots):
    n = len(owns)

    def body(*refs):
        for a in range(n):
            o_ref, g_ref, out_ref = refs[a], refs[n + a], refs[2 * n + a]
            out_ref[...] = (o_ref[...] + g_ref[1]) + (g_ref[0] + g_ref[2])

    return pl.pallas_call(
        body, name="sum_small", out_shape=[jax.ShapeDtypeStruct(o.shape, F32) for o in owns],
        compiler_params=pltpu.CompilerParams(vmem_limit_bytes=VMEM_LIMIT),
    )(*owns, *gots)


def _sum_chips(own, gots, tr, name):
    r, ccols = own.shape

    def body(o_ref, gx_ref, gy_ref, gxy_ref, out_ref):
        out_ref[...] = (o_ref[...] + gy_ref[...].astype(F32)) + (gx_ref[...].astype(F32) + gxy_ref[...].astype(F32))

    return pl.pallas_call(
        body, name=name, grid=(r // tr,),
        in_specs=[pl.BlockSpec((tr, ccols), lambda i: (i, 0))]
        + [pl.BlockSpec((None, tr, ccols), lambda i, slot=slot: (slot, i, 0)) for _, slot in gots],
        out_specs=pl.BlockSpec((tr, ccols), lambda i: (i, 0)),
        out_shape=jax.ShapeDtypeStruct((r, ccols), F32),
        compiler_params=_params(("arbitrary",)),
    )(own, *[g for g, _ in gots])


def _exchange_cores(parts):
    npart = len(parts)

    def body(*refs):
        in_refs = refs[0:npart]
        out_refs = refs[npart:2 * npart]
        send_sems, recv_sems = refs[2 * npart:]
        sibling = (lax.axis_index("x"), lax.axis_index("y"), 1 - lax.axis_index("c"))
        copies = [pltpu.make_async_remote_copy(
            src_ref=in_refs[a], dst_ref=out_refs[a], send_sem=send_sems.at[a], recv_sem=recv_sems.at[a],
            device_id=sibling, device_id_type=MESH_ID) for a in range(npart)]
        for cp in copies:
            cp.start()
        for cp in copies:
            cp.wait_recv()
        for cp in copies:
            cp.wait_send()

    return pl.pallas_call(
        body, name="exchange_cores", out_shape=[jax.ShapeDtypeStruct(p.shape, p.dtype) for p in parts],
        in_specs=[ANY] * npart, out_specs=[ANY] * npart,
        scratch_shapes=[pltpu.SemaphoreType.DMA((npart,)), pltpu.SemaphoreType.DMA((npart,))],
    )(*parts)


def _adam(g, w, m, v):
    mn = ADAM_B1 * m + (1.0 - ADAM_B1) * g
    vn = ADAM_B2 * v + (1.0 - ADAM_B2) * (g * g)
    return g, -ADAM_LR * ((mn / BC1) / (jnp.sqrt(vn / BC2) + ADAM_EPS) + ADAM_WD * w), mn, vn


def _adamw(a, b, w, m, v, tr, name):
    r, ccols = w.shape

    def body(a_ref, b_ref, w_ref, m_ref, v_ref, *outs):
        res = _adam(a_ref[...] + b_ref[...], w_ref[...], m_ref[...], v_ref[...])
        for which in range(4):
            outs[which][...] = res[which]

    spec = pl.BlockSpec((tr, ccols), lambda i: (i, 0))
    shape = jax.ShapeDtypeStruct((r, ccols), F32)
    return pl.pallas_call(
        body, name=name, grid=(r // tr,), in_specs=[spec] * 5, out_specs=[spec] * 4, out_shape=[shape] * 4,
        compiler_params=_params(("arbitrary",)),
    )(a, b, w, m, v)


SMALL = ["b_in", "conv_b", "conv_ln_g", "conv_ln_b", "gmlp_ln_g", "gmlp_ln_b", "gmlp_ws", "gmlp_bs", "b_kv", "b_out",
         "ln_g", "ln_b"]


def _adamw_small(a_parts, b_parts, params):
    nparts, nparams = len(a_parts), len(params)

    def body(*refs):
        a = refs[0:nparts]
        b = refs[nparts:2 * nparts]
        prm = refs[2 * nparts:2 * nparts + 3 * nparams]
        outs = refs[2 * nparts + 3 * nparams:]
        gb_in, g768, gws, gbs_t, gb_kv, vec3 = [a[q][...] + b[q][...] for q in range(nparts)]
        grads = [gb_in, g768[ROW_CB:ROW_CB + 1], g768[ROW_CLG:ROW_CLG + 1], g768[ROW_CLB:ROW_CLB + 1],
                 g768[ROW_GLG:ROW_GLG + 1], g768[ROW_GLB:ROW_GLB + 1], gws, jnp.transpose(gbs_t)[0:N_GHEADS, :],
                 gb_kv, vec3[0:1], vec3[1:2], vec3[2:3]]
        for q, g in enumerate(grads):
            res = _adam(g, prm[3 * q][...], prm[3 * q + 1][...], prm[3 * q + 2][...])
            for which in range(4):
                outs[4 * q + which][...] = res[which]

    flat = [t for p in params for t in p]
    out_shape = [jax.ShapeDtypeStruct(p[0].shape, F32) for p in params for _ in range(4)]
    return pl.pallas_call(
        body, name="adamw_small", out_shape=out_shape,
        compiler_params=pltpu.CompilerParams(vmem_limit_bytes=VMEM_LIMIT),
    )(*a_parts, *b_parts, *flat)


def kernel(x, mem, w_in, b_in, conv_w, conv_b, conv_ln_g, conv_ln_b, gmlp_ln_g, gmlp_ln_b, gmlp_ws, gmlp_bs, w_kv, b_kv, w_out, b_out, ln_g, ln_b, loss_target, m_w_in, m_b_in, m_conv_w, m_conv_b, m_conv_ln_g, m_conv_ln_b, m_gmlp_ln_g, m_gmlp_ln_b, m_gmlp_ws, m_gmlp_bs, m_w_kv, m_b_kv, m_w_out, m_b_out, m_ln_g, m_ln_b, v_w_in, v_b_in, v_conv_w, v_conv_b, v_conv_ln_g, v_conv_ln_b, v_gmlp_ln_g, v_gmlp_ln_b, v_gmlp_ws, v_gmlp_bs, v_w_kv, v_b_kv, v_w_out, v_b_out, v_ln_g, v_ln_b):
    weights = dict(b_in=b_in, conv_b=conv_b, conv_ln_g=conv_ln_g, conv_ln_b=conv_ln_b, gmlp_ln_g=gmlp_ln_g,
                   gmlp_ln_b=gmlp_ln_b, gmlp_ws=gmlp_ws, gmlp_bs=gmlp_bs, b_kv=b_kv, b_out=b_out, ln_g=ln_g, ln_b=ln_b)
    mom_m = dict(b_in=m_b_in, conv_b=m_conv_b, conv_ln_g=m_conv_ln_g, conv_ln_b=m_conv_ln_b, gmlp_ln_g=m_gmlp_ln_g,
                 gmlp_ln_b=m_gmlp_ln_b, gmlp_ws=m_gmlp_ws, gmlp_bs=m_gmlp_bs, b_kv=m_b_kv, b_out=m_b_out,
                 ln_g=m_ln_g, ln_b=m_ln_b)
    mom_v = dict(b_in=v_b_in, conv_b=v_conv_b, conv_ln_g=v_conv_ln_g, conv_ln_b=v_conv_ln_b, gmlp_ln_g=v_gmlp_ln_g,
                 gmlp_ln_b=v_gmlp_ln_b, gmlp_ws=v_gmlp_ws, gmlp_bs=v_gmlp_bs, b_kv=v_b_kv, b_out=v_b_out,
                 ln_g=v_ln_g, ln_b=v_ln_b)
    nb, seq, _ = x.shape
    n = nb * seq
    tm = 256
    tk = min(2048, n)
    x2 = x.reshape(n, D_MODEL)
    tgt2 = loss_target.reshape(n, D_MODEL)
    mem2 = mem.reshape(nb * MEM_LEN, D_MODEL)
    chip = 2 * lax.axis_index("x") + lax.axis_index("y")
    bs_t = jnp.transpose(gmlp_bs[0])

    own_kv = [w_kv[0].astype(BF16), conv_w[0]]
    own_out = [w_out[0].astype(BF16)]
    ga = _start_exchange("gather_kv_start", own_kv, [False] * 2)
    win_g = _gather_w_in(w_in[0].astype(BF16), ga["token"])
    wkv_g, cw_g = _place_shards(own_kv, _wait_exchange("gather_kv_wait", ga, win_g), "place_kv")
    wkv_full = wkv_g.reshape(D_MODEL, D_MODEL)
    cw_full = jnp.transpose(cw_g, (1, 0, 2)).reshape(CONV_WIDTH, D_CONV)
    kv = _kv_proj(mem2, wkv_full, b_kv)
    gb = _start_exchange("gather_out_start", own_out, [False])
    h, ycat, z, x_bf = _forward(x2, win_g, b_in, kv, cw_full, conv_b, conv_ln_g, conv_ln_b, gmlp_ln_g, gmlp_ln_b,
                                gmlp_ws[0], bs_t, nb, seq, tm, gb["token"])
    (wout_g,) = _place_shards(own_out, _wait_exchange("gather_out_wait", gb, ycat), "place_out")
    wout_full = wout_g.reshape(D_MIX, D_MODEL)
    dr, dr_bf, dycat, vec3 = _out_proj_loss(ycat, wout_full, b_out, ln_g, ln_b, x2, tgt2, min(512, n))

    own_wout, gwout_b = _grad_w(ycat, dr_bf, min(2048, n), "grad_w_out")
    ex1 = _start_exchange("exchange1_start", [gwout_b, vec3], [True, False])
    dh, gb_in, g768, gws, gbs_t, dkv, gb_kv, grad_x2 = _branch_bwd(
        h, z, dycat, kv, cw_full, conv_ln_g, conv_ln_b, gmlp_ln_g, gmlp_ln_b, gmlp_ws[0], bs_t, win_g, dr,
        nb, seq, tm, ex1["token"])
    own_wkv, gwkv_b = _grad_w(mem2, dkv, nb * MEM_LEN, "grad_w_kv")
    small2 = [gb_in, g768, gws, gbs_t, gb_kv]
    ex2 = _start_exchange("exchange2_start", [gwkv_b] + small2, [True] + [False] * 5)
    block_of = lambda flip_bits: (chip ^ flip_bits).astype(jnp.int32).reshape(1)
    after, ex3 = ex2["token"], {}
    for flip, bits in ((2, 3), (1, 1), (0, 2)):
        part = _grad_w_block(x_bf, dh, block_of(bits), tk, f"grad_w_in_{flip}", BF16, after)
        ex3[flip] = _start_exchange(f"exchange3{flip}_start", [part], [False], (flip,))
        after = ex3[flip]["token"]
    own_win = _grad_w_block(x_bf, dh, block_of(0), tk, "grad_w_in_own", F32, after)
    got_wout, got_vec3 = _wait_exchange("exchange1_wait", ex1, own_win)
    got2 = _wait_exchange("exchange2_wait", ex2, own_win)
    got_win = {flip: _wait_exchange(f"exchange3{flip}_wait", ex3[flip], own_win)[0] for flip in (2, 1, 0)}

    sum_win = _sum_chips(own_win, [(got_win[0], 0), (got_win[1], 0), (got_win[2], 0)], 256, "sum_w_in")
    sum_wout = _sum_chips(own_wout, [(got_wout, 0), (got_wout, 1), (got_wout, 2)], 256, "sum_w_out")
    sum_wkv = _sum_chips(own_wkv, [(got2[0], 0), (got2[0], 1), (got2[0], 2)], 256, "sum_w_kv")
    sum_small = list(_sum_small(small2 + [vec3], got2[1:] + [got_vec3]))
    sib = list(_exchange_cores([sum_win, sum_wout, sum_wkv] + sum_small))
    loss = sum_small[5][ROW_LOSS, 0] + sib[8][ROW_LOSS, 0]

    big = {}
    big["w_in"] = _adamw(sum_win, sib[0], w_in[0], m_w_in[0], v_w_in[0], 256, "adamw_w_in")
    big["w_out"] = _adamw(sum_wout, sib[1], w_out[0], m_w_out[0], v_w_out[0], 256, "adamw_w_out")
    big["w_kv"] = _adamw(sum_wkv, sib[2], w_kv[0], m_w_kv[0], v_w_kv[0], 256, "adamw_w_kv")
    lead = {"gmlp_ws", "gmlp_bs"}
    strip = lambda k, t: t[0] if k in lead else t
    sm = _adamw_small(sum_small, sib[3:], [tuple(strip(k, t[k]) for t in (weights, mom_m, mom_v)) for k in SMALL])
    small_out = {k: [sm[4 * q + which][None] if k in lead else sm[4 * q + which] for which in range(4)]
                 for q, k in enumerate(SMALL)}
    cw_a = lax.dynamic_slice_in_dim(sum_small[1][0:CONV_WIDTH + 1], chip * CONV_SHARD, CONV_SHARD, axis=1)
    cw_b = lax.dynamic_slice_in_dim(sib[4][0:CONV_WIDTH + 1], chip * CONV_SHARD, CONV_SHARD, axis=1)
    cwp = ((0, 1), (0, 0))
    cw_out = _adamw(cw_a, cw_b, jnp.pad(conv_w[0], cwp), jnp.pad(m_conv_w[0], cwp), jnp.pad(v_conv_w[0], cwp),
                    CONV_WIDTH + 1, "adamw_conv_w")

    order = ["w_in", "b_in", "conv_w", "conv_b", "conv_ln_g", "conv_ln_b", "gmlp_ln_g", "gmlp_ln_b", "gmlp_ws",
             "gmlp_bs", "w_kv", "b_kv", "w_out", "b_out", "ln_g", "ln_b"]
    result = [loss, grad_x2.reshape(nb, seq, D_MODEL)]
    for which in range(4):
        for k in order:
            if k in big:
                result.append(big[k][which][None])
            elif k == "conv_w":
                result.append(cw_out[which][0:CONV_WIDTH][None])
            else:
                result.append(small_out[k][which])
    return tuple(result)
```

```python
import functools
import math

import jax
import jax.numpy as jnp
from jax import lax
from jax.experimental import pallas as pl
from jax.experimental.pallas import tpu as pltpu

F32 = jnp.float32
BF16 = jnp.bfloat16

D_MODEL = 1024
MEM_LEN = 256
D_MIX = 2048
D_CONV = 768
D_GMLP = 768
D_XATT = 512
N_XHEADS = 4
XHEAD = 128
CONV_WIDTH = 31
CHUNK = 128
N_GHEADS = 6
D_IN = 3 * D_CONV + 3 * D_GMLP + 2 * D_XATT
ALPHA = 2.0 ** 0.25
LN_EPS = 1e-5
N_CHIPS = 4
W_IN_SHARD = D_IN // N_CHIPS
W_OUT_SHARD = D_MIX // N_CHIPS
W_KV_SHARD = D_MODEL // N_CHIPS
CONV_SHARD = D_CONV // N_CHIPS
HALO = 32

C_A, C_GLU, C_GATE = 0, 768, 1536
G_U, G_V, G_GATE = 2304, 3072, 3840
X_Q, X_GATE = 4608, 5120

ADAM_LR = 0.001
ADAM_B1 = 0.9
ADAM_B2 = 0.999
ADAM_EPS = 1e-08
ADAM_WD = 0.01
ADAM_STEP = 10
BC1 = 1.0 - ADAM_B1 ** ADAM_STEP
BC2 = 1.0 - ADAM_B2 ** ADAM_STEP

VMEM_LIMIT = 56 * 1024 * 1024
MESH_ID = pl.DeviceIdType.MESH
ANY = pl.BlockSpec(memory_space=pl.ANY)

GELU_C0 = math.sqrt(2.0 / math.pi)
GELU_C1 = 0.044715


def _sigmoid(v):
    return 0.5 + 0.5 * jnp.tanh(0.5 * v)


def _f32(ref, rows, col, width):
    return ref[rows, col:col + width].astype(F32)


def _dsilu(v, s):
    return s * (1.0 + v * (1.0 - s))


def _gelu_and_grad(v):
    t = jnp.tanh(GELU_C0 * (v + GELU_C1 * v * v * v))
    g = 0.5 * v * (1.0 + t)
    dg = 0.5 * (1.0 + t) + 0.5 * v * (1.0 - t * t) * (GELU_C0 * (1.0 + 3.0 * GELU_C1 * v * v))
    return g, dg


def _gelu(v):
    return 0.5 * v * (1.0 + jnp.tanh(GELU_C0 * (v + GELU_C1 * v * v * v)))


def _ln_stats(v):
    mu = jnp.mean(v, axis=-1, keepdims=True)
    vc = v - mu
    var = jnp.mean(vc * vc, axis=-1, keepdims=True)
    rstd = lax.rsqrt(var + LN_EPS)
    return vc * rstd, rstd


def _ln_bwd(dvhat, vhat, rstd):
    m1 = jnp.mean(dvhat, axis=-1, keepdims=True)
    m2 = jnp.mean(dvhat * vhat, axis=-1, keepdims=True)
    return rstd * (dvhat - m1 - vhat * m2)


def _colsum(v):
    return jnp.sum(v, axis=0, keepdims=True)


def _dot(a, b):
    return jnp.dot(a, b, preferred_element_type=F32)


def _dot_nt(a, b):
    return lax.dot_general(a, b, (((1,), (1,)), ((), ())), preferred_element_type=F32)


def _dot_tn(a, b):
    return lax.dot_general(a, b, (((0,), (0,)), ((), ())), preferred_element_type=F32)


def _causal_ws(ws_ref):
    row = lax.broadcasted_iota(jnp.int32, (CHUNK, CHUNK), 0)
    col = lax.broadcasted_iota(jnp.int32, (CHUNK, CHUNK), 1)
    keep = col <= row
    return [jnp.where(keep, ws_ref[hd], 0.0).astype(BF16) for hd in range(N_GHEADS)], keep


def _params(sem):
    return pltpu.CompilerParams(dimension_semantics=sem, vmem_limit_bytes=VMEM_LIMIT)


def _peer_chips():
    x, y, c = lax.axis_index("x"), lax.axis_index("y"), lax.axis_index("c")
    return [(1 - x, y), (x, 1 - y), (1 - x, 1 - y)], 2 * x + y, c


HBM = pl.BlockSpec(memory_space=pltpu.HBM)
SEM = pl.BlockSpec(memory_space=pltpu.SEMAPHORE)
EFFECT = pltpu.SideEffectType.DATAFLOW_SIDE_EFFECTING
ALL_FLIPS = (0, 1, 2)


def _exchange_copies(src_refs, land_refs, per_chip, flips, send_sems, recv_sems):
    chips, _, c = _peer_chips()
    n = len(src_refs)
    copies = []
    for q, p in enumerate(flips):
        px, py = chips[p]
        for a in range(n):
            src = src_refs[a].at[2 * px + py] if per_chip[a] else src_refs[a]
            copies.append(pltpu.make_async_remote_copy(
                src_ref=src, dst_ref=land_refs[a].at[q], send_sem=send_sems.at[n * q + a],
                recv_sem=recv_sems.at[n * q + a], device_id=(px, py, c), device_id_type=MESH_ID))
    return copies


def _start_exchange(name, srcs, per_chip, flips=ALL_FLIPS):
    n = len(srcs)
    nf = len(flips)
    lands = [lax.empty((nf,) + (s.shape[1:] if pc else s.shape), s.dtype) for s, pc in zip(srcs, per_chip)]

    def body(*refs):
        src_refs, land_refs = refs[0:n], refs[n:2 * n]
        send_sems, recv_sems = refs[2 * n], refs[2 * n + 1]
        token = refs[4 * n + 2]
        for cp in _exchange_copies(src_refs, land_refs, per_chip, flips, send_sems, recv_sems):
            cp.start()
        token[...] = jnp.zeros_like(token)

    out = pl.pallas_call(
        body, name=name,
        out_shape=(pltpu.SemaphoreType.DMA((nf * n,)), pltpu.SemaphoreType.DMA((nf * n,)),
                   *[pltpu.HBM(a.shape, a.dtype) for a in srcs + lands], jax.ShapeDtypeStruct((8, 128), F32)),
        in_specs=[HBM] * (2 * n),
        out_specs=(SEM, SEM, *[HBM] * (2 * n), pl.BlockSpec(memory_space=pltpu.VMEM)),
        input_output_aliases={a: 2 + a for a in range(2 * n)},
        compiler_params=pltpu.CompilerParams(has_side_effects=EFFECT),
    )(*[pltpu.with_memory_space_constraint(a, pltpu.HBM) for a in srcs + lands])
    return dict(send=out[0], recv=out[1], thru=list(out[2:2 * n + 2]), token=out[2 * n + 2], per_chip=per_chip,
                flips=flips)


def _wait_exchange(name, started, after):
    thru, per_chip, flips = started["thru"], started["per_chip"], started["flips"]
    n = len(thru) // 2

    def body(*refs):
        src_refs, land_refs = refs[0:n], refs[n:2 * n]
        send_sems, recv_sems = refs[2 * n], refs[2 * n + 1]
        for cp in _exchange_copies(src_refs, land_refs, per_chip, flips, send_sems, recv_sems):
            cp.wait_send()
            cp.wait_recv()

    out = pl.pallas_call(
        body, name=name, out_shape=tuple(pltpu.HBM(a.shape, a.dtype) for a in thru),
        in_specs=[HBM] * (2 * n) + [SEM, SEM, ANY], out_specs=tuple([HBM] * (2 * n)),
        input_output_aliases={a: a for a in range(2 * n)},
        compiler_params=pltpu.CompilerParams(has_side_effects=EFFECT),
    )(*thru, started["send"], started["recv"], after)
    return list(out[n:2 * n])


def _place_shards(owns, landeds, name):
    n = len(owns)
    mine = (2 * lax.axis_index("x") + lax.axis_index("y")).astype(jnp.int32).reshape(1)

    def body(mine_ref, *refs):
        own_refs, land_refs, out_refs = refs[0:n], refs[n:2 * n], refs[2 * n:3 * n]
        k = pl.program_id(0)
        for a in range(n):
            @pl.when(k == mine_ref[0])
            def _():
                out_refs[a][...] = own_refs[a][...]

            @pl.when(k != mine_ref[0])
            def _():
                out_refs[a][...] = land_refs[a][...]

    def slot(k, mine_ref):
        d = k ^ mine_ref[0]
        return jnp.where(d == 1, 1, jnp.where(d == 3, 2, 0))

    zeros = lambda o: (0,) * len(o.shape)
    grid_spec = pltpu.PrefetchScalarGridSpec(
        num_scalar_prefetch=1, grid=(N_CHIPS,),
        in_specs=[pl.BlockSpec(o.shape, lambda k, m, o=o: zeros(o)) for o in owns]
        + [pl.BlockSpec((None,) + o.shape, lambda k, m, o=o: (slot(k, m),) + zeros(o)) for o in owns],
        out_specs=[pl.BlockSpec((None,) + o.shape, lambda k, m, o=o: (k,) + zeros(o)) for o in owns])
    return pl.pallas_call(
        body, name=name, grid_spec=grid_spec,
        out_shape=[jax.ShapeDtypeStruct((N_CHIPS,) + o.shape, o.dtype) for o in owns],
        compiler_params=_params(("arbitrary",)),
    )(mine, *owns, *landeds)


def _gather_w_in(w_shard, after):
    half = D_MODEL // 2

    def body(w_ref, after_ref, out_ref, ici_send, ici_recv, d2d_send, d2d_recv, loc_sem):
        chips, mine, c = _peer_chips()
        sibling = (lax.axis_index("x"), lax.axis_index("y"), 1 - c)
        my_rows = pl.ds(pl.multiple_of(c * half, half), half)
        their_rows = pl.ds(pl.multiple_of((1 - c) * half, half), half)
        own = pltpu.make_async_copy(w_ref, out_ref.at[mine], loc_sem)
        own.start()

        def over_ici(p, slot):
            px, py = chips[p]
            return pltpu.make_async_remote_copy(
                src_ref=w_ref.at[my_rows, :], dst_ref=out_ref.at[slot, my_rows, :], send_sem=ici_send.at[p],
                recv_sem=ici_recv.at[p], device_id=(px, py, c), device_id_type=MESH_ID)

        def over_d2d(p, rows):
            px, py = chips[p]
            part = out_ref.at[2 * px + py, rows, :]
            return pltpu.make_async_remote_copy(
                src_ref=part, dst_ref=part, send_sem=d2d_send.at[p], recv_sem=d2d_recv.at[p],
                device_id=sibling, device_id_type=MESH_ID)

        for p in (1, 0, 2):
            over_ici(p, mine).start()
        for p in (1, 0, 2):
            px, py = chips[p]
            over_ici(p, 2 * px + py).wait_recv()
            over_d2d(p, my_rows).start()
        for p in (1, 0, 2):
            over_d2d(p, their_rows).wait_recv()
        for p in range(3):
            over_ici(p, mine).wait_send()
            over_d2d(p, my_rows).wait_send()
        own.wait()

    return pl.pallas_call(
        body, name="gather_w_in", out_shape=jax.ShapeDtypeStruct((N_CHIPS,) + w_shard.shape, w_shard.dtype),
        in_specs=[ANY, ANY], out_specs=ANY,
        scratch_shapes=[pltpu.SemaphoreType.DMA((3,)), pltpu.SemaphoreType.DMA((3,)), pltpu.SemaphoreType.DMA((3,)),
                        pltpu.SemaphoreType.DMA((3,)), pltpu.SemaphoreType.DMA],
    )(w_shard, after)


def _kv_proj(mem2, wkv_full, b_kv):
    m = mem2.shape[0]

    def body(m_ref, w_ref, b_ref, o_ref):
        o_ref[...] = (_dot(m_ref[...].astype(BF16), w_ref[...]) + b_ref[...]).astype(BF16)

    return pl.pallas_call(
        body, name="kv_proj", grid=(m // MEM_LEN,),
        in_specs=[pl.BlockSpec((MEM_LEN, D_MODEL), lambda i: (i, 0)),
                  pl.BlockSpec((D_MODEL, D_MODEL), lambda i: (0, 0)),
                  pl.BlockSpec((1, D_MODEL), lambda i: (0, 0))],
        out_specs=pl.BlockSpec((MEM_LEN, D_MODEL), lambda i: (i, 0)),
        out_shape=jax.ShapeDtypeStruct((m, D_MODEL), BF16),
        compiler_params=_params(("arbitrary",)),
    )(mem2, wkv_full, b_kv)


CONV_ROWS = 16
SUBLANES = 8


def _shifted_planes(buf, tm):
    rows = tm + HALO - SUBLANES
    for s in range(1, SUBLANES):
        buf[s, 0:rows, :] = buf[0, s:s + rows, :]


def _window(buf, start, rows):
    s = start % SUBLANES
    return buf[s, start - s:start - s + rows, :]


def _forward(x2, w_full, b_in, kv, cw, cb, cg, cbeta, gg_, gb_, ws, bs_t, nb, seq, tm, after):
    nt = seq // tm
    n = nb * seq
    tiles = nb * nt

    def in_proj_piece(x_ref, w_ref, b_ref, hout_ref, xb_ref, hkeep, keep_now):
        xb = x_ref[...].astype(BF16)
        xb_ref[...] = xb

        def h_piece(j):
            cols = slice(j * W_IN_SHARD, (j + 1) * W_IN_SHARD)
            hj = (_dot(xb, w_ref[j]) + b_ref[:, cols]).astype(BF16)
            hout_ref[:, cols] = hj
            hkeep[keep_now, :, cols] = hj

        return h_piece

    def first_step(x_ref, w_ref, b_ref, kv_ref, cw_ref, cb_ref, clg_ref, clb_ref, glg_ref, glb_ref, ws_ref, bst_ref,
                   after_ref, hout_ref, y_ref, z_ref, xb_ref, hcbuf, hkeep):
        h_piece = in_proj_piece(x_ref, w_ref, b_ref, hout_ref, xb_ref, hkeep, 0)
        for j in range(N_CHIPS):
            h_piece(j)

    def later_step(x_ref, w_ref, b_ref, kv_ref, cw_ref, cb_ref, clg_ref, clb_ref, glg_ref, glb_ref, ws_ref, bst_ref,
                   after_ref, hout_ref, y_ref, z_ref, xb_ref, hcbuf, hkeep):
        step = pl.program_id(0)
        i = (step - 1) % nt
        keep_now = step % 2
        h_ref = hkeep.at[1 - keep_now]
        h_piece = in_proj_piece(x_ref, w_ref, b_ref, hout_ref, xb_ref, hkeep, keep_now)

        starts = i == 0

        @pl.when(starts)
        def _():
            hcbuf[0, 0:HALO, :] = jnp.zeros((HALO, D_CONV), F32)

        @pl.when(jnp.logical_not(starts))
        def _():
            hcbuf[0, 0:HALO, :] = hcbuf[0, tm:tm + HALO, :]

        every = slice(None)
        hcbuf[0, HALO:HALO + tm, :] = _f32(h_ref, every, C_A, D_CONV) * _sigmoid(_f32(h_ref, every, C_GLU, D_CONV))
        _shifted_planes(hcbuf, tm)
        for r in range(tm // CONV_ROWS):
            base = r * CONV_ROWS
            acc = jnp.broadcast_to(cb_ref[...], (CONV_ROWS, D_CONV))
            for k in range(CONV_WIDTH):
                acc = acc + cw_ref[k:k + 1, :] * _window(hcbuf, base + 2 + k, CONV_ROWS)
            z_ref[base:base + CONV_ROWS, :] = acc
            if r % 4 == 1:
                h_piece(r // 4)
        zhat, _ = _ln_stats(z_ref[...])
        zn = zhat * clg_ref[...] + clb_ref[...]
        cgate = _f32(h_ref, every, C_GATE, D_CONV)
        y_ref[:, 0:D_CONV] = (zn * _sigmoid(zn) * (cgate * _sigmoid(cgate))).astype(BF16)

        wsc, _ = _causal_ws(ws_ref)
        vhat, _ = _ln_stats(_gelu(_f32(h_ref, every, G_V, D_GMLP)))
        vn = (vhat * glg_ref[...] + glb_ref[...]).astype(BF16)
        for ch in range(tm // CHUNK):
            rows = slice(ch * CHUNK, (ch + 1) * CHUNK)
            for hd in range(N_GHEADS):
                cols = slice(hd * CHUNK, (hd + 1) * CHUNK)
                s = _dot(wsc[hd], vn[rows, cols]) + bst_ref[:, hd:hd + 1]
                u = _gelu(_f32(h_ref, rows, G_U + hd * CHUNK, CHUNK))
                gate = _f32(h_ref, rows, G_GATE + hd * CHUNK, CHUNK)
                y_ref[rows, D_CONV + hd * CHUNK:D_CONV + (hd + 1) * CHUNK] = (
                    u * s * (gate * _sigmoid(gate))).astype(BF16)

        scale = XHEAD ** -0.5
        for hd in range(N_XHEADS):
            q = h_ref[:, X_Q + hd * XHEAD:X_Q + (hd + 1) * XHEAD]
            k = kv_ref[:, hd * XHEAD:(hd + 1) * XHEAD]
            v = kv_ref[:, D_XATT + hd * XHEAD:D_XATT + (hd + 1) * XHEAD]
            s = _dot_nt(q, k) * scale
            e = jnp.exp(s - jnp.max(s, axis=-1, keepdims=True))
            p = e * (1.0 / jnp.sum(e, axis=-1, keepdims=True))
            o = _dot(p.astype(BF16), v)
            gate = _f32(h_ref, every, X_GATE + hd * XHEAD, XHEAD)
            y_ref[:, 2 * D_CONV + hd * XHEAD:2 * D_CONV + (hd + 1) * XHEAD] = (
                o * (gate * _sigmoid(gate))).astype(BF16)

    def body(*refs):
        step = pl.program_id(0)

        @pl.when(step == 0)
        def _():
            first_step(*refs)

        @pl.when(step > 0)
        def _():
            later_step(*refs)

    ahead = lambda s: (jnp.minimum(s, tiles - 1), 0)
    behind = lambda s: (jnp.maximum(s - 1, 0), 0)
    example = lambda s: (jnp.maximum(s - 1, 0) // nt, 0)
    const2 = lambda s: (0, 0)
    const3 = lambda s: (0, 0, 0)
    vec = pl.BlockSpec((1, D_CONV), const2)
    return pl.pallas_call(
        body, name="forward", grid=(tiles + 1,),
        in_specs=[pl.BlockSpec((tm, D_MODEL), ahead),
                  pl.BlockSpec((N_CHIPS, D_MODEL, W_IN_SHARD), const3, pipeline_mode=pl.Buffered(1)),
                  pl.BlockSpec((1, D_IN), const2),
                  pl.BlockSpec((MEM_LEN, D_MODEL), example),
                  pl.BlockSpec((CONV_WIDTH, D_CONV), const2), vec, vec, vec, vec, vec,
                  pl.BlockSpec((N_GHEADS, CHUNK, CHUNK), const3),
                  pl.BlockSpec((CHUNK, N_GHEADS), const2), ANY],
        out_specs=[pl.BlockSpec((tm, D_IN), ahead), pl.BlockSpec((tm, D_MIX), behind),
                   pl.BlockSpec((tm, D_CONV), behind), pl.BlockSpec((tm, D_MODEL), ahead)],
        out_shape=[jax.ShapeDtypeStruct((n, D_IN), BF16), jax.ShapeDtypeStruct((n, D_MIX), BF16),
                   jax.ShapeDtypeStruct((n, D_CONV), F32), jax.ShapeDtypeStruct((n, D_MODEL), BF16)],
        scratch_shapes=[pltpu.VMEM((SUBLANES, HALO + tm, D_CONV), F32), pltpu.VMEM((2, tm, D_IN), BF16)],
        compiler_params=_params(("arbitrary",)),
    )(x2, w_full, b_in, kv, cw, cb, cg, cbeta, gg_, gb_, ws, bs_t, after)


ROW_LOSS = 3


def _out_proj_loss(ycat, wout_full, b_out, ln_g, ln_b, x2, tgt2, tm):
    n = x2.shape[0]

    def body(y_ref, w_ref, bo_ref, g_ref, b_ref, x_ref, t_ref, dr_ref, drb_ref, dy_ref, vec_ref):
        i = pl.program_id(0)

        @pl.when(i == 0)
        def _():
            vec_ref[...] = jnp.zeros_like(vec_ref)

        r = ALPHA * x_ref[...] + _dot(y_ref[...], w_ref[...]) + bo_ref[...]
        rhat, rstd = _ln_stats(r)
        diff = rhat * g_ref[...] + b_ref[...] - t_ref[...]
        loss = 0.5 * jnp.sum(jnp.mean(diff * diff, axis=-1, keepdims=True), axis=0, keepdims=True)
        dout = diff * (1.0 / D_MODEL)
        dr = _ln_bwd(dout * g_ref[...], rhat, rstd)
        vec_ref[0:1, :] += _colsum(dr)
        vec_ref[1:2, :] += _colsum(dout * rhat)
        vec_ref[2:3, :] += _colsum(dout)
        vec_ref[ROW_LOSS:ROW_LOSS + 1, :] += jnp.broadcast_to(loss, (1, D_MODEL))
        dr_ref[...] = dr
        drb = dr.astype(BF16)
        drb_ref[...] = drb
        dy_ref[...] = _dot_nt(drb, w_ref[...]).astype(BF16)

    row = lambda i: (i, 0)
    const = lambda i: (0, 0)
    vec = pl.BlockSpec((1, D_MODEL), const)
    return pl.pallas_call(
        body, name="out_proj_loss", grid=(n // tm,),
        in_specs=[pl.BlockSpec((tm, D_MIX), row), pl.BlockSpec((D_MIX, D_MODEL), const), vec, vec, vec,
                  pl.BlockSpec((tm, D_MODEL), row), pl.BlockSpec((tm, D_MODEL), row)],
        out_specs=[pl.BlockSpec((tm, D_MODEL), row), pl.BlockSpec((tm, D_MODEL), row), pl.BlockSpec((tm, D_MIX), row),
                   pl.BlockSpec((8, D_MODEL), const)],
        out_shape=[jax.ShapeDtypeStruct((n, D_MODEL), F32), jax.ShapeDtypeStruct((n, D_MODEL), BF16),
                   jax.ShapeDtypeStruct((n, D_MIX), BF16), jax.ShapeDtypeStruct((8, D_MODEL), F32)],
        compiler_params=_params(("arbitrary",)),
    )(ycat, wout_full, b_out, ln_g, ln_b, x2, tgt2)


ROW_CB, ROW_CLG, ROW_CLB, ROW_GLG, ROW_GLB = 32, 33, 34, 35, 36


def _branch_bwd(h, z, dy, kv, cw, clg, clb, glg, glb, ws, bs_t, w_full, dr, nb, seq, tm, after):
    nt = seq // tm
    n = nb * seq
    tiles = nb * nt

    def place(s):
        sc = jnp.minimum(s, tiles - 1)
        return sc // nt, nt - 1 - sc % nt

    def dx_of_previous(w_ref, dr_ref, gx_ref, dhkeep, prev):
        gx_ref[...] = ALPHA * dr_ref[...]

        def dx_piece(j):
            gx_ref[...] += _dot_nt(dhkeep[prev, :, j * W_IN_SHARD:(j + 1) * W_IN_SHARD], w_ref[j])

        return dx_piece

    def last_step(h_ref, z_ref, dy_ref, kv_ref, cw_ref, clg_ref, clb_ref, glg_ref, glb_ref,
                  ws_ref, bst_ref, w_ref, dr_ref, after_ref,
                  dh_ref, gbin_ref, g768_ref, gws_ref, gbst_ref, dkv_ref, gbkv_ref, gx_ref,
                  dzbuf, dvnbuf, dhkeep):
        dx_piece = dx_of_previous(w_ref, dr_ref, gx_ref, dhkeep, 1 - tiles % 2)
        for j in range(N_CHIPS):
            dx_piece(j)
        _, keep = _causal_ws(ws_ref)
        for hd in range(N_GHEADS):
            gws_ref[hd] = jnp.where(keep, gws_ref[hd], 0.0)

    def tile_step(h_ref, z_ref, dy_ref, kv_ref, cw_ref, clg_ref, clb_ref, glg_ref, glb_ref,
                  ws_ref, bst_ref, w_ref, dr_ref, after_ref,
                  dh_ref, gbin_ref, g768_ref, gws_ref, gbst_ref, dkv_ref, gbkv_ref, gx_ref,
                  dzbuf, dvnbuf, dhkeep):
        step = pl.program_id(0)
        i = step % nt
        every = slice(None)
        keep_now = step % 2

        @pl.when(step == 0)
        def _():
            gbin_ref[...] = jnp.zeros_like(gbin_ref)
            g768_ref[...] = jnp.zeros_like(g768_ref)
            gws_ref[...] = jnp.zeros_like(gws_ref)
            gbst_ref[...] = jnp.zeros_like(gbst_ref)
            gbkv_ref[...] = jnp.zeros_like(gbkv_ref)
            dhkeep[1] = jnp.zeros((tm, D_IN), BF16)

        @pl.when(i == 0)
        def _():
            dkv_ref[...] = jnp.zeros_like(dkv_ref)

        dx_piece = dx_of_previous(w_ref, dr_ref, gx_ref, dhkeep, 1 - keep_now)

        def put(rows, col, width, val):
            vb = val.astype(BF16)
            dh_ref[rows, col:col + width] = vb
            dhkeep[keep_now, rows, col:col + width] = vb

        def emit(col, width, val):
            gbin_ref[:, col:col + width] += _colsum(val)
            put(every, col, width, val)

        d_c = dy_ref[:, 0:D_CONV].astype(F32)
        cgate = _f32(h_ref, every, C_GATE, D_CONV)
        sg = _sigmoid(cgate)
        zhat, zrstd = _ln_stats(z_ref[...])
        zn = zhat * clg_ref[...] + clb_ref[...]
        szn = _sigmoid(zn)
        emit(C_GATE, D_CONV, d_c * (zn * szn) * _dsilu(cgate, sg))
        dzn = d_c * (cgate * sg) * _dsilu(zn, szn)
        g768_ref[ROW_CLG:ROW_CLG + 1, :] += _colsum(dzn * zhat)
        g768_ref[ROW_CLB:ROW_CLB + 1, :] += _colsum(dzn)
        dz = _ln_bwd(dzn * clg_ref[...], zhat, zrstd)
        g768_ref[ROW_CB:ROW_CB + 1, :] += _colsum(dz)

        @pl.when(i == 0)
        def _():
            dzbuf[0, tm:tm + HALO, :] = jnp.zeros((HALO, D_CONV), F32)

        @pl.when(i > 0)
        def _():
            dzbuf[0, tm:tm + HALO, :] = dzbuf[0, 0:HALO, :]

        dzbuf[0, 0:tm, :] = dz
        _shifted_planes(dzbuf, tm)
        a = _f32(h_ref, every, C_A, D_CONV)
        sgl = _sigmoid(_f32(h_ref, every, C_GLU, D_CONV))
        hc = a * sgl

        for r in range(tm // CONV_ROWS):
            base = r * CONV_ROWS
            acc = jnp.zeros((CONV_ROWS, D_CONV), F32)
            for k in range(CONV_WIDTH):
                acc = acc + cw_ref[k:k + 1, :] * _window(dzbuf, base + 30 - k, CONV_ROWS)
            dvnbuf[base:base + CONV_ROWS, :] = acc
            if r % 4 == 1:
                dx_piece(r // 4)
        dhc = dvnbuf[...]
        emit(C_A, D_CONV, dhc * sgl)
        emit(C_GLU, D_CONV, dhc * a * sgl * (1.0 - sgl))
        for k in range(CONV_WIDTH):
            g768_ref[k:k + 1, :] += _colsum(hc * _window(dzbuf, 30 - k, tm))

        wsc, _ = _causal_ws(ws_ref)
        v, dgelu_v = _gelu_and_grad(_f32(h_ref, every, G_V, D_GMLP))
        vhat, vrstd = _ln_stats(v)
        vn = (vhat * glg_ref[...] + glb_ref[...]).astype(BF16)
        for ch in range(tm // CHUNK):
            rows = slice(ch * CHUNK, (ch + 1) * CHUNK)
            for hd in range(N_GHEADS):
                cols = slice(hd * CHUNK, (hd + 1) * CHUNK)
                vn_blk = vn[rows, cols]
                s = _dot(wsc[hd], vn_blk) + bst_ref[:, hd:hd + 1]
                u, dgelu_u = _gelu_and_grad(_f32(h_ref, rows, G_U + hd * CHUNK, CHUNK))
                gate = _f32(h_ref, rows, G_GATE + hd * CHUNK, CHUNK)
                sgate = _sigmoid(gate)
                d_g = dy_ref[rows, D_CONV + hd * CHUNK:D_CONV + (hd + 1) * CHUNK].astype(F32)
                dgate = d_g * (u * s) * _dsilu(gate, sgate)
                gbin_ref[:, G_GATE + hd * CHUNK:G_GATE + (hd + 1) * CHUNK] += _colsum(dgate)
                put(rows, G_GATE + hd * CHUNK, CHUNK, dgate)
                dyg = d_g * (gate * sgate)
                du = dyg * s * dgelu_u
                gbin_ref[:, G_U + hd * CHUNK:G_U + (hd + 1) * CHUNK] += _colsum(du)
                put(rows, G_U + hd * CHUNK, CHUNK, du)
                ds = dyg * u
                dsb = ds.astype(BF16)
                gws_ref[hd] += _dot_nt(dsb, vn_blk)
                gbst_ref[:, hd:hd + 1] += jnp.sum(ds, axis=1, keepdims=True)
                dvnbuf[rows, cols] = _dot_tn(wsc[hd], dsb)
        dvn = dvnbuf[...]
        g768_ref[ROW_GLG:ROW_GLG + 1, :] += _colsum(dvn * vhat)
        g768_ref[ROW_GLB:ROW_GLB + 1, :] += _colsum(dvn)
        emit(G_V, D_GMLP, _ln_bwd(dvn * glg_ref[...], vhat, vrstd) * dgelu_v)

        scale = XHEAD ** -0.5
        for hd in range(N_XHEADS):
            q = h_ref[:, X_Q + hd * XHEAD:X_Q + (hd + 1) * XHEAD]
            k = kv_ref[:, hd * XHEAD:(hd + 1) * XHEAD]
            vv = kv_ref[:, D_XATT + hd * XHEAD:D_XATT + (hd + 1) * XHEAD]
            s = _dot_nt(q, k) * scale
            e = jnp.exp(s - jnp.max(s, axis=-1, keepdims=True))
            p = e * (1.0 / jnp.sum(e, axis=-1, keepdims=True))
            pb = p.astype(BF16)
            o = _dot(pb, vv)
            gate = _f32(h_ref, every, X_GATE + hd * XHEAD, XHEAD)
            sgate = _sigmoid(gate)
            d_x = dy_ref[:, 2 * D_CONV + hd * XHEAD:2 * D_CONV + (hd + 1) * XHEAD].astype(F32)
            emit(X_GATE + hd * XHEAD, XHEAD, d_x * o * _dsilu(gate, sgate))
            do = (d_x * (gate * sgate)).astype(BF16)
            dp = _dot_nt(do, vv)
            dsc = (p * (dp - jnp.sum(dp * p, axis=-1, keepdims=True))).astype(BF16)
            emit(X_Q + hd * XHEAD, XHEAD, _dot(dsc, k) * scale)
            dkv_ref[:, hd * XHEAD:(hd + 1) * XHEAD] += _dot_tn(dsc, q) * scale
            dkv_ref[:, D_XATT + hd * XHEAD:D_XATT + (hd + 1) * XHEAD] += _dot_tn(pb, do)

        @pl.when(i == nt - 1)
        def _():
            gbkv_ref[...] += _colsum(dkv_ref[...])

    def body(*refs):
        step = pl.program_id(0)

        @pl.when(step < tiles)
        def _():
            tile_step(*refs)

        @pl.when(step == tiles)
        def _():
            last_step(*refs)

    def row(s):
        b, ri = place(s)
        return b * nt + ri, 0

    def row_prev(s):
        b, ri = place(jnp.maximum(s - 1, 0))
        return b * nt + ri, 0

    example = lambda s: (place(s)[0], 0)
    const2 = lambda s: (0, 0)
    const3 = lambda s: (0, 0, 0)
    vec = pl.BlockSpec((1, D_CONV), const2)

    return pl.pallas_call(
        body, name="branch_bwd", grid=(tiles + 1,),
        in_specs=[pl.BlockSpec((tm, D_IN), row),
                  pl.BlockSpec((tm, D_CONV), row), pl.BlockSpec((tm, D_MIX), row),
                  pl.BlockSpec((MEM_LEN, D_MODEL), example),
                  pl.BlockSpec((CONV_WIDTH, D_CONV), const2), vec, vec, vec, vec,
                  pl.BlockSpec((N_GHEADS, CHUNK, CHUNK), const3),
                  pl.BlockSpec((CHUNK, N_GHEADS), const2),
                  pl.BlockSpec((N_CHIPS, D_MODEL, W_IN_SHARD), const3, pipeline_mode=pl.Buffered(1)),
                  pl.BlockSpec((tm, D_MODEL), row_prev), ANY],
        out_specs=[pl.BlockSpec((tm, D_IN), row),
                   pl.BlockSpec((1, D_IN), const2),
                   pl.BlockSpec((40, D_CONV), const2),
                   pl.BlockSpec((N_GHEADS, CHUNK, CHUNK), const3),
                   pl.BlockSpec((CHUNK, CHUNK), const2),
                   pl.BlockSpec((MEM_LEN, D_MODEL), example),
                   pl.BlockSpec((1, D_MODEL), const2),
                   pl.BlockSpec((tm, D_MODEL), row_prev)],
        out_shape=[jax.ShapeDtypeStruct((n, D_IN), BF16),
                   jax.ShapeDtypeStruct((1, D_IN), F32),
                   jax.ShapeDtypeStruct((40, D_CONV), F32),
                   jax.ShapeDtypeStruct((N_GHEADS, CHUNK, CHUNK), F32),
                   jax.ShapeDtypeStruct((CHUNK, CHUNK), F32),
                   jax.ShapeDtypeStruct((nb * MEM_LEN, D_MODEL), F32),
                   jax.ShapeDtypeStruct((1, D_MODEL), F32),
                   jax.ShapeDtypeStruct((n, D_MODEL), F32)],
        scratch_shapes=[pltpu.VMEM((SUBLANES, tm + HALO, D_CONV), F32), pltpu.VMEM((tm, D_CONV), F32),
                        pltpu.VMEM((2, tm, D_IN), BF16)],
        compiler_params=_params(("arbitrary",)),
    )(h, z, dy, kv, cw, clg, clb, glg, glb, ws, bs_t, w_full, dr, after)


def _grad_w(a, b, tk, name):
    m = a.shape[1]
    kdim, ncols = b.shape
    nk = kdim // tk
    shard = m // N_CHIPS
    oshape = (shard, ncols)
    a_spec = pl.BlockSpec((tk, shard), lambda j, kk: (kk, j))
    b_spec = pl.BlockSpec((tk, ncols), lambda j, kk: (kk, 0))

    def body(a_ref, b_ref, own_ref, ob_ref, acc):
        j = pl.program_id(0)
        kk = pl.program_id(1)
        mine = 2 * lax.axis_index("x") + lax.axis_index("y")

        @pl.when(kk == 0)
        def _():
            acc[...] = jnp.zeros_like(acc)

        acc[...] += _dot_tn(a_ref[...].astype(BF16), b_ref[...].astype(BF16))

        @pl.when(kk == nk - 1)
        def _():
            ob_ref[...] = acc[...].astype(BF16)

        @pl.when((kk == nk - 1) & (j == mine))
        def _():
            own_ref[...] = acc[...]

    return pl.pallas_call(
        body, name=name, grid=(N_CHIPS, nk),
        in_specs=[a_spec, b_spec],
        out_specs=[pl.BlockSpec(oshape, lambda j, kk: (0, 0)),
                   pl.BlockSpec((None,) + oshape, lambda j, kk: (j, 0, 0))],
        out_shape=[jax.ShapeDtypeStruct(oshape, F32), jax.ShapeDtypeStruct((N_CHIPS,) + oshape, BF16)],
        scratch_shapes=[pltpu.VMEM(oshape, F32)],
        compiler_params=_params(("arbitrary", "arbitrary")),
    )(a, b)


def _grad_w_block(a, b, block, tk, name, dtype, after):
    kdim, m = a.shape
    shard = b.shape[1] // N_CHIPS
    nk = kdim // tk

    def body(block_ref, a_ref, b_ref, after_ref, o_ref, acc):
        kk = pl.program_id(0)

        @pl.when(kk == 0)
        def _():
            acc[...] = jnp.zeros_like(acc)

        acc[...] += _dot_tn(a_ref[...].astype(BF16), b_ref[...])

        @pl.when(kk == nk - 1)
        def _():
            o_ref[...] = acc[...].astype(dtype)

    grid_spec = pltpu.PrefetchScalarGridSpec(
        num_scalar_prefetch=1, grid=(nk,),
        in_specs=[pl.BlockSpec((tk, m), lambda kk, blk: (kk, 0)),
                  pl.BlockSpec((tk, shard), lambda kk, blk: (kk, blk[0])), ANY],
        out_specs=pl.BlockSpec((m, shard), lambda kk, blk: (0, 0)),
        scratch_shapes=[pltpu.VMEM((m, shard), F32)])
    return pl.pallas_call(
        body, name=name, grid_spec=grid_spec, out_shape=jax.ShapeDtypeStruct((m, shard), dtype),
        compiler_params=_params(("arbitrary",)),
    )(block, a, b, after)


def _sum_small(owns, gots):
    n = len(owns)

    def body(*refs):
        for a in range(n):
            o_ref, g_ref, out_ref = refs[a], refs[n + a], refs[2 * n + a]
            out_ref[...] = (o_ref[...] + g_ref[1]) + (g_ref[0] + g_ref[2])

    return pl.pallas_call(
        body, name="sum_small", out_shape=[jax.ShapeDtypeStruct(o.shape, F32) for o in owns],
        compiler_params=pltpu.CompilerParams(vmem_limit_bytes=VMEM_LIMIT),
    )(*owns, *gots)


def _sum_chips(own, gots, tr, name):
    r, ccols = own.shape

    def body(o_ref, gx_ref, gy_ref, gxy_ref, out_ref):
        out_ref[...] = (o_ref[...] + gy_ref[...].astype(F32)) + (gx_ref[...].astype(F32) + gxy_ref[...].astype(F32))

    return pl.pallas_call(
        body, name=name, grid=(r // tr,),
        in_specs=[pl.BlockSpec((tr, ccols), lambda i: (i, 0))]
        + [pl.BlockSpec((None, tr, ccols), lambda i, slot=slot: (slot, i, 0)) for _, slot in gots],
        out_specs=pl.BlockSpec((tr, ccols), lambda i: (i, 0)),
        out_shape=jax.ShapeDtypeStruct((r, ccols), F32),
        compiler_params=_params(("arbitrary",)),
    )(own, *[g for g, _ in gots])


def _exchange_cores(parts, name):
    npart = len(parts)

    def body(*refs):
        in_refs = refs[0:npart]
        out_refs = refs[npart:2 * npart]
        send_sems, recv_sems = refs[2 * npart:]
        sibling = (lax.axis_index("x"), lax.axis_index("y"), 1 - lax.axis_index("c"))
        copies = [pltpu.make_async_remote_copy(
            src_ref=in_refs[a], dst_ref=out_refs[a], send_sem=send_sems.at[a], recv_sem=recv_sems.at[a],
            device_id=sibling, device_id_type=MESH_ID) for a in range(npart)]
        for cp in copies:
            cp.start()
        for cp in copies:
            cp.wait_recv()
        for cp in copies:
            cp.wait_send()

    return pl.pallas_call(
        body, name=name, out_shape=[jax.ShapeDtypeStruct(p.shape, p.dtype) for p in parts],
        in_specs=[ANY] * npart, out_specs=[ANY] * npart,
        scratch_shapes=[pltpu.SemaphoreType.DMA((npart,)), pltpu.SemaphoreType.DMA((npart,))],
    )(*parts)


def _adam(g, w, m, v):
    mn = ADAM_B1 * m + (1.0 - ADAM_B1) * g
    vn = ADAM_B2 * v + (1.0 - ADAM_B2) * (g * g)
    return g, -ADAM_LR * ((mn / BC1) / (jnp.sqrt(vn / BC2) + ADAM_EPS) + ADAM_WD * w), mn, vn


def _adamw(a, b, w, m, v, tr, name):
    r, ccols = w.shape

    def body(a_ref, b_ref, w_ref, m_ref, v_ref, *outs):
        res = _adam(a_ref[...] + b_ref[...], w_ref[...], m_ref[...], v_ref[...])
        for which in range(4):
            outs[which][...] = res[which]

    spec = pl.BlockSpec((tr, ccols), lambda i: (i, 0))
    shape = jax.ShapeDtypeStruct((r, ccols), F32)
    return pl.pallas_call(
        body, name=name, grid=(r // tr,), in_specs=[spec] * 5, out_specs=[spec] * 4, out_shape=[shape] * 4,
        compiler_params=_params(("arbitrary",)),
    )(a, b, w, m, v)


SMALL = ["b_in", "conv_b", "conv_ln_g", "conv_ln_b", "gmlp_ln_g", "gmlp_ln_b", "gmlp_ws", "gmlp_bs", "b_kv", "b_out",
         "ln_g", "ln_b"]


def _adamw_small(a_parts, b_parts, params):
    nparts, nparams = len(a_parts), len(params)

    def body(*refs):
        a = refs[0:nparts]
        b = refs[nparts:2 * nparts]
        prm = refs[2 * nparts:2 * nparts + 3 * nparams]
        outs = refs[2 * nparts + 3 * nparams:]
        gb_in, g768, gws, gbs_t, gb_kv, vec3 = [a[q][...] + b[q][...] for q in range(nparts)]
        grads = [gb_in, g768[ROW_CB:ROW_CB + 1], g768[ROW_CLG:ROW_CLG + 1], g768[ROW_CLB:ROW_CLB + 1],
                 g768[ROW_GLG:ROW_GLG + 1], g768[ROW_GLB:ROW_GLB + 1], gws, jnp.transpose(gbs_t)[0:N_GHEADS, :],
                 gb_kv, vec3[0:1], vec3[1:2], vec3[2:3]]
        for q, g in enumerate(grads):
            res = _adam(g, prm[3 * q][...], prm[3 * q + 1][...], prm[3 * q + 2][...])
            for which in range(4):
                outs[4 * q + which][...] = res[which]

    flat = [t for p in params for t in p]
    out_shape = [jax.ShapeDtypeStruct(p[0].shape, F32) for p in params for _ in range(4)]
    return pl.pallas_call(
        body, name="adamw_small", out_shape=out_shape,
        compiler_params=pltpu.CompilerParams(vmem_limit_bytes=VMEM_LIMIT),
    )(*a_parts, *b_parts, *flat)


def kernel(x, mem, w_in, b_in, conv_w, conv_b, conv_ln_g, conv_ln_b, gmlp_ln_g, gmlp_ln_b, gmlp_ws, gmlp_bs, w_kv, b_kv, w_out, b_out, ln_g, ln_b, loss_target, m_w_in, m_b_in, m_conv_w, m_conv_b, m_conv_ln_g, m_conv_ln_b, m_gmlp_ln_g, m_gmlp_ln_b, m_gmlp_ws, m_gmlp_bs, m_w_kv, m_b_kv, m_w_out, m_b_out, m_ln_g, m_ln_b, v_w_in, v_b_in, v_conv_w, v_conv_b, v_conv_ln_g, v_conv_ln_b, v_gmlp_ln_g, v_gmlp_ln_b, v_gmlp_ws, v_gmlp_bs, v_w_kv, v_b_kv, v_w_out, v_b_out, v_ln_g, v_ln_b):
    weights = dict(b_in=b_in, conv_b=conv_b, conv_ln_g=conv_ln_g, conv_ln_b=conv_ln_b, gmlp_ln_g=gmlp_ln_g,
                   gmlp_ln_b=gmlp_ln_b, gmlp_ws=gmlp_ws, gmlp_bs=gmlp_bs, b_kv=b_kv, b_out=b_out, ln_g=ln_g, ln_b=ln_b)
    mom_m = dict(b_in=m_b_in, conv_b=m_conv_b, conv_ln_g=m_conv_ln_g, conv_ln_b=m_conv_ln_b, gmlp_ln_g=m_gmlp_ln_g,
                 gmlp_ln_b=m_gmlp_ln_b, gmlp_ws=m_gmlp_ws, gmlp_bs=m_gmlp_bs, b_kv=m_b_kv, b_out=m_b_out,
                 ln_g=m_ln_g, ln_b=m_ln_b)
    mom_v = dict(b_in=v_b_in, conv_b=v_conv_b, conv_ln_g=v_conv_ln_g, conv_ln_b=v_conv_ln_b, gmlp_ln_g=v_gmlp_ln_g,
                 gmlp_ln_b=v_gmlp_ln_b, gmlp_ws=v_gmlp_ws, gmlp_bs=v_gmlp_bs, b_kv=v_b_kv, b_out=v_b_out,
                 ln_g=v_ln_g, ln_b=v_ln_b)
    nb, seq, _ = x.shape
    n = nb * seq
    tm = 256
    tk = min(2048, n)
    x2 = x.reshape(n, D_MODEL)
    tgt2 = loss_target.reshape(n, D_MODEL)
    mem2 = mem.reshape(nb * MEM_LEN, D_MODEL)
    chip = 2 * lax.axis_index("x") + lax.axis_index("y")
    bs_t = jnp.transpose(gmlp_bs[0])

    own_kv = [w_kv[0].astype(BF16), conv_w[0]]
    own_out = [w_out[0].astype(BF16)]
    ga = _start_exchange("gather_kv_start", own_kv, [False] * 2)
    win_g = _gather_w_in(w_in[0].astype(BF16), ga["token"])
    wkv_g, cw_g = _place_shards(own_kv, _wait_exchange("gather_kv_wait", ga, win_g), "place_kv")
    wkv_full = wkv_g.reshape(D_MODEL, D_MODEL)
    cw_full = jnp.transpose(cw_g, (1, 0, 2)).reshape(CONV_WIDTH, D_CONV)
    kv = _kv_proj(mem2, wkv_full, b_kv)
    gb = _start_exchange("gather_out_start", own_out, [False])
    h, ycat, z, x_bf = _forward(x2, win_g, b_in, kv, cw_full, conv_b, conv_ln_g, conv_ln_b, gmlp_ln_g, gmlp_ln_b,
                                gmlp_ws[0], bs_t, nb, seq, tm, gb["token"])
    (wout_g,) = _place_shards(own_out, _wait_exchange("gather_out_wait", gb, ycat), "place_out")
    wout_full = wout_g.reshape(D_MIX, D_MODEL)
    dr, dr_bf, dycat, vec3 = _out_proj_loss(ycat, wout_full, b_out, ln_g, ln_b, x2, tgt2, min(512, n))

    own_wout, gwout_b = _grad_w(ycat, dr_bf, min(2048, n), "grad_w_out")
    ex1 = _start_exchange("exchange1_start", [gwout_b, vec3], [True, False])
    dh, gb_in, g768, gws, gbs_t, dkv, gb_kv, grad_x2 = _branch_bwd(
        h, z, dycat, kv, cw_full, conv_ln_g, conv_ln_b, gmlp_ln_g, gmlp_ln_b, gmlp_ws[0], bs_t, win_g, dr,
        nb, seq, tm, ex1["token"])
    own_wkv, gwkv_b = _grad_w(mem2, dkv, nb * MEM_LEN, "grad_w_kv")
    small2 = [gb_in, g768, gws, gbs_t, gb_kv]
    ex2 = _start_exchange("exchange2_start", [gwkv_b] + small2, [True] + [False] * 5)
    block_of = lambda flip_bits: (chip ^ flip_bits).astype(jnp.int32).reshape(1)
    after, ex3 = ex2["token"], {}
    for flip, bits in ((2, 3), (1, 1), (0, 2)):
        part = _grad_w_block(x_bf, dh, block_of(bits), tk, f"grad_w_in_{flip}", BF16, after)
        ex3[flip] = _start_exchange(f"exchange3{flip}_start", [part], [False], (flip,))
        after = ex3[flip]["token"]
    own_win = _grad_w_block(x_bf, dh, block_of(0), tk, "grad_w_in_own", F32, after)
    got_wout, got_vec3 = _wait_exchange("exchange1_wait", ex1, own_win)
    got2 = _wait_exchange("exchange2_wait", ex2, own_win)

    sum_wout = _sum_chips(own_wout, [(got_wout, 0), (got_wout, 1), (got_wout, 2)], 256, "sum_w_out")
    sum_wkv = _sum_chips(own_wkv, [(got2[0], 0), (got2[0], 1), (got2[0], 2)], 256, "sum_w_kv")
    sum_small = list(_sum_small(small2 + [vec3], got2[1:] + [got_vec3]))
    sib = list(_exchange_cores([sum_wout, sum_wkv] + sum_small, "exchange_cores_rest"))
    loss = sum_small[5][ROW_LOSS, 0] + sib[7][ROW_LOSS, 0]
    big = {}
    big["w_out"] = _adamw(sum_wout, sib[0], w_out[0], m_w_out[0], v_w_out[0], 256, "adamw_w_out")
    big["w_kv"] = _adamw(sum_wkv, sib[1], w_kv[0], m_w_kv[0], v_w_kv[0], 256, "adamw_w_kv")
    lead = {"gmlp_ws", "gmlp_bs"}
    strip = lambda k, t: t[0] if k in lead else t
    sm = _adamw_small(sum_small, sib[2:], [tuple(strip(k, t[k]) for t in (weights, mom_m, mom_v)) for k in SMALL])
    small_out = {k: [sm[4 * q + which][None] if k in lead else sm[4 * q + which] for which in range(4)]
                 for q, k in enumerate(SMALL)}
    cw_a = lax.dynamic_slice_in_dim(sum_small[1][0:CONV_WIDTH + 1], chip * CONV_SHARD, CONV_SHARD, axis=1)
    cw_b = lax.dynamic_slice_in_dim(sib[3][0:CONV_WIDTH + 1], chip * CONV_SHARD, CONV_SHARD, axis=1)
    cwp = ((0, 1), (0, 0))
    cw_out = _adamw(cw_a, cw_b, jnp.pad(conv_w[0], cwp), jnp.pad(m_conv_w[0], cwp), jnp.pad(v_conv_w[0], cwp),
                    CONV_WIDTH + 1, "adamw_conv_w")

    done = cw_out[0]
    got_win = {flip: _wait_exchange(f"exchange3{flip}_wait", ex3[flip], done)[0] for flip in (2, 1, 0)}
    sum_win = _sum_chips(own_win, [(got_win[0], 0), (got_win[1], 0), (got_win[2], 0)], 256, "sum_w_in")
    (sib_win,) = _exchange_cores([sum_win], "exchange_cores_w_in")
    big["w_in"] = _adamw(sum_win, sib_win, w_in[0], m_w_in[0], v_w_in[0], 256, "adamw_w_in")

    order = ["w_in", "b_in", "conv_w", "conv_b", "conv_ln_g", "conv_ln_b", "gmlp_ln_g", "gmlp_ln_b", "gmlp_ws",
             "gmlp_bs", "w_kv", "b_kv", "w_out", "b_out", "ln_g", "ln_b"]
    result = [loss, grad_x2.reshape(nb, seq, D_MODEL)]
    for which in range(4):
        for k in order:
            if k in big:
                result.append(big[k][which][None])
            elif k == "conv_w":
                result.append(cw_out[which][0:CONV_WIDTH][None])
            else:
                result.append(small_out[k][which])
    return tuple(result)
```

```python
import functools
import math

import jax
import jax.numpy as jnp
from jax import lax
from jax.experimental import pallas as pl
from jax.experimental.pallas import tpu as pltpu

F32 = jnp.float32
BF16 = jnp.bfloat16

D_MODEL = 1024
MEM_LEN = 256
D_MIX = 2048
D_CONV = 768
D_GMLP = 768
D_XATT = 512
N_XHEADS = 4
XHEAD = 128
CONV_WIDTH = 31
CHUNK = 128
N_GHEADS = 6
D_IN = 3 * D_CONV + 3 * D_GMLP + 2 * D_XATT
ALPHA = 2.0 ** 0.25
LN_EPS = 1e-5
N_CHIPS = 4
W_IN_SHARD = D_IN // N_CHIPS
W_OUT_SHARD = D_MIX // N_CHIPS
W_KV_SHARD = D_MODEL // N_CHIPS
CONV_SHARD = D_CONV // N_CHIPS
HALO = 32

C_A, C_GLU, C_GATE = 0, 768, 1536
G_U, G_V, G_GATE = 2304, 3072, 3840
X_Q, X_GATE = 4608, 5120

ADAM_LR = 0.001
ADAM_B1 = 0.9
ADAM_B2 = 0.999
ADAM_EPS = 1e-08
ADAM_WD = 0.01
ADAM_STEP = 10
BC1 = 1.0 - ADAM_B1 ** ADAM_STEP
BC2 = 1.0 - ADAM_B2 ** ADAM_STEP

VMEM_LIMIT = 56 * 1024 * 1024
MESH_ID = pl.DeviceIdType.MESH
ANY = pl.BlockSpec(memory_space=pl.ANY)

GELU_C0 = math.sqrt(2.0 / math.pi)
GELU_C1 = 0.044715


def _sigmoid(v):
    return 0.5 + 0.5 * jnp.tanh(0.5 * v)


def _f32(ref, rows, col, width):
    return ref[rows, col:col + width].astype(F32)


def _dsilu(v, s):
    return s * (1.0 + v * (1.0 - s))


def _gelu_and_grad(v):
    t = jnp.tanh(GELU_C0 * (v + GELU_C1 * v * v * v))
    g = 0.5 * v * (1.0 + t)
    dg = 0.5 * (1.0 + t) + 0.5 * v * (1.0 - t * t) * (GELU_C0 * (1.0 + 3.0 * GELU_C1 * v * v))
    return g, dg


def _gelu(v):
    return 0.5 * v * (1.0 + jnp.tanh(GELU_C0 * (v + GELU_C1 * v * v * v)))


def _ln_stats(v):
    mu = jnp.mean(v, axis=-1, keepdims=True)
    vc = v - mu
    var = jnp.mean(vc * vc, axis=-1, keepdims=True)
    rstd = lax.rsqrt(var + LN_EPS)
    return vc * rstd, rstd


def _ln_bwd(dvhat, vhat, rstd):
    m1 = jnp.mean(dvhat, axis=-1, keepdims=True)
    m2 = jnp.mean(dvhat * vhat, axis=-1, keepdims=True)
    return rstd * (dvhat - m1 - vhat * m2)


def _colsum(v):
    return jnp.sum(v, axis=0, keepdims=True)


def _dot(a, b):
    return jnp.dot(a, b, preferred_element_type=F32)


def _dot_nt(a, b):
    return lax.dot_general(a, b, (((1,), (1,)), ((), ())), preferred_element_type=F32)


def _dot_tn(a, b):
    return lax.dot_general(a, b, (((0,), (0,)), ((), ())), preferred_element_type=F32)


def _causal_ws(ws_ref):
    row = lax.broadcasted_iota(jnp.int32, (CHUNK, CHUNK), 0)
    col = lax.broadcasted_iota(jnp.int32, (CHUNK, CHUNK), 1)
    keep = col <= row
    return [jnp.where(keep, ws_ref[hd], 0.0).astype(BF16) for hd in range(N_GHEADS)], keep


def _params(sem):
    return pltpu.CompilerParams(dimension_semantics=sem, vmem_limit_bytes=VMEM_LIMIT)


def _peer_chips():
    x, y, c = lax.axis_index("x"), lax.axis_index("y"), lax.axis_index("c")
    return [(1 - x, y), (x, 1 - y), (1 - x, 1 - y)], 2 * x + y, c


HBM = pl.BlockSpec(memory_space=pltpu.HBM)
SEM = pl.BlockSpec(memory_space=pltpu.SEMAPHORE)
EFFECT = pltpu.SideEffectType.DATAFLOW_SIDE_EFFECTING
ALL_FLIPS = (0, 1, 2)


def _exchange_copies(src_refs, land_refs, per_chip, flips, send_sems, recv_sems):
    chips, _, c = _peer_chips()
    n = len(src_refs)
    copies = []
    for q, p in enumerate(flips):
        px, py = chips[p]
        for a in range(n):
            src = src_refs[a].at[2 * px + py] if per_chip[a] else src_refs[a]
            copies.append(pltpu.make_async_remote_copy(
                src_ref=src, dst_ref=land_refs[a].at[q], send_sem=send_sems.at[n * q + a],
                recv_sem=recv_sems.at[n * q + a], device_id=(px, py, c), device_id_type=MESH_ID))
    return copies


def _start_exchange(name, srcs, per_chip, flips=ALL_FLIPS):
    n = len(srcs)
    nf = len(flips)
    lands = [lax.empty((nf,) + (s.shape[1:] if pc else s.shape), s.dtype) for s, pc in zip(srcs, per_chip)]

    def body(*refs):
        src_refs, land_refs = refs[0:n], refs[n:2 * n]
        send_sems, recv_sems = refs[2 * n], refs[2 * n + 1]
        token = refs[4 * n + 2]
        for cp in _exchange_copies(src_refs, land_refs, per_chip, flips, send_sems, recv_sems):
            cp.start()
        token[...] = jnp.zeros_like(token)

    out = pl.pallas_call(
        body, name=name,
        out_shape=(pltpu.SemaphoreType.DMA((nf * n,)), pltpu.SemaphoreType.DMA((nf * n,)),
                   *[pltpu.HBM(a.shape, a.dtype) for a in srcs + lands], jax.ShapeDtypeStruct((8, 128), F32)),
        in_specs=[HBM] * (2 * n),
        out_specs=(SEM, SEM, *[HBM] * (2 * n), pl.BlockSpec(memory_space=pltpu.VMEM)),
        input_output_aliases={a: 2 + a for a in range(2 * n)},
        compiler_params=pltpu.CompilerParams(has_side_effects=EFFECT),
    )(*[pltpu.with_memory_space_constraint(a, pltpu.HBM) for a in srcs + lands])
    return dict(send=out[0], recv=out[1], thru=list(out[2:2 * n + 2]), token=out[2 * n + 2], per_chip=per_chip,
                flips=flips)


def _wait_exchange(name, started, after):
    thru, per_chip, flips = started["thru"], started["per_chip"], started["flips"]
    n = len(thru) // 2

    def body(*refs):
        src_refs, land_refs = refs[0:n], refs[n:2 * n]
        send_sems, recv_sems = refs[2 * n], refs[2 * n + 1]
        for cp in _exchange_copies(src_refs, land_refs, per_chip, flips, send_sems, recv_sems):
            cp.wait_send()
            cp.wait_recv()

    out = pl.pallas_call(
        body, name=name, out_shape=tuple(pltpu.HBM(a.shape, a.dtype) for a in thru),
        in_specs=[HBM] * (2 * n) + [SEM, SEM, ANY], out_specs=tuple([HBM] * (2 * n)),
        input_output_aliases={a: a for a in range(2 * n)},
        compiler_params=pltpu.CompilerParams(has_side_effects=EFFECT),
    )(*thru, started["send"], started["recv"], after)
    return list(out[n:2 * n])


def _place_shards(owns, landeds, name):
    n = len(owns)
    mine = (2 * lax.axis_index("x") + lax.axis_index("y")).astype(jnp.int32).reshape(1)

    def body(mine_ref, *refs):
        own_refs, land_refs, out_refs = refs[0:n], refs[n:2 * n], refs[2 * n:3 * n]
        k = pl.program_id(0)
        for a in range(n):
            @pl.when(k == mine_ref[0])
            def _():
                out_refs[a][...] = own_refs[a][...]

            @pl.when(k != mine_ref[0])
            def _():
                out_refs[a][...] = land_refs[a][...]

    def slot(k, mine_ref):
        d = k ^ mine_ref[0]
        return jnp.where(d == 1, 1, jnp.where(d == 3, 2, 0))

    zeros = lambda o: (0,) * len(o.shape)
    grid_spec = pltpu.PrefetchScalarGridSpec(
        num_scalar_prefetch=1, grid=(N_CHIPS,),
        in_specs=[pl.BlockSpec(o.shape, lambda k, m, o=o: zeros(o)) for o in owns]
        + [pl.BlockSpec((None,) + o.shape, lambda k, m, o=o: (slot(k, m),) + zeros(o)) for o in owns],
        out_specs=[pl.BlockSpec((None,) + o.shape, lambda k, m, o=o: (k,) + zeros(o)) for o in owns])
    return pl.pallas_call(
        body, name=name, grid_spec=grid_spec,
        out_shape=[jax.ShapeDtypeStruct((N_CHIPS,) + o.shape, o.dtype) for o in owns],
        compiler_params=_params(("arbitrary",)),
    )(mine, *owns, *landeds)


def _gather_w_in(w_shard, after):
    half = D_MODEL // 2

    def body(w_ref, after_ref, out_ref, ici_send, ici_recv, d2d_send, d2d_recv, loc_sem):
        chips, mine, c = _peer_chips()
        sibling = (lax.axis_index("x"), lax.axis_index("y"), 1 - c)
        my_rows = pl.ds(pl.multiple_of(c * half, half), half)
        their_rows = pl.ds(pl.multiple_of((1 - c) * half, half), half)
        own = pltpu.make_async_copy(w_ref, out_ref.at[mine], loc_sem)
        own.start()

        def over_ici(p, slot):
            px, py = chips[p]
            return pltpu.make_async_remote_copy(
                src_ref=w_ref.at[my_rows, :], dst_ref=out_ref.at[slot, my_rows, :], send_sem=ici_send.at[p],
                recv_sem=ici_recv.at[p], device_id=(px, py, c), device_id_type=MESH_ID)

        def over_d2d(p, rows):
            px, py = chips[p]
            part = out_ref.at[2 * px + py, rows, :]
            return pltpu.make_async_remote_copy(
                src_ref=part, dst_ref=part, send_sem=d2d_send.at[p], recv_sem=d2d_recv.at[p],
                device_id=sibling, device_id_type=MESH_ID)

        for p in (1, 0, 2):
            over_ici(p, mine).start()
        for p in (1, 0, 2):
            px, py = chips[p]
            over_ici(p, 2 * px + py).wait_recv()
            over_d2d(p, my_rows).start()
        for p in (1, 0, 2):
            over_d2d(p, their_rows).wait_recv()
        for p in range(3):
            over_ici(p, mine).wait_send()
            over_d2d(p, my_rows).wait_send()
        own.wait()

    return pl.pallas_call(
        body, name="gather_w_in", out_shape=jax.ShapeDtypeStruct((N_CHIPS,) + w_shard.shape, w_shard.dtype),
        in_specs=[ANY, ANY], out_specs=ANY,
        scratch_shapes=[pltpu.SemaphoreType.DMA((3,)), pltpu.SemaphoreType.DMA((3,)), pltpu.SemaphoreType.DMA((3,)),
                        pltpu.SemaphoreType.DMA((3,)), pltpu.SemaphoreType.DMA],
    )(w_shard, after)


def _kv_proj(mem2, wkv_full, b_kv):
    m = mem2.shape[0]

    def body(m_ref, w_ref, b_ref, o_ref):
        o_ref[...] = (_dot(m_ref[...].astype(BF16), w_ref[...]) + b_ref[...]).astype(BF16)

    return pl.pallas_call(
        body, name="kv_proj", grid=(m // MEM_LEN,),
        in_specs=[pl.BlockSpec((MEM_LEN, D_MODEL), lambda i: (i, 0)),
                  pl.BlockSpec((D_MODEL, D_MODEL), lambda i: (0, 0)),
                  pl.BlockSpec((1, D_MODEL), lambda i: (0, 0))],
        out_specs=pl.BlockSpec((MEM_LEN, D_MODEL), lambda i: (i, 0)),
        out_shape=jax.ShapeDtypeStruct((m, D_MODEL), BF16),
        compiler_params=_params(("arbitrary",)),
    )(mem2, wkv_full, b_kv)


CONV_ROWS = 16
SUBLANES = 8


def _shifted_planes(buf, tm):
    rows = tm + HALO - SUBLANES
    for s in range(1, SUBLANES):
        buf[s, 0:rows, :] = buf[0, s:s + rows, :]


def _window(buf, start, rows):
    s = start % SUBLANES
    return buf[s, start - s:start - s + rows, :]


def _forward(x2, w_full, b_in, kv, cw, cb, cg, cbeta, gg_, gb_, ws, bs_t, nb, seq, tm, after):
    nt = seq // tm
    n = nb * seq
    tiles = nb * nt

    def in_proj_piece(x_ref, w_ref, b_ref, hout_ref, xb_ref, hkeep, keep_now):
        xb = x_ref[...].astype(BF16)
        xb_ref[...] = xb

        def h_piece(j):
            cols = slice(j * W_IN_SHARD, (j + 1) * W_IN_SHARD)
            hj = (_dot(xb, w_ref[j]) + b_ref[:, cols]).astype(BF16)
            hout_ref[:, cols] = hj
            hkeep[keep_now, :, cols] = hj

        return h_piece

    def first_step(x_ref, w_ref, b_ref, kv_ref, cw_ref, cb_ref, clg_ref, clb_ref, glg_ref, glb_ref, ws_ref, bst_ref,
                   after_ref, hout_ref, y_ref, z_ref, xb_ref, hcbuf, hkeep):
        h_piece = in_proj_piece(x_ref, w_ref, b_ref, hout_ref, xb_ref, hkeep, 0)
        for j in range(N_CHIPS):
            h_piece(j)

    def later_step(x_ref, w_ref, b_ref, kv_ref, cw_ref, cb_ref, clg_ref, clb_ref, glg_ref, glb_ref, ws_ref, bst_ref,
                   after_ref, hout_ref, y_ref, z_ref, xb_ref, hcbuf, hkeep):
        step = pl.program_id(0)
        i = (step - 1) % nt
        keep_now = step % 2
        h_ref = hkeep.at[1 - keep_now]
        h_piece = in_proj_piece(x_ref, w_ref, b_ref, hout_ref, xb_ref, hkeep, keep_now)

        starts = i == 0

        @pl.when(starts)
        def _():
            hcbuf[0, 0:HALO, :] = jnp.zeros((HALO, D_CONV), F32)

        @pl.when(jnp.logical_not(starts))
        def _():
            hcbuf[0, 0:HALO, :] = hcbuf[0, tm:tm + HALO, :]

        every = slice(None)
        hcbuf[0, HALO:HALO + tm, :] = _f32(h_ref, every, C_A, D_CONV) * _sigmoid(_f32(h_ref, every, C_GLU, D_CONV))
        _shifted_planes(hcbuf, tm)
        for r in range(tm // CONV_ROWS):
            base = r * CONV_ROWS
            acc = jnp.broadcast_to(cb_ref[...], (CONV_ROWS, D_CONV))
            for k in range(CONV_WIDTH):
                acc = acc + cw_ref[k:k + 1, :] * _window(hcbuf, base + 2 + k, CONV_ROWS)
            z_ref[base:base + CONV_ROWS, :] = acc
            if r % 4 == 1:
                h_piece(r // 4)
        zhat, _ = _ln_stats(z_ref[...])
        zn = zhat * clg_ref[...] + clb_ref[...]
        cgate = _f32(h_ref, every, C_GATE, D_CONV)
        y_ref[:, 0:D_CONV] = (zn * _sigmoid(zn) * (cgate * _sigmoid(cgate))).astype(BF16)

        wsc, _ = _causal_ws(ws_ref)
        vhat, _ = _ln_stats(_gelu(_f32(h_ref, every, G_V, D_GMLP)))
        vn = (vhat * glg_ref[...] + glb_ref[...]).astype(BF16)
        for ch in range(tm // CHUNK):
            rows = slice(ch * CHUNK, (ch + 1) * CHUNK)
            for hd in range(N_GHEADS):
                cols = slice(hd * CHUNK, (hd + 1) * CHUNK)
                s = _dot(wsc[hd], vn[rows, cols]) + bst_ref[:, hd:hd + 1]
                u = _gelu(_f32(h_ref, rows, G_U + hd * CHUNK, CHUNK))
                gate = _f32(h_ref, rows, G_GATE + hd * CHUNK, CHUNK)
                y_ref[rows, D_CONV + hd * CHUNK:D_CONV + (hd + 1) * CHUNK] = (
                    u * s * (gate * _sigmoid(gate))).astype(BF16)

        scale = XHEAD ** -0.5
        for hd in range(N_XHEADS):
            q = h_ref[:, X_Q + hd * XHEAD:X_Q + (hd + 1) * XHEAD]
            k = kv_ref[:, hd * XHEAD:(hd + 1) * XHEAD]
            v = kv_ref[:, D_XATT + hd * XHEAD:D_XATT + (hd + 1) * XHEAD]
            s = _dot_nt(q, k) * scale
            e = jnp.exp(s - jnp.max(s, axis=-1, keepdims=True))
            p = e * (1.0 / jnp.sum(e, axis=-1, keepdims=True))
            o = _dot(p.astype(BF16), v)
            gate = _f32(h_ref, every, X_GATE + hd * XHEAD, XHEAD)
            y_ref[:, 2 * D_CONV + hd * XHEAD:2 * D_CONV + (hd + 1) * XHEAD] = (
                o * (gate * _sigmoid(gate))).astype(BF16)

    def body(*refs):
        step = pl.program_id(0)

        @pl.when(step == 0)
        def _():
            first_step(*refs)

        @pl.when(step > 0)
        def _():
            later_step(*refs)

    ahead = lambda s: (jnp.minimum(s, tiles - 1), 0)
    behind = lambda s: (jnp.maximum(s - 1, 0), 0)
    example = lambda s: (jnp.maximum(s - 1, 0) // nt, 0)
    const2 = lambda s: (0, 0)
    const3 = lambda s: (0, 0, 0)
    vec = pl.BlockSpec((1, D_CONV), const2)
    return pl.pallas_call(
        body, name="forward", grid=(tiles + 1,),
        in_specs=[pl.BlockSpec((tm, D_MODEL), ahead),
                  pl.BlockSpec((N_CHIPS, D_MODEL, W_IN_SHARD), const3, pipeline_mode=pl.Buffered(1)),
                  pl.BlockSpec((1, D_IN), const2),
                  pl.BlockSpec((MEM_LEN, D_MODEL), example),
                  pl.BlockSpec((CONV_WIDTH, D_CONV), const2), vec, vec, vec, vec, vec,
                  pl.BlockSpec((N_GHEADS, CHUNK, CHUNK), const3),
                  pl.BlockSpec((CHUNK, N_GHEADS), const2), ANY],
        out_specs=[pl.BlockSpec((tm, D_IN), ahead), pl.BlockSpec((tm, D_MIX), behind),
                   pl.BlockSpec((tm, D_CONV), behind), pl.BlockSpec((tm, D_MODEL), ahead)],
        out_shape=[jax.ShapeDtypeStruct((n, D_IN), BF16), jax.ShapeDtypeStruct((n, D_MIX), BF16),
                   jax.ShapeDtypeStruct((n, D_CONV), F32), jax.ShapeDtypeStruct((n, D_MODEL), BF16)],
        scratch_shapes=[pltpu.VMEM((SUBLANES, HALO + tm, D_CONV), F32), pltpu.VMEM((2, tm, D_IN), BF16)],
        compiler_params=_params(("arbitrary",)),
    )(x2, w_full, b_in, kv, cw, cb, cg, cbeta, gg_, gb_, ws, bs_t, after)


ROW_LOSS = 3


def _out_proj_loss(ycat, wout_full, b_out, ln_g, ln_b, x2, tgt2, tm):
    n = x2.shape[0]

    def body(y_ref, w_ref, bo_ref, g_ref, b_ref, x_ref, t_ref, dr_ref, drb_ref, dy_ref, vec_ref):
        i = pl.program_id(0)

        @pl.when(i == 0)
        def _():
            vec_ref[...] = jnp.zeros_like(vec_ref)

        r = ALPHA * x_ref[...] + _dot(y_ref[...], w_ref[...]) + bo_ref[...]
        rhat, rstd = _ln_stats(r)
        diff = rhat * g_ref[...] + b_ref[...] - t_ref[...]
        loss = 0.5 * jnp.sum(jnp.mean(diff * diff, axis=-1, keepdims=True), axis=0, keepdims=True)
        dout = diff * (1.0 / D_MODEL)
        dr = _ln_bwd(dout * g_ref[...], rhat, rstd)
        vec_ref[0:1, :] += _colsum(dr)
        vec_ref[1:2, :] += _colsum(dout * rhat)
        vec_ref[2:3, :] += _colsum(dout)
        vec_ref[ROW_LOSS:ROW_LOSS + 1, :] += jnp.broadcast_to(loss, (1, D_MODEL))
        dr_ref[...] = dr
        drb = dr.astype(BF16)
        drb_ref[...] = drb
        dy_ref[...] = _dot_nt(drb, w_ref[...]).astype(BF16)

    row = lambda i: (i, 0)
    const = lambda i: (0, 0)
    vec = pl.BlockSpec((1, D_MODEL), const)
    return pl.pallas_call(
        body, name="out_proj_loss", grid=(n // tm,),
        in_specs=[pl.BlockSpec((tm, D_MIX), row), pl.BlockSpec((D_MIX, D_MODEL), const), vec, vec, vec,
                  pl.BlockSpec((tm, D_MODEL), row), pl.BlockSpec((tm, D_MODEL), row)],
        out_specs=[pl.BlockSpec((tm, D_MODEL), row), pl.BlockSpec((tm, D_MODEL), row), pl.BlockSpec((tm, D_MIX), row),
                   pl.BlockSpec((8, D_MODEL), const)],
        out_shape=[jax.ShapeDtypeStruct((n, D_MODEL), F32), jax.ShapeDtypeStruct((n, D_MODEL), BF16),
                   jax.ShapeDtypeStruct((n, D_MIX), BF16), jax.ShapeDtypeStruct((8, D_MODEL), F32)],
        compiler_params=_params(("arbitrary",)),
    )(ycat, wout_full, b_out, ln_g, ln_b, x2, tgt2)


ROW_CB, ROW_CLG, ROW_CLB, ROW_GLG, ROW_GLB = 32, 33, 34, 35, 36


def _branch_bwd(h, z, dy, kv, cw, clg, clb, glg, glb, ws, bs_t, w_full, dr, nb, seq, tm, after):
    nt = seq // tm
    n = nb * seq
    tiles = nb * nt

    def place(s):
        sc = jnp.minimum(s, tiles - 1)
        return sc // nt, nt - 1 - sc % nt

    def dx_of_previous(w_ref, dr_ref, gx_ref, dhkeep, prev):
        gx_ref[...] = ALPHA * dr_ref[...]

        def dx_piece(j):
            gx_ref[...] += _dot_nt(dhkeep[prev, :, j * W_IN_SHARD:(j + 1) * W_IN_SHARD], w_ref[j])

        return dx_piece

    def last_step(h_ref, z_ref, dy_ref, kv_ref, cw_ref, clg_ref, clb_ref, glg_ref, glb_ref,
                  ws_ref, bst_ref, w_ref, dr_ref, after_ref,
                  dh_ref, gbin_ref, g768_ref, gws_ref, gbst_ref, dkv_ref, gbkv_ref, gx_ref,
                  dzbuf, dvnbuf, dhkeep):
        dx_piece = dx_of_previous(w_ref, dr_ref, gx_ref, dhkeep, 1 - tiles % 2)
        for j in range(N_CHIPS):
            dx_piece(j)
        _, keep = _causal_ws(ws_ref)
        for hd in range(N_GHEADS):
            gws_ref[hd] = jnp.where(keep, gws_ref[hd], 0.0)

    def tile_step(h_ref, z_ref, dy_ref, kv_ref, cw_ref, clg_ref, clb_ref, glg_ref, glb_ref,
                  ws_ref, bst_ref, w_ref, dr_ref, after_ref,
                  dh_ref, gbin_ref, g768_ref, gws_ref, gbst_ref, dkv_ref, gbkv_ref, gx_ref,
                  dzbuf, dvnbuf, dhkeep):
        step = pl.program_id(0)
        i = step % nt
        every = slice(None)
        keep_now = step % 2

        @pl.when(step == 0)
        def _():
            gbin_ref[...] = jnp.zeros_like(gbin_ref)
            g768_ref[...] = jnp.zeros_like(g768_ref)
            gws_ref[...] = jnp.zeros_like(gws_ref)
            gbst_ref[...] = jnp.zeros_like(gbst_ref)
            gbkv_ref[...] = jnp.zeros_like(gbkv_ref)
            dhkeep[1] = jnp.zeros((tm, D_IN), BF16)

        @pl.when(i == 0)
        def _():
            dkv_ref[...] = jnp.zeros_like(dkv_ref)

        dx_piece = dx_of_previous(w_ref, dr_ref, gx_ref, dhkeep, 1 - keep_now)

        def put(rows, col, width, val):
            vb = val.astype(BF16)
            dh_ref[rows, col:col + width] = vb
            dhkeep[keep_now, rows, col:col + width] = vb

        def emit(col, width, val):
            gbin_ref[:, col:col + width] += _colsum(val)
            put(every, col, width, val)

        d_c = dy_ref[:, 0:D_CONV].astype(F32)
        cgate = _f32(h_ref, every, C_GATE, D_CONV)
        sg = _sigmoid(cgate)
        zhat, zrstd = _ln_stats(z_ref[...])
        zn = zhat * clg_ref[...] + clb_ref[...]
        szn = _sigmoid(zn)
        emit(C_GATE, D_CONV, d_c * (zn * szn) * _dsilu(cgate, sg))
        dx_piece(0)
        dzn = d_c * (cgate * sg) * _dsilu(zn, szn)
        g768_ref[ROW_CLG:ROW_CLG + 1, :] += _colsum(dzn * zhat)
        g768_ref[ROW_CLB:ROW_CLB + 1, :] += _colsum(dzn)
        dz = _ln_bwd(dzn * clg_ref[...], zhat, zrstd)
        g768_ref[ROW_CB:ROW_CB + 1, :] += _colsum(dz)

        @pl.when(i == 0)
        def _():
            dzbuf[0, tm:tm + HALO, :] = jnp.zeros((HALO, D_CONV), F32)

        @pl.when(i > 0)
        def _():
            dzbuf[0, tm:tm + HALO, :] = dzbuf[0, 0:HALO, :]

        dzbuf[0, 0:tm, :] = dz
        _shifted_planes(dzbuf, tm)
        a = _f32(h_ref, every, C_A, D_CONV)
        sgl = _sigmoid(_f32(h_ref, every, C_GLU, D_CONV))
        hc = a * sgl

        for r in range(tm // CONV_ROWS):
            base = r * CONV_ROWS
            acc = jnp.zeros((CONV_ROWS, D_CONV), F32)
            for k in range(CONV_WIDTH):
                acc = acc + cw_ref[k:k + 1, :] * _window(dzbuf, base + 30 - k, CONV_ROWS)
            dvnbuf[base:base + CONV_ROWS, :] = acc
            if r % 8 == 3:
                dx_piece(1 + r // 8)
        dhc = dvnbuf[...]
        emit(C_A, D_CONV, dhc * sgl)
        emit(C_GLU, D_CONV, dhc * a * sgl * (1.0 - sgl))
        for k in range(CONV_WIDTH):
            g768_ref[k:k + 1, :] += _colsum(hc * _window(dzbuf, 30 - k, tm))
            if k == CONV_WIDTH // 2:
                dx_piece(3)

        wsc, _ = _causal_ws(ws_ref)
        v, dgelu_v = _gelu_and_grad(_f32(h_ref, every, G_V, D_GMLP))
        vhat, vrstd = _ln_stats(v)
        vn = (vhat * glg_ref[...] + glb_ref[...]).astype(BF16)
        for ch in range(tm // CHUNK):
            rows = slice(ch * CHUNK, (ch + 1) * CHUNK)
            for hd in range(N_GHEADS):
                cols = slice(hd * CHUNK, (hd + 1) * CHUNK)
                vn_blk = vn[rows, cols]
                s = _dot(wsc[hd], vn_blk) + bst_ref[:, hd:hd + 1]
                u, dgelu_u = _gelu_and_grad(_f32(h_ref, rows, G_U + hd * CHUNK, CHUNK))
                gate = _f32(h_ref, rows, G_GATE + hd * CHUNK, CHUNK)
                sgate = _sigmoid(gate)
                d_g = dy_ref[rows, D_CONV + hd * CHUNK:D_CONV + (hd + 1) * CHUNK].astype(F32)
                dgate = d_g * (u * s) * _dsilu(gate, sgate)
                gbin_ref[:, G_GATE + hd * CHUNK:G_GATE + (hd + 1) * CHUNK] += _colsum(dgate)
                put(rows, G_GATE + hd * CHUNK, CHUNK, dgate)
                dyg = d_g * (gate * sgate)
                du = dyg * s * dgelu_u
                gbin_ref[:, G_U + hd * CHUNK:G_U + (hd + 1) * CHUNK] += _colsum(du)
                put(rows, G_U + hd * CHUNK, CHUNK, du)
                ds = dyg * u
                dsb = ds.astype(BF16)
                gws_ref[hd] += _dot_nt(dsb, vn_blk)
                gbst_ref[:, hd:hd + 1] += jnp.sum(ds, axis=1, keepdims=True)
                dvnbuf[rows, cols] = _dot_tn(wsc[hd], dsb)
        dvn = dvnbuf[...]
        g768_ref[ROW_GLG:ROW_GLG + 1, :] += _colsum(dvn * vhat)
        g768_ref[ROW_GLB:ROW_GLB + 1, :] += _colsum(dvn)
        emit(G_V, D_GMLP, _ln_bwd(dvn * glg_ref[...], vhat, vrstd) * dgelu_v)

        scale = XHEAD ** -0.5
        for hd in range(N_XHEADS):
            q = h_ref[:, X_Q + hd * XHEAD:X_Q + (hd + 1) * XHEAD]
            k = kv_ref[:, hd * XHEAD:(hd + 1) * XHEAD]
            vv = kv_ref[:, D_XATT + hd * XHEAD:D_XATT + (hd + 1) * XHEAD]
            s = _dot_nt(q, k) * scale
            e = jnp.exp(s - jnp.max(s, axis=-1, keepdims=True))
            p = e * (1.0 / jnp.sum(e, axis=-1, keepdims=True))
            pb = p.astype(BF16)
            o = _dot(pb, vv)
            gate = _f32(h_ref, every, X_GATE + hd * XHEAD, XHEAD)
            sgate = _sigmoid(gate)
            d_x = dy_ref[:, 2 * D_CONV + hd * XHEAD:2 * D_CONV + (hd + 1) * XHEAD].astype(F32)
            emit(X_GATE + hd * XHEAD, XHEAD, d_x * o * _dsilu(gate, sgate))
            do = (d_x * (gate * sgate)).astype(BF16)
            dp = _dot_nt(do, vv)
            dsc = (p * (dp - jnp.sum(dp * p, axis=-1, keepdims=True))).astype(BF16)
            emit(X_Q + hd * XHEAD, XHEAD, _dot(dsc, k) * scale)
            dkv_ref[:, hd * XHEAD:(hd + 1) * XHEAD] += _dot_tn(dsc, q) * scale
            dkv_ref[:, D_XATT + hd * XHEAD:D_XATT + (hd + 1) * XHEAD] += _dot_tn(pb, do)

        @pl.when(i == nt - 1)
        def _():
            gbkv_ref[...] += _colsum(dkv_ref[...])

    def body(*refs):
        step = pl.program_id(0)

        @pl.when(step < tiles)
        def _():
            tile_step(*refs)

        @pl.when(step == tiles)
        def _():
            last_step(*refs)

    def row(s):
        b, ri = place(s)
        return b * nt + ri, 0

    def row_prev(s):
        b, ri = place(jnp.maximum(s - 1, 0))
        return b * nt + ri, 0

    example = lambda s: (place(s)[0], 0)
    const2 = lambda s: (0, 0)
    const3 = lambda s: (0, 0, 0)
    vec = pl.BlockSpec((1, D_CONV), const2)

    return pl.pallas_call(
        body, name="branch_bwd", grid=(tiles + 1,),
        in_specs=[pl.BlockSpec((tm, D_IN), row),
                  pl.BlockSpec((tm, D_CONV), row), pl.BlockSpec((tm, D_MIX), row),
                  pl.BlockSpec((MEM_LEN, D_MODEL), example),
                  pl.BlockSpec((CONV_WIDTH, D_CONV), const2), vec, vec, vec, vec,
                  pl.BlockSpec((N_GHEADS, CHUNK, CHUNK), const3),
                  pl.BlockSpec((CHUNK, N_GHEADS), const2),
                  pl.BlockSpec((N_CHIPS, D_MODEL, W_IN_SHARD), const3, pipeline_mode=pl.Buffered(1)),
                  pl.BlockSpec((tm, D_MODEL), row_prev), ANY],
        out_specs=[pl.BlockSpec((tm, D_IN), row),
                   pl.BlockSpec((1, D_IN), const2),
                   pl.BlockSpec((40, D_CONV), const2),
                   pl.BlockSpec((N_GHEADS, CHUNK, CHUNK), const3),
                   pl.BlockSpec((CHUNK, CHUNK), const2),
                   pl.BlockSpec((MEM_LEN, D_MODEL), example),
                   pl.BlockSpec((1, D_MODEL), const2),
                   pl.BlockSpec((tm, D_MODEL), row_prev)],
        out_shape=[jax.ShapeDtypeStruct((n, D_IN), BF16),
                   jax.ShapeDtypeStruct((1, D_IN), F32),
                   jax.ShapeDtypeStruct((40, D_CONV), F32),
                   jax.ShapeDtypeStruct((N_GHEADS, CHUNK, CHUNK), F32),
                   jax.ShapeDtypeStruct((CHUNK, CHUNK), F32),
                   jax.ShapeDtypeStruct((nb * MEM_LEN, D_MODEL), F32),
                   jax.ShapeDtypeStruct((1, D_MODEL), F32),
                   jax.ShapeDtypeStruct((n, D_MODEL), F32)],
        scratch_shapes=[pltpu.VMEM((SUBLANES, tm + HALO, D_CONV), F32), pltpu.VMEM((tm, D_CONV), F32),
                        pltpu.VMEM((2, tm, D_IN), BF16)],
        compiler_params=_params(("arbitrary",)),
    )(h, z, dy, kv, cw, clg, clb, glg, glb, ws, bs_t, w_full, dr, after)


def _grad_w(a, b, tk, name):
    m = a.shape[1]
    kdim, ncols = b.shape
    nk = kdim // tk
    shard = m // N_CHIPS
    oshape = (shard, ncols)
    a_spec = pl.BlockSpec((tk, shard), lambda j, kk: (kk, j))
    b_spec = pl.BlockSpec((tk, ncols), lambda j, kk: (kk, 0))

    def body(a_ref, b_ref, own_ref, ob_ref, acc):
        j = pl.program_id(0)
        kk = pl.program_id(1)
        mine = 2 * lax.axis_index("x") + lax.axis_index("y")

        @pl.when(kk == 0)
        def _():
            acc[...] = jnp.zeros_like(acc)

        acc[...] += _dot_tn(a_ref[...].astype(BF16), b_ref[...].astype(BF16))

        @pl.when(kk == nk - 1)
        def _():
            ob_ref[...] = acc[...].astype(BF16)

        @pl.when((kk == nk - 1) & (j == mine))
        def _():
            own_ref[...] = acc[...]

    return pl.pallas_call(
        body, name=name, grid=(N_CHIPS, nk),
        in_specs=[a_spec, b_spec],
        out_specs=[pl.BlockSpec(oshape, lambda j, kk: (0, 0)),
                   pl.BlockSpec((None,) + oshape, lambda j, kk: (j, 0, 0))],
        out_shape=[jax.ShapeDtypeStruct(oshape, F32), jax.ShapeDtypeStruct((N_CHIPS,) + oshape, BF16)],
        scratch_shapes=[pltpu.VMEM(oshape, F32)],
        compiler_params=_params(("arbitrary", "arbitrary")),
    )(a, b)


def _grad_w_block(a, b, block, tk, name, dtype, after):
    kdim, m = a.shape
    shard = b.shape[1] // N_CHIPS
    nk = kdim // tk

    def body(block_ref, a_ref, b_ref, after_ref, o_ref, acc):
        kk = pl.program_id(0)

        @pl.when(kk == 0)
        def _():
            acc[...] = jnp.zeros_like(acc)

        acc[...] += _dot_tn(a_ref[...].astype(BF16), b_ref[...])

        @pl.when(kk == nk - 1)
        def _():
            o_ref[...] = acc[...].astype(dtype)

    grid_spec = pltpu.PrefetchScalarGridSpec(
        num_scalar_prefetch=1, grid=(nk,),
        in_specs=[pl.BlockSpec((tk, m), lambda kk, blk: (kk, 0)),
                  pl.BlockSpec((tk, shard), lambda kk, blk: (kk, blk[0])), ANY],
        out_specs=pl.BlockSpec((m, shard), lambda kk, blk: (0, 0)),
        scratch_shapes=[pltpu.VMEM((m, shard), F32)])
    return pl.pallas_call(
        body, name=name, grid_spec=grid_spec, out_shape=jax.ShapeDtypeStruct((m, shard), dtype),
        compiler_params=_params(("arbitrary",)),
    )(block, a, b, after)


def _sum_small(owns, gots):
    n = len(owns)

    def body(*refs):
        for a in range(n):
            o_ref, g_ref, out_ref = refs[a], refs[n + a], refs[2 * n + a]
            out_ref[...] = (o_ref[...] + g_ref[1]) + (g_ref[0] + g_ref[2])

    return pl.pallas_call(
        body, name="sum_small", out_shape=[jax.ShapeDtypeStruct(o.shape, F32) for o in owns],
        compiler_params=pltpu.CompilerParams(vmem_limit_bytes=VMEM_LIMIT),
    )(*owns, *gots)


def _sum_chips(own, gots, tr, name):
    r, ccols = own.shape

    def body(o_ref, gx_ref, gy_ref, gxy_ref, out_ref):
        out_ref[...] = (o_ref[...] + gy_ref[...].astype(F32)) + (gx_ref[...].astype(F32) + gxy_ref[...].astype(F32))

    return pl.pallas_call(
        body, name=name, grid=(r // tr,),
        in_specs=[pl.BlockSpec((tr, ccols), lambda i: (i, 0))]
        + [pl.BlockSpec((None, tr, ccols), lambda i, slot=slot: (slot, i, 0)) for _, slot in gots],
        out_specs=pl.BlockSpec((tr, ccols), lambda i: (i, 0)),
        out_shape=jax.ShapeDtypeStruct((r, ccols), F32),
        compiler_params=_params(("arbitrary",)),
    )(own, *[g for g, _ in gots])


def _exchange_cores(parts, name):
    npart = len(parts)

    def body(*refs):
        in_refs = refs[0:npart]
        out_refs = refs[npart:2 * npart]
        send_sems, recv_sems = refs[2 * npart:]
        sibling = (lax.axis_index("x"), lax.axis_index("y"), 1 - lax.axis_index("c"))
        copies = [pltpu.make_async_remote_copy(
            src_ref=in_refs[a], dst_ref=out_refs[a], send_sem=send_sems.at[a], recv_sem=recv_sems.at[a],
            device_id=sibling, device_id_type=MESH_ID) for a in range(npart)]
        for cp in copies:
            cp.start()
        for cp in copies:
            cp.wait_recv()
        for cp in copies:
            cp.wait_send()

    return pl.pallas_call(
        body, name=name, out_shape=[jax.ShapeDtypeStruct(p.shape, p.dtype) for p in parts],
        in_specs=[ANY] * npart, out_specs=[ANY] * npart,
        scratch_shapes=[pltpu.SemaphoreType.DMA((npart,)), pltpu.SemaphoreType.DMA((npart,))],
    )(*parts)


def _adam(g, w, m, v):
    mn = ADAM_B1 * m + (1.0 - ADAM_B1) * g
    vn = ADAM_B2 * v + (1.0 - ADAM_B2) * (g * g)
    return g, -ADAM_LR * ((mn / BC1) / (jnp.sqrt(vn / BC2) + ADAM_EPS) + ADAM_WD * w), mn, vn


def _adamw(a, b, w, m, v, tr, name):
    r, ccols = w.shape

    def body(a_ref, b_ref, w_ref, m_ref, v_ref, *outs):
        res = _adam(a_ref[...] + b_ref[...], w_ref[...], m_ref[...], v_ref[...])
        for which in range(4):
            outs[which][...] = res[which]

    spec = pl.BlockSpec((tr, ccols), lambda i: (i, 0))
    shape = jax.ShapeDtypeStruct((r, ccols), F32)
    return pl.pallas_call(
        body, name=name, grid=(r // tr,), in_specs=[spec] * 5, out_specs=[spec] * 4, out_shape=[shape] * 4,
        compiler_params=_params(("arbitrary",)),
    )(a, b, w, m, v)


SMALL = ["b_in", "conv_b", "conv_ln_g", "conv_ln_b", "gmlp_ln_g", "gmlp_ln_b", "gmlp_ws", "gmlp_bs", "b_kv", "b_out",
         "ln_g", "ln_b"]


def _adamw_small(a_parts, b_parts, params):
    nparts, nparams = len(a_parts), len(params)

    def body(*refs):
        a = refs[0:nparts]
        b = refs[nparts:2 * nparts]
        prm = refs[2 * nparts:2 * nparts + 3 * nparams]
        outs = refs[2 * nparts + 3 * nparams:]
        gb_in, g768, gws, gbs_t, gb_kv, vec3 = [a[q][...] + b[q][...] for q in range(nparts)]
        grads = [gb_in, g768[ROW_CB:ROW_CB + 1], g768[ROW_CLG:ROW_CLG + 1], g768[ROW_CLB:ROW_CLB + 1],
                 g768[ROW_GLG:ROW_GLG + 1], g768[ROW_GLB:ROW_GLB + 1], gws, jnp.transpose(gbs_t)[0:N_GHEADS, :],
                 gb_kv, vec3[0:1], vec3[1:2], vec3[2:3]]
        for q, g in enumerate(grads):
            res = _adam(g, prm[3 * q][...], prm[3 * q + 1][...], prm[3 * q + 2][...])
            for which in range(4):
                outs[4 * q + which][...] = res[which]

    flat = [t for p in params for t in p]
    out_shape = [jax.ShapeDtypeStruct(p[0].shape, F32) for p in params for _ in range(4)]
    return pl.pallas_call(
        body, name="adamw_small", out_shape=out_shape,
        compiler_params=pltpu.CompilerParams(vmem_limit_bytes=VMEM_LIMIT),
    )(*a_parts, *b_parts, *flat)


def kernel(x, mem, w_in, b_in, conv_w, conv_b, conv_ln_g, conv_ln_b, gmlp_ln_g, gmlp_ln_b, gmlp_ws, gmlp_bs, w_kv, b_kv, w_out, b_out, ln_g, ln_b, loss_target, m_w_in, m_b_in, m_conv_w, m_conv_b, m_conv_ln_g, m_conv_ln_b, m_gmlp_ln_g, m_gmlp_ln_b, m_gmlp_ws, m_gmlp_bs, m_w_kv, m_b_kv, m_w_out, m_b_out, m_ln_g, m_ln_b, v_w_in, v_b_in, v_conv_w, v_conv_b, v_conv_ln_g, v_conv_ln_b, v_gmlp_ln_g, v_gmlp_ln_b, v_gmlp_ws, v_gmlp_bs, v_w_kv, v_b_kv, v_w_out, v_b_out, v_ln_g, v_ln_b):
    weights = dict(b_in=b_in, conv_b=conv_b, conv_ln_g=conv_ln_g, conv_ln_b=conv_ln_b, gmlp_ln_g=gmlp_ln_g,
                   gmlp_ln_b=gmlp_ln_b, gmlp_ws=gmlp_ws, gmlp_bs=gmlp_bs, b_kv=b_kv, b_out=b_out, ln_g=ln_g, ln_b=ln_b)
    mom_m = dict(b_in=m_b_in, conv_b=m_conv_b, conv_ln_g=m_conv_ln_g, conv_ln_b=m_conv_ln_b, gmlp_ln_g=m_gmlp_ln_g,
                 gmlp_ln_b=m_gmlp_ln_b, gmlp_ws=m_gmlp_ws, gmlp_bs=m_gmlp_bs, b_kv=m_b_kv, b_out=m_b_out,
                 ln_g=m_ln_g, ln_b=m_ln_b)
    mom_v = dict(b_in=v_b_in, conv_b=v_conv_b, conv_ln_g=v_conv_ln_g, conv_ln_b=v_conv_ln_b, gmlp_ln_g=v_gmlp_ln_g,
                 gmlp_ln_b=v_gmlp_ln_b, gmlp_ws=v_gmlp_ws, gmlp_bs=v_gmlp_bs, b_kv=v_b_kv, b_out=v_b_out,
                 ln_g=v_ln_g, ln_b=v_ln_b)
    nb, seq, _ = x.shape
    n = nb * seq
    tm = 256
    tk = min(2048, n)
    x2 = x.reshape(n, D_MODEL)
    tgt2 = loss_target.reshape(n, D_MODEL)
    mem2 = mem.reshape(nb * MEM_LEN, D_MODEL)
    chip = 2 * lax.axis_index("x") + lax.axis_index("y")
    bs_t = jnp.transpose(gmlp_bs[0])

    own_kv = [w_kv[0].astype(BF16), conv_w[0]]
    own_out = [w_out[0].astype(BF16)]
    ga = _start_exchange("gather_kv_start", own_kv, [False] * 2)
    win_g = _gather_w_in(w_in[0].astype(BF16), ga["token"])
    wkv_g, cw_g = _place_shards(own_kv, _wait_exchange("gather_kv_wait", ga, win_g), "place_kv")
    wkv_full = wkv_g.reshape(D_MODEL, D_MODEL)
    cw_full = jnp.transpose(cw_g, (1, 0, 2)).reshape(CONV_WIDTH, D_CONV)
    kv = _kv_proj(mem2, wkv_full, b_kv)
    gb = _start_exchange("gather_out_start", own_out, [False])
    h, ycat, z, x_bf = _forward(x2, win_g, b_in, kv, cw_full, conv_b, conv_ln_g, conv_ln_b, gmlp_ln_g, gmlp_ln_b,
                                gmlp_ws[0], bs_t, nb, seq, tm, gb["token"])
    (wout_g,) = _place_shards(own_out, _wait_exchange("gather_out_wait", gb, ycat), "place_out")
    wout_full = wout_g.reshape(D_MIX, D_MODEL)
    dr, dr_bf, dycat, vec3 = _out_proj_loss(ycat, wout_full, b_out, ln_g, ln_b, x2, tgt2, min(512, n))

    own_wout, gwout_b = _grad_w(ycat, dr_bf, min(2048, n), "grad_w_out")
    ex1 = _start_exchange("exchange1_start", [gwout_b, vec3], [True, False])
    dh, gb_in, g768, gws, gbs_t, dkv, gb_kv, grad_x2 = _branch_bwd(
        h, z, dycat, kv, cw_full, conv_ln_g, conv_ln_b, gmlp_ln_g, gmlp_ln_b, gmlp_ws[0], bs_t, win_g, dr,
        nb, seq, tm, ex1["token"])
    own_wkv, gwkv_b = _grad_w(mem2, dkv, nb * MEM_LEN, "grad_w_kv")
    small2 = [gb_in, g768, gws, gbs_t, gb_kv]
    ex2 = _start_exchange("exchange2_start", [gwkv_b] + small2, [True] + [False] * 5)
    block_of = lambda flip_bits: (chip ^ flip_bits).astype(jnp.int32).reshape(1)
    after, ex3 = ex2["token"], {}
    for flip, bits in ((2, 3), (1, 1), (0, 2)):
        part = _grad_w_block(x_bf, dh, block_of(bits), tk, f"grad_w_in_{flip}", BF16, after)
        ex3[flip] = _start_exchange(f"exchange3{flip}_start", [part], [False], (flip,))
        after = ex3[flip]["token"]
    own_win = _grad_w_block(x_bf, dh, block_of(0), tk, "grad_w_in_own", F32, after)
    got_wout, got_vec3 = _wait_exchange("exchange1_wait", ex1, own_win)
    got2 = _wait_exchange("exchange2_wait", ex2, own_win)

    sum_wout = _sum_chips(own_wout, [(got_wout, 0), (got_wout, 1), (got_wout, 2)], 256, "sum_w_out")
    sum_wkv = _sum_chips(own_wkv, [(got2[0], 0), (got2[0], 1), (got2[0], 2)], 256, "sum_w_kv")
    sum_small = list(_sum_small(small2 + [vec3], got2[1:] + [got_vec3]))
    sib = list(_exchange_cores([sum_wout, sum_wkv] + sum_small, "exchange_cores_rest"))
    loss = sum_small[5][ROW_LOSS, 0] + sib[7][ROW_LOSS, 0]
    big = {}
    big["w_out"] = _adamw(sum_wout, sib[0], w_out[0], m_w_out[0], v_w_out[0], 256, "adamw_w_out")
    big["w_kv"] = _adamw(sum_wkv, sib[1], w_kv[0], m_w_kv[0], v_w_kv[0], 256, "adamw_w_kv")
    lead = {"gmlp_ws", "gmlp_bs"}
    strip = lambda k, t: t[0] if k in lead else t
    sm = _adamw_small(sum_small, sib[2:], [tuple(strip(k, t[k]) for t in (weights, mom_m, mom_v)) for k in SMALL])
    small_out = {k: [sm[4 * q + which][None] if k in lead else sm[4 * q + which] for which in range(4)]
                 for q, k in enumerate(SMALL)}
    cw_a = lax.dynamic_slice_in_dim(sum_small[1][0:CONV_WIDTH + 1], chip * CONV_SHARD, CONV_SHARD, axis=1)
    cw_b = lax.dynamic_slice_in_dim(sib[3][0:CONV_WIDTH + 1], chip * CONV_SHARD, CONV_SHARD, axis=1)
    cwp = ((0, 1), (0, 0))
    cw_out = _adamw(cw_a, cw_b, jnp.pad(conv_w[0], cwp), jnp.pad(m_conv_w[0], cwp), jnp.pad(v_conv_w[0], cwp),
                    CONV_WIDTH + 1, "adamw_conv_w")

    done = cw_out[0]
    got_win = {flip: _wait_exchange(f"exchange3{flip}_wait", ex3[flip], done)[0] for flip in (2, 1, 0)}
    sum_win = _sum_chips(own_win, [(got_win[0], 0), (got_win[1], 0), (got_win[2], 0)], 256, "sum_w_in")
    (sib_win,) = _exchange_cores([sum_win], "exchange_cores_w_in")
    big["w_in"] = _adamw(sum_win, sib_win, w_in[0], m_w_in[0], v_w_in[0], 256, "adamw_w_in")

    order = ["w_in", "b_in", "conv_w", "conv_b", "conv_ln_g", "conv_ln_b", "gmlp_ln_g", "gmlp_ln_b", "gmlp_ws",
             "gmlp_bs", "w_kv", "b_kv", "w_out", "b_out", "ln_g", "ln_b"]
    result = [loss, grad_x2.reshape(nb, seq, D_MODEL)]
    for which in range(4):
        for k in order:
            if k in big:
                result.append(big[k][which][None])
            elif k == "conv_w":
                result.append(cw_out[which][0:CONV_WIDTH][None])
            else:
                result.append(small_out[k][which])
    return tuple(result)
```

```python
import functools
import math

import jax
import jax.numpy as jnp
from jax import lax
from jax.experimental import pallas as pl
from jax.experimental.pallas import tpu as pltpu

F32 = jnp.float32
BF16 = jnp.bfloat16

D_MODEL = 1024
MEM_LEN = 256
D_MIX = 2048
D_CONV = 768
D_GMLP = 768
D_XATT = 512
N_XHEADS = 4
XHEAD = 128
CONV_WIDTH = 31
CHUNK = 128
N_GHEADS = 6
D_IN = 3 * D_CONV + 3 * D_GMLP + 2 * D_XATT
ALPHA = 2.0 ** 0.25
LN_EPS = 1e-5
N_CHIPS = 4
W_IN_SHARD = D_IN // N_CHIPS
W_OUT_SHARD = D_MIX // N_CHIPS
W_KV_SHARD = D_MODEL // N_CHIPS
CONV_SHARD = D_CONV // N_CHIPS
HALO = 32

C_A, C_GLU, C_GATE = 0, 768, 1536
G_U, G_V, G_GATE = 2304, 3072, 3840
X_Q, X_GATE = 4608, 5120

ADAM_LR = 0.001
ADAM_B1 = 0.9
ADAM_B2 = 0.999
ADAM_EPS = 1e-08
ADAM_WD = 0.01
ADAM_STEP = 10
BC1 = 1.0 - ADAM_B1 ** ADAM_STEP
BC2 = 1.0 - ADAM_B2 ** ADAM_STEP

VMEM_LIMIT = 56 * 1024 * 1024
MESH_ID = pl.DeviceIdType.MESH
ANY = pl.BlockSpec(memory_space=pl.ANY)

GELU_C0 = math.sqrt(2.0 / math.pi)
GELU_C1 = 0.044715


def _sigmoid(v):
    return 0.5 + 0.5 * jnp.tanh(0.5 * v)


def _f32(ref, rows, col, width):
    return ref[rows, col:col + width].astype(F32)


def _dsilu(v, s):
    return s * (1.0 + v * (1.0 - s))


def _gelu_and_grad(v):
    t = jnp.tanh(GELU_C0 * (v + GELU_C1 * v * v * v))
    g = 0.5 * v * (1.0 + t)
    dg = 0.5 * (1.0 + t) + 0.5 * v * (1.0 - t * t) * (GELU_C0 * (1.0 + 3.0 * GELU_C1 * v * v))
    return g, dg


def _gelu(v):
    return 0.5 * v * (1.0 + jnp.tanh(GELU_C0 * (v + GELU_C1 * v * v * v)))


def _ln_stats(v):
    mu = jnp.mean(v, axis=-1, keepdims=True)
    vc = v - mu
    var = jnp.mean(vc * vc, axis=-1, keepdims=True)
    rstd = lax.rsqrt(var + LN_EPS)
    return vc * rstd, rstd


def _ln_bwd(dvhat, vhat, rstd):
    m1 = jnp.mean(dvhat, axis=-1, keepdims=True)
    m2 = jnp.mean(dvhat * vhat, axis=-1, keepdims=True)
    return rstd * (dvhat - m1 - vhat * m2)


def _colsum(v):
    return jnp.sum(v, axis=0, keepdims=True)


def _dot(a, b):
    return jnp.dot(a, b, preferred_element_type=F32)


def _dot_nt(a, b):
    return lax.dot_general(a, b, (((1,), (1,)), ((), ())), preferred_element_type=F32)


def _dot_tn(a, b):
    return lax.dot_general(a, b, (((0,), (0,)), ((), ())), preferred_element_type=F32)


def _causal_ws(ws_ref):
    row = lax.broadcasted_iota(jnp.int32, (CHUNK, CHUNK), 0)
    col = lax.broadcasted_iota(jnp.int32, (CHUNK, CHUNK), 1)
    keep = col <= row
    return [jnp.where(keep, ws_ref[hd], 0.0).astype(BF16) for hd in range(N_GHEADS)], keep


def _params(sem):
    return pltpu.CompilerParams(dimension_semantics=sem, vmem_limit_bytes=VMEM_LIMIT)


def _peer_chips():
    x, y, c = lax.axis_index("x"), lax.axis_index("y"), lax.axis_index("c")
    return [(1 - x, y), (x, 1 - y), (1 - x, 1 - y)], 2 * x + y, c


HBM = pl.BlockSpec(memory_space=pltpu.HBM)
SEM = pl.BlockSpec(memory_space=pltpu.SEMAPHORE)
EFFECT = pltpu.SideEffectType.DATAFLOW_SIDE_EFFECTING
ALL_FLIPS = (0, 1, 2)


def _exchange_copies(src_refs, land_refs, per_chip, flips, send_sems, recv_sems):
    chips, _, c = _peer_chips()
    n = len(src_refs)
    copies = []
    for q, p in enumerate(flips):
        px, py = chips[p]
        for a in range(n):
            src = src_refs[a].at[2 * px + py] if per_chip[a] else src_refs[a]
            copies.append(pltpu.make_async_remote_copy(
                src_ref=src, dst_ref=land_refs[a].at[q], send_sem=send_sems.at[n * q + a],
                recv_sem=recv_sems.at[n * q + a], device_id=(px, py, c), device_id_type=MESH_ID))
    return copies


def _start_exchange(name, srcs, per_chip, flips=ALL_FLIPS):
    n = len(srcs)
    nf = len(flips)
    lands = [lax.empty((nf,) + (s.shape[1:] if pc else s.shape), s.dtype) for s, pc in zip(srcs, per_chip)]

    def body(*refs):
        src_refs, land_refs = refs[0:n], refs[n:2 * n]
        send_sems, recv_sems = refs[2 * n], refs[2 * n + 1]
        token = refs[4 * n + 2]
        for cp in _exchange_copies(src_refs, land_refs, per_chip, flips, send_sems, recv_sems):
            cp.start()
        token[...] = jnp.zeros_like(token)

    out = pl.pallas_call(
        body, name=name,
        out_shape=(pltpu.SemaphoreType.DMA((nf * n,)), pltpu.SemaphoreType.DMA((nf * n,)),
                   *[pltpu.HBM(a.shape, a.dtype) for a in srcs + lands], jax.ShapeDtypeStruct((8, 128), F32)),
        in_specs=[HBM] * (2 * n),
        out_specs=(SEM, SEM, *[HBM] * (2 * n), pl.BlockSpec(memory_space=pltpu.VMEM)),
        input_output_aliases={a: 2 + a for a in range(2 * n)},
        compiler_params=pltpu.CompilerParams(has_side_effects=EFFECT),
    )(*[pltpu.with_memory_space_constraint(a, pltpu.HBM) for a in srcs + lands])
    return dict(send=out[0], recv=out[1], thru=list(out[2:2 * n + 2]), token=out[2 * n + 2], per_chip=per_chip,
                flips=flips)


def _wait_exchange(name, started, after):
    thru, per_chip, flips = started["thru"], started["per_chip"], started["flips"]
    n = len(thru) // 2

    def body(*refs):
        src_refs, land_refs = refs[0:n], refs[n:2 * n]
        send_sems, recv_sems = refs[2 * n], refs[2 * n + 1]
        for cp in _exchange_copies(src_refs, land_refs, per_chip, flips, send_sems, recv_sems):
            cp.wait_send()
            cp.wait_recv()

    out = pl.pallas_call(
        body, name=name, out_shape=tuple(pltpu.HBM(a.shape, a.dtype) for a in thru),
        in_specs=[HBM] * (2 * n) + [SEM, SEM, ANY], out_specs=tuple([HBM] * (2 * n)),
        input_output_aliases={a: a for a in range(2 * n)},
        compiler_params=pltpu.CompilerParams(has_side_effects=EFFECT),
    )(*thru, started["send"], started["recv"], after)
    return list(out[n:2 * n])


def _place_shards(owns, landeds, name):
    n = len(owns)
    mine = (2 * lax.axis_index("x") + lax.axis_index("y")).astype(jnp.int32).reshape(1)

    def body(mine_ref, *refs):
        own_refs, land_refs, out_refs = refs[0:n], refs[n:2 * n], refs[2 * n:3 * n]
        k = pl.program_id(0)
        for a in range(n):
            @pl.when(k == mine_ref[0])
            def _():
                out_refs[a][...] = own_refs[a][...]

            @pl.when(k != mine_ref[0])
            def _():
                out_refs[a][...] = land_refs[a][...]

    def slot(k, mine_ref):
        d = k ^ mine_ref[0]
        return jnp.where(d == 1, 1, jnp.where(d == 3, 2, 0))

    zeros = lambda o: (0,) * len(o.shape)
    grid_spec = pltpu.PrefetchScalarGridSpec(
        num_scalar_prefetch=1, grid=(N_CHIPS,),
        in_specs=[pl.BlockSpec(o.shape, lambda k, m, o=o: zeros(o)) for o in owns]
        + [pl.BlockSpec((None,) + o.shape, lambda k, m, o=o: (slot(k, m),) + zeros(o)) for o in owns],
        out_specs=[pl.BlockSpec((None,) + o.shape, lambda k, m, o=o: (k,) + zeros(o)) for o in owns])
    return pl.pallas_call(
        body, name=name, grid_spec=grid_spec,
        out_shape=[jax.ShapeDtypeStruct((N_CHIPS,) + o.shape, o.dtype) for o in owns],
        compiler_params=_params(("arbitrary",)),
    )(mine, *owns, *landeds)


def _gather_w_in(w_shard, after):
    half = D_MODEL // 2

    def body(w_ref, after_ref, out_ref, ici_send, ici_recv, d2d_send, d2d_recv, loc_sem):
        chips, mine, c = _peer_chips()
        sibling = (lax.axis_index("x"), lax.axis_index("y"), 1 - c)
        my_rows = pl.ds(pl.multiple_of(c * half, half), half)
        their_rows = pl.ds(pl.multiple_of((1 - c) * half, half), half)
        own = pltpu.make_async_copy(w_ref, out_ref.at[mine], loc_sem)
        own.start()

        def over_ici(p, slot):
            px, py = chips[p]
            return pltpu.make_async_remote_copy(
                src_ref=w_ref.at[my_rows, :], dst_ref=out_ref.at[slot, my_rows, :], send_sem=ici_send.at[p],
                recv_sem=ici_recv.at[p], device_id=(px, py, c), device_id_type=MESH_ID)

        def over_d2d(p, rows):
            px, py = chips[p]
            part = out_ref.at[2 * px + py, rows, :]
            return pltpu.make_async_remote_copy(
                src_ref=part, dst_ref=part, send_sem=d2d_send.at[p], recv_sem=d2d_recv.at[p],
                device_id=sibling, device_id_type=MESH_ID)

        for p in (1, 0, 2):
            over_ici(p, mine).start()
        for p in (1, 0, 2):
            px, py = chips[p]
            over_ici(p, 2 * px + py).wait_recv()
            over_d2d(p, my_rows).start()
        for p in (1, 0, 2):
            over_d2d(p, their_rows).wait_recv()
        for p in range(3):
            over_ici(p, mine).wait_send()
            over_d2d(p, my_rows).wait_send()
        own.wait()

    return pl.pallas_call(
        body, name="gather_w_in", out_shape=jax.ShapeDtypeStruct((N_CHIPS,) + w_shard.shape, w_shard.dtype),
        in_specs=[ANY, ANY], out_specs=ANY,
        scratch_shapes=[pltpu.SemaphoreType.DMA((3,)), pltpu.SemaphoreType.DMA((3,)), pltpu.SemaphoreType.DMA((3,)),
                        pltpu.SemaphoreType.DMA((3,)), pltpu.SemaphoreType.DMA],
    )(w_shard, after)


def _kv_proj(mem2, wkv_full, b_kv):
    m = mem2.shape[0]

    def body(m_ref, w_ref, b_ref, o_ref):
        o_ref[...] = (_dot(m_ref[...].astype(BF16), w_ref[...]) + b_ref[...]).astype(BF16)

    return pl.pallas_call(
        body, name="kv_proj", grid=(m // MEM_LEN,),
        in_specs=[pl.BlockSpec((MEM_LEN, D_MODEL), lambda i: (i, 0)),
                  pl.BlockSpec((D_MODEL, D_MODEL), lambda i: (0, 0)),
                  pl.BlockSpec((1, D_MODEL), lambda i: (0, 0))],
        out_specs=pl.BlockSpec((MEM_LEN, D_MODEL), lambda i: (i, 0)),
        out_shape=jax.ShapeDtypeStruct((m, D_MODEL), BF16),
        compiler_params=_params(("arbitrary",)),
    )(mem2, wkv_full, b_kv)


CONV_ROWS = 16
SUBLANES = 8


def _shifted_planes(buf, tm):
    rows = tm + HALO - SUBLANES
    for s in range(1, SUBLANES):
        buf[s, 0:rows, :] = buf[0, s:s + rows, :]


def _window(buf, start, rows):
    s = start % SUBLANES
    return buf[s, start - s:start - s + rows, :]


def _forward(x2, w_full, b_in, kv, cw, cb, cg, cbeta, gg_, gb_, ws, bs_t, nb, seq, tm, after):
    nt = seq // tm
    n = nb * seq
    tiles = nb * nt

    def in_proj_piece(x_ref, w_ref, b_ref, hout_ref, xb_ref, hkeep, keep_now):
        xb = x_ref[...].astype(BF16)
        xb_ref[...] = xb

        def h_piece(j):
            cols = slice(j * W_IN_SHARD, (j + 1) * W_IN_SHARD)
            hj = (_dot(xb, w_ref[j]) + b_ref[:, cols]).astype(BF16)
            hout_ref[:, cols] = hj
            hkeep[keep_now, :, cols] = hj

        return h_piece

    def first_step(x_ref, w_ref, b_ref, kv_ref, cw_ref, cb_ref, clg_ref, clb_ref, glg_ref, glb_ref, ws_ref, bst_ref,
                   after_ref, hout_ref, y_ref, z_ref, xb_ref, hcbuf, hkeep):
        h_piece = in_proj_piece(x_ref, w_ref, b_ref, hout_ref, xb_ref, hkeep, 0)
        for j in range(N_CHIPS):
            h_piece(j)

    def later_step(x_ref, w_ref, b_ref, kv_ref, cw_ref, cb_ref, clg_ref, clb_ref, glg_ref, glb_ref, ws_ref, bst_ref,
                   after_ref, hout_ref, y_ref, z_ref, xb_ref, hcbuf, hkeep):
        step = pl.program_id(0)
        i = (step - 1) % nt
        keep_now = step % 2
        h_ref = hkeep.at[1 - keep_now]
        h_piece = in_proj_piece(x_ref, w_ref, b_ref, hout_ref, xb_ref, hkeep, keep_now)

        starts = i == 0

        @pl.when(starts)
        def _():
            hcbuf[0, 0:HALO, :] = jnp.zeros((HALO, D_CONV), F32)

        @pl.when(jnp.logical_not(starts))
        def _():
            hcbuf[0, 0:HALO, :] = hcbuf[0, tm:tm + HALO, :]

        every = slice(None)
        hcbuf[0, HALO:HALO + tm, :] = _f32(h_ref, every, C_A, D_CONV) * _sigmoid(_f32(h_ref, every, C_GLU, D_CONV))
        h_piece(0)
        _shifted_planes(hcbuf, tm)
        for r in range(tm // CONV_ROWS):
            base = r * CONV_ROWS
            acc = jnp.broadcast_to(cb_ref[...], (CONV_ROWS, D_CONV))
            for k in range(CONV_WIDTH):
                acc = acc + cw_ref[k:k + 1, :] * _window(hcbuf, base + 2 + k, CONV_ROWS)
            z_ref[base:base + CONV_ROWS, :] = acc
            if r % 8 == 3:
                h_piece(1 + r // 8)
        zhat, _ = _ln_stats(z_ref[...])
        zn = zhat * clg_ref[...] + clb_ref[...]
        cgate = _f32(h_ref, every, C_GATE, D_CONV)
        y_ref[:, 0:D_CONV] = (zn * _sigmoid(zn) * (cgate * _sigmoid(cgate))).astype(BF16)

        wsc, _ = _causal_ws(ws_ref)
        g_v = _gelu(_f32(h_ref, every, G_V, D_GMLP))
        h_piece(3)
        vhat, _ = _ln_stats(g_v)
        vn = (vhat * glg_ref[...] + glb_ref[...]).astype(BF16)
        for ch in range(tm // CHUNK):
            rows = slice(ch * CHUNK, (ch + 1) * CHUNK)
            for hd in range(N_GHEADS):
                cols = slice(hd * CHUNK, (hd + 1) * CHUNK)
                s = _dot(wsc[hd], vn[rows, cols]) + bst_ref[:, hd:hd + 1]
                u = _gelu(_f32(h_ref, rows, G_U + hd * CHUNK, CHUNK))
                gate = _f32(h_ref, rows, G_GATE + hd * CHUNK, CHUNK)
                y_ref[rows, D_CONV + hd * CHUNK:D_CONV + (hd + 1) * CHUNK] = (
                    u * s * (gate * _sigmoid(gate))).astype(BF16)

        scale = XHEAD ** -0.5
        for hd in range(N_XHEADS):
            q = h_ref[:, X_Q + hd * XHEAD:X_Q + (hd + 1) * XHEAD]
            k = kv_ref[:, hd * XHEAD:(hd + 1) * XHEAD]
            v = kv_ref[:, D_XATT + hd * XHEAD:D_XATT + (hd + 1) * XHEAD]
            s = _dot_nt(q, k) * scale
            e = jnp.exp(s - jnp.max(s, axis=-1, keepdims=True))
            p = e * (1.0 / jnp.sum(e, axis=-1, keepdims=True))
            o = _dot(p.astype(BF16), v)
            gate = _f32(h_ref, every, X_GATE + hd * XHEAD, XHEAD)
            y_ref[:, 2 * D_CONV + hd * XHEAD:2 * D_CONV + (hd + 1) * XHEAD] = (
                o * (gate * _sigmoid(gate))).astype(BF16)

    def body(*refs):
        step = pl.program_id(0)

        @pl.when(step == 0)
        def _():
            first_step(*refs)

        @pl.when(step > 0)
        def _():
            later_step(*refs)

    ahead = lambda s: (jnp.minimum(s, tiles - 1), 0)
    behind = lambda s: (jnp.maximum(s - 1, 0), 0)
    example = lambda s: (jnp.maximum(s - 1, 0) // nt, 0)
    const2 = lambda s: (0, 0)
    const3 = lambda s: (0, 0, 0)
    vec = pl.BlockSpec((1, D_CONV), const2)
    return pl.pallas_call(
        body, name="forward", grid=(tiles + 1,),
        in_specs=[pl.BlockSpec((tm, D_MODEL), ahead),
                  pl.BlockSpec((N_CHIPS, D_MODEL, W_IN_SHARD), const3, pipeline_mode=pl.Buffered(1)),
                  pl.BlockSpec((1, D_IN), const2),
                  pl.BlockSpec((MEM_LEN, D_MODEL), example),
                  pl.BlockSpec((CONV_WIDTH, D_CONV), const2), vec, vec, vec, vec, vec,
                  pl.BlockSpec((N_GHEADS, CHUNK, CHUNK), const3),
                  pl.BlockSpec((CHUNK, N_GHEADS), const2), ANY],
        out_specs=[pl.BlockSpec((tm, D_IN), ahead), pl.BlockSpec((tm, D_MIX), behind),
                   pl.BlockSpec((tm, D_CONV), behind), pl.BlockSpec((tm, D_MODEL), ahead)],
        out_shape=[jax.ShapeDtypeStruct((n, D_IN), BF16), jax.ShapeDtypeStruct((n, D_MIX), BF16),
                   jax.ShapeDtypeStruct((n, D_CONV), F32), jax.ShapeDtypeStruct((n, D_MODEL), BF16)],
        scratch_shapes=[pltpu.VMEM((SUBLANES, HALO + tm, D_CONV), F32), pltpu.VMEM((2, tm, D_IN), BF16)],
        compiler_params=_params(("arbitrary",)),
    )(x2, w_full, b_in, kv, cw, cb, cg, cbeta, gg_, gb_, ws, bs_t, after)


ROW_LOSS = 3


def _out_proj_loss(ycat, wout_full, b_out, ln_g, ln_b, x2, tgt2, tm):
    n = x2.shape[0]

    def body(y_ref, w_ref, bo_ref, g_ref, b_ref, x_ref, t_ref, dr_ref, drb_ref, dy_ref, vec_ref):
        i = pl.program_id(0)

        @pl.when(i == 0)
        def _():
            vec_ref[...] = jnp.zeros_like(vec_ref)

        r = ALPHA * x_ref[...] + _dot(y_ref[...], w_ref[...]) + bo_ref[...]
        rhat, rstd = _ln_stats(r)
        diff = rhat * g_ref[...] + b_ref[...] - t_ref[...]
        loss = 0.5 * jnp.sum(jnp.mean(diff * diff, axis=-1, keepdims=True), axis=0, keepdims=True)
        dout = diff * (1.0 / D_MODEL)
        dr = _ln_bwd(dout * g_ref[...], rhat, rstd)
        vec_ref[0:1, :] += _colsum(dr)
        vec_ref[1:2, :] += _colsum(dout * rhat)
        vec_ref[2:3, :] += _colsum(dout)
        vec_ref[ROW_LOSS:ROW_LOSS + 1, :] += jnp.broadcast_to(loss, (1, D_MODEL))
        dr_ref[...] = dr
        drb = dr.astype(BF16)
        drb_ref[...] = drb
        dy_ref[...] = _dot_nt(drb, w_ref[...]).astype(BF16)

    row = lambda i: (i, 0)
    const = lambda i: (0, 0)
    vec = pl.BlockSpec((1, D_MODEL), const)
    return pl.pallas_call(
        body, name="out_proj_loss", grid=(n // tm,),
        in_specs=[pl.BlockSpec((tm, D_MIX), row), pl.BlockSpec((D_MIX, D_MODEL), const), vec, vec, vec,
                  pl.BlockSpec((tm, D_MODEL), row), pl.BlockSpec((tm, D_MODEL), row)],
        out_specs=[pl.BlockSpec((tm, D_MODEL), row), pl.BlockSpec((tm, D_MODEL), row), pl.BlockSpec((tm, D_MIX), row),
                   pl.BlockSpec((8, D_MODEL), const)],
        out_shape=[jax.ShapeDtypeStruct((n, D_MODEL), F32), jax.ShapeDtypeStruct((n, D_MODEL), BF16),
                   jax.ShapeDtypeStruct((n, D_MIX), BF16), jax.ShapeDtypeStruct((8, D_MODEL), F32)],
        compiler_params=_params(("arbitrary",)),
    )(ycat, wout_full, b_out, ln_g, ln_b, x2, tgt2)


ROW_CB, ROW_CLG, ROW_CLB, ROW_GLG, ROW_GLB = 32, 33, 34, 35, 36


def _branch_bwd(h, z, dy, kv, cw, clg, clb, glg, glb, ws, bs_t, w_full, dr, nb, seq, tm, after):
    nt = seq // tm
    n = nb * seq
    tiles = nb * nt

    def place(s):
        sc = jnp.minimum(s, tiles - 1)
        return sc // nt, nt - 1 - sc % nt

    def dx_of_previous(w_ref, dr_ref, gx_ref, dhkeep, prev):
        gx_ref[...] = ALPHA * dr_ref[...]

        def dx_piece(j):
            gx_ref[...] += _dot_nt(dhkeep[prev, :, j * W_IN_SHARD:(j + 1) * W_IN_SHARD], w_ref[j])

        return dx_piece

    def last_step(h_ref, z_ref, dy_ref, kv_ref, cw_ref, clg_ref, clb_ref, glg_ref, glb_ref,
                  ws_ref, bst_ref, w_ref, dr_ref, after_ref,
                  dh_ref, gbin_ref, g768_ref, gws_ref, gbst_ref, dkv_ref, gbkv_ref, gx_ref,
                  dzbuf, dvnbuf, dhkeep):
        dx_piece = dx_of_previous(w_ref, dr_ref, gx_ref, dhkeep, 1 - tiles % 2)
        for j in range(N_CHIPS):
            dx_piece(j)
        _, keep = _causal_ws(ws_ref)
        for hd in range(N_GHEADS):
            gws_ref[hd] = jnp.where(keep, gws_ref[hd], 0.0)

    def tile_step(h_ref, z_ref, dy_ref, kv_ref, cw_ref, clg_ref, clb_ref, glg_ref, glb_ref,
                  ws_ref, bst_ref, w_ref, dr_ref, after_ref,
                  dh_ref, gbin_ref, g768_ref, gws_ref, gbst_ref, dkv_ref, gbkv_ref, gx_ref,
                  dzbuf, dvnbuf, dhkeep):
        step = pl.program_id(0)
        i = step % nt
        every = slice(None)
        keep_now = step % 2

        @pl.when(step == 0)
        def _():
            gbin_ref[...] = jnp.zeros_like(gbin_ref)
            g768_ref[...] = jnp.zeros_like(g768_ref)
            gws_ref[...] = jnp.zeros_like(gws_ref)
            gbst_ref[...] = jnp.zeros_like(gbst_ref)
            gbkv_ref[...] = jnp.zeros_like(gbkv_ref)
            dhkeep[1] = jnp.zeros((tm, D_IN), BF16)

        @pl.when(i == 0)
        def _():
            dkv_ref[...] = jnp.zeros_like(dkv_ref)

        dx_piece = dx_of_previous(w_ref, dr_ref, gx_ref, dhkeep, 1 - keep_now)

        def put(rows, col, width, val):
            vb = val.astype(BF16)
            dh_ref[rows, col:col + width] = vb
            dhkeep[keep_now, rows, col:col + width] = vb

        def emit(col, width, val):
            gbin_ref[:, col:col + width] += _colsum(val)
            put(every, col, width, val)

        d_c = dy_ref[:, 0:D_CONV].astype(F32)
        cgate = _f32(h_ref, every, C_GATE, D_CONV)
        sg = _sigmoid(cgate)
        zhat, zrstd = _ln_stats(z_ref[...])
        zn = zhat * clg_ref[...] + clb_ref[...]
        szn = _sigmoid(zn)
        emit(C_GATE, D_CONV, d_c * (zn * szn) * _dsilu(cgate, sg))
        dx_piece(0)
        dzn = d_c * (cgate * sg) * _dsilu(zn, szn)
        g768_ref[ROW_CLG:ROW_CLG + 1, :] += _colsum(dzn * zhat)
        g768_ref[ROW_CLB:ROW_CLB + 1, :] += _colsum(dzn)
        dz = _ln_bwd(dzn * clg_ref[...], zhat, zrstd)
        g768_ref[ROW_CB:ROW_CB + 1, :] += _colsum(dz)

        @pl.when(i == 0)
        def _():
            dzbuf[0, tm:tm + HALO, :] = jnp.zeros((HALO, D_CONV), F32)

        @pl.when(i > 0)
        def _():
            dzbuf[0, tm:tm + HALO, :] = dzbuf[0, 0:HALO, :]

        dzbuf[0, 0:tm, :] = dz
        _shifted_planes(dzbuf, tm)
        a = _f32(h_ref, every, C_A, D_CONV)
        sgl = _sigmoid(_f32(h_ref, every, C_GLU, D_CONV))
        hc = a * sgl

        for r in range(tm // CONV_ROWS):
            base = r * CONV_ROWS
            acc = jnp.zeros((CONV_ROWS, D_CONV), F32)
            for k in range(CONV_WIDTH):
                acc = acc + cw_ref[k:k + 1, :] * _window(dzbuf, base + 30 - k, CONV_ROWS)
            dvnbuf[base:base + CONV_ROWS, :] = acc
            if r % 8 == 3:
                dx_piece(1 + r // 8)
        dhc = dvnbuf[...]
        emit(C_A, D_CONV, dhc * sgl)
        emit(C_GLU, D_CONV, dhc * a * sgl * (1.0 - sgl))
        for k in range(CONV_WIDTH):
            g768_ref[k:k + 1, :] += _colsum(hc * _window(dzbuf, 30 - k, tm))
            if k == CONV_WIDTH // 2:
                dx_piece(3)

        wsc, _ = _causal_ws(ws_ref)
        v, dgelu_v = _gelu_and_grad(_f32(h_ref, every, G_V, D_GMLP))
        vhat, vrstd = _ln_stats(v)
        vn = (vhat * glg_ref[...] + glb_ref[...]).astype(BF16)
        for ch in range(tm // CHUNK):
            rows = slice(ch * CHUNK, (ch + 1) * CHUNK)
            for hd in range(N_GHEADS):
                cols = slice(hd * CHUNK, (hd + 1) * CHUNK)
                vn_blk = vn[rows, cols]
                s = _dot(wsc[hd], vn_blk) + bst_ref[:, hd:hd + 1]
                u, dgelu_u = _gelu_and_grad(_f32(h_ref, rows, G_U + hd * CHUNK, CHUNK))
                gate = _f32(h_ref, rows, G_GATE + hd * CHUNK, CHUNK)
                sgate = _sigmoid(gate)
                d_g = dy_ref[rows, D_CONV + hd * CHUNK:D_CONV + (hd + 1) * CHUNK].astype(F32)
                dgate = d_g * (u * s) * _dsilu(gate, sgate)
                gbin_ref[:, G_GATE + hd * CHUNK:G_GATE + (hd + 1) * CHUNK] += _colsum(dgate)
                put(rows, G_GATE + hd * CHUNK, CHUNK, dgate)
                dyg = d_g * (gate * sgate)
                du = dyg * s * dgelu_u
                gbin_ref[:, G_U + hd * CHUNK:G_U + (hd + 1) * CHUNK] += _colsum(du)
                put(rows, G_U + hd * CHUNK, CHUNK, du)
                ds = dyg * u
                dsb = ds.astype(BF16)
                gws_ref[hd] += _dot_nt(dsb, vn_blk)
                gbst_ref[:, hd:hd + 1] += jnp.sum(ds, axis=1, keepdims=True)
                dvnbuf[rows, cols] = _dot_tn(wsc[hd], dsb)
        dvn = dvnbuf[...]
        g768_ref[ROW_GLG:ROW_GLG + 1, :] += _colsum(dvn * vhat)
        g768_ref[ROW_GLB:ROW_GLB + 1, :] += _colsum(dvn)
        emit(G_V, D_GMLP, _ln_bwd(dvn * glg_ref[...], vhat, vrstd) * dgelu_v)

        scale = XHEAD ** -0.5
        for hd in range(N_XHEADS):
            q = h_ref[:, X_Q + hd * XHEAD:X_Q + (hd + 1) * XHEAD]
            k = kv_ref[:, hd * XHEAD:(hd + 1) * XHEAD]
            vv = kv_ref[:, D_XATT + hd * XHEAD:D_XATT + (hd + 1) * XHEAD]
            s = _dot_nt(q, k) * scale
            e = jnp.exp(s - jnp.max(s, axis=-1, keepdims=True))
            p = e * (1.0 / jnp.sum(e, axis=-1, keepdims=True))
            pb = p.astype(BF16)
            o = _dot(pb, vv)
            gate = _f32(h_ref, every, X_GATE + hd * XHEAD, XHEAD)
            sgate = _sigmoid(gate)
            d_x = dy_ref[:, 2 * D_CONV + hd * XHEAD:2 * D_CONV + (hd + 1) * XHEAD].astype(F32)
            emit(X_GATE + hd * XHEAD, XHEAD, d_x * o * _dsilu(gate, sgate))
            do = (d_x * (gate * sgate)).astype(BF16)
            dp = _dot_nt(do, vv)
            dsc = (p * (dp - jnp.sum(dp * p, axis=-1, keepdims=True))).astype(BF16)
            emit(X_Q + hd * XHEAD, XHEAD, _dot(dsc, k) * scale)
            dkv_ref[:, hd * XHEAD:(hd + 1) * XHEAD] += _dot_tn(dsc, q) * scale
            dkv_ref[:, D_XATT + hd * XHEAD:D_XATT + (hd + 1) * XHEAD] += _dot_tn(pb, do)

        @pl.when(i == nt - 1)
        def _():
            gbkv_ref[...] += _colsum(dkv_ref[...])

    def body(*refs):
        step = pl.program_id(0)

        @pl.when(step < tiles)
        def _():
            tile_step(*refs)

        @pl.when(step == tiles)
        def _():
            last_step(*refs)

    def row(s):
        b, ri = place(s)
        return b * nt + ri, 0

    def row_prev(s):
        b, ri = place(jnp.maximum(s - 1, 0))
        return b * nt + ri, 0

    example = lambda s: (place(s)[0], 0)
    const2 = lambda s: (0, 0)
    const3 = lambda s: (0, 0, 0)
    vec = pl.BlockSpec((1, D_CONV), const2)

    return pl.pallas_call(
        body, name="branch_bwd", grid=(tiles + 1,),
        in_specs=[pl.BlockSpec((tm, D_IN), row),
                  pl.BlockSpec((tm, D_CONV), row), pl.BlockSpec((tm, D_MIX), row),
                  pl.BlockSpec((MEM_LEN, D_MODEL), example),
                  pl.BlockSpec((CONV_WIDTH, D_CONV), const2), vec, vec, vec, vec,
                  pl.BlockSpec((N_GHEADS, CHUNK, CHUNK), const3),
                  pl.BlockSpec((CHUNK, N_GHEADS), const2),
                  pl.BlockSpec((N_CHIPS, D_MODEL, W_IN_SHARD), const3, pipeline_mode=pl.Buffered(1)),
                  pl.BlockSpec((tm, D_MODEL), row_prev), ANY],
        out_specs=[pl.BlockSpec((tm, D_IN), row),
                   pl.BlockSpec((1, D_IN), const2),
                   pl.BlockSpec((40, D_CONV), const2),
                   pl.BlockSpec((N_GHEADS, CHUNK, CHUNK), const3),
                   pl.BlockSpec((CHUNK, CHUNK), const2),
                   pl.BlockSpec((MEM_LEN, D_MODEL), example),
                   pl.BlockSpec((1, D_MODEL), const2),
                   pl.BlockSpec((tm, D_MODEL), row_prev)],
        out_shape=[jax.ShapeDtypeStruct((n, D_IN), BF16),
                   jax.ShapeDtypeStruct((1, D_IN), F32),
                   jax.ShapeDtypeStruct((40, D_CONV), F32),
                   jax.ShapeDtypeStruct((N_GHEADS, CHUNK, CHUNK), F32),
                   jax.ShapeDtypeStruct((CHUNK, CHUNK), F32),
                   jax.ShapeDtypeStruct((nb * MEM_LEN, D_MODEL), F32),
                   jax.ShapeDtypeStruct((1, D_MODEL), F32),
                   jax.ShapeDtypeStruct((n, D_MODEL), F32)],
        scratch_shapes=[pltpu.VMEM((SUBLANES, tm + HALO, D_CONV), F32), pltpu.VMEM((tm, D_CONV), F32),
                        pltpu.VMEM((2, tm, D_IN), BF16)],
        compiler_params=_params(("arbitrary",)),
    )(h, z, dy, kv, cw, clg, clb, glg, glb, ws, bs_t, w_full, dr, after)


def _grad_w(a, b, tk, name):
    m = a.shape[1]
    kdim, ncols = b.shape
    nk = kdim // tk
    shard = m // N_CHIPS
    oshape = (shard, ncols)
    a_spec = pl.BlockSpec((tk, shard), lambda j, kk: (kk, j))
    b_spec = pl.BlockSpec((tk, ncols), lambda j, kk: (kk, 0))

    def body(a_ref, b_ref, own_ref, ob_ref, acc):
        j = pl.program_id(0)
        kk = pl.program_id(1)
        mine = 2 * lax.axis_index("x") + lax.axis_index("y")

        @pl.when(kk == 0)
        def _():
            acc[...] = jnp.zeros_like(acc)

        acc[...] += _dot_tn(a_ref[...].astype(BF16), b_ref[...].astype(BF16))

        @pl.when(kk == nk - 1)
        def _():
            ob_ref[...] = acc[...].astype(BF16)

        @pl.when((kk == nk - 1) & (j == mine))
        def _():
            own_ref[...] = acc[...]

    return pl.pallas_call(
        body, name=name, grid=(N_CHIPS, nk),
        in_specs=[a_spec, b_spec],
        out_specs=[pl.BlockSpec(oshape, lambda j, kk: (0, 0)),
                   pl.BlockSpec((None,) + oshape, lambda j, kk: (j, 0, 0))],
        out_shape=[jax.ShapeDtypeStruct(oshape, F32), jax.ShapeDtypeStruct((N_CHIPS,) + oshape, BF16)],
        scratch_shapes=[pltpu.VMEM(oshape, F32)],
        compiler_params=_params(("arbitrary", "arbitrary")),
    )(a, b)


def _grad_w_block(a, b, block, tk, name, dtype, after):
    kdim, m = a.shape
    shard = b.shape[1] // N_CHIPS
    nk = kdim // tk

    def body(block_ref, a_ref, b_ref, after_ref, o_ref, acc):
        kk = pl.program_id(0)

        @pl.when(kk == 0)
        def _():
            acc[...] = jnp.zeros_like(acc)

        acc[...] += _dot_tn(a_ref[...].astype(BF16), b_ref[...])

        @pl.when(kk == nk - 1)
        def _():
            o_ref[...] = acc[...].astype(dtype)

    grid_spec = pltpu.PrefetchScalarGridSpec(
        num_scalar_prefetch=1, grid=(nk,),
        in_specs=[pl.BlockSpec((tk, m), lambda kk, blk: (kk, 0)),
                  pl.BlockSpec((tk, shard), lambda kk, blk: (kk, blk[0])), ANY],
        out_specs=pl.BlockSpec((m, shard), lambda kk, blk: (0, 0)),
        scratch_shapes=[pltpu.VMEM((m, shard), F32)])
    return pl.pallas_call(
        body, name=name, grid_spec=grid_spec, out_shape=jax.ShapeDtypeStruct((m, shard), dtype),
        compiler_params=_params(("arbitrary",)),
    )(block, a, b, after)


def _sum_small(owns, gots):
    n = len(owns)

    def body(*refs):
        for a in range(n):
            o_ref, g_ref, out_ref = refs[a], refs[n + a], refs[2 * n + a]
            out_ref[...] = (o_ref[...] + g_ref[1]) + (g_ref[0] + g_ref[2])

    return pl.pallas_call(
        body, name="sum_small", out_shape=[jax.ShapeDtypeStruct(o.shape, F32) for o in owns],
        compiler_params=pltpu.CompilerParams(vmem_limit_bytes=VMEM_LIMIT),
    )(*owns, *gots)


def _sum_chips(own, gots, tr, name):
    r, ccols = own.shape

    def body(o_ref, gx_ref, gy_ref, gxy_ref, out_ref):
        out_ref[...] = (o_ref[...] + gy_ref[...].astype(F32)) + (gx_ref[...].astype(F32) + gxy_ref[...].astype(F32))

    return pl.pallas_call(
        body, name=name, grid=(r // tr,),
        in_specs=[pl.BlockSpec((tr, ccols), lambda i: (i, 0))]
        + [pl.BlockSpec((None, tr, ccols), lambda i, slot=slot: (slot, i, 0)) for _, slot in gots],
        out_specs=pl.BlockSpec((tr, ccols), lambda i: (i, 0)),
        out_shape=jax.ShapeDtypeStruct((r, ccols), F32),
        compiler_params=_params(("arbitrary",)),
    )(own, *[g for g, _ in gots])


def _exchange_cores(parts, name):
    npart = len(parts)

    def body(*refs):
        in_refs = refs[0:npart]
        out_refs = refs[npart:2 * npart]
        send_sems, recv_sems = refs[2 * npart:]
        sibling = (lax.axis_index("x"), lax.axis_index("y"), 1 - lax.axis_index("c"))
        copies = [pltpu.make_async_remote_copy(
            src_ref=in_refs[a], dst_ref=out_refs[a], send_sem=send_sems.at[a], recv_sem=recv_sems.at[a],
            device_id=sibling, device_id_type=MESH_ID) for a in range(npart)]
        for cp in copies:
            cp.start()
        for cp in copies:
            cp.wait_recv()
        for cp in copies:
            cp.wait_send()

    return pl.pallas_call(
        body, name=name, out_shape=[jax.ShapeDtypeStruct(p.shape, p.dtype) for p in parts],
        in_specs=[ANY] * npart, out_specs=[ANY] * npart,
        scratch_shapes=[pltpu.SemaphoreType.DMA((npart,)), pltpu.SemaphoreType.DMA((npart,))],
    )(*parts)


def _adam(g, w, m, v):
    mn = ADAM_B1 * m + (1.0 - ADAM_B1) * g
    vn = ADAM_B2 * v + (1.0 - ADAM_B2) * (g * g)
    return g, -ADAM_LR * ((mn / BC1) / (jnp.sqrt(vn / BC2) + ADAM_EPS) + ADAM_WD * w), mn, vn


def _adamw(a, b, w, m, v, tr, name):
    r, ccols = w.shape

    def body(a_ref, b_ref, w_ref, m_ref, v_ref, *outs):
        res = _adam(a_ref[...] + b_ref[...], w_ref[...], m_ref[...], v_ref[...])
        for which in range(4):
            outs[which][...] = res[which]

    spec = pl.BlockSpec((tr, ccols), lambda i: (i, 0))
    shape = jax.ShapeDtypeStruct((r, ccols), F32)
    return pl.pallas_call(
        body, name=name, grid=(r // tr,), in_specs=[spec] * 5, out_specs=[spec] * 4, out_shape=[shape] * 4,
        compiler_params=_params(("arbitrary",)),
    )(a, b, w, m, v)


SMALL = ["b_in", "conv_b", "conv_ln_g", "conv_ln_b", "gmlp_ln_g", "gmlp_ln_b", "gmlp_ws", "gmlp_bs", "b_kv", "b_out",
         "ln_g", "ln_b"]


def _adamw_small(a_parts, b_parts, params):
    nparts, nparams = len(a_parts), len(params)

    def body(*refs):
        a = refs[0:nparts]
        b = refs[nparts:2 * nparts]
        prm = refs[2 * nparts:2 * nparts + 3 * nparams]
        outs = refs[2 * nparts + 3 * nparams:]
        gb_in, g768, gws, gbs_t, gb_kv, vec3 = [a[q][...] + b[q][...] for q in range(nparts)]
        grads = [gb_in, g768[ROW_CB:ROW_CB + 1], g768[ROW_CLG:ROW_CLG + 1], g768[ROW_CLB:ROW_CLB + 1],
                 g768[ROW_GLG:ROW_GLG + 1], g768[ROW_GLB:ROW_GLB + 1], gws, jnp.transpose(gbs_t)[0:N_GHEADS, :],
                 gb_kv, vec3[0:1], vec3[1:2], vec3[2:3]]
        for q, g in enumerate(grads):
            res = _adam(g, prm[3 * q][...], prm[3 * q + 1][...], prm[3 * q + 2][...])
            for which in range(4):
                outs[4 * q + which][...] = res[which]

    flat = [t for p in params for t in p]
    out_shape = [jax.ShapeDtypeStruct(p[0].shape, F32) for p in params for _ in range(4)]
    return pl.pallas_call(
        body, name="adamw_small", out_shape=out_shape,
        compiler_params=pltpu.CompilerParams(vmem_limit_bytes=VMEM_LIMIT),
    )(*a_parts, *b_parts, *flat)


def kernel(x, mem, w_in, b_in, conv_w, conv_b, conv_ln_g, conv_ln_b, gmlp_ln_g, gmlp_ln_b, gmlp_ws, gmlp_bs, w_kv, b_kv, w_out, b_out, ln_g, ln_b, loss_target, m_w_in, m_b_in, m_conv_w, m_conv_b, m_conv_ln_g, m_conv_ln_b, m_gmlp_ln_g, m_gmlp_ln_b, m_gmlp_ws, m_gmlp_bs, m_w_kv, m_b_kv, m_w_out, m_b_out, m_ln_g, m_ln_b, v_w_in, v_b_in, v_conv_w, v_conv_b, v_conv_ln_g, v_conv_ln_b, v_gmlp_ln_g, v_gmlp_ln_b, v_gmlp_ws, v_gmlp_bs, v_w_kv, v_b_kv, v_w_out, v_b_out, v_ln_g, v_ln_b):
    weights = dict(b_in=b_in, conv_b=conv_b, conv_ln_g=conv_ln_g, conv_ln_b=conv_ln_b, gmlp_ln_g=gmlp_ln_g,
                   gmlp_ln_b=gmlp_ln_b, gmlp_ws=gmlp_ws, gmlp_bs=gmlp_bs, b_kv=b_kv, b_out=b_out, ln_g=ln_g, ln_b=ln_b)
    mom_m = dict(b_in=m_b_in, conv_b=m_conv_b, conv_ln_g=m_conv_ln_g, conv_ln_b=m_conv_ln_b, gmlp_ln_g=m_gmlp_ln_g,
                 gmlp_ln_b=m_gmlp_ln_b, gmlp_ws=m_gmlp_ws, gmlp_bs=m_gmlp_bs, b_kv=m_b_kv, b_out=m_b_out,
                 ln_g=m_ln_g, ln_b=m_ln_b)
    mom_v = dict(b_in=v_b_in, conv_b=v_conv_b, conv_ln_g=v_conv_ln_g, conv_ln_b=v_conv_ln_b, gmlp_ln_g=v_gmlp_ln_g,
                 gmlp_ln_b=v_gmlp_ln_b, gmlp_ws=v_gmlp_ws, gmlp_bs=v_gmlp_bs, b_kv=v_b_kv, b_out=v_b_out,
                 ln_g=v_ln_g, ln_b=v_ln_b)
    nb, seq, _ = x.shape
    n = nb * seq
    tm = 256
    tk = min(2048, n)
    x2 = x.reshape(n, D_MODEL)
    tgt2 = loss_target.reshape(n, D_MODEL)
    mem2 = mem.reshape(nb * MEM_LEN, D_MODEL)
    chip = 2 * lax.axis_index("x") + lax.axis_index("y")
    bs_t = jnp.transpose(gmlp_bs[0])

    own_kv = [w_kv[0].astype(BF16), conv_w[0]]
    own_out = [w_out[0].astype(BF16)]
    ga = _start_exchange("gather_kv_start", own_kv, [False] * 2)
    win_g = _gather_w_in(w_in[0].astype(BF16), ga["token"])
    wkv_g, cw_g = _place_shards(own_kv, _wait_exchange("gather_kv_wait", ga, win_g), "place_kv")
    wkv_full = wkv_g.reshape(D_MODEL, D_MODEL)
    cw_full = jnp.transpose(cw_g, (1, 0, 2)).reshape(CONV_WIDTH, D_CONV)
    kv = _kv_proj(mem2, wkv_full, b_kv)
    gb = _start_exchange("gather_out_start", own_out, [False])
    h, ycat, z, x_bf = _forward(x2, win_g, b_in, kv, cw_full, conv_b, conv_ln_g, conv_ln_b, gmlp_ln_g, gmlp_ln_b,
                                gmlp_ws[0], bs_t, nb, seq, tm, gb["token"])
    (wout_g,) = _place_shards(own_out, _wait_exchange("gather_out_wait", gb, ycat), "place_out")
    wout_full = wout_g.reshape(D_MIX, D_MODEL)
    dr, dr_bf, dycat, vec3 = _out_proj_loss(ycat, wout_full, b_out, ln_g, ln_b, x2, tgt2, min(512, n))

    own_wout, gwout_b = _grad_w(ycat, dr_bf, min(2048, n), "grad_w_out")
    ex1 = _start_exchange("exchange1_start", [gwout_b, vec3], [True, False])
    dh, gb_in, g768, gws, gbs_t, dkv, gb_kv, grad_x2 = _branch_bwd(
        h, z, dycat, kv, cw_full, conv_ln_g, conv_ln_b, gmlp_ln_g, gmlp_ln_b, gmlp_ws[0], bs_t, win_g, dr,
        nb, seq, tm, ex1["token"])
    own_wkv, gwkv_b = _grad_w(mem2, dkv, nb * MEM_LEN, "grad_w_kv")
    small2 = [gb_in, g768, gws, gbs_t, gb_kv]
    ex2 = _start_exchange("exchange2_start", [gwkv_b] + small2, [True] + [False] * 5)
    block_of = lambda flip_bits: (chip ^ flip_bits).astype(jnp.int32).reshape(1)
    after, ex3 = ex2["token"], {}
    for flip, bits in ((2, 3), (1, 1), (0, 2)):
        part = _grad_w_block(x_bf, dh, block_of(bits), tk, f"grad_w_in_{flip}", BF16, after)
        ex3[flip] = _start_exchange(f"exchange3{flip}_start", [part], [False], (flip,))
        after = ex3[flip]["token"]
    own_win = _grad_w_block(x_bf, dh, block_of(0), tk, "grad_w_in_own", F32, after)
    got_wout, got_vec3 = _wait_exchange("exchange1_wait", ex1, own_win)
    got2 = _wait_exchange("exchange2_wait", ex2, own_win)

    sum_wout = _sum_chips(own_wout, [(got_wout, 0), (got_wout, 1), (got_wout, 2)], 256, "sum_w_out")
    sum_wkv = _sum_chips(own_wkv, [(got2[0], 0), (got2[0], 1), (got2[0], 2)], 256, "sum_w_kv")
    sum_small = list(_sum_small(small2 + [vec3], got2[1:] + [got_vec3]))
    sib = list(_exchange_cores([sum_wout, sum_wkv] + sum_small, "exchange_cores_rest"))
    loss = sum_small[5][ROW_LOSS, 0] + sib[7][ROW_LOSS, 0]
    big = {}
    big["w_out"] = _adamw(sum_wout, sib[0], w_out[0], m_w_out[0], v_w_out[0], 256, "adamw_w_out")
    big["w_kv"] = _adamw(sum_wkv, sib[1], w_kv[0], m_w_kv[0], v_w_kv[0], 256, "adamw_w_kv")
    lead = {"gmlp_ws", "gmlp_bs"}
    strip = lambda k, t: t[0] if k in lead else t
    sm = _adamw_small(sum_small, sib[2:], [tuple(strip(k, t[k]) for t in (weights, mom_m, mom_v)) for k in SMALL])
    small_out = {k: [sm[4 * q + which][None] if k in lead else sm[4 * q + which] for which in range(4)]
                 for q, k in enumerate(SMALL)}
    cw_a = lax.dynamic_slice_in_dim(sum_small[1][0:CONV_WIDTH + 1], chip * CONV_SHARD, CONV_SHARD, axis=1)
    cw_b = lax.dynamic_slice_in_dim(sib[3][0:CONV_WIDTH + 1], chip * CONV_SHARD, CONV_SHARD, axis=1)
    cwp = ((0, 1), (0, 0))
    cw_out = _adamw(cw_a, cw_b, jnp.pad(conv_w[0], cwp), jnp.pad(m_conv_w[0], cwp), jnp.pad(v_conv_w[0], cwp),
                    CONV_WIDTH + 1, "adamw_conv_w")

    done = cw_out[0]
    got_win = {flip: _wait_exchange(f"exchange3{flip}_wait", ex3[flip], done)[0] for flip in (2, 1, 0)}
    sum_win = _sum_chips(own_win, [(got_win[0], 0), (got_win[1], 0), (got_win[2], 0)], 256, "sum_w_in")
    (sib_win,) = _exchange_cores([sum_win], "exchange_cores_w_in")
    big["w_in"] = _adamw(sum_win, sib_win, w_in[0], m_w_in[0], v_w_in[0], 256, "adamw_w_in")

    order = ["w_in", "b_in", "conv_w", "conv_b", "conv_ln_g", "conv_ln_b", "gmlp_ln_g", "gmlp_ln_b", "gmlp_ws",
             "gmlp_bs", "w_kv", "b_kv", "w_out", "b_out", "ln_g", "ln_b"]
    result = [loss, grad_x2.reshape(nb, seq, D_MODEL)]
    for which in range(4):
        for k in order:
            if k in big:
                result.append(big[k][which][None])
            elif k == "conv_w":
                result.append(cw_out[which][0:CONV_WIDTH][None])
            else:
                result.append(small_out[k][which])
    return tuple(result)
```

```python
import functools
import math

import jax
import jax.numpy as jnp
from jax import lax
from jax.experimental import pallas as pl
from jax.experimental.pallas import tpu as pltpu

F32 = jnp.float32
BF16 = jnp.bfloat16

D_MODEL = 1024
MEM_LEN = 256
D_MIX = 2048
D_CONV = 768
D_GMLP = 768
D_XATT = 512
N_XHEADS = 4
XHEAD = 128
CONV_WIDTH = 31
CHUNK = 128
N_GHEADS = 6
D_IN = 3 * D_CONV + 3 * D_GMLP + 2 * D_XATT
ALPHA = 2.0 ** 0.25
LN_EPS = 1e-5
N_CHIPS = 4
W_IN_SHARD = D_IN // N_CHIPS
W_OUT_SHARD = D_MIX // N_CHIPS
W_KV_SHARD = D_MODEL // N_CHIPS
CONV_SHARD = D_CONV // N_CHIPS
HALO = 32

C_A, C_GLU, C_GATE = 0, 768, 1536
G_U, G_V, G_GATE = 2304, 3072, 3840
X_Q, X_GATE = 4608, 5120

ADAM_LR = 0.001
ADAM_B1 = 0.9
ADAM_B2 = 0.999
ADAM_EPS = 1e-08
ADAM_WD = 0.01
ADAM_STEP = 10
BC1 = 1.0 - ADAM_B1 ** ADAM_STEP
BC2 = 1.0 - ADAM_B2 ** ADAM_STEP

VMEM_LIMIT = 56 * 1024 * 1024
MESH_ID = pl.DeviceIdType.MESH
ANY = pl.BlockSpec(memory_space=pl.ANY)

GELU_C0 = math.sqrt(2.0 / math.pi)
GELU_C1 = 0.044715


def _sigmoid(v):
    return 0.5 + 0.5 * jnp.tanh(0.5 * v)


def _f32(ref, rows, col, width):
    return ref[rows, col:col + width].astype(F32)


def _dsilu(v, s):
    return s * (1.0 + v * (1.0 - s))


def _gelu_and_grad(v):
    t = jnp.tanh(GELU_C0 * (v + GELU_C1 * v * v * v))
    g = 0.5 * v * (1.0 + t)
    dg = 0.5 * (1.0 + t) + 0.5 * v * (1.0 - t * t) * (GELU_C0 * (1.0 + 3.0 * GELU_C1 * v * v))
    return g, dg


def _gelu(v):
    return 0.5 * v * (1.0 + jnp.tanh(GELU_C0 * (v + GELU_C1 * v * v * v)))


def _ln_stats(v):
    mu = jnp.mean(v, axis=-1, keepdims=True)
    vc = v - mu
    var = jnp.mean(vc * vc, axis=-1, keepdims=True)
    rstd = lax.rsqrt(var + LN_EPS)
    return vc * rstd, rstd


def _ln_bwd(dvhat, vhat, rstd):
    m1 = jnp.mean(dvhat, axis=-1, keepdims=True)
    m2 = jnp.mean(dvhat * vhat, axis=-1, keepdims=True)
    return rstd * (dvhat - m1 - vhat * m2)


def _colsum(v):
    return jnp.sum(v, axis=0, keepdims=True)


def _dot(a, b):
    return jnp.dot(a, b, preferred_element_type=F32)


def _dot_nt(a, b):
    return lax.dot_general(a, b, (((1,), (1,)), ((), ())), preferred_element_type=F32)


def _dot_tn(a, b):
    return lax.dot_general(a, b, (((0,), (0,)), ((), ())), preferred_element_type=F32)


def _causal_ws(ws_ref):
    row = lax.broadcasted_iota(jnp.int32, (CHUNK, CHUNK), 0)
    col = lax.broadcasted_iota(jnp.int32, (CHUNK, CHUNK), 1)
    keep = col <= row
    return [jnp.where(keep, ws_ref[hd], 0.0).astype(BF16) for hd in range(N_GHEADS)], keep


def _params(sem):
    return pltpu.CompilerParams(dimension_semantics=sem, vmem_limit_bytes=VMEM_LIMIT)


def _peer_chips():
    x, y, c = lax.axis_index("x"), lax.axis_index("y"), lax.axis_index("c")
    return [(1 - x, y), (x, 1 - y), (1 - x, 1 - y)], 2 * x + y, c


HBM = pl.BlockSpec(memory_space=pltpu.HBM)
SEM = pl.BlockSpec(memory_space=pltpu.SEMAPHORE)
EFFECT = pltpu.SideEffectType.DATAFLOW_SIDE_EFFECTING
ALL_FLIPS = (0, 1, 2)


def _exchange_copies(src_refs, land_refs, per_chip, flips, send_sems, recv_sems):
    chips, _, c = _peer_chips()
    n = len(src_refs)
    copies = []
    for q, p in enumerate(flips):
        px, py = chips[p]
        for a in range(n):
            src = src_refs[a].at[2 * px + py] if per_chip[a] else src_refs[a]
            copies.append(pltpu.make_async_remote_copy(
                src_ref=src, dst_ref=land_refs[a].at[q], send_sem=send_sems.at[n * q + a],
                recv_sem=recv_sems.at[n * q + a], device_id=(px, py, c), device_id_type=MESH_ID))
    return copies


def _start_exchange(name, srcs, per_chip, flips=ALL_FLIPS):
    n = len(srcs)
    nf = len(flips)
    lands = [lax.empty((nf,) + (s.shape[1:] if pc else s.shape), s.dtype) for s, pc in zip(srcs, per_chip)]

    def body(*refs):
        src_refs, land_refs = refs[0:n], refs[n:2 * n]
        send_sems, recv_sems = refs[2 * n], refs[2 * n + 1]
        token = refs[4 * n + 2]
        for cp in _exchange_copies(src_refs, land_refs, per_chip, flips, send_sems, recv_sems):
            cp.start()
        token[...] = jnp.zeros_like(token)

    out = pl.pallas_call(
        body, name=name,
        out_shape=(pltpu.SemaphoreType.DMA((nf * n,)), pltpu.SemaphoreType.DMA((nf * n,)),
                   *[pltpu.HBM(a.shape, a.dtype) for a in srcs + lands], jax.ShapeDtypeStruct((8, 128), F32)),
        in_specs=[HBM] * (2 * n),
        out_specs=(SEM, SEM, *[HBM] * (2 * n), pl.BlockSpec(memory_space=pltpu.VMEM)),
        input_output_aliases={a: 2 + a for a in range(2 * n)},
        compiler_params=pltpu.CompilerParams(has_side_effects=EFFECT),
    )(*[pltpu.with_memory_space_constraint(a, pltpu.HBM) for a in srcs + lands])
    return dict(send=out[0], recv=out[1], thru=list(out[2:2 * n + 2]), token=out[2 * n + 2], per_chip=per_chip,
                flips=flips)


def _wait_exchange(name, started, after):
    thru, per_chip, flips = started["thru"], started["per_chip"], started["flips"]
    n = len(thru) // 2

    def body(*refs):
        src_refs, land_refs = refs[0:n], refs[n:2 * n]
        send_sems, recv_sems = refs[2 * n], refs[2 * n + 1]
        for cp in _exchange_copies(src_refs, land_refs, per_chip, flips, send_sems, recv_sems):
            cp.wait_send()
            cp.wait_recv()

    out = pl.pallas_call(
        body, name=name, out_shape=tuple(pltpu.HBM(a.shape, a.dtype) for a in thru),
        in_specs=[HBM] * (2 * n) + [SEM, SEM, ANY], out_specs=tuple([HBM] * (2 * n)),
        input_output_aliases={a: a for a in range(2 * n)},
        compiler_params=pltpu.CompilerParams(has_side_effects=EFFECT),
    )(*thru, started["send"], started["recv"], after)
    return list(out[n:2 * n])


def _place_shards(owns, landeds, name):
    n = len(owns)
    mine = (2 * lax.axis_index("x") + lax.axis_index("y")).astype(jnp.int32).reshape(1)

    def body(mine_ref, *refs):
        own_refs, land_refs, out_refs = refs[0:n], refs[n:2 * n], refs[2 * n:3 * n]
        k = pl.program_id(0)
        for a in range(n):
            @pl.when(k == mine_ref[0])
            def _():
                out_refs[a][...] = own_refs[a][...]

            @pl.when(k != mine_ref[0])
            def _():
                out_refs[a][...] = land_refs[a][...]

    def slot(k, mine_ref):
        d = k ^ mine_ref[0]
        return jnp.where(d == 1, 1, jnp.where(d == 3, 2, 0))

    zeros = lambda o: (0,) * len(o.shape)
    grid_spec = pltpu.PrefetchScalarGridSpec(
        num_scalar_prefetch=1, grid=(N_CHIPS,),
        in_specs=[pl.BlockSpec(o.shape, lambda k, m, o=o: zeros(o)) for o in owns]
        + [pl.BlockSpec((None,) + o.shape, lambda k, m, o=o: (slot(k, m),) + zeros(o)) for o in owns],
        out_specs=[pl.BlockSpec((None,) + o.shape, lambda k, m, o=o: (k,) + zeros(o)) for o in owns])
    return pl.pallas_call(
        body, name=name, grid_spec=grid_spec,
        out_shape=[jax.ShapeDtypeStruct((N_CHIPS,) + o.shape, o.dtype) for o in owns],
        compiler_params=_params(("arbitrary",)),
    )(mine, *owns, *landeds)


def _gather_w_in(w_shard, after):
    half = D_MODEL // 2

    def body(w_ref, after_ref, out_ref, ici_send, ici_recv, d2d_send, d2d_recv, loc_sem):
        chips, mine, c = _peer_chips()
        sibling = (lax.axis_index("x"), lax.axis_index("y"), 1 - c)
        my_rows = pl.ds(pl.multiple_of(c * half, half), half)
        their_rows = pl.ds(pl.multiple_of((1 - c) * half, half), half)
        own = pltpu.make_async_copy(w_ref, out_ref.at[mine], loc_sem)
        own.start()

        def over_ici(p, slot):
            px, py = chips[p]
            return pltpu.make_async_remote_copy(
                src_ref=w_ref.at[my_rows, :], dst_ref=out_ref.at[slot, my_rows, :], send_sem=ici_send.at[p],
                recv_sem=ici_recv.at[p], device_id=(px, py, c), device_id_type=MESH_ID)

        def over_d2d(p, rows):
            px, py = chips[p]
            part = out_ref.at[2 * px + py, rows, :]
            return pltpu.make_async_remote_copy(
                src_ref=part, dst_ref=part, send_sem=d2d_send.at[p], recv_sem=d2d_recv.at[p],
                device_id=sibling, device_id_type=MESH_ID)

        for p in (1, 0, 2):
            over_ici(p, mine).start()
        for p in (1, 0, 2):
            px, py = chips[p]
            over_ici(p, 2 * px + py).wait_recv()
            over_d2d(p, my_rows).start()
        for p in (1, 0, 2):
            over_d2d(p, their_rows).wait_recv()
        for p in range(3):
            over_ici(p, mine).wait_send()
            over_d2d(p, my_rows).wait_send()
        own.wait()

    return pl.pallas_call(
        body, name="gather_w_in", out_shape=jax.ShapeDtypeStruct((N_CHIPS,) + w_shard.shape, w_shard.dtype),
        in_specs=[ANY, ANY], out_specs=ANY,
        scratch_shapes=[pltpu.SemaphoreType.DMA((3,)), pltpu.SemaphoreType.DMA((3,)), pltpu.SemaphoreType.DMA((3,)),
                        pltpu.SemaphoreType.DMA((3,)), pltpu.SemaphoreType.DMA],
    )(w_shard, after)


def _kv_proj(mem2, wkv_full, b_kv):
    m = mem2.shape[0]

    def body(m_ref, w_ref, b_ref, o_ref):
        o_ref[...] = (_dot(m_ref[...].astype(BF16), w_ref[...]) + b_ref[...]).astype(BF16)

    return pl.pallas_call(
        body, name="kv_proj", grid=(m // MEM_LEN,),
        in_specs=[pl.BlockSpec((MEM_LEN, D_MODEL), lambda i: (i, 0)),
                  pl.BlockSpec((D_MODEL, D_MODEL), lambda i: (0, 0)),
                  pl.BlockSpec((1, D_MODEL), lambda i: (0, 0))],
        out_specs=pl.BlockSpec((MEM_LEN, D_MODEL), lambda i: (i, 0)),
        out_shape=jax.ShapeDtypeStruct((m, D_MODEL), BF16),
        compiler_params=_params(("arbitrary",)),
    )(mem2, wkv_full, b_kv)


CONV_ROWS = 16
SUBLANES = 8


def _shifted_planes(buf, tm):
    rows = tm + HALO - SUBLANES
    for s in range(1, SUBLANES):
        buf[s, 0:rows, :] = buf[0, s:s + rows, :]


def _window(buf, start, rows):
    s = start % SUBLANES
    return buf[s, start - s:start - s + rows, :]


def _forward(x2, w_full, b_in, kv, cw, cb, cg, cbeta, gg_, gb_, ws, bs_t, nb, seq, tm, after):
    nt = seq // tm
    n = nb * seq
    tiles = nb * nt

    def in_proj_piece(x_ref, w_ref, b_ref, hout_ref, xb_ref, hkeep, keep_now):
        xb = x_ref[...].astype(BF16)
        xb_ref[...] = xb

        def h_piece(j):
            cols = slice(j * W_IN_SHARD, (j + 1) * W_IN_SHARD)
            hj = (_dot(xb, w_ref[j]) + b_ref[:, cols]).astype(BF16)
            hout_ref[:, cols] = hj
            hkeep[keep_now, :, cols] = hj

        return h_piece

    def first_step(x_ref, w_ref, b_ref, kv_ref, cw_ref, cb_ref, clg_ref, clb_ref, glg_ref, glb_ref, ws_ref, bst_ref,
                   after_ref, hout_ref, y_ref, z_ref, xb_ref, hcbuf, hkeep):
        h_piece = in_proj_piece(x_ref, w_ref, b_ref, hout_ref, xb_ref, hkeep, 0)
        for j in range(N_CHIPS):
            h_piece(j)

    def later_step(x_ref, w_ref, b_ref, kv_ref, cw_ref, cb_ref, clg_ref, clb_ref, glg_ref, glb_ref, ws_ref, bst_ref,
                   after_ref, hout_ref, y_ref, z_ref, xb_ref, hcbuf, hkeep):
        step = pl.program_id(0)
        i = (step - 1) % nt
        keep_now = step % 2
        h_ref = hkeep.at[1 - keep_now]
        h_piece = in_proj_piece(x_ref, w_ref, b_ref, hout_ref, xb_ref, hkeep, keep_now)

        starts = i == 0

        @pl.when(starts)
        def _():
            hcbuf[0, 0:HALO, :] = jnp.zeros((HALO, D_CONV), F32)

        @pl.when(jnp.logical_not(starts))
        def _():
            hcbuf[0, 0:HALO, :] = hcbuf[0, tm:tm + HALO, :]

        every = slice(None)
        hcbuf[0, HALO:HALO + tm, :] = _f32(h_ref, every, C_A, D_CONV) * _sigmoid(_f32(h_ref, every, C_GLU, D_CONV))
        h_piece(0)
        _shifted_planes(hcbuf, tm)
        for r in range(tm // CONV_ROWS):
            base = r * CONV_ROWS
            acc = jnp.broadcast_to(cb_ref[...], (CONV_ROWS, D_CONV))
            for k in range(CONV_WIDTH):
                acc = acc + cw_ref[k:k + 1, :] * _window(hcbuf, base + 2 + k, CONV_ROWS)
            z_ref[base:base + CONV_ROWS, :] = acc
            if r % 8 == 3:
                h_piece(1 + r // 8)
        zhat, _ = _ln_stats(z_ref[...])
        zn = zhat * clg_ref[...] + clb_ref[...]
        cgate = _f32(h_ref, every, C_GATE, D_CONV)
        y_ref[:, 0:D_CONV] = (zn * _sigmoid(zn) * (cgate * _sigmoid(cgate))).astype(BF16)

        wsc, _ = _causal_ws(ws_ref)
        g_v = _gelu(_f32(h_ref, every, G_V, D_GMLP))
        h_piece(3)
        vhat, _ = _ln_stats(g_v)
        vn = (vhat * glg_ref[...] + glb_ref[...]).astype(BF16)
        for ch in range(tm // CHUNK):
            rows = slice(ch * CHUNK, (ch + 1) * CHUNK)
            for hd in range(N_GHEADS):
                cols = slice(hd * CHUNK, (hd + 1) * CHUNK)
                s = _dot(wsc[hd], vn[rows, cols]) + bst_ref[:, hd:hd + 1]
                u = _gelu(_f32(h_ref, rows, G_U + hd * CHUNK, CHUNK))
                gate = _f32(h_ref, rows, G_GATE + hd * CHUNK, CHUNK)
                y_ref[rows, D_CONV + hd * CHUNK:D_CONV + (hd + 1) * CHUNK] = (
                    u * s * (gate * _sigmoid(gate))).astype(BF16)

        scale = XHEAD ** -0.5
        for hd in range(N_XHEADS):
            q = h_ref[:, X_Q + hd * XHEAD:X_Q + (hd + 1) * XHEAD]
            k = kv_ref[:, hd * XHEAD:(hd + 1) * XHEAD]
            v = kv_ref[:, D_XATT + hd * XHEAD:D_XATT + (hd + 1) * XHEAD]
            s = _dot_nt(q, k) * scale
            e = jnp.exp(s - jnp.max(s, axis=-1, keepdims=True))
            p = e * (1.0 / jnp.sum(e, axis=-1, keepdims=True))
            o = _dot(p.astype(BF16), v)
            gate = _f32(h_ref, every, X_GATE + hd * XHEAD, XHEAD)
            y_ref[:, 2 * D_CONV + hd * XHEAD:2 * D_CONV + (hd + 1) * XHEAD] = (
                o * (gate * _sigmoid(gate))).astype(BF16)

    def body(*refs):
        step = pl.program_id(0)

        @pl.when(step == 0)
        def _():
            first_step(*refs)

        @pl.when(step > 0)
        def _():
            later_step(*refs)

    ahead = lambda s: (jnp.minimum(s, tiles - 1), 0)
    behind = lambda s: (jnp.maximum(s - 1, 0), 0)
    example = lambda s: (jnp.maximum(s - 1, 0) // nt, 0)
    const2 = lambda s: (0, 0)
    const3 = lambda s: (0, 0, 0)
    vec = pl.BlockSpec((1, D_CONV), const2)
    return pl.pallas_call(
        body, name="forward", grid=(tiles + 1,),
        in_specs=[pl.BlockSpec((tm, D_MODEL), ahead),
                  pl.BlockSpec((N_CHIPS, D_MODEL, W_IN_SHARD), const3, pipeline_mode=pl.Buffered(1)),
                  pl.BlockSpec((1, D_IN), const2),
                  pl.BlockSpec((MEM_LEN, D_MODEL), example),
                  pl.BlockSpec((CONV_WIDTH, D_CONV), const2), vec, vec, vec, vec, vec,
                  pl.BlockSpec((N_GHEADS, CHUNK, CHUNK), const3),
                  pl.BlockSpec((CHUNK, N_GHEADS), const2), ANY],
        out_specs=[pl.BlockSpec((tm, D_IN), ahead), pl.BlockSpec((tm, D_MIX), behind),
                   pl.BlockSpec((tm, D_CONV), behind), pl.BlockSpec((tm, D_MODEL), ahead)],
        out_shape=[jax.ShapeDtypeStruct((n, D_IN), BF16), jax.ShapeDtypeStruct((n, D_MIX), BF16),
                   jax.ShapeDtypeStruct((n, D_CONV), F32), jax.ShapeDtypeStruct((n, D_MODEL), BF16)],
        scratch_shapes=[pltpu.VMEM((SUBLANES, HALO + tm, D_CONV), F32), pltpu.VMEM((2, tm, D_IN), BF16)],
        compiler_params=_params(("arbitrary",)),
    )(x2, w_full, b_in, kv, cw, cb, cg, cbeta, gg_, gb_, ws, bs_t, after)


ROW_LOSS = 3


def _out_proj_loss(ycat, wout_full, b_out, ln_g, ln_b, x2, tgt2, tm):
    n = x2.shape[0]

    def body(y_ref, w_ref, bo_ref, g_ref, b_ref, x_ref, t_ref, dr_ref, drb_ref, dy_ref, vec_ref):
        i = pl.program_id(0)

        @pl.when(i == 0)
        def _():
            vec_ref[...] = jnp.zeros_like(vec_ref)

        r = ALPHA * x_ref[...] + _dot(y_ref[...], w_ref[...]) + bo_ref[...]
        rhat, rstd = _ln_stats(r)
        diff = rhat * g_ref[...] + b_ref[...] - t_ref[...]
        loss = 0.5 * jnp.sum(jnp.mean(diff * diff, axis=-1, keepdims=True), axis=0, keepdims=True)
        dout = diff * (1.0 / D_MODEL)
        dr = _ln_bwd(dout * g_ref[...], rhat, rstd)
        vec_ref[0:1, :] += _colsum(dr)
        vec_ref[1:2, :] += _colsum(dout * rhat)
        vec_ref[2:3, :] += _colsum(dout)
        vec_ref[ROW_LOSS:ROW_LOSS + 1, :] += jnp.broadcast_to(loss, (1, D_MODEL))
        dr_ref[...] = dr
        drb = dr.astype(BF16)
        drb_ref[...] = drb
        dy_ref[...] = _dot_nt(drb, w_ref[...]).astype(BF16)

    row = lambda i: (i, 0)
    const = lambda i: (0, 0)
    vec = pl.BlockSpec((1, D_MODEL), const)
    return pl.pallas_call(
        body, name="out_proj_loss", grid=(n // tm,),
        in_specs=[pl.BlockSpec((tm, D_MIX), row), pl.BlockSpec((D_MIX, D_MODEL), const), vec, vec, vec,
                  pl.BlockSpec((tm, D_MODEL), row), pl.BlockSpec((tm, D_MODEL), row)],
        out_specs=[pl.BlockSpec((tm, D_MODEL), row), pl.BlockSpec((tm, D_MODEL), row), pl.BlockSpec((tm, D_MIX), row),
                   pl.BlockSpec((8, D_MODEL), const)],
        out_shape=[jax.ShapeDtypeStruct((n, D_MODEL), F32), jax.ShapeDtypeStruct((n, D_MODEL), BF16),
                   jax.ShapeDtypeStruct((n, D_MIX), BF16), jax.ShapeDtypeStruct((8, D_MODEL), F32)],
        compiler_params=_params(("arbitrary",)),
    )(ycat, wout_full, b_out, ln_g, ln_b, x2, tgt2)


ROW_CB, ROW_CLG, ROW_CLB, ROW_GLG, ROW_GLB = 32, 33, 34, 35, 36


def _branch_bwd(h, z, dy, kv, cw, clg, clb, glg, glb, ws, bs_t, w_full, dr, nb, seq, tm, after):
    nt = seq // tm
    n = nb * seq
    tiles = nb * nt

    def place(s):
        sc = jnp.minimum(s, tiles - 1)
        return sc // nt, nt - 1 - sc % nt

    def dx_of_previous(w_ref, dr_ref, gx_ref, dhkeep, prev):
        gx_ref[...] = ALPHA * dr_ref[...]

        def dx_piece(j):
            gx_ref[...] += _dot_nt(dhkeep[prev, :, j * W_IN_SHARD:(j + 1) * W_IN_SHARD], w_ref[j])

        return dx_piece

    def last_step(h_ref, z_ref, dy_ref, kv_ref, cw_ref, clg_ref, clb_ref, glg_ref, glb_ref,
                  ws_ref, bst_ref, w_ref, dr_ref, after_ref,
                  dh_ref, gbin_ref, g768_ref, gws_ref, gbst_ref, dkv_ref, gbkv_ref, gx_ref,
                  dzbuf, dvnbuf, dhkeep):
        dx_piece = dx_of_previous(w_ref, dr_ref, gx_ref, dhkeep, 1 - tiles % 2)
        for j in range(N_CHIPS):
            dx_piece(j)
        _, keep = _causal_ws(ws_ref)
        for hd in range(N_GHEADS):
            gws_ref[hd] = jnp.where(keep, gws_ref[hd], 0.0)

    def tile_step(h_ref, z_ref, dy_ref, kv_ref, cw_ref, clg_ref, clb_ref, glg_ref, glb_ref,
                  ws_ref, bst_ref, w_ref, dr_ref, after_ref,
                  dh_ref, gbin_ref, g768_ref, gws_ref, gbst_ref, dkv_ref, gbkv_ref, gx_ref,
                  dzbuf, dvnbuf, dhkeep):
        step = pl.program_id(0)
        i = step % nt
        every = slice(None)
        keep_now = step % 2

        @pl.when(step == 0)
        def _():
            gbin_ref[...] = jnp.zeros_like(gbin_ref)
            g768_ref[...] = jnp.zeros_like(g768_ref)
            gws_ref[...] = jnp.zeros_like(gws_ref)
            gbst_ref[...] = jnp.zeros_like(gbst_ref)
            gbkv_ref[...] = jnp.zeros_like(gbkv_ref)
            dhkeep[1] = jnp.zeros((tm, D_IN), BF16)

        @pl.when(i == 0)
        def _():
            dkv_ref[...] = jnp.zeros_like(dkv_ref)

        dx_piece = dx_of_previous(w_ref, dr_ref, gx_ref, dhkeep, 1 - keep_now)

        def put(rows, col, width, val):
            vb = val.astype(BF16)
            dh_ref[rows, col:col + width] = vb
            dhkeep[keep_now, rows, col:col + width] = vb

        def emit(col, width, val):
            gbin_ref[:, col:col + width] += _colsum(val)
            put(every, col, width, val)

        d_c = dy_ref[:, 0:D_CONV].astype(F32)
        cgate = _f32(h_ref, every, C_GATE, D_CONV)
        sg = _sigmoid(cgate)
        zhat, zrstd = _ln_stats(z_ref[...])
        zn = zhat * clg_ref[...] + clb_ref[...]
        szn = _sigmoid(zn)
        emit(C_GATE, D_CONV, d_c * (zn * szn) * _dsilu(cgate, sg))
        dx_piece(0)
        dzn = d_c * (cgate * sg) * _dsilu(zn, szn)
        g768_ref[ROW_CLG:ROW_CLG + 1, :] += _colsum(dzn * zhat)
        g768_ref[ROW_CLB:ROW_CLB + 1, :] += _colsum(dzn)
        dz = _ln_bwd(dzn * clg_ref[...], zhat, zrstd)
        g768_ref[ROW_CB:ROW_CB + 1, :] += _colsum(dz)

        @pl.when(i == 0)
        def _():
            dzbuf[0, tm:tm + HALO, :] = jnp.zeros((HALO, D_CONV), F32)

        @pl.when(i > 0)
        def _():
            dzbuf[0, tm:tm + HALO, :] = dzbuf[0, 0:HALO, :]

        dzbuf[0, 0:tm, :] = dz
        _shifted_planes(dzbuf, tm)
        a = _f32(h_ref, every, C_A, D_CONV)
        sgl = _sigmoid(_f32(h_ref, every, C_GLU, D_CONV))
        hc = a * sgl

        for r in range(tm // CONV_ROWS):
            base = r * CONV_ROWS
            acc = jnp.zeros((CONV_ROWS, D_CONV), F32)
            for k in range(CONV_WIDTH):
                acc = acc + cw_ref[k:k + 1, :] * _window(dzbuf, base + 30 - k, CONV_ROWS)
            dvnbuf[base:base + CONV_ROWS, :] = acc
            if r % 8 == 3:
                dx_piece(1 + r // 8)
        dhc = dvnbuf[...]
        emit(C_A, D_CONV, dhc * sgl)
        emit(C_GLU, D_CONV, dhc * a * sgl * (1.0 - sgl))
        for k in range(CONV_WIDTH):
            g768_ref[k:k + 1, :] += _colsum(hc * _window(dzbuf, 30 - k, tm))
            if k == CONV_WIDTH // 2:
                dx_piece(3)

        wsc, _ = _causal_ws(ws_ref)
        v, dgelu_v = _gelu_and_grad(_f32(h_ref, every, G_V, D_GMLP))
        vhat, vrstd = _ln_stats(v)
        vn = (vhat * glg_ref[...] + glb_ref[...]).astype(BF16)
        for ch in range(tm // CHUNK):
            rows = slice(ch * CHUNK, (ch + 1) * CHUNK)
            for hd in range(N_GHEADS):
                cols = slice(hd * CHUNK, (hd + 1) * CHUNK)
                vn_blk = vn[rows, cols]
                s = _dot(wsc[hd], vn_blk) + bst_ref[:, hd:hd + 1]
                u, dgelu_u = _gelu_and_grad(_f32(h_ref, rows, G_U + hd * CHUNK, CHUNK))
                gate = _f32(h_ref, rows, G_GATE + hd * CHUNK, CHUNK)
                sgate = _sigmoid(gate)
                d_g = dy_ref[rows, D_CONV + hd * CHUNK:D_CONV + (hd + 1) * CHUNK].astype(F32)
                dgate = d_g * (u * s) * _dsilu(gate, sgate)
                gbin_ref[:, G_GATE + hd * CHUNK:G_GATE + (hd + 1) * CHUNK] += _colsum(dgate)
                put(rows, G_GATE + hd * CHUNK, CHUNK, dgate)
                dyg = d_g * (gate * sgate)
                du = dyg * s * dgelu_u
                gbin_ref[:, G_U + hd * CHUNK:G_U + (hd + 1) * CHUNK] += _colsum(du)
                put(rows, G_U + hd * CHUNK, CHUNK, du)
                ds = dyg * u
                dsb = ds.astype(BF16)
                gws_ref[hd] += _dot_nt(dsb, vn_blk)
                gbst_ref[:, hd:hd + 1] += jnp.sum(ds, axis=1, keepdims=True)
                dvnbuf[rows, cols] = _dot_tn(wsc[hd], dsb)
        dvn = dvnbuf[...]
        g768_ref[ROW_GLG:ROW_GLG + 1, :] += _colsum(dvn * vhat)
        g768_ref[ROW_GLB:ROW_GLB + 1, :] += _colsum(dvn)
        emit(G_V, D_GMLP, _ln_bwd(dvn * glg_ref[...], vhat, vrstd) * dgelu_v)

        scale = XHEAD ** -0.5
        for hd in range(N_XHEADS):
            q = h_ref[:, X_Q + hd * XHEAD:X_Q + (hd + 1) * XHEAD]
            k = kv_ref[:, hd * XHEAD:(hd + 1) * XHEAD]
            vv = kv_ref[:, D_XATT + hd * XHEAD:D_XATT + (hd + 1) * XHEAD]
            s = _dot_nt(q, k) * scale
            e = jnp.exp(s - jnp.max(s, axis=-1, keepdims=True))
            p = e * (1.0 / jnp.sum(e, axis=-1, keepdims=True))
            pb = p.astype(BF16)
            o = _dot(pb, vv)
            gate = _f32(h_ref, every, X_GATE + hd * XHEAD, XHEAD)
            sgate = _sigmoid(gate)
            d_x = dy_ref[:, 2 * D_CONV + hd * XHEAD:2 * D_CONV + (hd + 1) * XHEAD].astype(F32)
            emit(X_GATE + hd * XHEAD, XHEAD, d_x * o * _dsilu(gate, sgate))
            do = (d_x * (gate * sgate)).astype(BF16)
            dp = _dot_nt(do, vv)
            dsc = (p * (dp - jnp.sum(dp * p, axis=-1, keepdims=True))).astype(BF16)
            emit(X_Q + hd * XHEAD, XHEAD, _dot(dsc, k) * scale)
            dkv_ref[:, hd * XHEAD:(hd + 1) * XHEAD] += _dot_tn(dsc, q) * scale
            dkv_ref[:, D_XATT + hd * XHEAD:D_XATT + (hd + 1) * XHEAD] += _dot_tn(pb, do)

        @pl.when(i == nt - 1)
        def _():
            gbkv_ref[...] += _colsum(dkv_ref[...])

    def body(*refs):
        step = pl.program_id(0)

        @pl.when(step < tiles)
        def _():
            tile_step(*refs)

        @pl.when(step == tiles)
        def _():
            last_step(*refs)

    def row(s):
        b, ri = place(s)
        return b * nt + ri, 0

    def row_prev(s):
        b, ri = place(jnp.maximum(s - 1, 0))
        return b * nt + ri, 0

    example = lambda s: (place(s)[0], 0)
    const2 = lambda s: (0, 0)
    const3 = lambda s: (0, 0, 0)
    vec = pl.BlockSpec((1, D_CONV), const2)

    return pl.pallas_call(
        body, name="branch_bwd", grid=(tiles + 1,),
        in_specs=[pl.BlockSpec((tm, D_IN), row),
                  pl.BlockSpec((tm, D_CONV), row), pl.BlockSpec((tm, D_MIX), row),
                  pl.BlockSpec((MEM_LEN, D_MODEL), example),
                  pl.BlockSpec((CONV_WIDTH, D_CONV), const2), vec, vec, vec, vec,
                  pl.BlockSpec((N_GHEADS, CHUNK, CHUNK), const3),
                  pl.BlockSpec((CHUNK, N_GHEADS), const2),
                  pl.BlockSpec((N_CHIPS, D_MODEL, W_IN_SHARD), const3, pipeline_mode=pl.Buffered(1)),
                  pl.BlockSpec((tm, D_MODEL), row_prev), ANY],
        out_specs=[pl.BlockSpec((tm, D_IN), row),
                   pl.BlockSpec((1, D_IN), const2),
                   pl.BlockSpec((40, D_CONV), const2),
                   pl.BlockSpec((N_GHEADS, CHUNK, CHUNK), const3),
                   pl.BlockSpec((CHUNK, CHUNK), const2),
                   pl.BlockSpec((MEM_LEN, D_MODEL), example),
                   pl.BlockSpec((1, D_MODEL), const2),
                   pl.BlockSpec((tm, D_MODEL), row_prev)],
        out_shape=[jax.ShapeDtypeStruct((n, D_IN), BF16),
                   jax.ShapeDtypeStruct((1, D_IN), F32),
                   jax.ShapeDtypeStruct((40, D_CONV), F32),
                   jax.ShapeDtypeStruct((N_GHEADS, CHUNK, CHUNK), F32),
                   jax.ShapeDtypeStruct((CHUNK, CHUNK), F32),
                   jax.ShapeDtypeStruct((nb * MEM_LEN, D_MODEL), F32),
                   jax.ShapeDtypeStruct((1, D_MODEL), F32),
                   jax.ShapeDtypeStruct((n, D_MODEL), F32)],
        scratch_shapes=[pltpu.VMEM((SUBLANES, tm + HALO, D_CONV), F32), pltpu.VMEM((tm, D_CONV), F32),
                        pltpu.VMEM((2, tm, D_IN), BF16)],
        compiler_params=_params(("arbitrary",)),
    )(h, z, dy, kv, cw, clg, clb, glg, glb, ws, bs_t, w_full, dr, after)


def _grad_w(a, b, tk, name):
    m = a.shape[1]
    kdim, ncols = b.shape
    nk = kdim // tk
    shard = m // N_CHIPS
    oshape = (shard, ncols)
    a_spec = pl.BlockSpec((tk, shard), lambda j, kk: (kk, j))
    b_spec = pl.BlockSpec((tk, ncols), lambda j, kk: (kk, 0))

    def body(a_ref, b_ref, own_ref, ob_ref, acc):
        j = pl.program_id(0)
        kk = pl.program_id(1)
        mine = 2 * lax.axis_index("x") + lax.axis_index("y")

        @pl.when(kk == 0)
        def _():
            acc[...] = jnp.zeros_like(acc)

        acc[...] += _dot_tn(a_ref[...].astype(BF16), b_ref[...].astype(BF16))

        @pl.when(kk == nk - 1)
        def _():
            ob_ref[...] = acc[...].astype(BF16)

        @pl.when((kk == nk - 1) & (j == mine))
        def _():
            own_ref[...] = acc[...]

    return pl.pallas_call(
        body, name=name, grid=(N_CHIPS, nk),
        in_specs=[a_spec, b_spec],
        out_specs=[pl.BlockSpec(oshape, lambda j, kk: (0, 0)),
                   pl.BlockSpec((None,) + oshape, lambda j, kk: (j, 0, 0))],
        out_shape=[jax.ShapeDtypeStruct(oshape, F32), jax.ShapeDtypeStruct((N_CHIPS,) + oshape, BF16)],
        scratch_shapes=[pltpu.VMEM(oshape, F32)],
        compiler_params=_params(("arbitrary", "arbitrary")),
    )(a, b)


def _grad_w_block(a, b, block, tk, name, dtype, after):
    kdim, m = a.shape
    shard = b.shape[1] // N_CHIPS
    nk = kdim // tk

    def body(block_ref, a_ref, b_ref, after_ref, o_ref, acc):
        kk = pl.program_id(0)

        @pl.when(kk == 0)
        def _():
            acc[...] = jnp.zeros_like(acc)

        acc[...] += _dot_tn(a_ref[...].astype(BF16), b_ref[...])

        @pl.when(kk == nk - 1)
        def _():
            o_ref[...] = acc[...].astype(dtype)

    grid_spec = pltpu.PrefetchScalarGridSpec(
        num_scalar_prefetch=1, grid=(nk,),
        in_specs=[pl.BlockSpec((tk, m), lambda kk, blk: (kk, 0)),
                  pl.BlockSpec((tk, shard), lambda kk, blk: (kk, blk[0])), ANY],
        out_specs=pl.BlockSpec((m, shard), lambda kk, blk: (0, 0)),
        scratch_shapes=[pltpu.VMEM((m, shard), F32)])
    return pl.pallas_call(
        body, name=name, grid_spec=grid_spec, out_shape=jax.ShapeDtypeStruct((m, shard), dtype),
        compiler_params=_params(("arbitrary",)),
    )(block, a, b, after)


def _sum_small(owns, gots):
    n = len(owns)

    def body(*refs):
        for a in range(n):
            o_ref, g_ref, out_ref = refs[a], refs[n + a], refs[2 * n + a]
            out_ref[...] = (o_ref[...] + g_ref[1]) + (g_ref[0] + g_ref[2])

    return pl.pallas_call(
        body, name="sum_small", out_shape=[jax.ShapeDtypeStruct(o.shape, F32) for o in owns],
        compiler_params=pltpu.CompilerParams(vmem_limit_bytes=VMEM_LIMIT),
    )(*owns, *gots)


def _sum_chips(own, gots, tr, name):
    r, ccols = own.shape

    def body(o_ref, gx_ref, gy_ref, gxy_ref, out_ref):
        out_ref[...] = (o_ref[...] + gy_ref[...].astype(F32)) + (gx_ref[...].astype(F32) + gxy_ref[...].astype(F32))

    return pl.pallas_call(
        body, name=name, grid=(r // tr,),
        in_specs=[pl.BlockSpec((tr, ccols), lambda i: (i, 0))]
        + [pl.BlockSpec((None, tr, ccols), lambda i, slot=slot: (slot, i, 0)) for _, slot in gots],
        out_specs=pl.BlockSpec((tr, ccols), lambda i: (i, 0)),
        out_shape=jax.ShapeDtypeStruct((r, ccols), F32),
        compiler_params=_params(("arbitrary",)),
    )(own, *[g for g, _ in gots])


def _sum_chips_send(own, gots, tr, name):
    r, ccols = own.shape
    nt = r // tr

    def body(o_ref, gx_ref, gy_ref, gxy_ref, out_ref, sib_ref, stage, send_sems, recv_sems):
        i = pl.program_id(0)
        sibling = (lax.axis_index("x"), lax.axis_index("y"), 1 - lax.axis_index("c"))

        def tile_copy(t, row0):
            return pltpu.make_async_remote_copy(
                src_ref=stage.at[t], dst_ref=sib_ref.at[pl.ds(row0, tr)], send_sem=send_sems.at[t],
                recv_sem=recv_sems.at[t], device_id=sibling, device_id_type=MESH_ID)

        total = (o_ref[...] + gy_ref[...].astype(F32)) + (gx_ref[...].astype(F32) + gxy_ref[...].astype(F32))
        out_ref[...] = total
        stage[i] = total
        tile_copy(i, pl.multiple_of(i * tr, tr)).start()

        @pl.when(i == nt - 1)
        def _():
            for t in range(nt):
                tile_copy(t, t * tr).wait_recv()
            for t in range(nt):
                tile_copy(t, t * tr).wait_send()

    shape = jax.ShapeDtypeStruct((r, ccols), F32)
    return pl.pallas_call(
        body, name=name, grid=(nt,),
        in_specs=[pl.BlockSpec((tr, ccols), lambda i: (i, 0))]
        + [pl.BlockSpec((None, tr, ccols), lambda i, slot=slot: (slot, i, 0)) for _, slot in gots],
        out_specs=[pl.BlockSpec((tr, ccols), lambda i: (i, 0)), ANY],
        out_shape=[shape, shape],
        scratch_shapes=[pltpu.VMEM((nt, tr, ccols), F32), pltpu.SemaphoreType.DMA((nt,)),
                        pltpu.SemaphoreType.DMA((nt,))],
        compiler_params=_params(("arbitrary",)),
    )(own, *[g for g, _ in gots])


def _exchange_cores(parts, name):
    npart = len(parts)

    def body(*refs):
        in_refs = refs[0:npart]
        out_refs = refs[npart:2 * npart]
        send_sems, recv_sems = refs[2 * npart:]
        sibling = (lax.axis_index("x"), lax.axis_index("y"), 1 - lax.axis_index("c"))
        copies = [pltpu.make_async_remote_copy(
            src_ref=in_refs[a], dst_ref=out_refs[a], send_sem=send_sems.at[a], recv_sem=recv_sems.at[a],
            device_id=sibling, device_id_type=MESH_ID) for a in range(npart)]
        for cp in copies:
            cp.start()
        for cp in copies:
            cp.wait_recv()
        for cp in copies:
            cp.wait_send()

    return pl.pallas_call(
        body, name=name, out_shape=[jax.ShapeDtypeStruct(p.shape, p.dtype) for p in parts],
        in_specs=[ANY] * npart, out_specs=[ANY] * npart,
        scratch_shapes=[pltpu.SemaphoreType.DMA((npart,)), pltpu.SemaphoreType.DMA((npart,))],
    )(*parts)


def _adam(g, w, m, v):
    mn = ADAM_B1 * m + (1.0 - ADAM_B1) * g
    vn = ADAM_B2 * v + (1.0 - ADAM_B2) * (g * g)
    return g, -ADAM_LR * ((mn / BC1) / (jnp.sqrt(vn / BC2) + ADAM_EPS) + ADAM_WD * w), mn, vn


def _adamw(a, b, w, m, v, tr, name):
    r, ccols = w.shape

    def body(a_ref, b_ref, w_ref, m_ref, v_ref, *outs):
        res = _adam(a_ref[...] + b_ref[...], w_ref[...], m_ref[...], v_ref[...])
        for which in range(4):
            outs[which][...] = res[which]

    spec = pl.BlockSpec((tr, ccols), lambda i: (i, 0))
    shape = jax.ShapeDtypeStruct((r, ccols), F32)
    return pl.pallas_call(
        body, name=name, grid=(r // tr,), in_specs=[spec] * 5, out_specs=[spec] * 4, out_shape=[shape] * 4,
        compiler_params=_params(("arbitrary",)),
    )(a, b, w, m, v)


SMALL = ["b_in", "conv_b", "conv_ln_g", "conv_ln_b", "gmlp_ln_g", "gmlp_ln_b", "gmlp_ws", "gmlp_bs", "b_kv", "b_out",
         "ln_g", "ln_b"]


def _adamw_small(a_parts, b_parts, params):
    nparts, nparams = len(a_parts), len(params)

    def body(*refs):
        a = refs[0:nparts]
        b = refs[nparts:2 * nparts]
        prm = refs[2 * nparts:2 * nparts + 3 * nparams]
        outs = refs[2 * nparts + 3 * nparams:]
        gb_in, g768, gws, gbs_t, gb_kv, vec3 = [a[q][...] + b[q][...] for q in range(nparts)]
        grads = [gb_in, g768[ROW_CB:ROW_CB + 1], g768[ROW_CLG:ROW_CLG + 1], g768[ROW_CLB:ROW_CLB + 1],
                 g768[ROW_GLG:ROW_GLG + 1], g768[ROW_GLB:ROW_GLB + 1], gws, jnp.transpose(gbs_t)[0:N_GHEADS, :],
                 gb_kv, vec3[0:1], vec3[1:2], vec3[2:3]]
        for q, g in enumerate(grads):
            res = _adam(g, prm[3 * q][...], prm[3 * q + 1][...], prm[3 * q + 2][...])
            for which in range(4):
                outs[4 * q + which][...] = res[which]

    flat = [t for p in params for t in p]
    out_shape = [jax.ShapeDtypeStruct(p[0].shape, F32) for p in params for _ in range(4)]
    return pl.pallas_call(
        body, name="adamw_small", out_shape=out_shape,
        compiler_params=pltpu.CompilerParams(vmem_limit_bytes=VMEM_LIMIT),
    )(*a_parts, *b_parts, *flat)


def kernel(x, mem, w_in, b_in, conv_w, conv_b, conv_ln_g, conv_ln_b, gmlp_ln_g, gmlp_ln_b, gmlp_ws, gmlp_bs, w_kv, b_kv, w_out, b_out, ln_g, ln_b, loss_target, m_w_in, m_b_in, m_conv_w, m_conv_b, m_conv_ln_g, m_conv_ln_b, m_gmlp_ln_g, m_gmlp_ln_b, m_gmlp_ws, m_gmlp_bs, m_w_kv, m_b_kv, m_w_out, m_b_out, m_ln_g, m_ln_b, v_w_in, v_b_in, v_conv_w, v_conv_b, v_conv_ln_g, v_conv_ln_b, v_gmlp_ln_g, v_gmlp_ln_b, v_gmlp_ws, v_gmlp_bs, v_w_kv, v_b_kv, v_w_out, v_b_out, v_ln_g, v_ln_b):
    weights = dict(b_in=b_in, conv_b=conv_b, conv_ln_g=conv_ln_g, conv_ln_b=conv_ln_b, gmlp_ln_g=gmlp_ln_g,
                   gmlp_ln_b=gmlp_ln_b, gmlp_ws=gmlp_ws, gmlp_bs=gmlp_bs, b_kv=b_kv, b_out=b_out, ln_g=ln_g, ln_b=ln_b)
    mom_m = dict(b_in=m_b_in, conv_b=m_conv_b, conv_ln_g=m_conv_ln_g, conv_ln_b=m_conv_ln_b, gmlp_ln_g=m_gmlp_ln_g,
                 gmlp_ln_b=m_gmlp_ln_b, gmlp_ws=m_gmlp_ws, gmlp_bs=m_gmlp_bs, b_kv=m_b_kv, b_out=m_b_out,
                 ln_g=m_ln_g, ln_b=m_ln_b)
    mom_v = dict(b_in=v_b_in, conv_b=v_conv_b, conv_ln_g=v_conv_ln_g, conv_ln_b=v_conv_ln_b, gmlp_ln_g=v_gmlp_ln_g,
                 gmlp_ln_b=v_gmlp_ln_b, gmlp_ws=v_gmlp_ws, gmlp_bs=v_gmlp_bs, b_kv=v_b_kv, b_out=v_b_out,
                 ln_g=v_ln_g, ln_b=v_ln_b)
    nb, seq, _ = x.shape
    n = nb * seq
    tm = 256
    tk = min(2048, n)
    x2 = x.reshape(n, D_MODEL)
    tgt2 = loss_target.reshape(n, D_MODEL)
    mem2 = mem.reshape(nb * MEM_LEN, D_MODEL)
    chip = 2 * lax.axis_index("x") + lax.axis_index("y")
    bs_t = jnp.transpose(gmlp_bs[0])

    own_kv = [w_kv[0].astype(BF16), conv_w[0]]
    own_out = [w_out[0].astype(BF16)]
    ga = _start_exchange("gather_kv_start", own_kv, [False] * 2)
    win_g = _gather_w_in(w_in[0].astype(BF16), ga["token"])
    wkv_g, cw_g = _place_shards(own_kv, _wait_exchange("gather_kv_wait", ga, win_g), "place_kv")
    wkv_full = wkv_g.reshape(D_MODEL, D_MODEL)
    cw_full = jnp.transpose(cw_g, (1, 0, 2)).reshape(CONV_WIDTH, D_CONV)
    kv = _kv_proj(mem2, wkv_full, b_kv)
    gb = _start_exchange("gather_out_start", own_out, [False])
    h, ycat, z, x_bf = _forward(x2, win_g, b_in, kv, cw_full, conv_b, conv_ln_g, conv_ln_b, gmlp_ln_g, gmlp_ln_b,
                                gmlp_ws[0], bs_t, nb, seq, tm, gb["token"])
    (wout_g,) = _place_shards(own_out, _wait_exchange("gather_out_wait", gb, ycat), "place_out")
    wout_full = wout_g.reshape(D_MIX, D_MODEL)
    dr, dr_bf, dycat, vec3 = _out_proj_loss(ycat, wout_full, b_out, ln_g, ln_b, x2, tgt2, min(512, n))

    own_wout, gwout_b = _grad_w(ycat, dr_bf, min(2048, n), "grad_w_out")
    ex1 = _start_exchange("exchange1_start", [gwout_b, vec3], [True, False])
    dh, gb_in, g768, gws, gbs_t, dkv, gb_kv, grad_x2 = _branch_bwd(
        h, z, dycat, kv, cw_full, conv_ln_g, conv_ln_b, gmlp_ln_g, gmlp_ln_b, gmlp_ws[0], bs_t, win_g, dr,
        nb, seq, tm, ex1["token"])
    own_wkv, gwkv_b = _grad_w(mem2, dkv, nb * MEM_LEN, "grad_w_kv")
    small2 = [gb_in, g768, gws, gbs_t, gb_kv]
    ex2 = _start_exchange("exchange2_start", [gwkv_b] + small2, [True] + [False] * 5)
    block_of = lambda flip_bits: (chip ^ flip_bits).astype(jnp.int32).reshape(1)
    after, ex3 = ex2["token"], {}
    for flip, bits in ((2, 3), (1, 1), (0, 2)):
        part = _grad_w_block(x_bf, dh, block_of(bits), tk, f"grad_w_in_{flip}", BF16, after)
        ex3[flip] = _start_exchange(f"exchange3{flip}_start", [part], [False], (flip,))
        after = ex3[flip]["token"]
    own_win = _grad_w_block(x_bf, dh, block_of(0), tk, "grad_w_in_own", F32, after)
    got_wout, got_vec3 = _wait_exchange("exchange1_wait", ex1, own_win)
    got2 = _wait_exchange("exchange2_wait", ex2, own_win)

    sum_wout = _sum_chips(own_wout, [(got_wout, 0), (got_wout, 1), (got_wout, 2)], 256, "sum_w_out")
    sum_wkv = _sum_chips(own_wkv, [(got2[0], 0), (got2[0], 1), (got2[0], 2)], 256, "sum_w_kv")
    sum_small = list(_sum_small(small2 + [vec3], got2[1:] + [got_vec3]))
    sib = list(_exchange_cores([sum_wout, sum_wkv] + sum_small, "exchange_cores_rest"))
    loss = sum_small[5][ROW_LOSS, 0] + sib[7][ROW_LOSS, 0]
    big = {}
    big["w_out"] = _adamw(sum_wout, sib[0], w_out[0], m_w_out[0], v_w_out[0], 256, "adamw_w_out")
    big["w_kv"] = _adamw(sum_wkv, sib[1], w_kv[0], m_w_kv[0], v_w_kv[0], 256, "adamw_w_kv")
    lead = {"gmlp_ws", "gmlp_bs"}
    strip = lambda k, t: t[0] if k in lead else t
    sm = _adamw_small(sum_small, sib[2:], [tuple(strip(k, t[k]) for t in (weights, mom_m, mom_v)) for k in SMALL])
    small_out = {k: [sm[4 * q + which][None] if k in lead else sm[4 * q + which] for which in range(4)]
                 for q, k in enumerate(SMALL)}
    cw_a = lax.dynamic_slice_in_dim(sum_small[1][0:CONV_WIDTH + 1], chip * CONV_SHARD, CONV_SHARD, axis=1)
    cw_b = lax.dynamic_slice_in_dim(sib[3][0:CONV_WIDTH + 1], chip * CONV_SHARD, CONV_SHARD, axis=1)
    cwp = ((0, 1), (0, 0))
    cw_out = _adamw(cw_a, cw_b, jnp.pad(conv_w[0], cwp), jnp.pad(m_conv_w[0], cwp), jnp.pad(v_conv_w[0], cwp),
                    CONV_WIDTH + 1, "adamw_conv_w")

    done = cw_out[0]
    got_win = {flip: _wait_exchange(f"exchange3{flip}_wait", ex3[flip], done)[0] for flip in (2, 1, 0)}
    sum_win, sib_win = _sum_chips_send(own_win, [(got_win[0], 0), (got_win[1], 0), (got_win[2], 0)], 256,
                                       "sum_send_w_in")
    big["w_in"] = _adamw(sum_win, sib_win, w_in[0], m_w_in[0], v_w_in[0], 256, "adamw_w_in")

    order = ["w_in", "b_in", "conv_w", "conv_b", "conv_ln_g", "conv_ln_b", "gmlp_ln_g", "gmlp_ln_b", "gmlp_ws",
             "gmlp_bs", "w_kv", "b_kv", "w_out", "b_out", "ln_g", "ln_b"]
    result = [loss, grad_x2.reshape(nb, seq, D_MODEL)]
    for which in range(4):
        for k in order:
            if k in big:
                result.append(big[k][which][None])
            elif k == "conv_w":
                result.append(cw_out[which][0:CONV_WIDTH][None])
            else:
                result.append(small_out[k][which])
    return tuple(result)
```

```python
import functools
import math

import jax
import jax.numpy as jnp
from jax import lax
from jax.experimental import pallas as pl
from jax.experimental.pallas import tpu as pltpu

F32 = jnp.float32
BF16 = jnp.bfloat16

D_MODEL = 1024
MEM_LEN = 256
D_MIX = 2048
D_CONV = 768
D_GMLP = 768
D_XATT = 512
N_XHEADS = 4
XHEAD = 128
CONV_WIDTH = 31
CHUNK = 128
N_GHEADS = 6
D_IN = 3 * D_CONV + 3 * D_GMLP + 2 * D_XATT
ALPHA = 2.0 ** 0.25
LN_EPS = 1e-5
N_CHIPS = 4
W_IN_SHARD = D_IN // N_CHIPS
W_OUT_SHARD = D_MIX // N_CHIPS
W_KV_SHARD = D_MODEL // N_CHIPS
CONV_SHARD = D_CONV // N_CHIPS
HALO = 32

C_A, C_GLU, C_GATE = 0, 768, 1536
G_U, G_V, G_GATE = 2304, 3072, 3840
X_Q, X_GATE = 4608, 5120

ADAM_LR = 0.001
ADAM_B1 = 0.9
ADAM_B2 = 0.999
ADAM_EPS = 1e-08
ADAM_WD = 0.01
ADAM_STEP = 10
BC1 = 1.0 - ADAM_B1 ** ADAM_STEP
BC2 = 1.0 - ADAM_B2 ** ADAM_STEP

VMEM_LIMIT = 56 * 1024 * 1024
MESH_ID = pl.DeviceIdType.MESH
ANY = pl.BlockSpec(memory_space=pl.ANY)

GELU_C0 = math.sqrt(2.0 / math.pi)
GELU_C1 = 0.044715


def _sigmoid(v):
    return 0.5 + 0.5 * jnp.tanh(0.5 * v)


def _f32(ref, rows, col, width):
    return ref[rows, col:col + width].astype(F32)


def _dsilu(v, s):
    return s * (1.0 + v * (1.0 - s))


def _gelu_and_grad(v):
    t = jnp.tanh(GELU_C0 * (v + GELU_C1 * v * v * v))
    g = 0.5 * v * (1.0 + t)
    dg = 0.5 * (1.0 + t) + 0.5 * v * (1.0 - t * t) * (GELU_C0 * (1.0 + 3.0 * GELU_C1 * v * v))
    return g, dg


def _gelu(v):
    return 0.5 * v * (1.0 + jnp.tanh(GELU_C0 * (v + GELU_C1 * v * v * v)))


def _ln_stats(v):
    mu = jnp.mean(v, axis=-1, keepdims=True)
    vc = v - mu
    var = jnp.mean(vc * vc, axis=-1, keepdims=True)
    rstd = lax.rsqrt(var + LN_EPS)
    return vc * rstd, rstd


def _ln_bwd(dvhat, vhat, rstd):
    m1 = jnp.mean(dvhat, axis=-1, keepdims=True)
    m2 = jnp.mean(dvhat * vhat, axis=-1, keepdims=True)
    return rstd * (dvhat - m1 - vhat * m2)


def _colsum(v):
    return jnp.sum(v, axis=0, keepdims=True)


def _dot(a, b):
    return jnp.dot(a, b, preferred_element_type=F32)


def _dot_nt(a, b):
    return lax.dot_general(a, b, (((1,), (1,)), ((), ())), preferred_element_type=F32)


def _dot_tn(a, b):
    return lax.dot_general(a, b, (((0,), (0,)), ((), ())), preferred_element_type=F32)


def _causal_ws(ws_ref):
    row = lax.broadcasted_iota(jnp.int32, (CHUNK, CHUNK), 0)
    col = lax.broadcasted_iota(jnp.int32, (CHUNK, CHUNK), 1)
    keep = col <= row
    return [jnp.where(keep, ws_ref[hd], 0.0).astype(BF16) for hd in range(N_GHEADS)], keep


def _params(sem):
    return pltpu.CompilerParams(dimension_semantics=sem, vmem_limit_bytes=VMEM_LIMIT)


def _peer_chips():
    x, y, c = lax.axis_index("x"), lax.axis_index("y"), lax.axis_index("c")
    return [(1 - x, y), (x, 1 - y), (1 - x, 1 - y)], 2 * x + y, c


HBM = pl.BlockSpec(memory_space=pltpu.HBM)
SEM = pl.BlockSpec(memory_space=pltpu.SEMAPHORE)
EFFECT = pltpu.SideEffectType.DATAFLOW_SIDE_EFFECTING
ALL_FLIPS = (0, 1, 2)


def _exchange_copies(src_refs, land_refs, per_chip, flips, send_sems, recv_sems):
    chips, _, c = _peer_chips()
    n = len(src_refs)
    copies = []
    for q, p in enumerate(flips):
        px, py = chips[p]
        for a in range(n):
            src = src_refs[a].at[2 * px + py] if per_chip[a] else src_refs[a]
            copies.append(pltpu.make_async_remote_copy(
                src_ref=src, dst_ref=land_refs[a].at[q], send_sem=send_sems.at[n * q + a],
                recv_sem=recv_sems.at[n * q + a], device_id=(px, py, c), device_id_type=MESH_ID))
    return copies


def _start_exchange(name, srcs, per_chip, flips=ALL_FLIPS):
    n = len(srcs)
    nf = len(flips)
    lands = [lax.empty((nf,) + (s.shape[1:] if pc else s.shape), s.dtype) for s, pc in zip(srcs, per_chip)]

    def body(*refs):
        src_refs, land_refs = refs[0:n], refs[n:2 * n]
        send_sems, recv_sems = refs[2 * n], refs[2 * n + 1]
        token = refs[4 * n + 2]
        for cp in _exchange_copies(src_refs, land_refs, per_chip, flips, send_sems, recv_sems):
            cp.start()
        token[...] = jnp.zeros_like(token)

    out = pl.pallas_call(
        body, name=name,
        out_shape=(pltpu.SemaphoreType.DMA((nf * n,)), pltpu.SemaphoreType.DMA((nf * n,)),
                   *[pltpu.HBM(a.shape, a.dtype) for a in srcs + lands], jax.ShapeDtypeStruct((8, 128), F32)),
        in_specs=[HBM] * (2 * n),
        out_specs=(SEM, SEM, *[HBM] * (2 * n), pl.BlockSpec(memory_space=pltpu.VMEM)),
        input_output_aliases={a: 2 + a for a in range(2 * n)},
        compiler_params=pltpu.CompilerParams(has_side_effects=EFFECT),
    )(*[pltpu.with_memory_space_constraint(a, pltpu.HBM) for a in srcs + lands])
    return dict(send=out[0], recv=out[1], thru=list(out[2:2 * n + 2]), token=out[2 * n + 2], per_chip=per_chip,
                flips=flips)


def _wait_exchange(name, started, after):
    thru, per_chip, flips = started["thru"], started["per_chip"], started["flips"]
    n = len(thru) // 2

    def body(*refs):
        src_refs, land_refs = refs[0:n], refs[n:2 * n]
        send_sems, recv_sems = refs[2 * n], refs[2 * n + 1]
        for cp in _exchange_copies(src_refs, land_refs, per_chip, flips, send_sems, recv_sems):
            cp.wait_send()
            cp.wait_recv()

    out = pl.pallas_call(
        body, name=name, out_shape=tuple(pltpu.HBM(a.shape, a.dtype) for a in thru),
        in_specs=[HBM] * (2 * n) + [SEM, SEM, ANY], out_specs=tuple([HBM] * (2 * n)),
        input_output_aliases={a: a for a in range(2 * n)},
        compiler_params=pltpu.CompilerParams(has_side_effects=EFFECT),
    )(*thru, started["send"], started["recv"], after)
    return list(out[n:2 * n])


def _place_shards(owns, landeds, name):
    n = len(owns)
    mine = (2 * lax.axis_index("x") + lax.axis_index("y")).astype(jnp.int32).reshape(1)

    def body(mine_ref, *refs):
        own_refs, land_refs, out_refs = refs[0:n], refs[n:2 * n], refs[2 * n:3 * n]
        k = pl.program_id(0)
        for a in range(n):
            @pl.when(k == mine_ref[0])
            def _():
                out_refs[a][...] = own_refs[a][...]

            @pl.when(k != mine_ref[0])
            def _():
                out_refs[a][...] = land_refs[a][...]

    def slot(k, mine_ref):
        d = k ^ mine_ref[0]
        return jnp.where(d == 1, 1, jnp.where(d == 3, 2, 0))

    zeros = lambda o: (0,) * len(o.shape)
    grid_spec = pltpu.PrefetchScalarGridSpec(
        num_scalar_prefetch=1, grid=(N_CHIPS,),
        in_specs=[pl.BlockSpec(o.shape, lambda k, m, o=o: zeros(o)) for o in owns]
        + [pl.BlockSpec((None,) + o.shape, lambda k, m, o=o: (slot(k, m),) + zeros(o)) for o in owns],
        out_specs=[pl.BlockSpec((None,) + o.shape, lambda k, m, o=o: (k,) + zeros(o)) for o in owns])
    return pl.pallas_call(
        body, name=name, grid_spec=grid_spec,
        out_shape=[jax.ShapeDtypeStruct((N_CHIPS,) + o.shape, o.dtype) for o in owns],
        compiler_params=_params(("arbitrary",)),
    )(mine, *owns, *landeds)


def _gather_w_in(w_shard, after):
    half = D_MODEL // 2

    def body(w_ref, after_ref, out_ref, ici_send, ici_recv, d2d_send, d2d_recv, loc_sem):
        chips, mine, c = _peer_chips()
        sibling = (lax.axis_index("x"), lax.axis_index("y"), 1 - c)
        my_rows = pl.ds(pl.multiple_of(c * half, half), half)
        their_rows = pl.ds(pl.multiple_of((1 - c) * half, half), half)
        own = pltpu.make_async_copy(w_ref, out_ref.at[mine], loc_sem)
        own.start()

        def over_ici(p, slot):
            px, py = chips[p]
            return pltpu.make_async_remote_copy(
                src_ref=w_ref.at[my_rows, :], dst_ref=out_ref.at[slot, my_rows, :], send_sem=ici_send.at[p],
                recv_sem=ici_recv.at[p], device_id=(px, py, c), device_id_type=MESH_ID)

        def over_d2d(p, rows):
            px, py = chips[p]
            part = out_ref.at[2 * px + py, rows, :]
            return pltpu.make_async_remote_copy(
                src_ref=part, dst_ref=part, send_sem=d2d_send.at[p], recv_sem=d2d_recv.at[p],
                device_id=sibling, device_id_type=MESH_ID)

        for p in (1, 0, 2):
            over_ici(p, mine).start()
        for p in (1, 0, 2):
            px, py = chips[p]
            over_ici(p, 2 * px + py).wait_recv()
            over_d2d(p, my_rows).start()
        for p in (1, 0, 2):
            over_d2d(p, their_rows).wait_recv()
        for p in range(3):
            over_ici(p, mine).wait_send()
            over_d2d(p, my_rows).wait_send()
        own.wait()

    return pl.pallas_call(
        body, name="gather_w_in", out_shape=jax.ShapeDtypeStruct((N_CHIPS,) + w_shard.shape, w_shard.dtype),
        in_specs=[ANY, ANY], out_specs=ANY,
        scratch_shapes=[pltpu.SemaphoreType.DMA((3,)), pltpu.SemaphoreType.DMA((3,)), pltpu.SemaphoreType.DMA((3,)),
                        pltpu.SemaphoreType.DMA((3,)), pltpu.SemaphoreType.DMA],
    )(w_shard, after)


def _kv_proj(mem2, wkv_full, b_kv):
    m = mem2.shape[0]

    def body(m_ref, w_ref, b_ref, o_ref):
        o_ref[...] = (_dot(m_ref[...].astype(BF16), w_ref[...]) + b_ref[...]).astype(BF16)

    return pl.pallas_call(
        body, name="kv_proj", grid=(m // MEM_LEN,),
        in_specs=[pl.BlockSpec((MEM_LEN, D_MODEL), lambda i: (i, 0)),
                  pl.BlockSpec((D_MODEL, D_MODEL), lambda i: (0, 0)),
                  pl.BlockSpec((1, D_MODEL), lambda i: (0, 0))],
        out_specs=pl.BlockSpec((MEM_LEN, D_MODEL), lambda i: (i, 0)),
        out_shape=jax.ShapeDtypeStruct((m, D_MODEL), BF16),
        compiler_params=_params(("arbitrary",)),
    )(mem2, wkv_full, b_kv)


CONV_ROWS = 16
SUBLANES = 8


def _shifted_planes(buf, tm):
    rows = tm + HALO - SUBLANES
    for s in range(1, SUBLANES):
        buf[s, 0:rows, :] = buf[0, s:s + rows, :]


def _window(buf, start, rows):
    s = start % SUBLANES
    return buf[s, start - s:start - s + rows, :]


def _forward(x2, w_full, b_in, kv, cw, cb, cg, cbeta, gg_, gb_, ws, bs_t, nb, seq, tm, after):
    nt = seq // tm
    n = nb * seq
    tiles = nb * nt

    def in_proj_piece(x_ref, w_ref, b_ref, hout_ref, xb_ref, hkeep, keep_now):
        xb = x_ref[...].astype(BF16)
        xb_ref[...] = xb

        def h_piece(j):
            cols = slice(j * W_IN_SHARD, (j + 1) * W_IN_SHARD)
            hj = (_dot(xb, w_ref[j]) + b_ref[:, cols]).astype(BF16)
            hout_ref[:, cols] = hj
            hkeep[keep_now, :, cols] = hj

        return h_piece

    def first_step(x_ref, w_ref, b_ref, kv_ref, cw_ref, cb_ref, clg_ref, clb_ref, glg_ref, glb_ref, ws_ref, bst_ref,
                   after_ref, hout_ref, y_ref, z_ref, xb_ref, hcbuf, hkeep):
        h_piece = in_proj_piece(x_ref, w_ref, b_ref, hout_ref, xb_ref, hkeep, 0)
        for j in range(N_CHIPS):
            h_piece(j)

    def later_step(x_ref, w_ref, b_ref, kv_ref, cw_ref, cb_ref, clg_ref, clb_ref, glg_ref, glb_ref, ws_ref, bst_ref,
                   after_ref, hout_ref, y_ref, z_ref, xb_ref, hcbuf, hkeep):
        step = pl.program_id(0)
        i = (step - 1) % nt
        keep_now = step % 2
        h_ref = hkeep.at[1 - keep_now]
        h_piece = in_proj_piece(x_ref, w_ref, b_ref, hout_ref, xb_ref, hkeep, keep_now)

        starts = i == 0

        @pl.when(starts)
        def _():
            hcbuf[0, 0:HALO, :] = jnp.zeros((HALO, D_CONV), F32)

        @pl.when(jnp.logical_not(starts))
        def _():
            hcbuf[0, 0:HALO, :] = hcbuf[0, tm:tm + HALO, :]

        every = slice(None)
        hcbuf[0, HALO:HALO + tm, :] = _f32(h_ref, every, C_A, D_CONV) * _sigmoid(_f32(h_ref, every, C_GLU, D_CONV))
        h_piece(0)
        _shifted_planes(hcbuf, tm)
        for r in range(tm // CONV_ROWS):
            base = r * CONV_ROWS
            acc = jnp.broadcast_to(cb_ref[...], (CONV_ROWS, D_CONV))
            for k in range(CONV_WIDTH):
                acc = acc + cw_ref[k:k + 1, :] * _window(hcbuf, base + 2 + k, CONV_ROWS)
            z_ref[base:base + CONV_ROWS, :] = acc
            if r % 8 == 3:
                h_piece(1 + r // 8)
        zhat, _ = _ln_stats(z_ref[...])
        zn = zhat * clg_ref[...] + clb_ref[...]
        cgate = _f32(h_ref, every, C_GATE, D_CONV)
        y_ref[:, 0:D_CONV] = (zn * _sigmoid(zn) * (cgate * _sigmoid(cgate))).astype(BF16)

        wsc, _ = _causal_ws(ws_ref)
        g_v = _gelu(_f32(h_ref, every, G_V, D_GMLP))
        h_piece(3)
        vhat, _ = _ln_stats(g_v)
        vn = (vhat * glg_ref[...] + glb_ref[...]).astype(BF16)
        for ch in range(tm // CHUNK):
            rows = slice(ch * CHUNK, (ch + 1) * CHUNK)
            for hd in range(N_GHEADS):
                cols = slice(hd * CHUNK, (hd + 1) * CHUNK)
                s = _dot(wsc[hd], vn[rows, cols]) + bst_ref[:, hd:hd + 1]
                u = _gelu(_f32(h_ref, rows, G_U + hd * CHUNK, CHUNK))
                gate = _f32(h_ref, rows, G_GATE + hd * CHUNK, CHUNK)
                y_ref[rows, D_CONV + hd * CHUNK:D_CONV + (hd + 1) * CHUNK] = (
                    u * s * (gate * _sigmoid(gate))).astype(BF16)

        scale = XHEAD ** -0.5
        for hd in range(N_XHEADS):
            q = h_ref[:, X_Q + hd * XHEAD:X_Q + (hd + 1) * XHEAD]
            k = kv_ref[:, hd * XHEAD:(hd + 1) * XHEAD]
            v = kv_ref[:, D_XATT + hd * XHEAD:D_XATT + (hd + 1) * XHEAD]
            s = _dot_nt(q, k) * scale
            e = jnp.exp(s - jnp.max(s, axis=-1, keepdims=True))
            p = e * (1.0 / jnp.sum(e, axis=-1, keepdims=True))
            o = _dot(p.astype(BF16), v)
            gate = _f32(h_ref, every, X_GATE + hd * XHEAD, XHEAD)
            y_ref[:, 2 * D_CONV + hd * XHEAD:2 * D_CONV + (hd + 1) * XHEAD] = (
                o * (gate * _sigmoid(gate))).astype(BF16)

    def body(*refs):
        step = pl.program_id(0)

        @pl.when(step == 0)
        def _():
            first_step(*refs)

        @pl.when(step > 0)
        def _():
            later_step(*refs)

    ahead = lambda s: (jnp.minimum(s, tiles - 1), 0)
    behind = lambda s: (jnp.maximum(s - 1, 0), 0)
    example = lambda s: (jnp.maximum(s - 1, 0) // nt, 0)
    const2 = lambda s: (0, 0)
    const3 = lambda s: (0, 0, 0)
    vec = pl.BlockSpec((1, D_CONV), const2)
    return pl.pallas_call(
        body, name="forward", grid=(tiles + 1,),
        in_specs=[pl.BlockSpec((tm, D_MODEL), ahead),
                  pl.BlockSpec((N_CHIPS, D_MODEL, W_IN_SHARD), const3, pipeline_mode=pl.Buffered(1)),
                  pl.BlockSpec((1, D_IN), const2),
                  pl.BlockSpec((MEM_LEN, D_MODEL), example),
                  pl.BlockSpec((CONV_WIDTH, D_CONV), const2), vec, vec, vec, vec, vec,
                  pl.BlockSpec((N_GHEADS, CHUNK, CHUNK), const3),
                  pl.BlockSpec((CHUNK, N_GHEADS), const2), ANY],
        out_specs=[pl.BlockSpec((tm, D_IN), ahead), pl.BlockSpec((tm, D_MIX), behind),
                   pl.BlockSpec((tm, D_CONV), behind), pl.BlockSpec((tm, D_MODEL), ahead)],
        out_shape=[jax.ShapeDtypeStruct((n, D_IN), BF16), jax.ShapeDtypeStruct((n, D_MIX), BF16),
                   jax.ShapeDtypeStruct((n, D_CONV), F32), jax.ShapeDtypeStruct((n, D_MODEL), BF16)],
        scratch_shapes=[pltpu.VMEM((SUBLANES, HALO + tm, D_CONV), F32), pltpu.VMEM((2, tm, D_IN), BF16)],
        compiler_params=_params(("arbitrary",)),
    )(x2, w_full, b_in, kv, cw, cb, cg, cbeta, gg_, gb_, ws, bs_t, after)


ROW_LOSS = 3


def _out_proj_loss(ycat, wout_full, b_out, ln_g, ln_b, x2, tgt2, tm):
    n = x2.shape[0]

    def body(y_ref, w_ref, bo_ref, g_ref, b_ref, x_ref, t_ref, dr_ref, drb_ref, dy_ref, vec_ref):
        i = pl.program_id(0)

        @pl.when(i == 0)
        def _():
            vec_ref[...] = jnp.zeros_like(vec_ref)

        r = ALPHA * x_ref[...] + _dot(y_ref[...], w_ref[...]) + bo_ref[...]
        rhat, rstd = _ln_stats(r)
        diff = rhat * g_ref[...] + b_ref[...] - t_ref[...]
        loss = 0.5 * jnp.sum(jnp.mean(diff * diff, axis=-1, keepdims=True), axis=0, keepdims=True)
        dout = diff * (1.0 / D_MODEL)
        dr = _ln_bwd(dout * g_ref[...], rhat, rstd)
        vec_ref[0:1, :] += _colsum(dr)
        vec_ref[1:2, :] += _colsum(dout * rhat)
        vec_ref[2:3, :] += _colsum(dout)
        vec_ref[ROW_LOSS:ROW_LOSS + 1, :] += jnp.broadcast_to(loss, (1, D_MODEL))
        dr_ref[...] = dr
        drb = dr.astype(BF16)
        drb_ref[...] = drb
        dy_ref[...] = _dot_nt(drb, w_ref[...]).astype(BF16)

    row = lambda i: (i, 0)
    const = lambda i: (0, 0)
    vec = pl.BlockSpec((1, D_MODEL), const)
    return pl.pallas_call(
        body, name="out_proj_loss", grid=(n // tm,),
        in_specs=[pl.BlockSpec((tm, D_MIX), row), pl.BlockSpec((D_MIX, D_MODEL), const), vec, vec, vec,
                  pl.BlockSpec((tm, D_MODEL), row), pl.BlockSpec((tm, D_MODEL), row)],
        out_specs=[pl.BlockSpec((tm, D_MODEL), row), pl.BlockSpec((tm, D_MODEL), row), pl.BlockSpec((tm, D_MIX), row),
                   pl.BlockSpec((8, D_MODEL), const)],
        out_shape=[jax.ShapeDtypeStruct((n, D_MODEL), F32), jax.ShapeDtypeStruct((n, D_MODEL), BF16),
                   jax.ShapeDtypeStruct((n, D_MIX), BF16), jax.ShapeDtypeStruct((8, D_MODEL), F32)],
        compiler_params=_params(("arbitrary",)),
    )(ycat, wout_full, b_out, ln_g, ln_b, x2, tgt2)


ROW_CB, ROW_CLG, ROW_CLB, ROW_GLG, ROW_GLB = 32, 33, 34, 35, 36


def _branch_bwd(h, z, dy, kv, cw, clg, clb, glg, glb, ws, bs_t, w_full, dr, nb, seq, tm, after):
    nt = seq // tm
    n = nb * seq
    tiles = nb * nt

    def place(s):
        sc = jnp.minimum(s, tiles - 1)
        return sc // nt, nt - 1 - sc % nt

    def dx_of_previous(w_ref, dr_ref, gx_ref, dhkeep, prev):
        gx_ref[...] = ALPHA * dr_ref[...]

        def dx_piece(j):
            gx_ref[...] += _dot_nt(dhkeep[prev, :, j * W_IN_SHARD:(j + 1) * W_IN_SHARD], w_ref[j])

        return dx_piece

    def last_step(h_ref, z_ref, dy_ref, kv_ref, cw_ref, clg_ref, clb_ref, glg_ref, glb_ref,
                  ws_ref, bst_ref, w_ref, dr_ref, after_ref,
                  dh_ref, gbin_ref, g768_ref, gws_ref, gbst_ref, dkv_ref, gbkv_ref, gx_ref,
                  dzbuf, dvnbuf, dhkeep):
        dx_piece = dx_of_previous(w_ref, dr_ref, gx_ref, dhkeep, 1 - tiles % 2)
        for j in range(N_CHIPS):
            dx_piece(j)
        _, keep = _causal_ws(ws_ref)
        for hd in range(N_GHEADS):
            gws_ref[hd] = jnp.where(keep, gws_ref[hd], 0.0)

    def tile_step(h_ref, z_ref, dy_ref, kv_ref, cw_ref, clg_ref, clb_ref, glg_ref, glb_ref,
                  ws_ref, bst_ref, w_ref, dr_ref, after_ref,
                  dh_ref, gbin_ref, g768_ref, gws_ref, gbst_ref, dkv_ref, gbkv_ref, gx_ref,
                  dzbuf, dvnbuf, dhkeep):
        step = pl.program_id(0)
        i = step % nt
        every = slice(None)
        keep_now = step % 2

        @pl.when(step == 0)
        def _():
            gbin_ref[...] = jnp.zeros_like(gbin_ref)
            g768_ref[...] = jnp.zeros_like(g768_ref)
            gws_ref[...] = jnp.zeros_like(gws_ref)
            gbst_ref[...] = jnp.zeros_like(gbst_ref)
            gbkv_ref[...] = jnp.zeros_like(gbkv_ref)
            dhkeep[1] = jnp.zeros((tm, D_IN), BF16)

        @pl.when(i == 0)
        def _():
            dkv_ref[...] = jnp.zeros_like(dkv_ref)

        dx_piece = dx_of_previous(w_ref, dr_ref, gx_ref, dhkeep, 1 - keep_now)

        def put(rows, col, width, val):
            vb = val.astype(BF16)
            dh_ref[rows, col:col + width] = vb
            dhkeep[keep_now, rows, col:col + width] = vb

        def emit(col, width, val):
            gbin_ref[:, col:col + width] += _colsum(val)
            put(every, col, width, val)

        d_c = dy_ref[:, 0:D_CONV].astype(F32)
        cgate = _f32(h_ref, every, C_GATE, D_CONV)
        sg = _sigmoid(cgate)
        zhat, zrstd = _ln_stats(z_ref[...])
        zn = zhat * clg_ref[...] + clb_ref[...]
        szn = _sigmoid(zn)
        emit(C_GATE, D_CONV, d_c * (zn * szn) * _dsilu(cgate, sg))
        dx_piece(0)
        dzn = d_c * (cgate * sg) * _dsilu(zn, szn)
        g768_ref[ROW_CLG:ROW_CLG + 1, :] += _colsum(dzn * zhat)
        g768_ref[ROW_CLB:ROW_CLB + 1, :] += _colsum(dzn)
        dz = _ln_bwd(dzn * clg_ref[...], zhat, zrstd)
        g768_ref[ROW_CB:ROW_CB + 1, :] += _colsum(dz)

        @pl.when(i == 0)
        def _():
            dzbuf[0, tm:tm + HALO, :] = jnp.zeros((HALO, D_CONV), F32)

        @pl.when(i > 0)
        def _():
            dzbuf[0, tm:tm + HALO, :] = dzbuf[0, 0:HALO, :]

        dzbuf[0, 0:tm, :] = dz
        _shifted_planes(dzbuf, tm)
        a = _f32(h_ref, every, C_A, D_CONV)
        sgl = _sigmoid(_f32(h_ref, every, C_GLU, D_CONV))
        hc = a * sgl

        for r in range(tm // CONV_ROWS):
            base = r * CONV_ROWS
            acc = jnp.zeros((CONV_ROWS, D_CONV), F32)
            for k in range(CONV_WIDTH):
                acc = acc + cw_ref[k:k + 1, :] * _window(dzbuf, base + 30 - k, CONV_ROWS)
            dvnbuf[base:base + CONV_ROWS, :] = acc
            if r % 8 == 3:
                dx_piece(1 + r // 8)
        dhc = dvnbuf[...]
        emit(C_A, D_CONV, dhc * sgl)
        emit(C_GLU, D_CONV, dhc * a * sgl * (1.0 - sgl))
        for k in range(CONV_WIDTH):
            g768_ref[k:k + 1, :] += _colsum(hc * _window(dzbuf, 30 - k, tm))
            if k == CONV_WIDTH // 2:
                dx_piece(3)

        wsc, _ = _causal_ws(ws_ref)
        v, dgelu_v = _gelu_and_grad(_f32(h_ref, every, G_V, D_GMLP))
        vhat, vrstd = _ln_stats(v)
        vn = (vhat * glg_ref[...] + glb_ref[...]).astype(BF16)
        for ch in range(tm // CHUNK):
            rows = slice(ch * CHUNK, (ch + 1) * CHUNK)
            for hd in range(N_GHEADS):
                cols = slice(hd * CHUNK, (hd + 1) * CHUNK)
                vn_blk = vn[rows, cols]
                s = _dot(wsc[hd], vn_blk) + bst_ref[:, hd:hd + 1]
                u, dgelu_u = _gelu_and_grad(_f32(h_ref, rows, G_U + hd * CHUNK, CHUNK))
                gate = _f32(h_ref, rows, G_GATE + hd * CHUNK, CHUNK)
                sgate = _sigmoid(gate)
                d_g = dy_ref[rows, D_CONV + hd * CHUNK:D_CONV + (hd + 1) * CHUNK].astype(F32)
                dgate = d_g * (u * s) * _dsilu(gate, sgate)
                gbin_ref[:, G_GATE + hd * CHUNK:G_GATE + (hd + 1) * CHUNK] += _colsum(dgate)
                put(rows, G_GATE + hd * CHUNK, CHUNK, dgate)
                dyg = d_g * (gate * sgate)
                du = dyg * s * dgelu_u
                gbin_ref[:, G_U + hd * CHUNK:G_U + (hd + 1) * CHUNK] += _colsum(du)
                put(rows, G_U + hd * CHUNK, CHUNK, du)
                ds = dyg * u
                dsb = ds.astype(BF16)
                gws_ref[hd] += _dot_nt(dsb, vn_blk)
                gbst_ref[:, hd:hd + 1] += jnp.sum(ds, axis=1, keepdims=True)
                dvnbuf[rows, cols] = _dot_tn(wsc[hd], dsb)
        dvn = dvnbuf[...]
        g768_ref[ROW_GLG:ROW_GLG + 1, :] += _colsum(dvn * vhat)
        g768_ref[ROW_GLB:ROW_GLB + 1, :] += _colsum(dvn)
        emit(G_V, D_GMLP, _ln_bwd(dvn * glg_ref[...], vhat, vrstd) * dgelu_v)

        scale = XHEAD ** -0.5
        for hd in range(N_XHEADS):
            q = h_ref[:, X_Q + hd * XHEAD:X_Q + (hd + 1) * XHEAD]
            k = kv_ref[:, hd * XHEAD:(hd + 1) * XHEAD]
            vv = kv_ref[:, D_XATT + hd * XHEAD:D_XATT + (hd + 1) * XHEAD]
            s = _dot_nt(q, k) * scale
            e = jnp.exp(s - jnp.max(s, axis=-1, keepdims=True))
            p = e * (1.0 / jnp.sum(e, axis=-1, keepdims=True))
            pb = p.astype(BF16)
            o = _dot(pb, vv)
            gate = _f32(h_ref, every, X_GATE + hd * XHEAD, XHEAD)
            sgate = _sigmoid(gate)
            d_x = dy_ref[:, 2 * D_CONV + hd * XHEAD:2 * D_CONV + (hd + 1) * XHEAD].astype(F32)
            emit(X_GATE + hd * XHEAD, XHEAD, d_x * o * _dsilu(gate, sgate))
            do = (d_x * (gate * sgate)).astype(BF16)
            dp = _dot_nt(do, vv)
            dsc = (p * (dp - jnp.sum(dp * p, axis=-1, keepdims=True))).astype(BF16)
            emit(X_Q + hd * XHEAD, XHEAD, _dot(dsc, k) * scale)
            dkv_ref[:, hd * XHEAD:(hd + 1) * XHEAD] += _dot_tn(dsc, q) * scale
            dkv_ref[:, D_XATT + hd * XHEAD:D_XATT + (hd + 1) * XHEAD] += _dot_tn(pb, do)

        @pl.when(i == nt - 1)
        def _():
            gbkv_ref[...] += _colsum(dkv_ref[...])

    def body(*refs):
        step = pl.program_id(0)

        @pl.when(step < tiles)
        def _():
            tile_step(*refs)

        @pl.when(step == tiles)
        def _():
            last_step(*refs)

    def row(s):
        b, ri = place(s)
        return b * nt + ri, 0

    def row_prev(s):
        b, ri = place(jnp.maximum(s - 1, 0))
        return b * nt + ri, 0

    example = lambda s: (place(s)[0], 0)
    const2 = lambda s: (0, 0)
    const3 = lambda s: (0, 0, 0)
    vec = pl.BlockSpec((1, D_CONV), const2)

    return pl.pallas_call(
        body, name="branch_bwd", grid=(tiles + 1,),
        in_specs=[pl.BlockSpec((tm, D_IN), row),
                  pl.BlockSpec((tm, D_CONV), row), pl.BlockSpec((tm, D_MIX), row),
                  pl.BlockSpec((MEM_LEN, D_MODEL), example),
                  pl.BlockSpec((CONV_WIDTH, D_CONV), const2), vec, vec, vec, vec,
                  pl.BlockSpec((N_GHEADS, CHUNK, CHUNK), const3),
                  pl.BlockSpec((CHUNK, N_GHEADS), const2),
                  pl.BlockSpec((N_CHIPS, D_MODEL, W_IN_SHARD), const3, pipeline_mode=pl.Buffered(1)),
                  pl.BlockSpec((tm, D_MODEL), row_prev), ANY],
        out_specs=[pl.BlockSpec((tm, D_IN), row),
                   pl.BlockSpec((1, D_IN), const2),
                   pl.BlockSpec((40, D_CONV), const2),
                   pl.BlockSpec((N_GHEADS, CHUNK, CHUNK), const3),
                   pl.BlockSpec((CHUNK, CHUNK), const2),
                   pl.BlockSpec((MEM_LEN, D_MODEL), example),
                   pl.BlockSpec((1, D_MODEL), const2),
                   pl.BlockSpec((tm, D_MODEL), row_prev)],
        out_shape=[jax.ShapeDtypeStruct((n, D_IN), BF16),
                   jax.ShapeDtypeStruct((1, D_IN), F32),
                   jax.ShapeDtypeStruct((40, D_CONV), F32),
                   jax.ShapeDtypeStruct((N_GHEADS, CHUNK, CHUNK), F32),
                   jax.ShapeDtypeStruct((CHUNK, CHUNK), F32),
                   jax.ShapeDtypeStruct((nb * MEM_LEN, D_MODEL), F32),
                   jax.ShapeDtypeStruct((1, D_MODEL), F32),
                   jax.ShapeDtypeStruct((n, D_MODEL), F32)],
        scratch_shapes=[pltpu.VMEM((SUBLANES, tm + HALO, D_CONV), F32), pltpu.VMEM((tm, D_CONV), F32),
                        pltpu.VMEM((2, tm, D_IN), BF16)],
        compiler_params=_params(("arbitrary",)),
    )(h, z, dy, kv, cw, clg, clb, glg, glb, ws, bs_t, w_full, dr, after)


def _grad_w(a, b, tk, name):
    m = a.shape[1]
    kdim, ncols = b.shape
    nk = kdim // tk
    shard = m // N_CHIPS
    oshape = (shard, ncols)
    a_spec = pl.BlockSpec((tk, shard), lambda j, kk: (kk, j))
    b_spec = pl.BlockSpec((tk, ncols), lambda j, kk: (kk, 0))

    def body(a_ref, b_ref, own_ref, ob_ref, acc):
        j = pl.program_id(0)
        kk = pl.program_id(1)
        mine = 2 * lax.axis_index("x") + lax.axis_index("y")

        @pl.when(kk == 0)
        def _():
            acc[...] = jnp.zeros_like(acc)

        acc[...] += _dot_tn(a_ref[...].astype(BF16), b_ref[...].astype(BF16))

        @pl.when(kk == nk - 1)
        def _():
            ob_ref[...] = acc[...].astype(BF16)

        @pl.when((kk == nk - 1) & (j == mine))
        def _():
            own_ref[...] = acc[...]

    return pl.pallas_call(
        body, name=name, grid=(N_CHIPS, nk),
        in_specs=[a_spec, b_spec],
        out_specs=[pl.BlockSpec(oshape, lambda j, kk: (0, 0)),
                   pl.BlockSpec((None,) + oshape, lambda j, kk: (j, 0, 0))],
        out_shape=[jax.ShapeDtypeStruct(oshape, F32), jax.ShapeDtypeStruct((N_CHIPS,) + oshape, BF16)],
        scratch_shapes=[pltpu.VMEM(oshape, F32)],
        compiler_params=_params(("arbitrary", "arbitrary")),
    )(a, b)


def _grad_w_block(a, b, block, tk, name, dtype, after):
    kdim, m = a.shape
    shard = b.shape[1] // N_CHIPS
    nk = kdim // tk

    def body(block_ref, a_ref, b_ref, after_ref, o_ref, acc):
        kk = pl.program_id(0)

        @pl.when(kk == 0)
        def _():
            acc[...] = jnp.zeros_like(acc)

        acc[...] += _dot_tn(a_ref[...].astype(BF16), b_ref[...])

        @pl.when(kk == nk - 1)
        def _():
            o_ref[...] = acc[...].astype(dtype)

    grid_spec = pltpu.PrefetchScalarGridSpec(
        num_scalar_prefetch=1, grid=(nk,),
        in_specs=[pl.BlockSpec((tk, m), lambda kk, blk: (kk, 0)),
                  pl.BlockSpec((tk, shard), lambda kk, blk: (kk, blk[0])), ANY],
        out_specs=pl.BlockSpec((m, shard), lambda kk, blk: (0, 0)),
        scratch_shapes=[pltpu.VMEM((m, shard), F32)])
    return pl.pallas_call(
        body, name=name, grid_spec=grid_spec, out_shape=jax.ShapeDtypeStruct((m, shard), dtype),
        compiler_params=_params(("arbitrary",)),
    )(block, a, b, after)


def _sum_small(owns, gots):
    n = len(owns)

    def body(*refs):
        for a in range(n):
            o_ref, g_ref, out_ref = refs[a], refs[n + a], refs[2 * n + a]
            out_ref[...] = (o_ref[...] + g_ref[1]) + (g_ref[0] + g_ref[2])

    return pl.pallas_call(
        body, name="sum_small", out_shape=[jax.ShapeDtypeStruct(o.shape, F32) for o in owns],
        compiler_params=pltpu.CompilerParams(vmem_limit_bytes=VMEM_LIMIT),
    )(*owns, *gots)


def _sum_chips(own, gots, tr, name):
    r, ccols = own.shape

    def body(o_ref, gx_ref, gy_ref, gxy_ref, out_ref):
        out_ref[...] = (o_ref[...] + gy_ref[...].astype(F32)) + (gx_ref[...].astype(F32) + gxy_ref[...].astype(F32))

    return pl.pallas_call(
        body, name=name, grid=(r // tr,),
        in_specs=[pl.BlockSpec((tr, ccols), lambda i: (i, 0))]
        + [pl.BlockSpec((None, tr, ccols), lambda i, slot=slot: (slot, i, 0)) for _, slot in gots],
        out_specs=pl.BlockSpec((tr, ccols), lambda i: (i, 0)),
        out_shape=jax.ShapeDtypeStruct((r, ccols), F32),
        compiler_params=_params(("arbitrary",)),
    )(own, *[g for g, _ in gots])


def _sum_chips_send(own, gots, tr, name):
    r, ccols = own.shape
    nt = r // tr

    def body(o_ref, gx_ref, gy_ref, gxy_ref, out_ref, sib_ref, stage, send_sems, recv_sems):
        i = pl.program_id(0)
        sibling = (lax.axis_index("x"), lax.axis_index("y"), 1 - lax.axis_index("c"))

        def tile_copy(t, row0):
            return pltpu.make_async_remote_copy(
                src_ref=stage.at[t], dst_ref=sib_ref.at[pl.ds(row0, tr)], send_sem=send_sems.at[t],
                recv_sem=recv_sems.at[t], device_id=sibling, device_id_type=MESH_ID)

        total = (o_ref[...] + gy_ref[...].astype(F32)) + (gx_ref[...].astype(F32) + gxy_ref[...].astype(F32))
        out_ref[...] = total
        stage[i] = total
        tile_copy(i, pl.multiple_of(i * tr, tr)).start()

        @pl.when(i == nt - 1)
        def _():
            for t in range(nt):
                tile_copy(t, t * tr).wait_recv()
            for t in range(nt):
                tile_copy(t, t * tr).wait_send()

    shape = jax.ShapeDtypeStruct((r, ccols), F32)
    return pl.pallas_call(
        body, name=name, grid=(nt,),
        in_specs=[pl.BlockSpec((tr, ccols), lambda i: (i, 0))]
        + [pl.BlockSpec((None, tr, ccols), lambda i, slot=slot: (slot, i, 0)) for _, slot in gots],
        out_specs=[pl.BlockSpec((tr, ccols), lambda i: (i, 0)), ANY],
        out_shape=[shape, shape],
        scratch_shapes=[pltpu.VMEM((nt, tr, ccols), F32), pltpu.SemaphoreType.DMA((nt,)),
                        pltpu.SemaphoreType.DMA((nt,))],
        compiler_params=_params(("arbitrary",)),
    )(own, *[g for g, _ in gots])


def _exchange_cores(parts, name):
    npart = len(parts)

    def body(*refs):
        in_refs = refs[0:npart]
        out_refs = refs[npart:2 * npart]
        send_sems, recv_sems = refs[2 * npart:]
        sibling = (lax.axis_index("x"), lax.axis_index("y"), 1 - lax.axis_index("c"))
        copies = [pltpu.make_async_remote_copy(
            src_ref=in_refs[a], dst_ref=out_refs[a], send_sem=send_sems.at[a], recv_sem=recv_sems.at[a],
            device_id=sibling, device_id_type=MESH_ID) for a in range(npart)]
        for cp in copies:
            cp.start()
        for cp in copies:
            cp.wait_recv()
        for cp in copies:
            cp.wait_send()

    return pl.pallas_call(
        body, name=name, out_shape=[jax.ShapeDtypeStruct(p.shape, p.dtype) for p in parts],
        in_specs=[ANY] * npart, out_specs=[ANY] * npart,
        scratch_shapes=[pltpu.SemaphoreType.DMA((npart,)), pltpu.SemaphoreType.DMA((npart,))],
    )(*parts)


def _adam(g, w, m, v):
    mn = ADAM_B1 * m + (1.0 - ADAM_B1) * g
    vn = ADAM_B2 * v + (1.0 - ADAM_B2) * (g * g)
    return g, -ADAM_LR * ((mn / BC1) / (jnp.sqrt(vn / BC2) + ADAM_EPS) + ADAM_WD * w), mn, vn


def _adamw(a, b, w, m, v, tr, name):
    r, ccols = w.shape

    def body(a_ref, b_ref, w_ref, m_ref, v_ref, *outs):
        res = _adam(a_ref[...] + b_ref[...], w_ref[...], m_ref[...], v_ref[...])
        for which in range(4):
            outs[which][...] = res[which]

    spec = pl.BlockSpec((tr, ccols), lambda i: (i, 0))
    shape = jax.ShapeDtypeStruct((r, ccols), F32)
    return pl.pallas_call(
        body, name=name, grid=(r // tr,), in_specs=[spec] * 5, out_specs=[spec] * 4, out_shape=[shape] * 4,
        compiler_params=_params(("arbitrary",)),
    )(a, b, w, m, v)


SMALL = ["b_in", "conv_b", "conv_ln_g", "conv_ln_b", "gmlp_ln_g", "gmlp_ln_b", "gmlp_ws", "gmlp_bs", "b_kv", "b_out",
         "ln_g", "ln_b"]


def _adamw_small(a_parts, b_parts, params):
    nparts, nparams = len(a_parts), len(params)

    def body(*refs):
        a = refs[0:nparts]
        b = refs[nparts:2 * nparts]
        prm = refs[2 * nparts:2 * nparts + 3 * nparams]
        outs = refs[2 * nparts + 3 * nparams:]
        gb_in, g768, gws, gbs_t, gb_kv, vec3 = [a[q][...] + b[q][...] for q in range(nparts)]
        grads = [gb_in, g768[ROW_CB:ROW_CB + 1], g768[ROW_CLG:ROW_CLG + 1], g768[ROW_CLB:ROW_CLB + 1],
                 g768[ROW_GLG:ROW_GLG + 1], g768[ROW_GLB:ROW_GLB + 1], gws, jnp.transpose(gbs_t)[0:N_GHEADS, :],
                 gb_kv, vec3[0:1], vec3[1:2], vec3[2:3]]
        for q, g in enumerate(grads):
            res = _adam(g, prm[3 * q][...], prm[3 * q + 1][...], prm[3 * q + 2][...])
            for which in range(4):
                outs[4 * q + which][...] = res[which]

    flat = [t for p in params for t in p]
    out_shape = [jax.ShapeDtypeStruct(p[0].shape, F32) for p in params for _ in range(4)]
    return pl.pallas_call(
        body, name="adamw_small", out_shape=out_shape,
        compiler_params=pltpu.CompilerParams(vmem_limit_bytes=VMEM_LIMIT),
    )(*a_parts, *b_parts, *flat)


def kernel(x, mem, w_in, b_in, conv_w, conv_b, conv_ln_g, conv_ln_b, gmlp_ln_g, gmlp_ln_b, gmlp_ws, gmlp_bs, w_kv, b_kv, w_out, b_out, ln_g, ln_b, loss_target, m_w_in, m_b_in, m_conv_w, m_conv_b, m_conv_ln_g, m_conv_ln_b, m_gmlp_ln_g, m_gmlp_ln_b, m_gmlp_ws, m_gmlp_bs, m_w_kv, m_b_kv, m_w_out, m_b_out, m_ln_g, m_ln_b, v_w_in, v_b_in, v_conv_w, v_conv_b, v_conv_ln_g, v_conv_ln_b, v_gmlp_ln_g, v_gmlp_ln_b, v_gmlp_ws, v_gmlp_bs, v_w_kv, v_b_kv, v_w_out, v_b_out, v_ln_g, v_ln_b):
    weights = dict(b_in=b_in, conv_b=conv_b, conv_ln_g=conv_ln_g, conv_ln_b=conv_ln_b, gmlp_ln_g=gmlp_ln_g,
                   gmlp_ln_b=gmlp_ln_b, gmlp_ws=gmlp_ws, gmlp_bs=gmlp_bs, b_kv=b_kv, b_out=b_out, ln_g=ln_g, ln_b=ln_b)
    mom_m = dict(b_in=m_b_in, conv_b=m_conv_b, conv_ln_g=m_conv_ln_g, conv_ln_b=m_conv_ln_b, gmlp_ln_g=m_gmlp_ln_g,
                 gmlp_ln_b=m_gmlp_ln_b, gmlp_ws=m_gmlp_ws, gmlp_bs=m_gmlp_bs, b_kv=m_b_kv, b_out=m_b_out,
                 ln_g=m_ln_g, ln_b=m_ln_b)
    mom_v = dict(b_in=v_b_in, conv_b=v_conv_b, conv_ln_g=v_conv_ln_g, conv_ln_b=v_conv_ln_b, gmlp_ln_g=v_gmlp_ln_g,
                 gmlp_ln_b=v_gmlp_ln_b, gmlp_ws=v_gmlp_ws, gmlp_bs=v_gmlp_bs, b_kv=v_b_kv, b_out=v_b_out,
                 ln_g=v_ln_g, ln_b=v_ln_b)
    nb, seq, _ = x.shape
    n = nb * seq
    tm = 256
    tk = min(2048, n)
    x2 = x.reshape(n, D_MODEL)
    tgt2 = loss_target.reshape(n, D_MODEL)
    mem2 = mem.reshape(nb * MEM_LEN, D_MODEL)
    chip = 2 * lax.axis_index("x") + lax.axis_index("y")
    bs_t = jnp.transpose(gmlp_bs[0])

    own_kv = [w_kv[0].astype(BF16), conv_w[0]]
    own_out = [w_out[0].astype(BF16)]
    ga = _start_exchange("gather_kv_start", own_kv, [False] * 2)
    win_g = _gather_w_in(w_in[0].astype(BF16), ga["token"])
    wkv_g, cw_g = _place_shards(own_kv, _wait_exchange("gather_kv_wait", ga, win_g), "place_kv")
    wkv_full = wkv_g.reshape(D_MODEL, D_MODEL)
    cw_full = jnp.transpose(cw_g, (1, 0, 2)).reshape(CONV_WIDTH, D_CONV)
    kv = _kv_proj(mem2, wkv_full, b_kv)
    gb = _start_exchange("gather_out_start", own_out, [False])
    h, ycat, z, x_bf = _forward(x2, win_g, b_in, kv, cw_full, conv_b, conv_ln_g, conv_ln_b, gmlp_ln_g, gmlp_ln_b,
                                gmlp_ws[0], bs_t, nb, seq, tm, gb["token"])
    (wout_g,) = _place_shards(own_out, _wait_exchange("gather_out_wait", gb, ycat), "place_out")
    wout_full = wout_g.reshape(D_MIX, D_MODEL)
    dr, dr_bf, dycat, vec3 = _out_proj_loss(ycat, wout_full, b_out, ln_g, ln_b, x2, tgt2, min(512, n))

    own_wout, gwout_b = _grad_w(ycat, dr_bf, min(2048, n), "grad_w_out")
    ex1 = _start_exchange("exchange1_start", [gwout_b, vec3], [True, False])
    dh, gb_in, g768, gws, gbs_t, dkv, gb_kv, grad_x2 = _branch_bwd(
        h, z, dycat, kv, cw_full, conv_ln_g, conv_ln_b, gmlp_ln_g, gmlp_ln_b, gmlp_ws[0], bs_t, win_g, dr,
        nb, seq, tm, ex1["token"])
    own_wkv, gwkv_b = _grad_w(mem2, dkv, nb * MEM_LEN, "grad_w_kv")
    small2 = [gb_in, g768, gws, gbs_t, gb_kv]
    ex2 = _start_exchange("exchange2_start", [gwkv_b] + small2, [True] + [False] * 5)
    block_of = lambda flip_bits: (chip ^ flip_bits).astype(jnp.int32).reshape(1)
    after, ex3 = ex2["token"], {}
    for flip, bits in ((2, 3), (1, 1), (0, 2)):
        part = _grad_w_block(x_bf, dh, block_of(bits), tk, f"grad_w_in_{flip}", BF16, after)
        ex3[flip] = _start_exchange(f"exchange3{flip}_start", [part], [False], (flip,))
        after = ex3[flip]["token"]
    own_win = _grad_w_block(x_bf, dh, block_of(0), tk, "grad_w_in_own", F32, after)
    got_wout, got_vec3 = _wait_exchange("exchange1_wait", ex1, own_win)
    got2 = _wait_exchange("exchange2_wait", ex2, own_win)

    sum_wout, sib_wout = _sum_chips_send(own_wout, [(got_wout, 0), (got_wout, 1), (got_wout, 2)], 128,
                                         "sum_send_w_out")
    sum_wkv, sib_wkv = _sum_chips_send(own_wkv, [(got2[0], 0), (got2[0], 1), (got2[0], 2)], 128, "sum_send_w_kv")
    sum_small = list(_sum_small(small2 + [vec3], got2[1:] + [got_vec3]))
    sib = [sib_wout, sib_wkv] + list(_exchange_cores(sum_small, "exchange_cores_rest"))
    loss = sum_small[5][ROW_LOSS, 0] + sib[7][ROW_LOSS, 0]
    big = {}
    big["w_out"] = _adamw(sum_wout, sib[0], w_out[0], m_w_out[0], v_w_out[0], 256, "adamw_w_out")
    big["w_kv"] = _adamw(sum_wkv, sib[1], w_kv[0], m_w_kv[0], v_w_kv[0], 256, "adamw_w_kv")
    lead = {"gmlp_ws", "gmlp_bs"}
    strip = lambda k, t: t[0] if k in lead else t
    sm = _adamw_small(sum_small, sib[2:], [tuple(strip(k, t[k]) for t in (weights, mom_m, mom_v)) for k in SMALL])
    small_out = {k: [sm[4 * q + which][None] if k in lead else sm[4 * q + which] for which in range(4)]
                 for q, k in enumerate(SMALL)}
    cw_a = lax.dynamic_slice_in_dim(sum_small[1][0:CONV_WIDTH + 1], chip * CONV_SHARD, CONV_SHARD, axis=1)
    cw_b = lax.dynamic_slice_in_dim(sib[3][0:CONV_WIDTH + 1], chip * CONV_SHARD, CONV_SHARD, axis=1)
    cwp = ((0, 1), (0, 0))
    cw_out = _adamw(cw_a, cw_b, jnp.pad(conv_w[0], cwp), jnp.pad(m_conv_w[0], cwp), jnp.pad(v_conv_w[0], cwp),
                    CONV_WIDTH + 1, "adamw_conv_w")

    done = cw_out[0]
    got_win = {flip: _wait_exchange(f"exchange3{flip}_wait", ex3[flip], done)[0] for flip in (2, 1, 0)}
    sum_win, sib_win = _sum_chips_send(own_win, [(got_win[0], 0), (got_win[1], 0), (got_win[2], 0)], 256,
                                       "sum_send_w_in")
    big["w_in"] = _adamw(sum_win, sib_win, w_in[0], m_w_in[0], v_w_in[0], 256, "adamw_w_in")

    order = ["w_in", "b_in", "conv_w", "conv_b", "conv_ln_g", "conv_ln_b", "gmlp_ln_g", "gmlp_ln_b", "gmlp_ws",
             "gmlp_bs", "w_kv", "b_kv", "w_out", "b_out", "ln_g", "ln_b"]
    result = [loss, grad_x2.reshape(nb, seq, D_MODEL)]
    for which in range(4):
        for k in order:
            if k in big:
                result.append(big[k][which][None])
            elif k == "conv_w":
                result.append(cw_out[which][0:CONV_WIDTH][None])
            else:
                result.append(small_out[k][which])
    return tuple(result)
```
